```python
import math
import jax
import jax.numpy as jnp
from jax import lax
import numpy as np

D_MODEL = 1024
BATCH = 4
SEQ = 4096
DEPTH = 2

NORM_EPS = 1e-6
S5_WIDTH = D_MODEL // 2
S5_GROUP = 16
S5_GROUPS = S5_WIDTH // S5_GROUP
S5_STATE = 64
HEAD_DIM = 64
ATTN_HEADS = D_MODEL // 128
ATTN_KV_HEADS = 2
WINDOW = 128
ATTN_BLOCK = 128
ROPE_THETA = 10000.0
SSD_WIDTH = D_MODEL
SSD_HEAD_DIM = 64
SSD_HEADS = SSD_WIDTH // SSD_HEAD_DIM
SSD_GROUPS = 2
SSD_STATE = 64
SSD_CONV = 4
SSD_CHUNK = 128
N_BRANCH = 3
MOE_GROUPS = 4
MOE_PER_GROUP = 8
MOE_EXPERTS = MOE_GROUPS * MOE_PER_GROUP
MOE_TOPK = 2
MOE_FF = D_MODEL // 2

ATTN_Q = ATTN_HEADS * HEAD_DIM
ATTN_KV = ATTN_KV_HEADS * HEAD_DIM
SSD_BC = SSD_GROUPS * SSD_STATE
SSD_CONV_CH = SSD_WIDTH + 2 * SSD_BC
IN_WIDTHS = (S5_WIDTH, ATTN_Q, ATTN_KV, ATTN_KV, SSD_WIDTH, SSD_CONV_CH, SSD_HEADS, N_BRANCH * D_MODEL)
N_IN = S5_WIDTH + ATTN_Q + 2 * ATTN_KV + SSD_WIDTH + SSD_CONV_CH + SSD_HEADS + N_BRANCH * D_MODEL

kernel_name = 'hybrid_s5_swa_ssd_hmoe'


def _rmsnorm(x, g):
    xf = x.astype(jnp.float32)
    y = xf * lax.rsqrt(jnp.mean(xf * xf, axis=-1, keepdims=True) + NORM_EPS)
    return (y * g.astype(jnp.float32)).astype(x.dtype)


def _split_in(proj):
    idx = []
    off = 0
    for w in IN_WIDTHS[:-1]:
        off += w
        idx.append(off)
    return jnp.split(proj, idx, axis=-1)


def _rope(x, pos):
    hd = x.shape[-1]
    half = hd // 2
    inv = jnp.power(ROPE_THETA, -jnp.arange(half, dtype=jnp.float32) * 2.0 / hd)
    ang = pos.astype(jnp.float32)[:, None] * inv[None, :]
    cos = jnp.cos(ang)[None, :, None, :]
    sin = jnp.sin(ang)[None, :, None, :]
    xf = x.astype(jnp.float32)
    x1, x2 = xf[..., :half], xf[..., half:]
    return jnp.concatenate([x1 * cos - x2 * sin, x2 * cos + x1 * sin], axis=-1).astype(x.dtype)


def _s5_mixer(u, lam_re, lam_im, b_re, b_im, c_re, c_im, d_skip, log_dt, w_glu1, w_glu2):
    bsz, seq, _ = u.shape
    f32 = jnp.float32
    lr = lam_re.astype(f32)
    li = lam_im.astype(f32)
    dt = jnp.exp(log_dt.astype(f32))[:, None]
    mag = jnp.exp(lr * dt)
    ab_re = mag * jnp.cos(li * dt)
    ab_im = mag * jnp.sin(li * dt)
    nr = ab_re - 1.0
    den = lr * lr + li * li
    f_re = (nr * lr + ab_im * li) / den
    f_im = (ab_im * lr - nr * li) / den
    br = b_re.astype(f32)
    bi = b_im.astype(f32)
    bb_re = f_re[..., None] * br - f_im[..., None] * bi
    bb_im = f_re[..., None] * bi + f_im[..., None] * br
    ug = u.astype(f32).reshape(bsz, seq, S5_GROUPS, S5_GROUP)
    bu_re = jnp.einsum('gpc,blgc->blgp', bb_re, ug)
    bu_im = jnp.einsum('gpc,blgc->blgp', bb_im, ug)
    a_re = jnp.broadcast_to(ab_re[None, None], (1, seq, S5_GROUPS, S5_STATE))
    a_im = jnp.broadcast_to(ab_im[None, None], (1, seq, S5_GROUPS, S5_STATE))

    def combine(e1, e2):
        a1r, a1i, b1r, b1i = e1
        a2r, a2i, b2r, b2i = e2
        return (a2r * a1r - a2i * a1i,
                a2r * a1i + a2i * a1r,
                a2r * b1r - a2i * b1i + b2r,
                a2r * b1i + a2i * b1r + b2i)

    _, _, h_re, h_im = lax.associative_scan(combine, (a_re, a_im, bu_re, bu_im), axis=1)
    y = (jnp.einsum('gcp,blgp->blgc', c_re.astype(f32), h_re)
         - jnp.einsum('gcp,blgp->blgc', c_im.astype(f32), h_im))
    y = y.reshape(bsz, seq, S5_WIDTH) + d_skip.astype(f32) * u.astype(f32)
    y = jax.nn.gelu(y)
    y = (y @ w_glu1.astype(f32)) * jax.nn.sigmoid(y @ w_glu2.astype(f32))
    return y.astype(u.dtype)


def _swa_attention(q, k, v, q_g, k_g, sinks, pos):
    bsz, seq, _ = q.shape
    f32 = jnp.float32
    rep = ATTN_HEADS // ATTN_KV_HEADS
    nb = seq // ATTN_BLOCK
    q = _rope(_rmsnorm(q.reshape(bsz, seq, ATTN_HEADS, HEAD_DIM), q_g), pos)
    k = _rope(_rmsnorm(k.reshape(bsz, seq, ATTN_KV_HEADS, HEAD_DIM), k_g), pos)
    v = v.reshape(bsz, seq, ATTN_KV_HEADS, HEAD_DIM)
    qb = q.reshape(bsz, nb, ATTN_BLOCK, ATTN_KV_HEADS, rep, HEAD_DIM).astype(f32)

    def band(t):
        tp = jnp.pad(t, ((0, 0), (ATTN_BLOCK, 0), (0, 0), (0, 0)))
        prev = tp[:, :seq].reshape(bsz, nb, ATTN_BLOCK, ATTN_KV_HEADS, HEAD_DIM)
        cur = t.reshape(bsz, nb, ATTN_BLOCK, ATTN_KV_HEADS, HEAD_DIM)
        return jnp.concatenate([prev, cur], axis=2).astype(f32)

    kb = band(k)
    vb = band(v)
    scores = jnp.einsum('bnqkrd,bnskd->bnkrqs', qb, kb) * (HEAD_DIM ** -0.5)
    qi = jnp.arange(ATTN_BLOCK)[:, None] + ATTN_BLOCK
    kj = jnp.arange(2 * ATTN_BLOCK)[None, :]
    band_ok = (kj <= qi) & (qi - kj < WINDOW)
    abs_k = (jnp.arange(nb) * ATTN_BLOCK)[:, None] - ATTN_BLOCK + kj
    mask = band_ok[None] & (abs_k >= 0)[:, None, :]
    scores = jnp.where(mask[None, :, None, None], scores, -jnp.inf)
    sink = sinks.astype(f32).reshape(ATTN_KV_HEADS, rep)[None, None, :, :, None, None]
    m = jnp.maximum(jnp.max(scores, axis=-1, keepdims=True), sink)
    p = jnp.exp(scores - m)
    denom = jnp.sum(p, axis=-1, keepdims=True) + jnp.exp(sink - m)
    out = jnp.einsum('bnkrqs,bnskd->bnqkrd', p / denom, vb)
    return out.reshape(bsz, seq, ATTN_Q).astype(q.dtype)


def _ssd_chunked(x, a, b, c):
    bsz, seq, nh, hp = x.shape
    ns = b.shape[-1]
    nc = seq // SSD_CHUNK
    x = x.reshape(bsz, nc, SSD_CHUNK, nh, hp)
    b = b.reshape(bsz, nc, SSD_CHUNK, nh, ns)
    c = c.reshape(bsz, nc, SSD_CHUNK, nh, ns)
    a = a.reshape(bsz, nc, SSD_CHUNK, nh).transpose(0, 3, 1, 2)
    a_cs = jnp.cumsum(a, axis=-1)
    causal = jnp.tril(jnp.ones((SSD_CHUNK, SSD_CHUNK), dtype=bool))
    seg = a_cs[..., :, None] - a_cs[..., None, :]
    lmat = jnp.exp(jnp.where(causal, seg, -jnp.inf))
    cb = jnp.einsum('bclhn,bcshn->bhcls', c, b) * lmat
    y_diag = jnp.einsum('bhcls,bcshp->bclhp', cb, x)
    decay_states = jnp.exp(a_cs[..., -1:] - a_cs)
    states = jnp.einsum('bclhn,bhcl,bclhp->bchpn', b, decay_states, x)
    chunk_decay = jnp.exp(a_cs[..., -1])

    def step(s, inp):
        st, dec = inp
        return s * dec[..., None, None] + st, s

    init = jnp.zeros((bsz, nh, hp, ns), dtype=x.dtype)
    _, prev_states = lax.scan(step, init, (jnp.moveaxis(states, 1, 0), jnp.moveaxis(chunk_decay, 2, 0)))
    prev_states = jnp.moveaxis(prev_states, 0, 1)
    y_off = jnp.einsum('bclhn,bchpn,bhcl->bclhp', c, prev_states, jnp.exp(a_cs))
    return (y_diag + y_off).reshape(bsz, seq, nh, hp)


def _ssd_mixer(z, xbc, dt_raw, conv_w, conv_b, dt_bias, a_log, d_skip, norm_g):
    bsz, seq, _ = xbc.shape
    f32 = jnp.float32
    conv = lax.conv_general_dilated(
        xbc, conv_w.astype(xbc.dtype)[:, None, :], window_strides=(1,),
        padding=[(SSD_CONV - 1, 0)], dimension_numbers=('NWC', 'WIO', 'NWC'),
        feature_group_count=SSD_CONV_CH)
    xbc = jax.nn.silu(conv.astype(f32) + conv_b.astype(f32))
    xs = xbc[..., :SSD_WIDTH].reshape(bsz, seq, SSD_HEADS, SSD_HEAD_DIM)
    bm = xbc[..., SSD_WIDTH:SSD_WIDTH + SSD_BC].reshape(bsz, seq, SSD_GROUPS, SSD_STATE)
    cm = xbc[..., SSD_WIDTH + SSD_BC:].reshape(bsz, seq, SSD_GROUPS, SSD_STATE)
    hpg = SSD_HEADS // SSD_GROUPS
    bm = jnp.repeat(bm, hpg, axis=2)
    cm = jnp.repeat(cm, hpg, axis=2)
    dt = jax.nn.softplus(dt_raw.astype(f32) + dt_bias.astype(f32))
    a_h = -jnp.exp(a_log.astype(f32))
    y = _ssd_chunked(xs * dt[..., None], dt * a_h, bm, cm)
    y = y + xs * d_skip.astype(f32)[:, None]
    y = y.reshape(bsz, seq, SSD_WIDTH) * jax.nn.silu(z.astype(f32))
    yg = y.reshape(bsz, seq, SSD_GROUPS, SSD_WIDTH // SSD_GROUPS)
    yg = yg * lax.rsqrt(jnp.mean(yg * yg, axis=-1, keepdims=True) + NORM_EPS)
    y = yg.reshape(bsz, seq, SSD_WIDTH) * norm_g.astype(f32)
    return y.astype(z.dtype)


def _hier_moe(h, w_rg, b_rg, w_re, b_re, w_gate, w_up, w_down):
    bsz, seq, dm = h.shape
    t = h.reshape(bsz * seq, dm)
    n_tok = t.shape[0]
    g_logits = (t @ w_rg).astype(jnp.float32) + b_rg.astype(jnp.float32)
    g_prob = jax.nn.softmax(g_logits, axis=-1)
    g_sel = jnp.argmax(g_logits, axis=-1)
    p_group = jnp.take_along_axis(g_prob, g_sel[:, None], axis=-1)
    e_logits = ((t @ w_re).astype(jnp.float32) + b_re.astype(jnp.float32)).reshape(n_tok, MOE_GROUPS, MOE_PER_GROUP)
    in_group = jnp.take_along_axis(e_logits, g_sel[:, None, None], axis=1)[:, 0]
    top_v, top_i = lax.top_k(in_group, MOE_TOPK)
    top_w = jax.nn.softmax(top_v, axis=-1) * p_group
    expert_id = g_sel[:, None] * MOE_PER_GROUP + top_i
    combine = jnp.einsum('tk,tke->te', top_w, jax.nn.one_hot(expert_id, MOE_EXPERTS, dtype=jnp.float32)).astype(t.dtype)
    out = jnp.zeros_like(t)
    for e in range(MOE_EXPERTS):
        hid = jax.nn.silu(t @ w_gate[e]) * (t @ w_up[e])
        out = out + combine[:, e:e + 1] * (hid @ w_down[e])
    return out.reshape(bsz, seq, dm)


def setup_inputs(seed: int = 0) -> dict:
    key = jax.random.key(seed)
    ks = iter(jax.random.split(key, 48))
    nrm = lambda shape, s: jax.random.normal(next(ks), shape, jnp.float32) * s
    gain = lambda shape: 1.0 + 0.02 * jax.random.normal(next(ks), shape, jnp.float32)
    L = DEPTH
    x = jax.random.normal(next(ks), (BATCH, SEQ, D_MODEL), jnp.float32)
    norm1_g = gain((L, D_MODEL))
    w_in = nrm((L, D_MODEL, N_IN), D_MODEL ** -0.5)
    b_gate = nrm((L, N_BRANCH, D_MODEL), 0.1)
    s5_lambda_re = -0.5 + nrm((L, S5_GROUPS, S5_STATE), 0.01)
    s5_lambda_im = jnp.pi * jnp.arange(S5_STATE, dtype=jnp.float32) + nrm((L, S5_GROUPS, S5_STATE), 0.01)
    s5_b_re = nrm((L, S5_GROUPS, S5_STATE, S5_GROUP), (2.0 * S5_GROUP) ** -0.5)
    s5_b_im = nrm((L, S5_GROUPS, S5_STATE, S5_GROUP), (2.0 * S5_GROUP) ** -0.5)
    s5_c_re = nrm((L, S5_GROUPS, S5_GROUP, S5_STATE), S5_STATE ** -0.5)
    s5_c_im = nrm((L, S5_GROUPS, S5_GROUP, S5_STATE), S5_STATE ** -0.5)
    s5_d = nrm((L, S5_WIDTH), 1.0)
    s5_log_dt = jax.random.uniform(next(ks), (L, S5_GROUPS), jnp.float32, math.log(1e-3), math.log(1e-1))
    s5_glu_w1 = nrm((L, S5_WIDTH, S5_WIDTH), S5_WIDTH ** -0.5)
    s5_glu_w2 = nrm((L, S5_WIDTH, S5_WIDTH), S5_WIDTH ** -0.5)
    q_norm_g = gain((L, HEAD_DIM))
    k_norm_g = gain((L, HEAD_DIM))
    attn_sinks = nrm((L, ATTN_HEADS), 0.5)
    ssd_conv_w = nrm((L, SSD_CONV, SSD_CONV_CH), SSD_CONV ** -0.5)
    ssd_conv_b = nrm((L, SSD_CONV_CH), 0.02)
    dt0 = jnp.exp(jax.random.uniform(next(ks), (L, SSD_HEADS), jnp.float32, math.log(1e-3), math.log(1e-1)))
    ssd_dt_bias = dt0 + jnp.log(-jnp.expm1(-dt0))
    ssd_a_log = jnp.log(jax.random.uniform(next(ks), (L, SSD_HEADS), jnp.float32, 1.0, 16.0))
    ssd_d = gain((L, SSD_HEADS))
    ssd_norm_g = gain((L, SSD_WIDTH))
    p_s5 = nrm((L, S5_WIDTH, D_MODEL), S5_WIDTH ** -0.5)
    p_attn = nrm((L, ATTN_Q, D_MODEL), ATTN_Q ** -0.5)
    p_ssd = nrm((L, SSD_WIDTH, D_MODEL), SSD_WIDTH ** -0.5)
    w_out = nrm((L, D_MODEL, D_MODEL), D_MODEL ** -0.5)
    norm2_g = gain((L, D_MODEL))
    w_router_group = nrm((L, D_MODEL, MOE_GROUPS), D_MODEL ** -0.5)
    b_router_group = nrm((L, MOE_GROUPS), 0.01)
    w_router_expert = nrm((L, D_MODEL, MOE_EXPERTS), D_MODEL ** -0.5)
    b_router_expert = nrm((L, MOE_EXPERTS), 0.01)
    w_exp_gate = nrm((L, MOE_EXPERTS, D_MODEL, MOE_FF), D_MODEL ** -0.5)
    w_exp_up = nrm((L, MOE_EXPERTS, D_MODEL, MOE_FF), D_MODEL ** -0.5)
    w_exp_down = nrm((L, MOE_EXPERTS, MOE_FF, D_MODEL), MOE_FF ** -0.5)
    return {'x': x, 'norm1_g': norm1_g, 'w_in': w_in, 'b_gate': b_gate,
            's5_lambda_re': s5_lambda_re, 's5_lambda_im': s5_lambda_im,
            's5_b_re': s5_b_re, 's5_b_im': s5_b_im, 's5_c_re': s5_c_re, 's5_c_im': s5_c_im,
            's5_d': s5_d, 's5_log_dt': s5_log_dt, 's5_glu_w1': s5_glu_w1, 's5_glu_w2': s5_glu_w2,
            'q_norm_g': q_norm_g, 'k_norm_g': k_norm_g, 'attn_sinks': attn_sinks,
            'ssd_conv_w': ssd_conv_w, 'ssd_conv_b': ssd_conv_b, 'ssd_dt_bias': ssd_dt_bias,
            'ssd_a_log': ssd_a_log, 'ssd_d': ssd_d, 'ssd_norm_g': ssd_norm_g,
            'p_s5': p_s5, 'p_attn': p_attn, 'p_ssd': p_ssd, 'w_out': w_out, 'norm2_g': norm2_g,
            'w_router_group': w_router_group, 'b_router_group': b_router_group,
            'w_router_expert': w_router_expert, 'b_router_expert': b_router_expert,
            'w_exp_gate': w_exp_gate, 'w_exp_up': w_exp_up, 'w_exp_down': w_exp_down}


def reference(x, norm1_g, w_in, b_gate, s5_lambda_re, s5_lambda_im, s5_b_re, s5_b_im,
              s5_c_re, s5_c_im, s5_d, s5_log_dt, s5_glu_w1, s5_glu_w2, q_norm_g, k_norm_g,
              attn_sinks, ssd_conv_w, ssd_conv_b, ssd_dt_bias, ssd_a_log, ssd_d, ssd_norm_g,
              p_s5, p_attn, p_ssd, w_out, norm2_g, w_router_group, b_router_group,
              w_router_expert, b_router_expert, w_exp_gate, w_exp_up, w_exp_down):
    bsz, seq, _ = x.shape
    pos = jnp.arange(seq, dtype=jnp.int32)
    for l in range(DEPTH):
        h = _rmsnorm(x, norm1_g[l])
        proj = h @ w_in[l]
        u_s5, q, k, v, z, xbc, dt_raw, gate_logits = _split_in(proj)
        y_a = _s5_mixer(u_s5, s5_lambda_re[l], s5_lambda_im[l], s5_b_re[l], s5_b_im[l],
                        s5_c_re[l], s5_c_im[l], s5_d[l], s5_log_dt[l], s5_glu_w1[l], s5_glu_w2[l])
        y_b = _swa_attention(q, k, v, q_norm_g[l], k_norm_g[l], attn_sinks[l], pos)
        y_c = _ssd_mixer(z, xbc, dt_raw, ssd_conv_w[l], ssd_conv_b[l], ssd_dt_bias[l],
                         ssd_a_log[l], ssd_d[l], ssd_norm_g[l])
        gates = jax.nn.sigmoid(gate_logits.reshape(bsz, seq, N_BRANCH, D_MODEL).astype(jnp.float32)
                               + b_gate[l].astype(jnp.float32)).astype(x.dtype)
        merged = (gates[:, :, 0] * (y_a @ p_s5[l])
                  + gates[:, :, 1] * (y_b @ p_attn[l])
                  + gates[:, :, 2] * (y_c @ p_ssd[l]))
        x = x + merged @ w_out[l]
        x = x + _hier_moe(_rmsnorm(x, norm2_g[l]), w_router_group[l], b_router_group[l],
                          w_router_expert[l], b_router_expert[l], w_exp_gate[l], w_exp_up[l], w_exp_down[l])
    return x
```

```python
import functools
import math

import jax
import jax.numpy as jnp
from jax import lax
from jax.experimental import pallas as pl
from jax.experimental.pallas import tpu as pltpu

F32 = jnp.float32
BF16 = jnp.bfloat16

D_MODEL = 1024
NORM_EPS = 1e-6
S5_WIDTH = 512
S5_GROUP = 16
S5_GROUPS = 32
S5_STATE = 64
HEAD_DIM = 64
ATTN_HEADS = 8
ATTN_KV_HEADS = 2
ATTN_REP = ATTN_HEADS // ATTN_KV_HEADS
ATTN_Q = ATTN_HEADS * HEAD_DIM
ATTN_KV = ATTN_KV_HEADS * HEAD_DIM
ATTN_BLOCK = 128
ROPE_THETA = 10000.0
SSD_WIDTH = 1024
SSD_HEAD_DIM = 64
SSD_HEADS = 16
SSD_GROUPS = 2
SSD_STATE = 64
SSD_CONV = 4
SSD_CHUNK = 128
SSD_BC = SSD_GROUPS * SSD_STATE
SSD_CONV_CH = SSD_WIDTH + 2 * SSD_BC
SSD_GROUP_W = SSD_WIDTH // SSD_GROUPS
N_BRANCH = 3
MOE_GROUPS = 4
MOE_PER_GROUP = 8
MOE_EXPERTS = 32
MOE_FF = 512
N_MIX = S5_WIDTH + ATTN_Q + 2 * ATTN_KV + SSD_WIDTH + SSD_CONV_CH + SSD_HEADS

LANES = 128
SUBLANES = 8
VMEM_LIMIT_BYTES = 56 * 1024 * 1024

S5_CHUNK = 16
S5_FLAT = S5_CHUNK * S5_GROUP
S5_PAIR = 2
TM_PROJ = 512
TM_MOE = 1024
DT_PAD = LANES
N_MIX_PAD = N_MIX - SSD_HEADS + DT_PAD
ROUTE_PAD = LANES


def _cparams(semantics):
    return pltpu.CompilerParams(dimension_semantics=semantics, vmem_limit_bytes=VMEM_LIMIT_BYTES)


def _const_spec(shape):
    zeros = (0,) * len(shape)
    return pl.BlockSpec(shape, lambda *_: zeros, pipeline_mode=pl.Buffered(1))


def _dot(a, b):
    return jnp.dot(a, b, preferred_element_type=F32)


def _split3(a):
    hi = a.astype(BF16)
    r1 = a - hi.astype(F32)
    mid = r1.astype(BF16)
    lo = (r1 - mid.astype(F32)).astype(BF16)
    return hi, mid, lo


def _dot_f32_lhs(a, b_bf16):
    hi, mid, lo = _split3(a)
    return _dot(hi, b_bf16) + _dot(mid, b_bf16) + _dot(lo, b_bf16)


def _rms_scale(x):
    return x * lax.rsqrt(jnp.mean(x * x, axis=-1, keepdims=True) + NORM_EPS)


def _silu(x):
    return x * jax.nn.sigmoid(x)


_OFF_U = 0
_OFF_Q = _OFF_U + S5_WIDTH
_OFF_K = _OFF_Q + ATTN_Q
_OFF_V = _OFF_K + ATTN_KV
_OFF_Z = _OFF_V + ATTN_KV
_OFF_XBC = _OFF_Z + SSD_WIDTH
_OFF_DT = _OFF_XBC + SSD_CONV_CH


def _inproj_kernel(x_ref, g_ref, w_ref, u_o, q_o, k_o, v_o, z_o, xbc_o, dt_o):
    hb = (_rms_scale(x_ref[...]) * g_ref[...]).astype(BF16)
    for o_ref, off in ((u_o, _OFF_U), (q_o, _OFF_Q), (k_o, _OFF_K), (v_o, _OFF_V),
                       (z_o, _OFF_Z), (xbc_o, _OFF_XBC), (dt_o, _OFF_DT)):
        width = o_ref.shape[1]
        o_ref[...] = _dot(hb, w_ref[:, off:off + width]).astype(o_ref.dtype)


def _inproj(x, g, w_mix):
    t = x.shape[0]
    tm = min(TM_PROJ, t)
    widths = (S5_WIDTH, ATTN_Q, ATTN_KV, ATTN_KV, SSD_WIDTH, SSD_CONV_CH, DT_PAD)
    dtypes = (BF16, BF16, BF16, BF16, BF16, BF16, F32)
    return pl.pallas_call(
        _inproj_kernel,
        grid=(t // tm,),
        in_specs=[pl.BlockSpec((tm, D_MODEL), lambda i: (i, 0)),
                  _const_spec((1, D_MODEL)),
                  _const_spec((D_MODEL, N_MIX_PAD))],
        out_specs=[pl.BlockSpec((tm, w), lambda i: (i, 0)) for w in widths],
        out_shape=[jax.ShapeDtypeStruct((t, w), d) for w, d in zip(widths, dtypes)],
        compiler_params=_cparams(("parallel",)),
        name="inproj",
    )(x, g, w_mix)


def _s5_tables(lam_re, lam_im, b_re, b_im, c_re, c_im, log_dt):
    hp = lax.Precision.HIGHEST
    g_n, p_n = lam_re.shape
    c_n = b_re.shape[-1]
    t_n = S5_CHUNK
    lr, li = lam_re.astype(F32), lam_im.astype(F32)
    dt = jnp.exp(log_dt.astype(F32))[:, None]
    mag = jnp.exp(lr * dt)
    ab_re = mag * jnp.cos(li * dt)
    ab_im = mag * jnp.sin(li * dt)
    nr = ab_re - 1.0
    den = lr * lr + li * li
    f_re = (nr * lr + ab_im * li) / den
    f_im = (ab_im * lr - nr * li) / den
    br, bi = b_re.astype(F32), b_im.astype(F32)
    bb_re = f_re[..., None] * br - f_im[..., None] * bi
    bb_im = f_re[..., None] * bi + f_im[..., None] * br
    j = jnp.arange(t_n + 1, dtype=F32)[:, None, None]
    pmag = jnp.exp(lr * dt * j)
    ang = li * dt * j
    p_re = pmag * jnp.cos(ang)
    p_im = pmag * jnp.sin(ang)
    cr, ci = c_re.astype(F32), c_im.astype(F32)
    ca_re = cr[None] * p_re[:, :, None, :] - ci[None] * p_im[:, :, None, :]
    ca_im = cr[None] * p_im[:, :, None, :] + ci[None] * p_re[:, :, None, :]
    kern = (jnp.einsum('jgcp,gpd->jgcd', ca_re[:t_n], bb_re, precision=hp)
            - jnp.einsum('jgcp,gpd->jgcd', ca_im[:t_n], bb_im, precision=hp))
    tt = jnp.arange(t_n)
    lag = tt[None, :] - tt[:, None]
    kst = kern[jnp.clip(lag, 0, t_n - 1)]
    kst = jnp.where((lag >= 0)[:, :, None, None, None], kst, 0.0)
    m_t = kst.transpose(2, 0, 4, 1, 3).reshape(g_n, t_n * c_n, t_n * c_n)
    q_re = p_re[t_n - 1 - tt]
    q_im = p_im[t_n - 1 - tt]
    bs_re = q_re[..., None] * bb_re[None] - q_im[..., None] * bb_im[None]
    bs_im = q_re[..., None] * bb_im[None] + q_im[..., None] * bb_re[None]
    bs_re = bs_re.transpose(1, 0, 3, 2).reshape(g_n, t_n * c_n, p_n)
    bs_im = bs_im.transpose(1, 0, 3, 2).reshape(g_n, t_n * c_n, p_n)
    co_re = ca_re[1:].transpose(1, 3, 0, 2).reshape(g_n, p_n, t_n * c_n)
    co_im = -ca_im[1:].transpose(1, 3, 0, 2).reshape(g_n, p_n, t_n * c_n)

    def pair_diag(a):
        gp = g_n // S5_PAIR
        a = a.reshape(gp, S5_PAIR, a.shape[1], a.shape[2])
        z = jnp.zeros_like(a[:, 0])
        top = jnp.concatenate([a[:, 0], z], axis=2)
        bot = jnp.concatenate([z, a[:, 1]], axis=2)
        return jnp.concatenate([top, bot], axis=1)

    return dict(
        m_t=m_t.astype(BF16),
        bs_re=pair_diag(bs_re).astype(BF16), bs_im=pair_diag(bs_im).astype(BF16),
        co_re=pair_diag(co_re).astype(BF16), co_im=pair_diag(co_im).astype(BF16),
        at_re=p_re[t_n].reshape(1, g_n * p_n), at_im=p_im[t_n].reshape(1, g_n * p_n))


_S5_PW = S5_PAIR * S5_FLAT
_S5_PS = S5_PAIR * S5_STATE


def _s5_state_kernel(u_ref, bre_ref, bim_ref, sre_o, sim_o):
    u = u_ref[...]
    sre_o[...] = _dot(u, bre_ref[0])
    sim_o[...] = _dot(u, bim_ref[0])


def _s5_scan_kernel(sre_ref, sim_ref, are_ref, aim_ref, hre_o, him_o, *, bsz, nck):
    a_re = are_ref[...]
    a_im = aim_ref[...]
    w = sre_ref.shape[1]

    def step(k, carry):
        h_re, h_im = carry
        rows = pl.ds(k, bsz, stride=nck)
        hre_o[rows, :] = h_re
        him_o[rows, :] = h_im
        s_re = sre_ref[rows, :]
        s_im = sim_ref[rows, :]
        return (a_re * h_re - a_im * h_im + s_re, a_re * h_im + a_im * h_re + s_im)

    zero = jnp.zeros((bsz, w), F32)
    lax.fori_loop(0, nck, step, (zero, zero))


def _s5_out_kernel(u_ref, hre_ref, him_ref, mt_ref, cre_ref, cim_ref, y_o):
    u = u_ref[...]
    y = _dot(hre_ref[...].astype(BF16), cre_ref[0]) + _dot(him_ref[...].astype(BF16), cim_ref[0])
    intra = [_dot(u[:, g * S5_FLAT:(g + 1) * S5_FLAT], mt_ref[g]) for g in range(S5_PAIR)]
    y_o[...] = (y + jnp.concatenate(intra, axis=1)).astype(y_o.dtype)


def _s5_mixer(u, tabs, bsz):
    t = u.shape[0]
    seq = t // bsz
    nck = seq // S5_CHUNK
    r = t // S5_CHUNK
    gp = S5_GROUPS // S5_PAIR
    u2 = u.reshape(r, S5_CHUNK, S5_GROUPS, S5_GROUP).transpose(0, 2, 1, 3).reshape(r, S5_GROUPS * S5_FLAT)
    sw = S5_GROUPS * S5_STATE
    s_re, s_im = pl.pallas_call(
        _s5_state_kernel,
        grid=(gp,),
        in_specs=[pl.BlockSpec((r, _S5_PW), lambda j: (0, j)),
                  pl.BlockSpec((1, _S5_PW, _S5_PS), lambda j: (j, 0, 0)),
                  pl.BlockSpec((1, _S5_PW, _S5_PS), lambda j: (j, 0, 0))],
        out_specs=[pl.BlockSpec((r, _S5_PS), lambda j: (0, j))] * 2,
        out_shape=[jax.ShapeDtypeStruct((r, sw), F32)] * 2,
        compiler_params=_cparams(("parallel",)),
        name="s5_state",
    )(u2, tabs['bs_re'], tabs['bs_im'])
    scan_w = LANES
    h_re, h_im = pl.pallas_call(
        functools.partial(_s5_scan_kernel, bsz=bsz, nck=nck),
        grid=(sw // scan_w,),
        in_specs=[pl.BlockSpec((r, scan_w), lambda j: (0, j))] * 2
                 + [pl.BlockSpec((1, scan_w), lambda j: (0, j))] * 2,
        out_specs=[pl.BlockSpec((r, scan_w), lambda j: (0, j))] * 2,
        out_shape=[jax.ShapeDtypeStruct((r, sw), F32)] * 2,
        compiler_params=_cparams(("parallel",)),
        name="s5_scan",
    )(s_re, s_im, tabs['at_re'], tabs['at_im'])
    y2 = pl.pallas_call(
        _s5_out_kernel,
        grid=(gp,),
        in_specs=[pl.BlockSpec((r, _S5_PW), lambda j: (0, j)),
                  pl.BlockSpec((r, _S5_PS), lambda j: (0, j)),
                  pl.BlockSpec((r, _S5_PS), lambda j: (0, j)),
                  pl.BlockSpec((S5_PAIR, S5_FLAT, S5_FLAT), lambda j: (j, 0, 0)),
                  pl.BlockSpec((1, _S5_PS, _S5_PW), lambda j: (j, 0, 0)),
                  pl.BlockSpec((1, _S5_PS, _S5_PW), lambda j: (j, 0, 0))],
        out_specs=pl.BlockSpec((r, _S5_PW), lambda j: (0, j)),
        out_shape=jax.ShapeDtypeStruct((r, S5_GROUPS * S5_FLAT), BF16),
        compiler_params=_cparams(("parallel",)),
        name="s5_out",
    )(u2, h_re, h_im, tabs['m_t'], tabs['co_re'], tabs['co_im'])
    return y2.reshape(r, S5_GROUPS, S5_CHUNK, S5_GROUP).transpose(0, 2, 1, 3).reshape(t, S5_WIDTH)


def _swap_rope_halves(y):
    w = y.shape[1]
    lane = lax.broadcasted_iota(jnp.int32, y.shape, 1)
    lower = (lane % HEAD_DIM) < (HEAD_DIM // 2)
    return jnp.where(lower, pltpu.roll(y, w - HEAD_DIM // 2, 1), pltpu.roll(y, HEAD_DIM // 2, 1))


def _norm_rope(x, gain, head_mean, cos2, sin2):
    reps = x.shape[1] // LANES
    ms = _dot_f32_lhs(x * x, head_mean)
    y = x * lax.rsqrt(ms + NORM_EPS) * gain
    if reps > 1:
        cos2 = jnp.concatenate([cos2] * reps, axis=1)
        sin2 = jnp.concatenate([sin2] * reps, axis=1)
    return y * cos2 + _swap_rope_halves(y) * sin2


def _attn_kernel(q_ref, kc_ref, kp_ref, vc_ref, vp_ref, cosc_ref, sinc_ref, cosp_ref, sinp_ref,
                 qg_ref, kg_ref, sink_ref, hmq_ref, hmk_ref, o_ref):
    n = pl.program_id(1)
    blk = ATTN_BLOCK
    q = _norm_rope(q_ref[...].astype(F32), qg_ref[...], hmq_ref[...], cosc_ref[...], sinc_ref[...])
    q = (q * (HEAD_DIM ** -0.5)).astype(BF16)
    kc = _norm_rope(kc_ref[...].astype(F32), kg_ref[...], hmk_ref[...], cosc_ref[...], sinc_ref[...]).astype(BF16)
    kp = _norm_rope(kp_ref[...].astype(F32), kg_ref[...], hmk_ref[...], cosp_ref[...], sinp_ref[...]).astype(BF16)
    vc = vc_ref[...]
    vp = vp_ref[...]
    qi = lax.broadcasted_iota(jnp.int32, (blk, 2 * blk), 0) + blk
    kj = lax.broadcasted_iota(jnp.int32, (blk, 2 * blk), 1)
    ok = (kj <= qi) & (qi - kj < blk) & ((kj >= blk) | (n > 0))
    ok = jnp.concatenate([ok] * ATTN_REP, axis=0)
    sinks = sink_ref[...]
    for j in range(ATTN_KV_HEADS):
        sl = slice(j * HEAD_DIM, (j + 1) * HEAD_DIM)
        kb = jnp.concatenate([kp[:, sl], kc[:, sl]], axis=0)
        vb = jnp.concatenate([vp[:, sl], vc[:, sl]], axis=0)
        heads = [j * ATTN_REP + r for r in range(ATTN_REP)]
        q4 = jnp.concatenate([q[:, h * HEAD_DIM:(h + 1) * HEAD_DIM] for h in heads], axis=0)
        s = lax.dot_general(q4, kb, (((1,), (1,)), ((), ())), preferred_element_type=F32)
        s = jnp.where(ok, s, -jnp.inf)
        sink = jnp.concatenate([jnp.broadcast_to(sinks[:, h:h + 1], (blk, 1)) for h in heads], axis=0)
        m = jnp.maximum(jnp.max(s, axis=-1, keepdims=True), sink)
        p = jnp.exp(s - m)
        denom = jnp.sum(p, axis=-1, keepdims=True) + jnp.exp(sink - m)
        out = _dot((p / denom).astype(BF16), vb)
        for r, h in enumerate(heads):
            o_ref[:, h * HEAD_DIM:(h + 1) * HEAD_DIM] = out[r * blk:(r + 1) * blk].astype(o_ref.dtype)


def _rope_tables(seq):
    half = HEAD_DIM // 2
    inv = jnp.power(ROPE_THETA, -jnp.arange(half, dtype=F32) * 2.0 / HEAD_DIM)
    ang = jnp.arange(seq, dtype=F32)[:, None] * inv[None, :]
    cos, sin = jnp.cos(ang), jnp.sin(ang)
    cos2 = jnp.concatenate([cos, cos, cos, cos], axis=1)
    sin2 = jnp.concatenate([-sin, sin, -sin, sin], axis=1)
    return cos2, sin2


def _head_mean_matrix(width):
    i = jnp.arange(width)
    return jnp.where((i[:, None] // HEAD_DIM) == (i[None, :] // HEAD_DIM), 1.0 / HEAD_DIM, 0.0).astype(BF16)


def _swa_attention(q, k, v, q_g, k_g, sinks, bsz):
    t = q.shape[0]
    seq = t // bsz
    nb = seq // ATTN_BLOCK
    cos2, sin2 = _rope_tables(seq)
    qg = jnp.tile(q_g.astype(F32), ATTN_HEADS)[None]
    kg = jnp.tile(k_g.astype(F32), ATTN_KV_HEADS)[None]
    cur = lambda b, n: (b * nb + n, 0)
    prev = lambda b, n: (b * nb + jnp.maximum(n - 1, 0), 0)
    tcur = lambda b, n: (n, 0)
    tprev = lambda b, n: (jnp.maximum(n - 1, 0), 0)
    blk = ATTN_BLOCK
    return pl.pallas_call(
        _attn_kernel,
        grid=(bsz, nb),
        in_specs=[pl.BlockSpec((blk, ATTN_Q), cur),
                  pl.BlockSpec((blk, ATTN_KV), cur), pl.BlockSpec((blk, ATTN_KV), prev),
                  pl.BlockSpec((blk, ATTN_KV), cur), pl.BlockSpec((blk, ATTN_KV), prev),
                  pl.BlockSpec((blk, LANES), tcur), pl.BlockSpec((blk, LANES), tcur),
                  pl.BlockSpec((blk, LANES), tprev), pl.BlockSpec((blk, LANES), tprev),
                  _const_spec((1, ATTN_Q)), _const_spec((1, ATTN_KV)), _const_spec((1, ATTN_HEADS)),
                  _const_spec((ATTN_Q, ATTN_Q)), _const_spec((ATTN_KV, ATTN_KV))],
        out_specs=pl.BlockSpec((blk, ATTN_Q), cur),
        out_shape=jax.ShapeDtypeStruct((t, ATTN_Q), BF16),
        compiler_params=_cparams(("parallel", "parallel")),
        name="swa_attention",
    )(q, k, k, v, v, cos2, sin2, cos2, sin2, qg, kg, sinks.astype(F32)[None],
      _head_mean_matrix(ATTN_Q), _head_mean_matrix(ATTN_KV))


_CONV_TAIL = SUBLANES


def _ssd_kernel(z_ref, xbc_ref, dt_ref, cw_ref, cb_ref, dtb_ref, alog_ref, dskip_ref, ng_ref, exp_ref,
                o_ref, xext, state, ybuf):
    c = pl.program_id(1)
    ch = SSD_CHUNK

    @pl.when(c == 0)
    def _():
        xext[0:_CONV_TAIL, :] = jnp.zeros((_CONV_TAIL, SSD_CONV_CH), F32)
        state[...] = jnp.zeros_like(state)

    xext[_CONV_TAIL:_CONV_TAIL + ch, :] = xbc_ref[...].astype(F32)
    conv = cb_ref[...]
    for j in range(SSD_CONV):
        conv = conv + cw_ref[j:j + 1, :] * xext[pl.ds(_CONV_TAIL - (SSD_CONV - 1) + j, ch), :]
    xext[0:_CONV_TAIL, :] = xext[ch:ch + _CONV_TAIL, :]
    act = _silu(conv)
    xs = act[:, :SSD_WIDTH]
    bm = act[:, SSD_WIDTH:SSD_WIDTH + SSD_BC].astype(BF16)
    cm = act[:, SSD_WIDTH + SSD_BC:].astype(BF16)

    lane = lax.broadcasted_iota(jnp.int32, (ch, LANES), 1)
    xdt = dt_ref[...] + dtb_ref[...]
    dt = jnp.maximum(xdt, 0.0) + jnp.log1p(jnp.exp(-jnp.abs(xdt)))
    dt = jnp.where(lane < SSD_HEADS, dt, 0.0)
    a = dt * (-jnp.exp(alog_ref[...]))
    row = lax.broadcasted_iota(jnp.int32, (ch, ch), 0)
    col = lax.broadcasted_iota(jnp.int32, (ch, ch), 1)
    causal = row >= col
    hi, mid, lo = _split3(a)
    tril = causal.astype(BF16)
    cs = _dot(tril, hi) + _dot(tril, mid) + _dot(tril, lo)
    cs_t = cs.T
    expand = exp_ref[...]
    dt_x = _dot_f32_lhs(dt, expand)
    cs_x = _dot_f32_lhs(cs, expand)
    cs_last_x = cs_x[ch - 1:ch, :]
    xdt_full = xs * dt_x
    in_decay = jnp.exp(cs_x)
    out_decay = jnp.exp(cs_last_x - cs_x)
    chunk_decay = jnp.exp(cs_last_x)
    xw = (xdt_full * out_decay).astype(BF16)
    xdt_b = xdt_full.astype(BF16)
    hpg = SSD_HEADS // SSD_GROUPS
    for g in range(SSD_GROUPS):
        gs = slice(g * SSD_STATE, (g + 1) * SSD_STATE)
        ws = slice(g * SSD_GROUP_W, (g + 1) * SSD_GROUP_W)
        bg = bm[:, gs]
        cg = cm[:, gs]
        cb = lax.dot_general(cg, bg, (((1,), (1,)), ((), ())), preferred_element_type=F32)
        prev = state[g]
        ybuf[:, ws] = _dot(cg, prev.astype(BF16)) * in_decay[:, ws]
        for h in range(hpg):
            hh = g * hpg + h
            seg = cs[:, hh:hh + 1] - cs_t[hh:hh + 1, :]
            lmat = jnp.where(causal, jnp.exp(seg), 0.0)
            hs = slice(hh * SSD_HEAD_DIM, (hh + 1) * SSD_HEAD_DIM)
            ybuf[:, hs] += _dot((cb * lmat).astype(BF16), xdt_b[:, hs])
        upd = lax.dot_general(bg, xw[:, ws], (((0,), (0,)), ((), ())), preferred_element_type=F32)
        state[g] = prev * chunk_decay[:, ws] + upd

    y = ybuf[...] + xs * dskip_ref[...]
    y = y * _silu(z_ref[...].astype(F32))
    parts = []
    for g in range(SSD_GROUPS):
        ws = slice(g * SSD_GROUP_W, (g + 1) * SSD_GROUP_W)
        parts.append(_rms_scale(y[:, ws]))
    o_ref[...] = (jnp.concatenate(parts, axis=1) * ng_ref[...]).astype(o_ref.dtype)


def _ssd_mixer(z, xbc, dt_raw, conv_w, conv_b, dt_bias, a_log, d_skip, norm_g, bsz):
    t = z.shape[0]
    seq = t // bsz
    nc = seq // SSD_CHUNK
    pad = lambda v: jnp.pad(v.astype(F32), (0, LANES - SSD_HEADS))[None]
    d_x = jnp.repeat(d_skip.astype(F32), SSD_HEAD_DIM)[None]
    hid = jnp.arange(SSD_WIDTH) // SSD_HEAD_DIM
    expand = (jnp.arange(LANES)[:, None] == hid[None, :]).astype(BF16)
    blk = lambda b, c: (b * nc + c, 0)
    return pl.pallas_call(
        _ssd_kernel,
        grid=(bsz, nc),
        in_specs=[pl.BlockSpec((SSD_CHUNK, SSD_WIDTH), blk),
                  pl.BlockSpec((SSD_CHUNK, SSD_CONV_CH), blk),
                  pl.BlockSpec((SSD_CHUNK, DT_PAD), blk),
                  _const_spec((SSD_CONV, SSD_CONV_CH)), _const_spec((1, SSD_CONV_CH)),
                  _const_spec((1, LANES)), _const_spec((1, LANES)),
                  _const_spec((1, SSD_WIDTH)), _const_spec((1, SSD_WIDTH)),
                  _const_spec((LANES, SSD_WIDTH))],
        out_specs=pl.BlockSpec((SSD_CHUNK, SSD_WIDTH), blk),
        out_shape=jax.ShapeDtypeStruct((t, SSD_WIDTH), BF16),
        scratch_shapes=[pltpu.VMEM((SSD_CHUNK + 2 * _CONV_TAIL, SSD_CONV_CH), F32),
                        pltpu.VMEM((SSD_GROUPS, SSD_STATE, SSD_GROUP_W), F32),
                        pltpu.VMEM((SSD_CHUNK, SSD_WIDTH), F32)],
        compiler_params=_cparams(("arbitrary", "arbitrary")),
        name="ssd_mixer",
    )(z, xbc, dt_raw, conv_w.astype(F32), conv_b.astype(F32)[None], pad(dt_bias), pad(a_log),
      d_x, norm_g.astype(F32)[None], expand)


def _route(logits):
    lane = lax.broadcasted_iota(jnp.int32, logits.shape, 1).astype(F32)
    big = float(ROUTE_PAD)
    is_g = lane < MOE_GROUPS
    gl = jnp.where(is_g, logits, -jnp.inf)
    gmax = jnp.max(gl, axis=1, keepdims=True)
    gsel = jnp.min(jnp.where(gl == gmax, lane, big), axis=1, keepdims=True)
    pg = 1.0 / jnp.sum(jnp.where(is_g, jnp.exp(logits - gmax), 0.0), axis=1, keepdims=True)
    lo = MOE_GROUPS + MOE_PER_GROUP * gsel
    ev = jnp.where((lane >= lo) & (lane < lo + MOE_PER_GROUP), logits, -jnp.inf)
    m1 = jnp.max(ev, axis=1, keepdims=True)
    i1 = jnp.min(jnp.where(ev == m1, lane, big), axis=1, keepdims=True)
    ev2 = jnp.where(lane == i1, -jnp.inf, ev)
    m2 = jnp.max(ev2, axis=1, keepdims=True)
    i2 = jnp.min(jnp.where(ev2 == m2, lane, big), axis=1, keepdims=True)
    e21 = jnp.exp(m2 - m1)
    w1 = pg / (1.0 + e21)
    w2 = pg * e21 / (1.0 + e21)
    out = jnp.where(lane == 0, i1 - MOE_GROUPS, 0.0)
    out = jnp.where(lane == 1, i2 - MOE_GROUPS, out)
    out = jnp.where(lane == 2, w1, out)
    return jnp.where(lane == 3, w2, out)


def _merge_kernel(x_ref, ys5_ref, u_ref, yb_ref, yc_ref, n1g_ref, wg_ref, bg_ref, d_ref, w1_ref, w2_ref,
                  ps5_ref, pat_ref, pssd_ref, wo_ref, n2g_ref, wr_ref, br_ref, x1_o, h2_o, route_o):
    x = x_ref[...]
    hb = (_rms_scale(x) * n1g_ref[...]).astype(BF16)
    ya = ys5_ref[...].astype(F32) + d_ref[...] * u_ref[...].astype(F32)
    yab = jax.nn.gelu(ya).astype(BF16)
    ya = _dot(yab, w1_ref[...]) * jax.nn.sigmoid(_dot(yab, w2_ref[...]))
    branches = ((ya.astype(BF16), ps5_ref), (yb_ref[...], pat_ref), (yc_ref[...], pssd_ref))
    merged = None
    for b, (yv, p_ref) in enumerate(branches):
        gate = jax.nn.sigmoid(_dot(hb, wg_ref[:, b * D_MODEL:(b + 1) * D_MODEL]) + bg_ref[b:b + 1, :])
        term = gate * _dot(yv, p_ref[...])
        merged = term if merged is None else merged + term
    x1 = x + _dot(merged.astype(BF16), wo_ref[...])
    x1_o[...] = x1
    h2 = _rms_scale(x1) * n2g_ref[...]
    h2_o[...] = h2.astype(h2_o.dtype)
    logits = jnp.dot(h2, wr_ref[...], precision=lax.Precision.HIGHEST, preferred_element_type=F32) + br_ref[...]
    route_o[...] = _route(logits)


def _merge(x, ys5, u, yb, yc, lw):
    t = x.shape[0]
    tm = min(TM_PROJ, t)
    row = lambda w: pl.BlockSpec((tm, w), lambda i: (i, 0))
    consts = [lw['n1g'], lw['w_gate'], lw['b_gate'], lw['s5_d'], lw['glu_w1'], lw['glu_w2'], lw['p_s5'],
              lw['p_attn'], lw['p_ssd'], lw['w_out'], lw['n2g'], lw['w_router'], lw['b_router']]
    return pl.pallas_call(
        _merge_kernel,
        grid=(t // tm,),
        in_specs=[row(D_MODEL), row(S5_WIDTH), row(S5_WIDTH), row(ATTN_Q), row(SSD_WIDTH)]
                 + [_const_spec(c.shape) for c in consts],
        out_specs=[row(D_MODEL), row(D_MODEL), row(ROUTE_PAD)],
        out_shape=[jax.ShapeDtypeStruct((t, D_MODEL), F32), jax.ShapeDtypeStruct((t, D_MODEL), BF16),
                   jax.ShapeDtypeStruct((t, ROUTE_PAD), F32)],
        compiler_params=_cparams(("parallel",)),
        name="merge_router",
    )(x, ys5, u, yb, yc, *consts)


def _moe_dense_kernel(h2_ref, route_ref, x1_ref, wg_ref, wu_ref, wd_ref, o_ref):
    e = pl.program_id(1)

    @pl.when(e == 0)
    def _():
        o_ref[...] = x1_ref[...]

    hb = h2_ref[...]
    hid = _silu(_dot(hb, wg_ref[0].astype(BF16))) * _dot(hb, wu_ref[0].astype(BF16))
    r = route_ref[...]
    ef = e.astype(F32)
    cw = jnp.where(r[:, 0:1] == ef, r[:, 2:3], 0.0) + jnp.where(r[:, 1:2] == ef, r[:, 3:4], 0.0)
    o_ref[...] += cw * _dot(hid.astype(BF16), wd_ref[0].astype(BF16))


def _moe_dense(h2, route, x1, w_gate, w_up, w_down):
    t = h2.shape[0]
    tm = min(TM_MOE, t)
    row = lambda w: pl.BlockSpec((tm, w), lambda i, e: (i, 0))
    return pl.pallas_call(
        _moe_dense_kernel,
        grid=(t // tm, MOE_EXPERTS),
        in_specs=[row(D_MODEL), row(ROUTE_PAD), row(D_MODEL),
                  pl.BlockSpec((1, D_MODEL, MOE_FF), lambda i, e: (e, 0, 0)),
                  pl.BlockSpec((1, D_MODEL, MOE_FF), lambda i, e: (e, 0, 0)),
                  pl.BlockSpec((1, MOE_FF, D_MODEL), lambda i, e: (e, 0, 0))],
        out_specs=row(D_MODEL),
        out_shape=jax.ShapeDtypeStruct((t, D_MODEL), F32),
        compiler_params=_cparams(("parallel", "arbitrary")),
        name="moe_dense",
    )(h2, route, x1, w_gate, w_up, w_down)


def _layer_weights(l, p):
    w_in = p['w_in'][l]
    w_mix = jnp.pad(w_in[:, :N_MIX], ((0, 0), (0, N_MIX_PAD - N_MIX))).astype(BF16)
    w_router = jnp.concatenate([p['w_router_group'][l], p['w_router_expert'][l]], axis=1).astype(F32)
    npad = ROUTE_PAD - w_router.shape[1]
    b_router = jnp.concatenate([p['b_router_group'][l], p['b_router_expert'][l]]).astype(F32)
    return dict(
        n1g=p['norm1_g'][l].astype(F32)[None], w_mix=w_mix,
        w_gate=w_in[:, N_MIX:].astype(BF16), b_gate=p['b_gate'][l].astype(F32),
        s5_d=p['s5_d'][l].astype(F32)[None],
        glu_w1=p['s5_glu_w1'][l].astype(BF16), glu_w2=p['s5_glu_w2'][l].astype(BF16),
        p_s5=p['p_s5'][l].astype(BF16), p_attn=p['p_attn'][l].astype(BF16), p_ssd=p['p_ssd'][l].astype(BF16),
        w_out=p['w_out'][l].astype(BF16), n2g=p['norm2_g'][l].astype(F32)[None],
        w_router=jnp.pad(w_router, ((0, 0), (0, npad))), b_router=jnp.pad(b_router, (0, npad))[None])


def _layer(x, l, p, bsz):
    lw = _layer_weights(l, p)
    u, q, k, v, z, xbc, dt_raw = _inproj(x, lw['n1g'], lw['w_mix'])
    tabs = _s5_tables(p['s5_lambda_re'][l], p['s5_lambda_im'][l], p['s5_b_re'][l], p['s5_b_im'][l],
                      p['s5_c_re'][l], p['s5_c_im'][l], p['s5_log_dt'][l])
    ys5 = _s5_mixer(u, tabs, bsz)
    yb = _swa_attention(q, k, v, p['q_norm_g'][l], p['k_norm_g'][l], p['attn_sinks'][l], bsz)
    yc = _ssd_mixer(z, xbc, dt_raw, p['ssd_conv_w'][l], p['ssd_conv_b'][l], p['ssd_dt_bias'][l],
                    p['ssd_a_log'][l], p['ssd_d'][l], p['ssd_norm_g'][l], bsz)
    x1, h2, route = _merge(x, ys5, u, yb, yc, lw)
    return _moe_dense(h2, route, x1, p['w_exp_gate'][l], p['w_exp_up'][l], p['w_exp_down'][l])


def kernel(x, norm1_g, w_in, b_gate, s5_lambda_re, s5_lambda_im, s5_b_re, s5_b_im, s5_c_re, s5_c_im, s5_d, s5_log_dt, s5_glu_w1, s5_glu_w2, q_norm_g, k_norm_g, attn_sinks, ssd_conv_w, ssd_conv_b, ssd_dt_bias, ssd_a_log, ssd_d, ssd_norm_g, p_s5, p_attn, p_ssd, w_out, norm2_g, w_router_group, b_router_group, w_router_expert, b_router_expert, w_exp_gate, w_exp_up, w_exp_down):
    p = dict(norm1_g=norm1_g, w_in=w_in, b_gate=b_gate, s5_lambda_re=s5_lambda_re, s5_lambda_im=s5_lambda_im,
             s5_b_re=s5_b_re, s5_b_im=s5_b_im, s5_c_re=s5_c_re, s5_c_im=s5_c_im, s5_d=s5_d,
             s5_log_dt=s5_log_dt, s5_glu_w1=s5_glu_w1, s5_glu_w2=s5_glu_w2, q_norm_g=q_norm_g,
             k_norm_g=k_norm_g, attn_sinks=attn_sinks, ssd_conv_w=ssd_conv_w, ssd_conv_b=ssd_conv_b,
             ssd_dt_bias=ssd_dt_bias, ssd_a_log=ssd_a_log, ssd_d=ssd_d, ssd_norm_g=ssd_norm_g, p_s5=p_s5,
             p_attn=p_attn, p_ssd=p_ssd, w_out=w_out, norm2_g=norm2_g, w_router_group=w_router_group,
             b_router_group=b_router_group, w_router_expert=w_router_expert, b_router_expert=b_router_expert,
             w_exp_gate=w_exp_gate, w_exp_up=w_exp_up, w_exp_down=w_exp_down)
    bsz, seq, dm = x.shape
    depth = w_in.shape[0]
    xt = x.reshape(bsz * seq, dm)
    for l in range(depth):
        xt = _layer(xt, l, p, bsz)
    return xt.reshape(bsz, seq, dm)
```

```python
import functools
import math

import jax
import jax.numpy as jnp
from jax import lax
from jax.experimental import pallas as pl
from jax.experimental.pallas import tpu as pltpu

F32 = jnp.float32
BF16 = jnp.bfloat16

D_MODEL = 1024
NORM_EPS = 1e-6
S5_WIDTH = 512
S5_GROUP = 16
S5_GROUPS = 32
S5_STATE = 64
HEAD_DIM = 64
ATTN_HEADS = 8
ATTN_KV_HEADS = 2
ATTN_REP = ATTN_HEADS // ATTN_KV_HEADS
ATTN_Q = ATTN_HEADS * HEAD_DIM
ATTN_KV = ATTN_KV_HEADS * HEAD_DIM
ATTN_BLOCK = 128
ROPE_THETA = 10000.0
SSD_WIDTH = 1024
SSD_HEAD_DIM = 64
SSD_HEADS = 16
SSD_GROUPS = 2
SSD_STATE = 64
SSD_CONV = 4
SSD_CHUNK = 128
SSD_BC = SSD_GROUPS * SSD_STATE
SSD_CONV_CH = SSD_WIDTH + 2 * SSD_BC
SSD_GROUP_W = SSD_WIDTH // SSD_GROUPS
N_BRANCH = 3
MOE_GROUPS = 4
MOE_PER_GROUP = 8
MOE_EXPERTS = 32
MOE_FF = 512
N_MIX = S5_WIDTH + ATTN_Q + 2 * ATTN_KV + SSD_WIDTH + SSD_CONV_CH + SSD_HEADS

LANES = 128
SUBLANES = 8
VMEM_LIMIT_BYTES = 56 * 1024 * 1024

S5_CHUNK = 16
S5_FLAT = S5_CHUNK * S5_GROUP
S5_PAIR = 2
ATTN_TILE = 512
TM_PROJ = 512
TM_G = 256
DT_PAD = LANES
N_MIX_PAD = N_MIX - SSD_HEADS + DT_PAD
ROUTE_PAD = LANES


def _cparams(semantics):
    return pltpu.CompilerParams(dimension_semantics=semantics, vmem_limit_bytes=VMEM_LIMIT_BYTES)


def _const_spec(shape):
    zeros = (0,) * len(shape)
    return pl.BlockSpec(shape, lambda *_: zeros, pipeline_mode=pl.Buffered(1))


def _dot(a, b):
    return jnp.dot(a, b, preferred_element_type=F32)


def _split3(a):
    hi = a.astype(BF16)
    r1 = a - hi.astype(F32)
    mid = r1.astype(BF16)
    lo = (r1 - mid.astype(F32)).astype(BF16)
    return hi, mid, lo


def _dot_f32_lhs(a, b_bf16):
    hi, mid, lo = _split3(a)
    return _dot(hi, b_bf16) + _dot(mid, b_bf16) + _dot(lo, b_bf16)


def _rms_scale(x):
    return x * lax.rsqrt(jnp.mean(x * x, axis=-1, keepdims=True) + NORM_EPS)


def _silu(x):
    return x * jax.nn.sigmoid(x)


_OFF_U = 0
_OFF_Q = _OFF_U + S5_WIDTH
_OFF_K = _OFF_Q + ATTN_Q
_OFF_V = _OFF_K + ATTN_KV
_OFF_Z = _OFF_V + ATTN_KV
_OFF_XBC = _OFF_Z + SSD_WIDTH
_OFF_DT = _OFF_XBC + SSD_CONV_CH


def _x_parts(xs, tm):
    arrs, specs = [], []
    for arr, k in xs:
        arrs.append(arr)
        if k is None:
            specs.append(pl.BlockSpec((tm, D_MODEL), lambda i: (i, 0)))
        else:
            specs.append(pl.BlockSpec((None, tm, D_MODEL), lambda i, k=k: (k, i, 0)))
    return arrs, specs


def _x_sum(x_refs):
    x = x_refs[0][...]
    for r in x_refs[1:]:
        x = x + r[...]
    return x


def _inproj_kernel(*refs, n_x):
    x_refs, (g_ref, w_ref), outs = refs[:n_x], refs[n_x:n_x + 2], refs[n_x + 2:]
    hb = (_rms_scale(_x_sum(x_refs)) * g_ref[...]).astype(BF16)
    offs = (_OFF_U, _OFF_Q, _OFF_K, _OFF_V, _OFF_Z, _OFF_XBC, _OFF_DT)
    for o_ref, off in zip(outs, offs):
        width = o_ref.shape[1]
        o_ref[...] = _dot(hb, w_ref[:, off:off + width]).astype(o_ref.dtype)


def _inproj(xs, t, g, w_mix):
    tm = min(TM_PROJ, t)
    widths = (S5_WIDTH, ATTN_Q, ATTN_KV, ATTN_KV, SSD_WIDTH, SSD_CONV_CH, DT_PAD)
    dtypes = (BF16, BF16, BF16, BF16, BF16, BF16, F32)
    x_arrs, x_specs = _x_parts(xs, tm)
    return pl.pallas_call(
        functools.partial(_inproj_kernel, n_x=len(xs)),
        grid=(t // tm,),
        in_specs=x_specs + [_const_spec((1, D_MODEL)), _const_spec((D_MODEL, N_MIX_PAD))],
        out_specs=[pl.BlockSpec((tm, w), lambda i: (i, 0)) for w in widths],
        out_shape=[jax.ShapeDtypeStruct((t, w), d) for w, d in zip(widths, dtypes)],
        compiler_params=_cparams(("parallel",)),
        name="inproj",
    )(*x_arrs, g, w_mix)


def _s5_tables(lam_re, lam_im, b_re, b_im, c_re, c_im, log_dt):
    hp = lax.Precision.HIGHEST
    g_n, p_n = lam_re.shape
    c_n = b_re.shape[-1]
    t_n = S5_CHUNK
    lr, li = lam_re.astype(F32), lam_im.astype(F32)
    dt = jnp.exp(log_dt.astype(F32))[:, None]
    mag = jnp.exp(lr * dt)
    ab_re = mag * jnp.cos(li * dt)
    ab_im = mag * jnp.sin(li * dt)
    nr = ab_re - 1.0
    den = lr * lr + li * li
    f_re = (nr * lr + ab_im * li) / den
    f_im = (ab_im * lr - nr * li) / den
    br, bi = b_re.astype(F32), b_im.astype(F32)
    bb_re = f_re[..., None] * br - f_im[..., None] * bi
    bb_im = f_re[..., None] * bi + f_im[..., None] * br
    j = jnp.arange(t_n + 1, dtype=F32)[:, None, None]
    pmag = jnp.exp(lr * dt * j)
    ang = li * dt * j
    p_re = pmag * jnp.cos(ang)
    p_im = pmag * jnp.sin(ang)
    cr, ci = c_re.astype(F32), c_im.astype(F32)
    ca_re = cr[None] * p_re[:, :, None, :] - ci[None] * p_im[:, :, None, :]
    ca_im = cr[None] * p_im[:, :, None, :] + ci[None] * p_re[:, :, None, :]
    kern = (jnp.einsum('jgcp,gpd->jgcd', ca_re[:t_n], bb_re, precision=hp)
            - jnp.einsum('jgcp,gpd->jgcd', ca_im[:t_n], bb_im, precision=hp))
    tt = jnp.arange(t_n)
    lag = tt[None, :] - tt[:, None]
    kst = kern[jnp.clip(lag, 0, t_n - 1)]
    kst = jnp.where((lag >= 0)[:, :, None, None, None], kst, 0.0)
    m_t = kst.transpose(2, 0, 4, 1, 3).reshape(g_n, t_n * c_n, t_n * c_n)
    q_re = p_re[t_n - 1 - tt]
    q_im = p_im[t_n - 1 - tt]
    bs_re = q_re[..., None] * bb_re[None] - q_im[..., None] * bb_im[None]
    bs_im = q_re[..., None] * bb_im[None] + q_im[..., None] * bb_re[None]
    bs_re = bs_re.transpose(1, 0, 3, 2).reshape(g_n, t_n * c_n, p_n)
    bs_im = bs_im.transpose(1, 0, 3, 2).reshape(g_n, t_n * c_n, p_n)
    co_re = ca_re[1:].transpose(1, 3, 0, 2).reshape(g_n, p_n, t_n * c_n)
    co_im = -ca_im[1:].transpose(1, 3, 0, 2).reshape(g_n, p_n, t_n * c_n)

    def pair_diag(a):
        gp = g_n // S5_PAIR
        a = a.reshape(gp, S5_PAIR, a.shape[1], a.shape[2])
        z = jnp.zeros_like(a[:, 0])
        top = jnp.concatenate([a[:, 0], z], axis=2)
        bot = jnp.concatenate([z, a[:, 1]], axis=2)
        return jnp.concatenate([top, bot], axis=1)

    return dict(
        m_t=m_t.astype(BF16),
        bs_re=pair_diag(bs_re).astype(BF16), bs_im=pair_diag(bs_im).astype(BF16),
        co_re=pair_diag(co_re).astype(BF16), co_im=pair_diag(co_im).astype(BF16),
        at_re=p_re[t_n].reshape(1, g_n * p_n), at_im=p_im[t_n].reshape(1, g_n * p_n))


_S5_PW = S5_PAIR * S5_FLAT
_S5_PS = S5_PAIR * S5_STATE


def _s5_state_kernel(u_ref, bre_ref, bim_ref, sre_o, sim_o):
    u = u_ref[...]
    sre_o[...] = _dot(u, bre_ref[0])
    sim_o[...] = _dot(u, bim_ref[0])


def _s5_scan_kernel(sre_ref, sim_ref, are_ref, aim_ref, hre_o, him_o, *, bsz, nck):
    a_re = are_ref[...]
    a_im = aim_ref[...]
    w = sre_ref.shape[1]

    def step(k, carry):
        h_re, h_im = carry
        rows = pl.ds(k, bsz, stride=nck)
        hre_o[rows, :] = h_re
        him_o[rows, :] = h_im
        s_re = sre_ref[rows, :]
        s_im = sim_ref[rows, :]
        return (a_re * h_re - a_im * h_im + s_re, a_re * h_im + a_im * h_re + s_im)

    zero = jnp.zeros((bsz, w), F32)
    lax.fori_loop(0, nck, step, (zero, zero))


def _s5_out_kernel(u_ref, hre_ref, him_ref, mt_ref, cre_ref, cim_ref, y_o):
    u = u_ref[...]
    y = _dot(hre_ref[...].astype(BF16), cre_ref[0]) + _dot(him_ref[...].astype(BF16), cim_ref[0])
    intra = [_dot(u[:, g * S5_FLAT:(g + 1) * S5_FLAT], mt_ref[g]) for g in range(S5_PAIR)]
    y_o[...] = (y + jnp.concatenate(intra, axis=1)).astype(y_o.dtype)


def _s5_mixer(u, tabs, bsz):
    t = u.shape[0]
    seq = t // bsz
    nck = seq // S5_CHUNK
    r = t // S5_CHUNK
    gp = S5_GROUPS // S5_PAIR
    u2 = u.reshape(r, S5_CHUNK, S5_GROUPS, S5_GROUP).transpose(0, 2, 1, 3).reshape(r, S5_GROUPS * S5_FLAT)
    sw = S5_GROUPS * S5_STATE
    s_re, s_im = pl.pallas_call(
        _s5_state_kernel,
        grid=(gp,),
        in_specs=[pl.BlockSpec((r, _S5_PW), lambda j: (0, j)),
                  pl.BlockSpec((1, _S5_PW, _S5_PS), lambda j: (j, 0, 0)),
                  pl.BlockSpec((1, _S5_PW, _S5_PS), lambda j: (j, 0, 0))],
        out_specs=[pl.BlockSpec((r, _S5_PS), lambda j: (0, j))] * 2,
        out_shape=[jax.ShapeDtypeStruct((r, sw), F32)] * 2,
        compiler_params=_cparams(("parallel",)),
        name="s5_state",
    )(u2, tabs['bs_re'], tabs['bs_im'])
    scan_w = LANES
    h_re, h_im = pl.pallas_call(
        functools.partial(_s5_scan_kernel, bsz=bsz, nck=nck),
        grid=(sw // scan_w,),
        in_specs=[pl.BlockSpec((r, scan_w), lambda j: (0, j))] * 2
                 + [pl.BlockSpec((1, scan_w), lambda j: (0, j))] * 2,
        out_specs=[pl.BlockSpec((r, scan_w), lambda j: (0, j))] * 2,
        out_shape=[jax.ShapeDtypeStruct((r, sw), F32)] * 2,
        compiler_params=_cparams(("parallel",)),
        name="s5_scan",
    )(s_re, s_im, tabs['at_re'], tabs['at_im'])
    y2 = pl.pallas_call(
        _s5_out_kernel,
        grid=(gp,),
        in_specs=[pl.BlockSpec((r, _S5_PW), lambda j: (0, j)),
                  pl.BlockSpec((r, _S5_PS), lambda j: (0, j)),
                  pl.BlockSpec((r, _S5_PS), lambda j: (0, j)),
                  pl.BlockSpec((S5_PAIR, S5_FLAT, S5_FLAT), lambda j: (j, 0, 0)),
                  pl.BlockSpec((1, _S5_PS, _S5_PW), lambda j: (j, 0, 0)),
                  pl.BlockSpec((1, _S5_PS, _S5_PW), lambda j: (j, 0, 0))],
        out_specs=pl.BlockSpec((r, _S5_PW), lambda j: (0, j)),
        out_shape=jax.ShapeDtypeStruct((r, S5_GROUPS * S5_FLAT), BF16),
        compiler_params=_cparams(("parallel",)),
        name="s5_out",
    )(u2, h_re, h_im, tabs['m_t'], tabs['co_re'], tabs['co_im'])
    return y2.reshape(r, S5_GROUPS, S5_CHUNK, S5_GROUP).transpose(0, 2, 1, 3).reshape(t, S5_WIDTH)


def _swap_rope_halves(y):
    w = y.shape[1]
    lane = lax.broadcasted_iota(jnp.int32, y.shape, 1)
    lower = (lane % HEAD_DIM) < (HEAD_DIM // 2)
    return jnp.where(lower, pltpu.roll(y, w - HEAD_DIM // 2, 1), pltpu.roll(y, HEAD_DIM // 2, 1))


def _norm_rope(x, gain, head_mean, cos2, sin2):
    reps = x.shape[1] // LANES
    ms = _dot_f32_lhs(x * x, head_mean)
    y = x * lax.rsqrt(ms + NORM_EPS) * gain
    if reps > 1:
        cos2 = jnp.concatenate([cos2] * reps, axis=1)
        sin2 = jnp.concatenate([sin2] * reps, axis=1)
    return y * cos2 + _swap_rope_halves(y) * sin2


def _attn_kernel(q_ref, kc_ref, kp_ref, vc_ref, vp_ref, cosc_ref, sinc_ref, cosp_ref, sinp_ref,
                 qg_ref, kg_ref, sink_ref, hmq_ref, hmk_ref, o_ref):
    seq_start = pl.program_id(1) == 0
    blk = ATTN_BLOCK
    nblk = q_ref.shape[0] // blk
    q = _norm_rope(q_ref[...].astype(F32), qg_ref[...], hmq_ref[...], cosc_ref[...], sinc_ref[...])
    q = (q * (HEAD_DIM ** -0.5)).astype(BF16)
    kc = _norm_rope(kc_ref[...].astype(F32), kg_ref[...], hmk_ref[...], cosc_ref[...], sinc_ref[...]).astype(BF16)
    kp = _norm_rope(kp_ref[...].astype(F32), kg_ref[...], hmk_ref[...], cosp_ref[...], sinp_ref[...]).astype(BF16)
    k_all = jnp.concatenate([kp, kc], axis=0)
    v_all = jnp.concatenate([vp_ref[...], vc_ref[...]], axis=0)
    shape = (ATTN_REP * blk, 2 * blk)
    qi = lax.broadcasted_iota(jnp.int32, shape, 0) % blk + blk
    kj = lax.broadcasted_iota(jnp.int32, shape, 1)
    band = (kj <= qi) & (qi - kj < blk)
    band_first = band & ((kj >= blk) | jnp.logical_not(seq_start))
    sinks = sink_ref[...]
    for j in range(ATTN_KV_HEADS):
        sl = slice(j * HEAD_DIM, (j + 1) * HEAD_DIM)
        heads = [j * ATTN_REP + r for r in range(ATTN_REP)]
        sink = jnp.concatenate([jnp.broadcast_to(sinks[:, h:h + 1], (blk, 1)) for h in heads], axis=0)
        for n in range(nblk):
            rows = slice(n * blk, (n + 1) * blk)
            kb = k_all[n * blk:(n + 2) * blk, sl]
            vb = v_all[n * blk:(n + 2) * blk, sl]
            q4 = jnp.concatenate([q[rows, h * HEAD_DIM:(h + 1) * HEAD_DIM] for h in heads], axis=0)
            s = lax.dot_general(q4, kb, (((1,), (1,)), ((), ())), preferred_element_type=F32)
            s = jnp.where(band_first if n == 0 else band, s, -jnp.inf)
            m = jnp.maximum(jnp.max(s, axis=-1, keepdims=True), sink)
            p = jnp.exp(s - m)
            denom = jnp.sum(p, axis=-1, keepdims=True) + jnp.exp(sink - m)
            out = _dot((p / denom).astype(BF16), vb)
            for r, h in enumerate(heads):
                o_ref[rows, h * HEAD_DIM:(h + 1) * HEAD_DIM] = out[r * blk:(r + 1) * blk].astype(o_ref.dtype)


def _rope_tables(seq):
    half = HEAD_DIM // 2
    inv = jnp.power(ROPE_THETA, -jnp.arange(half, dtype=F32) * 2.0 / HEAD_DIM)
    ang = jnp.arange(seq, dtype=F32)[:, None] * inv[None, :]
    cos, sin = jnp.cos(ang), jnp.sin(ang)
    cos2 = jnp.concatenate([cos, cos, cos, cos], axis=1)
    sin2 = jnp.concatenate([-sin, sin, -sin, sin], axis=1)
    return cos2, sin2


def _head_mean_matrix(width):
    i = jnp.arange(width)
    return jnp.where((i[:, None] // HEAD_DIM) == (i[None, :] // HEAD_DIM), 1.0 / HEAD_DIM, 0.0).astype(BF16)


def _swa_attention(q, k, v, q_g, k_g, sinks, bsz):
    t = q.shape[0]
    seq = t // bsz
    tile = min(ATTN_TILE, seq)
    nt = seq // tile
    bpt = tile // ATTN_BLOCK
    nb = seq // ATTN_BLOCK
    cos2, sin2 = _rope_tables(seq)
    qg = jnp.tile(q_g.astype(F32), ATTN_HEADS)[None]
    kg = jnp.tile(k_g.astype(F32), ATTN_KV_HEADS)[None]
    cur = lambda b, n: (b * nt + n, 0)
    prev = lambda b, n: (b * nb + jnp.maximum(n * bpt - 1, 0), 0)
    tcur = lambda b, n: (n, 0)
    tprev = lambda b, n: (jnp.maximum(n * bpt - 1, 0), 0)
    blk = ATTN_BLOCK
    return pl.pallas_call(
        _attn_kernel,
        grid=(bsz, nt),
        in_specs=[pl.BlockSpec((tile, ATTN_Q), cur),
                  pl.BlockSpec((tile, ATTN_KV), cur), pl.BlockSpec((blk, ATTN_KV), prev),
                  pl.BlockSpec((tile, ATTN_KV), cur), pl.BlockSpec((blk, ATTN_KV), prev),
                  pl.BlockSpec((tile, LANES), tcur), pl.BlockSpec((tile, LANES), tcur),
                  pl.BlockSpec((blk, LANES), tprev), pl.BlockSpec((blk, LANES), tprev),
                  _const_spec((1, ATTN_Q)), _const_spec((1, ATTN_KV)), _const_spec((1, ATTN_HEADS)),
                  _const_spec((ATTN_Q, ATTN_Q)), _const_spec((ATTN_KV, ATTN_KV))],
        out_specs=pl.BlockSpec((tile, ATTN_Q), cur),
        out_shape=jax.ShapeDtypeStruct((t, ATTN_Q), BF16),
        compiler_params=_cparams(("parallel", "parallel")),
        name="swa_attention",
    )(q, k, k, v, v, cos2, sin2, cos2, sin2, qg, kg, sinks.astype(F32)[None],
      _head_mean_matrix(ATTN_Q), _head_mean_matrix(ATTN_KV))


_CONV_TAIL = SUBLANES


def _ssd_kernel(z_ref, xbc_ref, dt_ref, cw_ref, cb_ref, dtb_ref, alog_ref, dskip_ref, ng_ref, exp_ref,
                o_ref, xext, state, ybuf):
    c = pl.program_id(1)
    ch = SSD_CHUNK

    @pl.when(c == 0)
    def _():
        xext[0:_CONV_TAIL, :] = jnp.zeros((_CONV_TAIL, SSD_CONV_CH), F32)
        state[...] = jnp.zeros_like(state)

    xext[_CONV_TAIL:_CONV_TAIL + ch, :] = xbc_ref[...].astype(F32)
    conv = cb_ref[...]
    for j in range(SSD_CONV):
        conv = conv + cw_ref[j:j + 1, :] * xext[pl.ds(_CONV_TAIL - (SSD_CONV - 1) + j, ch), :]
    xext[0:_CONV_TAIL, :] = xext[ch:ch + _CONV_TAIL, :]
    act = _silu(conv)
    xs = act[:, :SSD_WIDTH]
    bm = act[:, SSD_WIDTH:SSD_WIDTH + SSD_BC].astype(BF16)
    cm = act[:, SSD_WIDTH + SSD_BC:].astype(BF16)

    lane = lax.broadcasted_iota(jnp.int32, (ch, LANES), 1)
    xdt = dt_ref[...] + dtb_ref[...]
    dt = jnp.maximum(xdt, 0.0) + jnp.log1p(jnp.exp(-jnp.abs(xdt)))
    dt = jnp.where(lane < SSD_HEADS, dt, 0.0)
    a = dt * (-jnp.exp(alog_ref[...]))
    row = lax.broadcasted_iota(jnp.int32, (ch, ch), 0)
    col = lax.broadcasted_iota(jnp.int32, (ch, ch), 1)
    causal = row >= col
    hi, mid, lo = _split3(a)
    tril = causal.astype(BF16)
    cs = _dot(tril, hi) + _dot(tril, mid) + _dot(tril, lo)
    cs_t = cs.T
    expand = exp_ref[...]
    dt_x = _dot_f32_lhs(dt, expand)
    cs_x = _dot_f32_lhs(cs, expand)
    cs_last_x = cs_x[ch - 1:ch, :]
    xdt_full = xs * dt_x
    in_decay = jnp.exp(cs_x)
    out_decay = jnp.exp(cs_last_x - cs_x)
    chunk_decay = jnp.exp(cs_last_x)
    xw = (xdt_full * out_decay).astype(BF16)
    xdt_b = xdt_full.astype(BF16)
    hpg = SSD_HEADS // SSD_GROUPS
    for g in range(SSD_GROUPS):
        gs = slice(g * SSD_STATE, (g + 1) * SSD_STATE)
        ws = slice(g * SSD_GROUP_W, (g + 1) * SSD_GROUP_W)
        bg = bm[:, gs]
        cg = cm[:, gs]
        cb = lax.dot_general(cg, bg, (((1,), (1,)), ((), ())), preferred_element_type=F32)
        prev = state[g]
        ybuf[:, ws] = _dot(cg, prev.astype(BF16)) * in_decay[:, ws]
        for h in range(hpg):
            hh = g * hpg + h
            seg = cs[:, hh:hh + 1] - cs_t[hh:hh + 1, :]
            lmat = jnp.where(causal, jnp.exp(seg), 0.0)
            hs = slice(hh * SSD_HEAD_DIM, (hh + 1) * SSD_HEAD_DIM)
            ybuf[:, hs] += _dot((cb * lmat).astype(BF16), xdt_b[:, hs])
        upd = lax.dot_general(bg, xw[:, ws], (((0,), (0,)), ((), ())), preferred_element_type=F32)
        state[g] = prev * chunk_decay[:, ws] + upd

    y = ybuf[...] + xs * dskip_ref[...]
    y = y * _silu(z_ref[...].astype(F32))
    parts = []
    for g in range(SSD_GROUPS):
        ws = slice(g * SSD_GROUP_W, (g + 1) * SSD_GROUP_W)
        parts.append(_rms_scale(y[:, ws]))
    o_ref[...] = (jnp.concatenate(parts, axis=1) * ng_ref[...]).astype(o_ref.dtype)


def _ssd_mixer(z, xbc, dt_raw, conv_w, conv_b, dt_bias, a_log, d_skip, norm_g, bsz):
    t = z.shape[0]
    seq = t // bsz
    nc = seq // SSD_CHUNK
    pad = lambda v: jnp.pad(v.astype(F32), (0, LANES - SSD_HEADS))[None]
    d_x = jnp.repeat(d_skip.astype(F32), SSD_HEAD_DIM)[None]
    hid = jnp.arange(SSD_WIDTH) // SSD_HEAD_DIM
    expand = (jnp.arange(LANES)[:, None] == hid[None, :]).astype(BF16)
    blk = lambda b, c: (b * nc + c, 0)
    return pl.pallas_call(
        _ssd_kernel,
        grid=(bsz, nc),
        in_specs=[pl.BlockSpec((SSD_CHUNK, SSD_WIDTH), blk),
                  pl.BlockSpec((SSD_CHUNK, SSD_CONV_CH), blk),
                  pl.BlockSpec((SSD_CHUNK, DT_PAD), blk),
                  _const_spec((SSD_CONV, SSD_CONV_CH)), _const_spec((1, SSD_CONV_CH)),
                  _const_spec((1, LANES)), _const_spec((1, LANES)),
                  _const_spec((1, SSD_WIDTH)), _const_spec((1, SSD_WIDTH)),
                  _const_spec((LANES, SSD_WIDTH))],
        out_specs=pl.BlockSpec((SSD_CHUNK, SSD_WIDTH), blk),
        out_shape=jax.ShapeDtypeStruct((t, SSD_WIDTH), BF16),
        scratch_shapes=[pltpu.VMEM((SSD_CHUNK + 2 * _CONV_TAIL, SSD_CONV_CH), F32),
                        pltpu.VMEM((SSD_GROUPS, SSD_STATE, SSD_GROUP_W), F32),
                        pltpu.VMEM((SSD_CHUNK, SSD_WIDTH), F32)],
        compiler_params=_cparams(("arbitrary", "arbitrary")),
        name="ssd_mixer",
    )(z, xbc, dt_raw, conv_w.astype(F32), conv_b.astype(F32)[None], pad(dt_bias), pad(a_log),
      d_x, norm_g.astype(F32)[None], expand)


def _route(logits):
    lane = lax.broadcasted_iota(jnp.int32, logits.shape, 1).astype(F32)
    big = float(ROUTE_PAD)
    is_g = lane < MOE_GROUPS
    gl = jnp.where(is_g, logits, -jnp.inf)
    gmax = jnp.max(gl, axis=1, keepdims=True)
    gsel = jnp.min(jnp.where(gl == gmax, lane, big), axis=1, keepdims=True)
    pg = 1.0 / jnp.sum(jnp.where(is_g, jnp.exp(logits - gmax), 0.0), axis=1, keepdims=True)
    lo = MOE_GROUPS + MOE_PER_GROUP * gsel
    ev = jnp.where((lane >= lo) & (lane < lo + MOE_PER_GROUP), logits, -jnp.inf)
    m1 = jnp.max(ev, axis=1, keepdims=True)
    i1 = jnp.min(jnp.where(ev == m1, lane, big), axis=1, keepdims=True)
    ev2 = jnp.where(lane == i1, -jnp.inf, ev)
    m2 = jnp.max(ev2, axis=1, keepdims=True)
    i2 = jnp.min(jnp.where(ev2 == m2, lane, big), axis=1, keepdims=True)
    e21 = jnp.exp(m2 - m1)
    w1 = pg / (1.0 + e21)
    w2 = pg * e21 / (1.0 + e21)
    out = jnp.where(lane == 0, i1 - MOE_GROUPS, 0.0)
    out = jnp.where(lane == 1, i2 - MOE_GROUPS, out)
    out = jnp.where(lane == 2, w1, out)
    return jnp.where(lane == 3, w2, out)


def _merge_kernel(*refs, n_x):
    x_refs = refs[:n_x]
    (ys5_ref, u_ref, yb_ref, yc_ref, n1g_ref, wg_ref, bg_ref, d_ref, w1_ref, w2_ref, ps5_ref, pat_ref,
     pssd_ref, wo_ref, n2g_ref, wr_ref, br_ref, x1_o, h2_o, route_o) = refs[n_x:]
    x = _x_sum(x_refs)
    hb = (_rms_scale(x) * n1g_ref[...]).astype(BF16)
    ya = ys5_ref[...].astype(F32) + d_ref[...] * u_ref[...].astype(F32)
    yab = jax.nn.gelu(ya).astype(BF16)
    ya = _dot(yab, w1_ref[...]) * jax.nn.sigmoid(_dot(yab, w2_ref[...]))
    branches = ((ya.astype(BF16), ps5_ref), (yb_ref[...], pat_ref), (yc_ref[...], pssd_ref))
    merged = None
    for b, (yv, p_ref) in enumerate(branches):
        gate = jax.nn.sigmoid(_dot(hb, wg_ref[:, b * D_MODEL:(b + 1) * D_MODEL]) + bg_ref[b:b + 1, :])
        term = gate * _dot(yv, p_ref[...])
        merged = term if merged is None else merged + term
    x1 = x + _dot(merged.astype(BF16), wo_ref[...])
    x1_o[...] = x1
    h2 = _rms_scale(x1) * n2g_ref[...]
    h2_o[...] = h2.astype(h2_o.dtype)
    logits = jnp.dot(h2, wr_ref[...], precision=lax.Precision.HIGHEST, preferred_element_type=F32) + br_ref[...]
    route_o[...] = _route(logits)


def _merge(xs, t, ys5, u, yb, yc, lw):
    tm = min(TM_PROJ, t)
    row = lambda w: pl.BlockSpec((tm, w), lambda i: (i, 0))
    consts = [lw['n1g'], lw['w_gate'], lw['b_gate'], lw['s5_d'], lw['glu_w1'], lw['glu_w2'], lw['p_s5'],
              lw['p_attn'], lw['p_ssd'], lw['w_out'], lw['n2g'], lw['w_router'], lw['b_router']]
    x_arrs, x_specs = _x_parts(xs, tm)
    return pl.pallas_call(
        functools.partial(_merge_kernel, n_x=len(xs)),
        grid=(t // tm,),
        in_specs=x_specs + [row(S5_WIDTH), row(S5_WIDTH), row(ATTN_Q), row(SSD_WIDTH)]
                 + [_const_spec(c.shape) for c in consts],
        out_specs=[row(D_MODEL), row(D_MODEL), row(ROUTE_PAD)],
        out_shape=[jax.ShapeDtypeStruct((t, D_MODEL), F32), jax.ShapeDtypeStruct((t, D_MODEL), F32),
                   jax.ShapeDtypeStruct((t, ROUTE_PAD), F32)],
        compiler_params=_cparams(("parallel",)),
        name="merge_router",
    )(*x_arrs, ys5, u, yb, yc, *consts)


def _moe_plan(route, t):
    n_exp = MOE_EXPERTS
    e = route[:, 0:2].astype(jnp.int32).reshape(-1)
    w = route[:, 2:4].reshape(-1)
    onehot = (e[:, None] == jnp.arange(n_exp, dtype=jnp.int32)[None, :]).astype(jnp.int32)
    csum = jnp.cumsum(onehot, axis=0)
    rank = jnp.sum(csum * onehot, axis=1) - 1
    counts = csum[-1]
    padded = ((counts + TM_G - 1) // TM_G) * TM_G
    pend = jnp.cumsum(padded)
    dest = (pend - padded)[e] + rank
    n_tiles = -(-(2 * t + n_exp * (TM_G - 1)) // TM_G)
    n_rows = n_tiles * TM_G
    pair = jnp.arange(2 * t, dtype=jnp.int32)
    tok, k = pair // 2, pair % 2
    plane = t + _MOE_SPARE
    src = jnp.zeros((n_rows,), jnp.int32).at[dest].set(tok, unique_indices=True)
    spare = t + jnp.arange(n_rows, dtype=jnp.int32) % _MOE_SPARE
    dst = spare.at[dest].set(k * plane + tok, unique_indices=True)
    wrow = jnp.zeros((n_rows,), F32).at[dest].set(w, unique_indices=True)
    tile_start = jnp.arange(n_tiles, dtype=jnp.int32) * TM_G
    tile_e = jnp.sum((pend[None, :] <= tile_start[:, None]).astype(jnp.int32), axis=1)
    tile_e = jnp.minimum(tile_e, n_exp - 1)
    return (src.reshape(n_tiles, 1, TM_G), dst.reshape(n_tiles, 1, TM_G), wrow.reshape(n_rows, 1), tile_e)


_MOE_SPARE = 2 * TM_G


def _gmm_kernel(te_ref, src_ref, srcn_ref, dst_ref, wrow_ref, h2_hbm, wg_ref, wu_ref, wd_ref, out_hbm,
                xbuf, ybuf, gsem, ssem):
    del te_ref
    i = pl.program_id(0)
    last = pl.num_programs(0) - 1
    slot = i % 2

    def gather_start(idx_ref, s):
        for r in range(TM_G):
            pltpu.make_async_copy(h2_hbm.at[pl.ds(idx_ref[0, 0, r], 1), :],
                                  xbuf.at[s, pl.ds(r, 1), :], gsem.at[s]).start()

    def gather_wait(s):
        pltpu.make_async_copy(h2_hbm.at[pl.ds(0, TM_G), :], xbuf.at[s], gsem.at[s]).wait()

    def scatter_wait(s):
        pltpu.make_async_copy(ybuf.at[s], out_hbm.at[pl.ds(0, TM_G), :], ssem.at[s]).wait()

    @pl.when(i == 0)
    def _():
        gather_start(src_ref, 0)
        ybuf[...] = jnp.zeros_like(ybuf)
        plane = out_hbm.shape[0] // 2
        fills = [pltpu.make_async_copy(ybuf.at[s],
                                       out_hbm.at[pl.ds(k * plane + plane - _MOE_SPARE + s * TM_G, TM_G), :],
                                       ssem.at[s]) for k in range(2) for s in range(2)]
        for f in fills:
            f.start()
        for f in fills:
            f.wait()

    gather_wait(slot)

    @pl.when(i >= 2)
    def _():
        scatter_wait(slot)

    gather_start(srcn_ref, 1 - slot)

    xb = xbuf[slot].astype(BF16)
    hid = _silu(_dot(xb, wg_ref[0].astype(BF16))) * _dot(xb, wu_ref[0].astype(BF16))
    ybuf[slot] = _dot(hid.astype(BF16), wd_ref[0].astype(BF16)) * wrow_ref[...]
    for r in range(TM_G):
        pltpu.make_async_copy(ybuf.at[slot, pl.ds(r, 1), :],
                              out_hbm.at[pl.ds(dst_ref[0, 0, r], 1), :], ssem.at[slot]).start()

    @pl.when(i == last)
    def _():
        scatter_wait(slot)
        gather_wait(1 - slot)

    @pl.when((i == last) & (i >= 1))
    def _():
        scatter_wait(1 - slot)


def _moe_sparse(h2, route, w_gate, w_up, w_down):
    t = h2.shape[0]
    src, dst, wrow, tile_e = _moe_plan(route, t)
    n_tiles = src.shape[0]
    smem = lambda imap: pl.BlockSpec((1, 1, TM_G), imap, memory_space=pltpu.SMEM)
    grid_spec = pltpu.PrefetchScalarGridSpec(
        num_scalar_prefetch=1,
        grid=(n_tiles,),
        in_specs=[smem(lambda i, te: (i, 0, 0)),
                  smem(lambda i, te: (jnp.minimum(i + 1, n_tiles - 1), 0, 0)),
                  smem(lambda i, te: (i, 0, 0)),
                  pl.BlockSpec((TM_G, 1), lambda i, te: (i, 0)),
                  pl.BlockSpec(memory_space=pl.ANY),
                  pl.BlockSpec((1, D_MODEL, MOE_FF), lambda i, te: (te[i], 0, 0)),
                  pl.BlockSpec((1, D_MODEL, MOE_FF), lambda i, te: (te[i], 0, 0)),
                  pl.BlockSpec((1, MOE_FF, D_MODEL), lambda i, te: (te[i], 0, 0))],
        out_specs=pl.BlockSpec(memory_space=pl.ANY),
        scratch_shapes=[pltpu.VMEM((2, TM_G, D_MODEL), F32), pltpu.VMEM((2, TM_G, D_MODEL), F32),
                        pltpu.SemaphoreType.DMA((2,)), pltpu.SemaphoreType.DMA((2,))])
    plane = t + _MOE_SPARE
    out = pl.pallas_call(
        _gmm_kernel,
        grid_spec=grid_spec,
        out_shape=jax.ShapeDtypeStruct((2 * plane, D_MODEL), F32),
        compiler_params=_cparams(("arbitrary",)),
        name="moe_gmm",
    )(tile_e, src, src, dst, wrow, h2, w_gate, w_up, w_down)
    return out.reshape(2, plane, D_MODEL)


def _sum_kernel(*refs):
    refs[-1][...] = _x_sum(refs[:-1])


def _residual_sum(xs, t):
    tm = min(TM_PROJ, t)
    x_arrs, x_specs = _x_parts(xs, tm)
    return pl.pallas_call(
        _sum_kernel,
        grid=(t // tm,),
        in_specs=x_specs,
        out_specs=pl.BlockSpec((tm, D_MODEL), lambda i: (i, 0)),
        out_shape=jax.ShapeDtypeStruct((t, D_MODEL), F32),
        compiler_params=_cparams(("parallel",)),
        name="residual_sum",
    )(*x_arrs)


def _layer_weights(l, p):
    w_in = p['w_in'][l]
    w_mix = jnp.pad(w_in[:, :N_MIX], ((0, 0), (0, N_MIX_PAD - N_MIX))).astype(BF16)
    w_router = jnp.concatenate([p['w_router_group'][l], p['w_router_expert'][l]], axis=1).astype(F32)
    npad = ROUTE_PAD - w_router.shape[1]
    b_router = jnp.concatenate([p['b_router_group'][l], p['b_router_expert'][l]]).astype(F32)
    return dict(
        n1g=p['norm1_g'][l].astype(F32)[None], w_mix=w_mix,
        w_gate=w_in[:, N_MIX:].astype(BF16), b_gate=p['b_gate'][l].astype(F32),
        s5_d=p['s5_d'][l].astype(F32)[None],
        glu_w1=p['s5_glu_w1'][l].astype(BF16), glu_w2=p['s5_glu_w2'][l].astype(BF16),
        p_s5=p['p_s5'][l].astype(BF16), p_attn=p['p_attn'][l].astype(BF16), p_ssd=p['p_ssd'][l].astype(BF16),
        w_out=p['w_out'][l].astype(BF16), n2g=p['norm2_g'][l].astype(F32)[None],
        w_router=jnp.pad(w_router, ((0, 0), (0, npad))), b_router=jnp.pad(b_router, (0, npad))[None])


def _layer(xs, t, l, p, bsz):
    lw = _layer_weights(l, p)
    u, q, k, v, z, xbc, dt_raw = _inproj(xs, t, lw['n1g'], lw['w_mix'])
    tabs = _s5_tables(p['s5_lambda_re'][l], p['s5_lambda_im'][l], p['s5_b_re'][l], p['s5_b_im'][l],
                      p['s5_c_re'][l], p['s5_c_im'][l], p['s5_log_dt'][l])
    ys5 = _s5_mixer(u, tabs, bsz)
    yb = _swa_attention(q, k, v, p['q_norm_g'][l], p['k_norm_g'][l], p['attn_sinks'][l], bsz)
    yc = _ssd_mixer(z, xbc, dt_raw, p['ssd_conv_w'][l], p['ssd_conv_b'][l], p['ssd_dt_bias'][l],
                    p['ssd_a_log'][l], p['ssd_d'][l], p['ssd_norm_g'][l], bsz)
    x1, h2, route = _merge(xs, t, ys5, u, yb, yc, lw)
    moe = _moe_sparse(h2, route, p['w_exp_gate'][l], p['w_exp_up'][l], p['w_exp_down'][l])
    return [(x1, None), (moe, 0), (moe, 1)]


def kernel(x, norm1_g, w_in, b_gate, s5_lambda_re, s5_lambda_im, s5_b_re, s5_b_im, s5_c_re, s5_c_im, s5_d, s5_log_dt, s5_glu_w1, s5_glu_w2, q_norm_g, k_norm_g, attn_sinks, ssd_conv_w, ssd_conv_b, ssd_dt_bias, ssd_a_log, ssd_d, ssd_norm_g, p_s5, p_attn, p_ssd, w_out, norm2_g, w_router_group, b_router_group, w_router_expert, b_router_expert, w_exp_gate, w_exp_up, w_exp_down):
    p = dict(norm1_g=norm1_g, w_in=w_in, b_gate=b_gate, s5_lambda_re=s5_lambda_re, s5_lambda_im=s5_lambda_im,
             s5_b_re=s5_b_re, s5_b_im=s5_b_im, s5_c_re=s5_c_re, s5_c_im=s5_c_im, s5_d=s5_d,
             s5_log_dt=s5_log_dt, s5_glu_w1=s5_glu_w1, s5_glu_w2=s5_glu_w2, q_norm_g=q_norm_g,
             k_norm_g=k_norm_g, attn_sinks=attn_sinks, ssd_conv_w=ssd_conv_w, ssd_conv_b=ssd_conv_b,
             ssd_dt_bias=ssd_dt_bias, ssd_a_log=ssd_a_log, ssd_d=ssd_d, ssd_norm_g=ssd_norm_g, p_s5=p_s5,
             p_attn=p_attn, p_ssd=p_ssd, w_out=w_out, norm2_g=norm2_g, w_router_group=w_router_group,
             b_router_group=b_router_group, w_router_expert=w_router_expert, b_router_expert=b_router_expert,
             w_exp_gate=w_exp_gate, w_exp_up=w_exp_up, w_exp_down=w_exp_down)
    bsz, seq, dm = x.shape
    depth = w_in.shape[0]
    t = bsz * seq
    xs = [(x.reshape(t, dm), None)]
    for l in range(depth):
        xs = _layer(xs, t, l, p, bsz)
    return _residual_sum(xs, t).reshape(bsz, seq, dm)
```

```python
import functools
import math

import jax
import jax.numpy as jnp
from jax import lax
from jax.experimental import pallas as pl
from jax.experimental.pallas import tpu as pltpu

F32 = jnp.float32
BF16 = jnp.bfloat16

D_MODEL = 1024
NORM_EPS = 1e-6
S5_WIDTH = 512
S5_GROUP = 16
S5_GROUPS = 32
S5_STATE = 64
HEAD_DIM = 64
ATTN_HEADS = 8
ATTN_KV_HEADS = 2
ATTN_REP = ATTN_HEADS // ATTN_KV_HEADS
ATTN_Q = ATTN_HEADS * HEAD_DIM
ATTN_KV = ATTN_KV_HEADS * HEAD_DIM
ATTN_BLOCK = 128
ROPE_THETA = 10000.0
SSD_WIDTH = 1024
SSD_HEAD_DIM = 64
SSD_HEADS = 16
SSD_GROUPS = 2
SSD_STATE = 64
SSD_CONV = 4
SSD_CHUNK = 128
SSD_BC = SSD_GROUPS * SSD_STATE
SSD_CONV_CH = SSD_WIDTH + 2 * SSD_BC
SSD_GROUP_W = SSD_WIDTH // SSD_GROUPS
N_BRANCH = 3
MOE_GROUPS = 4
MOE_PER_GROUP = 8
MOE_EXPERTS = 32
MOE_FF = 512
N_MIX = S5_WIDTH + ATTN_Q + 2 * ATTN_KV + SSD_WIDTH + SSD_CONV_CH + SSD_HEADS

LANES = 128
SUBLANES = 8
VMEM_LIMIT_BYTES = 56 * 1024 * 1024

S5_CHUNK = 16
S5_TG = 8
ATTN_TILE = 512
TM_PROJ = 512
TM_G = 256
DT_PAD = LANES
N_MIX_PAD = N_MIX - SSD_HEADS + DT_PAD
ROUTE_PAD = LANES


def _cparams(semantics):
    return pltpu.CompilerParams(dimension_semantics=semantics, vmem_limit_bytes=VMEM_LIMIT_BYTES)


def _const_spec(shape):
    zeros = (0,) * len(shape)
    return pl.BlockSpec(shape, lambda *_: zeros, pipeline_mode=pl.Buffered(1))


def _dot(a, b):
    return jnp.dot(a, b, preferred_element_type=F32)


def _split3(a):
    hi = a.astype(BF16)
    r1 = a - hi.astype(F32)
    mid = r1.astype(BF16)
    lo = (r1 - mid.astype(F32)).astype(BF16)
    return hi, mid, lo


def _dot_f32_lhs(a, b_bf16):
    hi, mid, lo = _split3(a)
    return _dot(hi, b_bf16) + _dot(mid, b_bf16) + _dot(lo, b_bf16)


def _rms_scale(x):
    return x * lax.rsqrt(jnp.mean(x * x, axis=-1, keepdims=True) + NORM_EPS)


def _silu(x):
    return x * jax.nn.sigmoid(x)


_OFF_U = 0
_OFF_Q = _OFF_U + S5_WIDTH
_OFF_K = _OFF_Q + ATTN_Q
_OFF_V = _OFF_K + ATTN_KV
_OFF_Z = _OFF_V + ATTN_KV
_OFF_XBC = _OFF_Z + SSD_WIDTH
_OFF_DT = _OFF_XBC + SSD_CONV_CH


def _x_parts(xs, tm):
    row = lambda w: pl.BlockSpec((tm, w), lambda i: (i, 0))
    if len(xs) == 1:
        return list(xs), [row(D_MODEL)]
    x1, route, moe = xs
    plane = lambda k: pl.BlockSpec((None, tm, D_MODEL), lambda i: (k, i, 0))
    return [x1, route, moe, moe], [row(D_MODEL), row(ROUTE_PAD), plane(0), plane(1)]


def _x_sum(x_refs):
    if len(x_refs) == 1:
        return x_refs[0][...]
    x1_ref, route_ref, m0_ref, m1_ref = x_refs
    r = route_ref[...]
    return x1_ref[...] + r[:, 2:3] * m0_ref[...] + r[:, 3:4] * m1_ref[...]


def _inproj_kernel(*refs, n_x):
    x_refs, (g_ref, w_ref), outs = refs[:n_x], refs[n_x:n_x + 2], refs[n_x + 2:]
    hb = (_rms_scale(_x_sum(x_refs)) * g_ref[...]).astype(BF16)
    offs = (_OFF_U, _OFF_Q, _OFF_K, _OFF_V, _OFF_Z, _OFF_XBC, _OFF_DT)
    for o_ref, off in zip(outs, offs):
        width = o_ref.shape[1]
        o_ref[...] = _dot(hb, w_ref[:, off:off + width]).astype(o_ref.dtype)


def _inproj(xs, t, g, w_mix):
    tm = min(TM_PROJ, t)
    widths = (S5_WIDTH, ATTN_Q, ATTN_KV, ATTN_KV, SSD_WIDTH, SSD_CONV_CH, DT_PAD)
    dtypes = (F32, BF16, BF16, BF16, BF16, BF16, F32)
    x_arrs, x_specs = _x_parts(xs, tm)
    return pl.pallas_call(
        functools.partial(_inproj_kernel, n_x=len(x_arrs)),
        grid=(t // tm,),
        in_specs=x_specs + [_const_spec((1, D_MODEL)), _const_spec((D_MODEL, N_MIX_PAD))],
        out_specs=[pl.BlockSpec((tm, w), lambda i: (i, 0)) for w in widths],
        out_shape=[jax.ShapeDtypeStruct((t, w), d) for w, d in zip(widths, dtypes)],
        compiler_params=_cparams(("parallel",)),
        name="inproj",
    )(*x_arrs, g, w_mix)


def _s5_tables(lam_re, lam_im, b_re, b_im, c_re, c_im, log_dt):
    hp = lax.Precision.HIGHEST
    g_n, p_n = lam_re.shape
    c_n = b_re.shape[-1]
    t_n = S5_CHUNK
    lr, li = lam_re.astype(F32), lam_im.astype(F32)
    dt = jnp.exp(log_dt.astype(F32))[:, None]
    mag = jnp.exp(lr * dt)
    ab_re = mag * jnp.cos(li * dt)
    ab_im = mag * jnp.sin(li * dt)
    nr = ab_re - 1.0
    den = lr * lr + li * li
    f_re = (nr * lr + ab_im * li) / den
    f_im = (ab_im * lr - nr * li) / den
    br, bi = b_re.astype(F32), b_im.astype(F32)
    bb_re = f_re[..., None] * br - f_im[..., None] * bi
    bb_im = f_re[..., None] * bi + f_im[..., None] * br
    j = jnp.arange(t_n + 1, dtype=F32)[:, None, None]
    pmag = jnp.exp(lr * dt * j)
    ang = li * dt * j
    p_re = pmag * jnp.cos(ang)
    p_im = pmag * jnp.sin(ang)
    cr, ci = c_re.astype(F32), c_im.astype(F32)
    ca_re = cr[None] * p_re[:, :, None, :] - ci[None] * p_im[:, :, None, :]
    ca_im = cr[None] * p_im[:, :, None, :] + ci[None] * p_re[:, :, None, :]
    kern = (jnp.einsum('jgcp,gpd->jgcd', ca_re[:t_n], bb_re, precision=hp)
            - jnp.einsum('jgcp,gpd->jgcd', ca_im[:t_n], bb_im, precision=hp))
    tt = jnp.arange(t_n)
    lag = tt[None, :] - tt[:, None]
    kst = kern[jnp.clip(lag, 0, t_n - 1)]
    kst = jnp.where((lag >= 0)[:, :, None, None, None], kst, 0.0)
    q_re = p_re[t_n - 1 - tt]
    q_im = p_im[t_n - 1 - tt]
    bs_re = q_re[..., None] * bb_re[None] - q_im[..., None] * bb_im[None]
    bs_im = q_re[..., None] * bb_im[None] + q_im[..., None] * bb_re[None]

    nt = g_n // S5_TG
    eye = jnp.eye(S5_TG, dtype=F32)
    k6 = kst.reshape(t_n, t_n, nt, S5_TG, c_n, c_n).transpose(2, 0, 5, 1, 3, 4)
    big = k6[:, :, None] * eye[None, None, :, None, None, :, None]
    big = big.reshape(nt, t_n * S5_TG * c_n, t_n * S5_TG * c_n)

    def in_table(a):
        a = a.reshape(t_n, nt, S5_TG, p_n, c_n).transpose(1, 0, 4, 2, 3)
        a = a[:, :, None] * eye[None, None, :, None, :, None]
        return a.reshape(nt, t_n * S5_TG * c_n, S5_TG * p_n)

    def out_table(a):
        a = a.reshape(t_n, nt, S5_TG, c_n, p_n).transpose(1, 4, 0, 2, 3)
        a = a[:, None] * eye[None, :, None, None, :, None]
        return a.reshape(nt, S5_TG * p_n, t_n * S5_TG * c_n)

    return dict(
        big=big.astype(BF16),
        bs_re=in_table(bs_re).astype(BF16), bs_im=in_table(bs_im).astype(BF16),
        co_re=out_table(ca_re[1:]).astype(BF16), co_im=out_table(-ca_im[1:]).astype(BF16),
        at_re=p_re[t_n].reshape(nt, 1, S5_TG * p_n), at_im=p_im[t_n].reshape(nt, 1, S5_TG * p_n))


_S5_XW = S5_CHUNK * LANES
_S5_SW = S5_TG * S5_STATE


def _s5_kernel(u_ref, big_ref, bre_ref, bim_ref, cre_ref, cim_ref, are_ref, aim_ref, y_ref,
               sre, sim, hre, him):
    nck = u_ref.shape[0] // S5_CHUNK
    xcat = jnp.concatenate([u_ref[pl.ds(s, nck, stride=S5_CHUNK), :].astype(BF16) for s in range(S5_CHUNK)],
                           axis=1)
    sre[...] = _dot(xcat, bre_ref[0])
    sim[...] = _dot(xcat, bim_ref[0])
    a_re = are_ref[0]
    a_im = aim_ref[0]

    def step(i, carry):
        h_re, h_im = carry
        base = pl.multiple_of(i * SUBLANES, SUBLANES)
        s_re = sre[pl.ds(base, SUBLANES), :]
        s_im = sim[pl.ds(base, SUBLANES), :]
        ent_re, ent_im = [], []
        for j in range(SUBLANES):
            ent_re.append(h_re)
            ent_im.append(h_im)
            h_re, h_im = (a_re * h_re - a_im * h_im + s_re[j:j + 1], a_re * h_im + a_im * h_re + s_im[j:j + 1])
        hre[pl.ds(base, SUBLANES), :] = jnp.concatenate(ent_re, axis=0)
        him[pl.ds(base, SUBLANES), :] = jnp.concatenate(ent_im, axis=0)
        return h_re, h_im

    zero = jnp.zeros((1, _S5_SW), F32)
    lax.fori_loop(0, nck // SUBLANES, step, (zero, zero))
    y = (_dot(xcat, big_ref[0]) + _dot(hre[...].astype(BF16), cre_ref[0])
         + _dot(him[...].astype(BF16), cim_ref[0]))
    for t in range(S5_CHUNK):
        y_ref[pl.ds(t, nck, stride=S5_CHUNK), :] = y[:, t * LANES:(t + 1) * LANES]


def _s5_mixer(u, tabs, bsz):
    t = u.shape[0]
    seq = t // bsz
    nck = seq // S5_CHUNK
    nt = S5_GROUPS // S5_TG
    tab = lambda r, c: pl.BlockSpec((1, r, c), lambda x, b: (x, 0, 0))
    return pl.pallas_call(
        _s5_kernel,
        grid=(nt, bsz),
        in_specs=[pl.BlockSpec((seq, LANES), lambda x, b: (b, x)),
                  tab(_S5_XW, _S5_XW), tab(_S5_XW, _S5_SW), tab(_S5_XW, _S5_SW),
                  tab(_S5_SW, _S5_XW), tab(_S5_SW, _S5_XW), tab(1, _S5_SW), tab(1, _S5_SW)],
        out_specs=pl.BlockSpec((seq, LANES), lambda x, b: (b, x)),
        out_shape=jax.ShapeDtypeStruct((t, S5_WIDTH), F32),
        scratch_shapes=[pltpu.VMEM((nck, _S5_SW), F32)] * 4,
        compiler_params=_cparams(("arbitrary", "arbitrary")),
        name="s5_mixer",
    )(u, tabs['big'], tabs['bs_re'], tabs['bs_im'], tabs['co_re'], tabs['co_im'], tabs['at_re'], tabs['at_im'])


def _swap_rope_halves(y):
    w = y.shape[1]
    lane = lax.broadcasted_iota(jnp.int32, y.shape, 1)
    lower = (lane % HEAD_DIM) < (HEAD_DIM // 2)
    return jnp.where(lower, pltpu.roll(y, w - HEAD_DIM // 2, 1), pltpu.roll(y, HEAD_DIM // 2, 1))


def _norm_rope(x, gain, head_mean, cos2, sin2):
    reps = x.shape[1] // LANES
    ms = _dot_f32_lhs(x * x, head_mean)
    y = x * lax.rsqrt(ms + NORM_EPS) * gain
    if reps > 1:
        cos2 = jnp.concatenate([cos2] * reps, axis=1)
        sin2 = jnp.concatenate([sin2] * reps, axis=1)
    return y * cos2 + _swap_rope_halves(y) * sin2


def _attn_kernel(q_ref, kc_ref, kp_ref, vc_ref, vp_ref, cosc_ref, sinc_ref, cosp_ref, sinp_ref,
                 qg_ref, kg_ref, sink_ref, hmq_ref, hmk_ref, o_ref):
    seq_start = pl.program_id(1) == 0
    blk = ATTN_BLOCK
    nblk = q_ref.shape[0] // blk
    q = _norm_rope(q_ref[...].astype(F32), qg_ref[...], hmq_ref[...], cosc_ref[...], sinc_ref[...])
    q = (q * (HEAD_DIM ** -0.5)).astype(BF16)
    kc = _norm_rope(kc_ref[...].astype(F32), kg_ref[...], hmk_ref[...], cosc_ref[...], sinc_ref[...]).astype(BF16)
    kp = _norm_rope(kp_ref[...].astype(F32), kg_ref[...], hmk_ref[...], cosp_ref[...], sinp_ref[...]).astype(BF16)
    k_all = jnp.concatenate([kp, kc], axis=0)
    v_all = jnp.concatenate([vp_ref[...], vc_ref[...]], axis=0)
    shape = (ATTN_REP * blk, 2 * blk)
    qi = lax.broadcasted_iota(jnp.int32, shape, 0) % blk + blk
    kj = lax.broadcasted_iota(jnp.int32, shape, 1)
    band = (kj <= qi) & (qi - kj < blk)
    band_first = band & ((kj >= blk) | jnp.logical_not(seq_start))
    sinks = sink_ref[...]
    for j in range(ATTN_KV_HEADS):
        sl = slice(j * HEAD_DIM, (j + 1) * HEAD_DIM)
        heads = [j * ATTN_REP + r for r in range(ATTN_REP)]
        sink = jnp.concatenate([jnp.broadcast_to(sinks[:, h:h + 1], (blk, 1)) for h in heads], axis=0)
        for n in range(nblk):
            rows = slice(n * blk, (n + 1) * blk)
            kb = k_all[n * blk:(n + 2) * blk, sl]
            vb = v_all[n * blk:(n + 2) * blk, sl]
            q4 = jnp.concatenate([q[rows, h * HEAD_DIM:(h + 1) * HEAD_DIM] for h in heads], axis=0)
            s = lax.dot_general(q4, kb, (((1,), (1,)), ((), ())), preferred_element_type=F32)
            s = jnp.where(band_first if n == 0 else band, s, -jnp.inf)
            m = jnp.maximum(jnp.max(s, axis=-1, keepdims=True), sink)
            p = jnp.exp(s - m)
            denom = jnp.sum(p, axis=-1, keepdims=True) + jnp.exp(sink - m)
            out = _dot((p / denom).astype(BF16), vb)
            for r, h in enumerate(heads):
                o_ref[rows, h * HEAD_DIM:(h + 1) * HEAD_DIM] = out[r * blk:(r + 1) * blk].astype(o_ref.dtype)


def _rope_tables(seq):
    half = HEAD_DIM // 2
    inv = jnp.power(ROPE_THETA, -jnp.arange(half, dtype=F32) * 2.0 / HEAD_DIM)
    ang = jnp.arange(seq, dtype=F32)[:, None] * inv[None, :]
    cos, sin = jnp.cos(ang), jnp.sin(ang)
    cos2 = jnp.concatenate([cos, cos, cos, cos], axis=1)
    sin2 = jnp.concatenate([-sin, sin, -sin, sin], axis=1)
    return cos2, sin2


def _head_mean_matrix(width):
    i = jnp.arange(width)
    return jnp.where((i[:, None] // HEAD_DIM) == (i[None, :] // HEAD_DIM), 1.0 / HEAD_DIM, 0.0).astype(BF16)


def _swa_attention(q, k, v, q_g, k_g, sinks, bsz):
    t = q.shape[0]
    seq = t // bsz
    tile = min(ATTN_TILE, seq)
    nt = seq // tile
    bpt = tile // ATTN_BLOCK
    nb = seq // ATTN_BLOCK
    cos2, sin2 = _rope_tables(seq)
    qg = jnp.tile(q_g.astype(F32), ATTN_HEADS)[None]
    kg = jnp.tile(k_g.astype(F32), ATTN_KV_HEADS)[None]
    cur = lambda b, n: (b * nt + n, 0)
    prev = lambda b, n: (b * nb + jnp.maximum(n * bpt - 1, 0), 0)
    tcur = lambda b, n: (n, 0)
    tprev = lambda b, n: (jnp.maximum(n * bpt - 1, 0), 0)
    blk = ATTN_BLOCK
    return pl.pallas_call(
        _attn_kernel,
        grid=(bsz, nt),
        in_specs=[pl.BlockSpec((tile, ATTN_Q), cur),
                  pl.BlockSpec((tile, ATTN_KV), cur), pl.BlockSpec((blk, ATTN_KV), prev),
                  pl.BlockSpec((tile, ATTN_KV), cur), pl.BlockSpec((blk, ATTN_KV), prev),
                  pl.BlockSpec((tile, LANES), tcur), pl.BlockSpec((tile, LANES), tcur),
                  pl.BlockSpec((blk, LANES), tprev), pl.BlockSpec((blk, LANES), tprev),
                  _const_spec((1, ATTN_Q)), _const_spec((1, ATTN_KV)), _const_spec((1, ATTN_HEADS)),
                  _const_spec((ATTN_Q, ATTN_Q)), _const_spec((ATTN_KV, ATTN_KV))],
        out_specs=pl.BlockSpec((tile, ATTN_Q), cur),
        out_shape=jax.ShapeDtypeStruct((t, ATTN_Q), BF16),
        compiler_params=_cparams(("parallel", "parallel")),
        name="swa_attention",
    )(q, k, k, v, v, cos2, sin2, cos2, sin2, qg, kg, sinks.astype(F32)[None],
      _head_mean_matrix(ATTN_Q), _head_mean_matrix(ATTN_KV))


_CONV_TAIL = SUBLANES


def _ssd_kernel(z_ref, xbc_ref, dt_ref, cw_ref, cb_ref, dtb_ref, alog_ref, dskip_ref, ng_ref, exp_ref,
                o_ref, xext, state, ybuf):
    c = pl.program_id(1)
    ch = SSD_CHUNK

    @pl.when(c == 0)
    def _():
        xext[0:_CONV_TAIL, :] = jnp.zeros((_CONV_TAIL, SSD_CONV_CH), F32)
        state[...] = jnp.zeros_like(state)

    xext[_CONV_TAIL:_CONV_TAIL + ch, :] = xbc_ref[...].astype(F32)
    conv = cb_ref[...]
    for j in range(SSD_CONV):
        conv = conv + cw_ref[j:j + 1, :] * xext[pl.ds(_CONV_TAIL - (SSD_CONV - 1) + j, ch), :]
    xext[0:_CONV_TAIL, :] = xext[ch:ch + _CONV_TAIL, :]
    act = _silu(conv)
    xs = act[:, :SSD_WIDTH]
    bm = act[:, SSD_WIDTH:SSD_WIDTH + SSD_BC].astype(BF16)
    cm = act[:, SSD_WIDTH + SSD_BC:].astype(BF16)

    lane = lax.broadcasted_iota(jnp.int32, (ch, LANES), 1)
    xdt = dt_ref[...] + dtb_ref[...]
    dt = jnp.maximum(xdt, 0.0) + jnp.log1p(jnp.exp(-jnp.abs(xdt)))
    dt = jnp.where(lane < SSD_HEADS, dt, 0.0)
    a = dt * (-jnp.exp(alog_ref[...]))
    row = lax.broadcasted_iota(jnp.int32, (ch, ch), 0)
    col = lax.broadcasted_iota(jnp.int32, (ch, ch), 1)
    causal = row >= col
    hi, mid, lo = _split3(a)
    tril = causal.astype(BF16)
    cs = _dot(tril, hi) + _dot(tril, mid) + _dot(tril, lo)
    cs_t = cs.T
    expand = exp_ref[...]
    dt_x = _dot_f32_lhs(dt, expand)
    cs_x = _dot_f32_lhs(cs, expand)
    cs_last_x = cs_x[ch - 1:ch, :]
    xdt_full = xs * dt_x
    in_decay = jnp.exp(cs_x)
    out_decay = jnp.exp(cs_last_x - cs_x)
    chunk_decay = jnp.exp(cs_last_x)
    xw = (xdt_full * out_decay).astype(BF16)
    xdt_b = xdt_full.astype(BF16)
    hpg = SSD_HEADS // SSD_GROUPS
    for g in range(SSD_GROUPS):
        gs = slice(g * SSD_STATE, (g + 1) * SSD_STATE)
        ws = slice(g * SSD_GROUP_W, (g + 1) * SSD_GROUP_W)
        bg = bm[:, gs]
        cg = cm[:, gs]
        cb = lax.dot_general(cg, bg, (((1,), (1,)), ((), ())), preferred_element_type=F32)
        prev = state[g]
        ybuf[:, ws] = _dot(cg, prev.astype(BF16)) * in_decay[:, ws]
        for h in range(hpg):
            hh = g * hpg + h
            seg = cs[:, hh:hh + 1] - cs_t[hh:hh + 1, :]
            lmat = jnp.where(causal, jnp.exp(seg), 0.0)
            hs = slice(hh * SSD_HEAD_DIM, (hh + 1) * SSD_HEAD_DIM)
            ybuf[:, hs] += _dot((cb * lmat).astype(BF16), xdt_b[:, hs])
        upd = lax.dot_general(bg, xw[:, ws], (((0,), (0,)), ((), ())), preferred_element_type=F32)
        state[g] = prev * chunk_decay[:, ws] + upd

    y = ybuf[...] + xs * dskip_ref[...]
    y = y * _silu(z_ref[...].astype(F32))
    parts = []
    for g in range(SSD_GROUPS):
        ws = slice(g * SSD_GROUP_W, (g + 1) * SSD_GROUP_W)
        parts.append(_rms_scale(y[:, ws]))
    o_ref[...] = (jnp.concatenate(parts, axis=1) * ng_ref[...]).astype(o_ref.dtype)


def _ssd_mixer(z, xbc, dt_raw, conv_w, conv_b, dt_bias, a_log, d_skip, norm_g, bsz):
    t = z.shape[0]
    seq = t // bsz
    nc = seq // SSD_CHUNK
    pad = lambda v: jnp.pad(v.astype(F32), (0, LANES - SSD_HEADS))[None]
    d_x = jnp.repeat(d_skip.astype(F32), SSD_HEAD_DIM)[None]
    hid = jnp.arange(SSD_WIDTH) // SSD_HEAD_DIM
    expand = (jnp.arange(LANES)[:, None] == hid[None, :]).astype(BF16)
    blk = lambda b, c: (b * nc + c, 0)
    return pl.pallas_call(
        _ssd_kernel,
        grid=(bsz, nc),
        in_specs=[pl.BlockSpec((SSD_CHUNK, SSD_WIDTH), blk),
                  pl.BlockSpec((SSD_CHUNK, SSD_CONV_CH), blk),
                  pl.BlockSpec((SSD_CHUNK, DT_PAD), blk),
                  _const_spec((SSD_CONV, SSD_CONV_CH)), _const_spec((1, SSD_CONV_CH)),
                  _const_spec((1, LANES)), _const_spec((1, LANES)),
                  _const_spec((1, SSD_WIDTH)), _const_spec((1, SSD_WIDTH)),
                  _const_spec((LANES, SSD_WIDTH))],
        out_specs=pl.BlockSpec((SSD_CHUNK, SSD_WIDTH), blk),
        out_shape=jax.ShapeDtypeStruct((t, SSD_WIDTH), BF16),
        scratch_shapes=[pltpu.VMEM((SSD_CHUNK + 2 * _CONV_TAIL, SSD_CONV_CH), F32),
                        pltpu.VMEM((SSD_GROUPS, SSD_STATE, SSD_GROUP_W), F32),
                        pltpu.VMEM((SSD_CHUNK, SSD_WIDTH), F32)],
        compiler_params=_cparams(("arbitrary", "arbitrary")),
        name="ssd_mixer",
    )(z, xbc, dt_raw, conv_w.astype(F32), conv_b.astype(F32)[None], pad(dt_bias), pad(a_log),
      d_x, norm_g.astype(F32)[None], expand)


def _route(logits):
    lane = lax.broadcasted_iota(jnp.int32, logits.shape, 1).astype(F32)
    big = float(ROUTE_PAD)
    is_g = lane < MOE_GROUPS
    gl = jnp.where(is_g, logits, -jnp.inf)
    gmax = jnp.max(gl, axis=1, keepdims=True)
    gsel = jnp.min(jnp.where(gl == gmax, lane, big), axis=1, keepdims=True)
    pg = 1.0 / jnp.sum(jnp.where(is_g, jnp.exp(logits - gmax), 0.0), axis=1, keepdims=True)
    lo = MOE_GROUPS + MOE_PER_GROUP * gsel
    ev = jnp.where((lane >= lo) & (lane < lo + MOE_PER_GROUP), logits, -jnp.inf)
    m1 = jnp.max(ev, axis=1, keepdims=True)
    i1 = jnp.min(jnp.where(ev == m1, lane, big), axis=1, keepdims=True)
    ev2 = jnp.where(lane == i1, -jnp.inf, ev)
    m2 = jnp.max(ev2, axis=1, keepdims=True)
    i2 = jnp.min(jnp.where(ev2 == m2, lane, big), axis=1, keepdims=True)
    e21 = jnp.exp(m2 - m1)
    w1 = pg / (1.0 + e21)
    w2 = pg * e21 / (1.0 + e21)
    out = jnp.where(lane == 0, i1 - MOE_GROUPS, 0.0)
    out = jnp.where(lane == 1, i2 - MOE_GROUPS, out)
    out = jnp.where(lane == 2, w1, out)
    return jnp.where(lane == 3, w2, out)


def _merge_kernel(*refs, n_x):
    x_refs = refs[:n_x]
    (ys5_ref, u_ref, yb_ref, yc_ref, n1g_ref, wg_ref, bg_ref, d_ref, w1_ref, w2_ref, ps5_ref, pat_ref,
     pssd_ref, wo_ref, n2g_ref, wrh_ref, wrm_ref, br_ref, x1_o, h2_o, route_o) = refs[n_x:]
    x = _x_sum(x_refs)
    hb = (_rms_scale(x) * n1g_ref[...]).astype(BF16)
    ya = ys5_ref[...].astype(F32) + d_ref[...] * u_ref[...].astype(F32)
    yab = jax.nn.gelu(ya).astype(BF16)
    ya = _dot(yab, w1_ref[...]) * jax.nn.sigmoid(_dot(yab, w2_ref[...]))
    branches = ((ya.astype(BF16), ps5_ref), (yb_ref[...], pat_ref), (yc_ref[...], pssd_ref))
    merged = None
    for b, (yv, p_ref) in enumerate(branches):
        gate = jax.nn.sigmoid(_dot(hb, wg_ref[:, b * D_MODEL:(b + 1) * D_MODEL]) + bg_ref[b:b + 1, :])
        term = gate * _dot(yv, p_ref[...])
        merged = term if merged is None else merged + term
    x1 = x + _dot(merged.astype(BF16), wo_ref[...])
    x1_o[...] = x1
    h2 = _rms_scale(x1) * n2g_ref[...]
    h2_o[...] = h2.astype(h2_o.dtype)
    h_hi = h2.astype(BF16)
    h_mid = (h2 - h_hi.astype(F32)).astype(BF16)
    logits = _dot(h_hi, wrh_ref[...]) + _dot(h_hi, wrm_ref[...]) + _dot(h_mid, wrh_ref[...]) + br_ref[...]
    route_o[...] = _route(logits)


def _merge(xs, t, ys5, u, yb, yc, lw):
    tm = min(TM_PROJ, t)
    row = lambda w: pl.BlockSpec((tm, w), lambda i: (i, 0))
    consts = [lw['n1g'], lw['w_gate'], lw['b_gate'], lw['s5_d'], lw['glu_w1'], lw['glu_w2'], lw['p_s5'],
              lw['p_attn'], lw['p_ssd'], lw['w_out'], lw['n2g'], lw['w_router_hi'], lw['w_router_mid'],
              lw['b_router']]
    x_arrs, x_specs = _x_parts(xs, tm)
    return pl.pallas_call(
        functools.partial(_merge_kernel, n_x=len(x_arrs)),
        grid=(t // tm,),
        in_specs=x_specs + [row(S5_WIDTH), row(S5_WIDTH), row(ATTN_Q), row(SSD_WIDTH)]
                 + [_const_spec(c.shape) for c in consts],
        out_specs=[row(D_MODEL), row(D_MODEL), row(ROUTE_PAD)],
        out_shape=[jax.ShapeDtypeStruct((t, D_MODEL), F32), jax.ShapeDtypeStruct((t, D_MODEL), F32),
                   jax.ShapeDtypeStruct((t, ROUTE_PAD), F32)],
        compiler_params=_cparams(("parallel",)),
        name="merge_router",
    )(*x_arrs, ys5, u, yb, yc, *consts)


def _moe_plan(route, t):
    n_exp = MOE_EXPERTS
    e = route[:, 0:2].astype(jnp.int32).reshape(-1)
    onehot = (e[:, None] == jnp.arange(n_exp, dtype=jnp.int32)[None, :]).astype(jnp.int32)
    csum = jnp.cumsum(onehot, axis=0)
    rank = jnp.sum(csum * onehot, axis=1) - 1
    counts = csum[-1]
    padded = ((counts + TM_G - 1) // TM_G) * TM_G
    pend = jnp.cumsum(padded)
    dest = (pend - padded)[e] + rank
    n_tiles = -(-(2 * t + n_exp * (TM_G - 1)) // TM_G)
    n_rows = n_tiles * TM_G
    pair = jnp.full((n_rows,), -1, jnp.int32).at[dest].set(jnp.arange(2 * t, dtype=jnp.int32),
                                                            unique_indices=True)
    valid = pair >= 0
    tok, k = pair // 2, pair % 2
    src = jnp.where(valid, tok, 0)
    spare = t + jnp.arange(n_rows, dtype=jnp.int32) % _MOE_SPARE
    dst = jnp.where(valid, k * (t + _MOE_SPARE) + tok, spare)
    tile_start = jnp.arange(n_tiles, dtype=jnp.int32) * TM_G
    tile_e = jnp.sum((pend[None, :] <= tile_start[:, None]).astype(jnp.int32), axis=1)
    tile_e = jnp.minimum(tile_e, n_exp - 1)
    return src.reshape(n_tiles, 1, TM_G), dst.reshape(n_tiles, 1, TM_G), tile_e


_MOE_SPARE = 2 * TM_G


def _gmm_kernel(te_ref, src_ref, srcn_ref, dst_ref, h2_hbm, wg_ref, wu_ref, wd_ref, out_hbm,
                xbuf, ybuf, gsem, ssem):
    del te_ref
    i = pl.program_id(0)
    last = pl.num_programs(0) - 1
    slot = i % 2

    def gather_start(idx_ref, s):
        for r in range(TM_G):
            pltpu.make_async_copy(h2_hbm.at[pl.ds(idx_ref[0, 0, r], 1), :],
                                  xbuf.at[s, pl.ds(r, 1), :], gsem.at[s]).start()

    def gather_wait(s):
        pltpu.make_async_copy(h2_hbm.at[pl.ds(0, TM_G), :], xbuf.at[s], gsem.at[s]).wait()

    def scatter_wait(s):
        pltpu.make_async_copy(ybuf.at[s], out_hbm.at[pl.ds(0, TM_G), :], ssem.at[s]).wait()

    @pl.when(i == 0)
    def _():
        gather_start(src_ref, 0)
        ybuf[...] = jnp.zeros_like(ybuf)
        plane = out_hbm.shape[0] // 2
        fills = [pltpu.make_async_copy(ybuf.at[s],
                                       out_hbm.at[pl.ds(k * plane + plane - _MOE_SPARE + s * TM_G, TM_G), :],
                                       ssem.at[s]) for k in range(2) for s in range(2)]
        for f in fills:
            f.start()
        for f in fills:
            f.wait()

    gather_wait(slot)

    @pl.when(i >= 2)
    def _():
        scatter_wait(slot)

    gather_start(srcn_ref, 1 - slot)

    xb = xbuf[slot].astype(BF16)
    hid = _silu(_dot(xb, wg_ref[0].astype(BF16))) * _dot(xb, wu_ref[0].astype(BF16))
    ybuf[slot] = _dot(hid.astype(BF16), wd_ref[0].astype(BF16))
    for r in range(TM_G):
        pltpu.make_async_copy(ybuf.at[slot, pl.ds(r, 1), :],
                              out_hbm.at[pl.ds(dst_ref[0, 0, r], 1), :], ssem.at[slot]).start()

    @pl.when(i == last)
    def _():
        scatter_wait(slot)
        gather_wait(1 - slot)

    @pl.when((i == last) & (i >= 1))
    def _():
        scatter_wait(1 - slot)


def _moe_sparse(h2, route, w_gate, w_up, w_down):
    t = h2.shape[0]
    src, dst, tile_e = _moe_plan(route, t)
    n_tiles = src.shape[0]
    smem = lambda imap: pl.BlockSpec((1, 1, TM_G), imap, memory_space=pltpu.SMEM)
    grid_spec = pltpu.PrefetchScalarGridSpec(
        num_scalar_prefetch=1,
        grid=(n_tiles,),
        in_specs=[smem(lambda i, te: (i, 0, 0)),
                  smem(lambda i, te: (jnp.minimum(i + 1, n_tiles - 1), 0, 0)),
                  smem(lambda i, te: (i, 0, 0)),
                  pl.BlockSpec(memory_space=pl.ANY),
                  pl.BlockSpec((1, D_MODEL, MOE_FF), lambda i, te: (te[i], 0, 0)),
                  pl.BlockSpec((1, D_MODEL, MOE_FF), lambda i, te: (te[i], 0, 0)),
                  pl.BlockSpec((1, MOE_FF, D_MODEL), lambda i, te: (te[i], 0, 0))],
        out_specs=pl.BlockSpec(memory_space=pl.ANY),
        scratch_shapes=[pltpu.VMEM((2, TM_G, D_MODEL), F32), pltpu.VMEM((2, TM_G, D_MODEL), F32),
                        pltpu.SemaphoreType.DMA((2,)), pltpu.SemaphoreType.DMA((2,))])
    plane = t + _MOE_SPARE
    out = pl.pallas_call(
        _gmm_kernel,
        grid_spec=grid_spec,
        out_shape=jax.ShapeDtypeStruct((2 * plane, D_MODEL), F32),
        compiler_params=_cparams(("arbitrary",)),
        name="moe_gmm",
    )(tile_e, src, src, dst, h2, w_gate, w_up, w_down)
    return out.reshape(2, plane, D_MODEL)


def _sum_kernel(*refs):
    refs[-1][...] = _x_sum(refs[:-1])


def _residual_sum(xs, t):
    tm = min(TM_PROJ, t)
    x_arrs, x_specs = _x_parts(xs, tm)
    return pl.pallas_call(
        _sum_kernel,
        grid=(t // tm,),
        in_specs=x_specs,
        out_specs=pl.BlockSpec((tm, D_MODEL), lambda i: (i, 0)),
        out_shape=jax.ShapeDtypeStruct((t, D_MODEL), F32),
        compiler_params=_cparams(("parallel",)),
        name="residual_sum",
    )(*x_arrs)


def _layer_weights(l, p):
    w_in = p['w_in'][l]
    w_mix = jnp.pad(w_in[:, :N_MIX], ((0, 0), (0, N_MIX_PAD - N_MIX))).astype(BF16)
    w_router = jnp.concatenate([p['w_router_group'][l], p['w_router_expert'][l]], axis=1).astype(F32)
    npad = ROUTE_PAD - w_router.shape[1]
    b_router = jnp.concatenate([p['b_router_group'][l], p['b_router_expert'][l]]).astype(F32)
    w_router = jnp.pad(w_router, ((0, 0), (0, npad)))
    w_router_hi = w_router.astype(BF16)
    return dict(
        n1g=p['norm1_g'][l].astype(F32)[None], w_mix=w_mix,
        w_gate=w_in[:, N_MIX:].astype(BF16), b_gate=p['b_gate'][l].astype(F32),
        s5_d=p['s5_d'][l].astype(F32)[None],
        glu_w1=p['s5_glu_w1'][l].astype(BF16), glu_w2=p['s5_glu_w2'][l].astype(BF16),
        p_s5=p['p_s5'][l].astype(BF16), p_attn=p['p_attn'][l].astype(BF16), p_ssd=p['p_ssd'][l].astype(BF16),
        w_out=p['w_out'][l].astype(BF16), n2g=p['norm2_g'][l].astype(F32)[None],
        w_router_hi=w_router_hi, w_router_mid=(w_router - w_router_hi.astype(F32)).astype(BF16),
        b_router=jnp.pad(b_router, (0, npad))[None])


def _layer(xs, t, l, p, bsz):
    lw = _layer_weights(l, p)
    u, q, k, v, z, xbc, dt_raw = _inproj(xs, t, lw['n1g'], lw['w_mix'])
    tabs = _s5_tables(p['s5_lambda_re'][l], p['s5_lambda_im'][l], p['s5_b_re'][l], p['s5_b_im'][l],
                      p['s5_c_re'][l], p['s5_c_im'][l], p['s5_log_dt'][l])
    ys5 = _s5_mixer(u, tabs, bsz)
    yb = _swa_attention(q, k, v, p['q_norm_g'][l], p['k_norm_g'][l], p['attn_sinks'][l], bsz)
    yc = _ssd_mixer(z, xbc, dt_raw, p['ssd_conv_w'][l], p['ssd_conv_b'][l], p['ssd_dt_bias'][l],
                    p['ssd_a_log'][l], p['ssd_d'][l], p['ssd_norm_g'][l], bsz)
    x1, h2, route = _merge(xs, t, ys5, u, yb, yc, lw)
    moe = _moe_sparse(h2, route, p['w_exp_gate'][l], p['w_exp_up'][l], p['w_exp_down'][l])
    return [x1, route, moe]


def kernel(x, norm1_g, w_in, b_gate, s5_lambda_re, s5_lambda_im, s5_b_re, s5_b_im, s5_c_re, s5_c_im, s5_d, s5_log_dt, s5_glu_w1, s5_glu_w2, q_norm_g, k_norm_g, attn_sinks, ssd_conv_w, ssd_conv_b, ssd_dt_bias, ssd_a_log, ssd_d, ssd_norm_g, p_s5, p_attn, p_ssd, w_out, norm2_g, w_router_group, b_router_group, w_router_expert, b_router_expert, w_exp_gate, w_exp_up, w_exp_down):
    p = dict(norm1_g=norm1_g, w_in=w_in, b_gate=b_gate, s5_lambda_re=s5_lambda_re, s5_lambda_im=s5_lambda_im,
             s5_b_re=s5_b_re, s5_b_im=s5_b_im, s5_c_re=s5_c_re, s5_c_im=s5_c_im, s5_d=s5_d,
             s5_log_dt=s5_log_dt, s5_glu_w1=s5_glu_w1, s5_glu_w2=s5_glu_w2, q_norm_g=q_norm_g,
             k_norm_g=k_norm_g, attn_sinks=attn_sinks, ssd_conv_w=ssd_conv_w, ssd_conv_b=ssd_conv_b,
             ssd_dt_bias=ssd_dt_bias, ssd_a_log=ssd_a_log, ssd_d=ssd_d, ssd_norm_g=ssd_norm_g, p_s5=p_s5,
             p_attn=p_attn, p_ssd=p_ssd, w_out=w_out, norm2_g=norm2_g, w_router_group=w_router_group,
             b_router_group=b_router_group, w_router_expert=w_router_expert, b_router_expert=b_router_expert,
             w_exp_gate=w_exp_gate, w_exp_up=w_exp_up, w_exp_down=w_exp_down)
    bsz, seq, dm = x.shape
    depth = w_in.shape[0]
    t = bsz * seq
    xs = [x.reshape(t, dm)]
    for l in range(depth):
        xs = _layer(xs, t, l, p, bsz)
    return _residual_sum(xs, t).reshape(bsz, seq, dm)
```

```python
import functools
import math

import jax
import jax.numpy as jnp
from jax import lax
from jax.experimental import pallas as pl
from jax.experimental.pallas import tpu as pltpu

F32 = jnp.float32
BF16 = jnp.bfloat16

D_MODEL = 1024
NORM_EPS = 1e-6
S5_WIDTH = 512
S5_GROUP = 16
S5_GROUPS = 32
S5_STATE = 64
HEAD_DIM = 64
ATTN_HEADS = 8
ATTN_KV_HEADS = 2
ATTN_REP = ATTN_HEADS // ATTN_KV_HEADS
ATTN_Q = ATTN_HEADS * HEAD_DIM
ATTN_KV = ATTN_KV_HEADS * HEAD_DIM
ATTN_BLOCK = 128
ROPE_THETA = 10000.0
SSD_WIDTH = 1024
SSD_HEAD_DIM = 64
SSD_HEADS = 16
SSD_GROUPS = 2
SSD_STATE = 64
SSD_CONV = 4
SSD_CHUNK = 128
SSD_BC = SSD_GROUPS * SSD_STATE
SSD_CONV_CH = SSD_WIDTH + 2 * SSD_BC
SSD_GROUP_W = SSD_WIDTH // SSD_GROUPS
N_BRANCH = 3
MOE_GROUPS = 4
MOE_PER_GROUP = 8
MOE_EXPERTS = 32
MOE_FF = 512
N_MIX = S5_WIDTH + ATTN_Q + 2 * ATTN_KV + SSD_WIDTH + SSD_CONV_CH + SSD_HEADS

LANES = 128
SUBLANES = 8
VMEM_LIMIT_BYTES = 56 * 1024 * 1024

S5_CHUNK = 16
S5_TG = 8
ATTN_TILE = 512
TM_PROJ = 512
TM_G = 256
DT_PAD = LANES
N_MIX_PAD = N_MIX - SSD_HEADS + DT_PAD
ROUTE_PAD = LANES


def _cparams(semantics):
    return pltpu.CompilerParams(dimension_semantics=semantics, vmem_limit_bytes=VMEM_LIMIT_BYTES)


def _const_spec(shape):
    zeros = (0,) * len(shape)
    return pl.BlockSpec(shape, lambda *_: zeros, pipeline_mode=pl.Buffered(1))


def _dot(a, b):
    return jnp.dot(a, b, preferred_element_type=F32)


def _split3(a):
    hi = a.astype(BF16)
    r1 = a - hi.astype(F32)
    mid = r1.astype(BF16)
    lo = (r1 - mid.astype(F32)).astype(BF16)
    return hi, mid, lo


def _dot_f32_lhs(a, b_bf16):
    hi, mid, lo = _split3(a)
    return _dot(hi, b_bf16) + _dot(mid, b_bf16) + _dot(lo, b_bf16)


def _rms_scale(x):
    return x * lax.rsqrt(jnp.mean(x * x, axis=-1, keepdims=True) + NORM_EPS)


def _silu(x):
    return x * jax.nn.sigmoid(x)


_OFF_U = 0
_OFF_Q = _OFF_U + S5_WIDTH
_OFF_K = _OFF_Q + ATTN_Q
_OFF_V = _OFF_K + ATTN_KV
_OFF_Z = _OFF_V + ATTN_KV
_OFF_XBC = _OFF_Z + SSD_WIDTH
_OFF_DT = _OFF_XBC + SSD_CONV_CH


def _x_parts(xs, tm):
    row = lambda w: pl.BlockSpec((tm, w), lambda i: (i, 0))
    if len(xs) == 1:
        return list(xs), [row(D_MODEL)]
    x1, route, moe = xs
    plane = lambda k: pl.BlockSpec((None, tm, D_MODEL), lambda i: (k, i, 0))
    return [x1, route, moe, moe], [row(D_MODEL), row(ROUTE_PAD), plane(0), plane(1)]


def _x_sum(x_refs):
    if len(x_refs) == 1:
        return x_refs[0][...]
    x1_ref, route_ref, m0_ref, m1_ref = x_refs
    r = route_ref[...]
    return x1_ref[...] + r[:, 2:3] * m0_ref[...] + r[:, 3:4] * m1_ref[...]


def _inproj_kernel(*refs, n_x):
    x_refs, (g_ref, w_ref), outs, wb = refs[:n_x], refs[n_x:n_x + 2], refs[n_x + 2:-1], refs[-1]

    @pl.when(pl.program_id(0) == 0)
    def _():
        wb[...] = w_ref[...].astype(BF16)

    hb = (_rms_scale(_x_sum(x_refs)) * g_ref[...]).astype(BF16)
    offs = (_OFF_U, _OFF_Q, _OFF_K, _OFF_V, _OFF_Z, _OFF_XBC, _OFF_DT)
    for o_ref, off in zip(outs, offs):
        width = o_ref.shape[1]
        o_ref[...] = _dot(hb, wb[:, off:off + width]).astype(o_ref.dtype)


def _inproj(xs, t, g, w_in, l):
    tm = min(TM_PROJ, t)
    widths = (S5_WIDTH, ATTN_Q, ATTN_KV, ATTN_KV, SSD_WIDTH, SSD_CONV_CH, DT_PAD)
    dtypes = (F32, BF16, BF16, BF16, BF16, BF16, F32)
    x_arrs, x_specs = _x_parts(xs, tm)
    w_spec = pl.BlockSpec((None, D_MODEL, N_MIX_PAD), lambda i: (l, 0, 0), pipeline_mode=pl.Buffered(1))
    return pl.pallas_call(
        functools.partial(_inproj_kernel, n_x=len(x_arrs)),
        grid=(t // tm,),
        in_specs=x_specs + [_const_spec((1, D_MODEL)), w_spec],
        out_specs=[pl.BlockSpec((tm, w), lambda i: (i, 0)) for w in widths],
        out_shape=[jax.ShapeDtypeStruct((t, w), d) for w, d in zip(widths, dtypes)],
        scratch_shapes=[pltpu.VMEM((D_MODEL, N_MIX_PAD), BF16)],
        compiler_params=_cparams(("arbitrary",)),
        name="inproj",
    )(*x_arrs, g, w_in)


def _s5_tables(lam_re, lam_im, b_re, b_im, c_re, c_im, log_dt):
    hp = lax.Precision.HIGHEST
    g_n, p_n = lam_re.shape
    c_n = b_re.shape[-1]
    t_n = S5_CHUNK
    lr, li = lam_re.astype(F32), lam_im.astype(F32)
    dt = jnp.exp(log_dt.astype(F32))[:, None]
    mag = jnp.exp(lr * dt)
    ab_re = mag * jnp.cos(li * dt)
    ab_im = mag * jnp.sin(li * dt)
    nr = ab_re - 1.0
    den = lr * lr + li * li
    f_re = (nr * lr + ab_im * li) / den
    f_im = (ab_im * lr - nr * li) / den
    br, bi = b_re.astype(F32), b_im.astype(F32)
    bb_re = f_re[..., None] * br - f_im[..., None] * bi
    bb_im = f_re[..., None] * bi + f_im[..., None] * br
    j = jnp.arange(t_n + 1, dtype=F32)[:, None, None]
    pmag = jnp.exp(lr * dt * j)
    ang = li * dt * j
    p_re = pmag * jnp.cos(ang)
    p_im = pmag * jnp.sin(ang)
    cr, ci = c_re.astype(F32), c_im.astype(F32)
    ca_re = cr[None] * p_re[:, :, None, :] - ci[None] * p_im[:, :, None, :]
    ca_im = cr[None] * p_im[:, :, None, :] + ci[None] * p_re[:, :, None, :]
    kern = (jnp.einsum('jgcp,gpd->jgcd', ca_re[:t_n], bb_re, precision=hp)
            - jnp.einsum('jgcp,gpd->jgcd', ca_im[:t_n], bb_im, precision=hp))
    q_re = p_re[t_n - 1 - jnp.arange(t_n)]
    q_im = p_im[t_n - 1 - jnp.arange(t_n)]
    bs_re = q_re[..., None] * bb_re[None] - q_im[..., None] * bb_im[None]
    bs_im = q_re[..., None] * bb_im[None] + q_im[..., None] * bb_re[None]

    nt = g_n // S5_TG
    k_strip = kern.reshape(t_n, nt, S5_TG, c_n, c_n).transpose(1, 0, 4, 2, 3)
    k_strip = k_strip.reshape(nt, t_n, c_n, S5_TG * c_n)

    def in_strip(a):
        a = a.reshape(t_n, nt, S5_TG, p_n, c_n).transpose(1, 0, 4, 2, 3)
        return a.reshape(nt, t_n, c_n, S5_TG * p_n)

    def out_strip(a):
        a = a.reshape(t_n, nt, S5_TG, c_n, p_n).transpose(1, 4, 0, 2, 3)
        return a.reshape(nt, p_n, t_n * S5_TG * c_n)

    return dict(
        k=k_strip, bs_re=in_strip(bs_re), bs_im=in_strip(bs_im),
        co_re=out_strip(ca_re[1:]), co_im=out_strip(-ca_im[1:]),
        at_re=p_re[t_n].reshape(nt, 1, S5_TG * p_n), at_im=p_im[t_n].reshape(nt, 1, S5_TG * p_n))


_S5_XW = S5_CHUNK * LANES
_S5_SW = S5_TG * S5_STATE


def _group_block(strip, lanes_per_group):
    rpg = strip.shape[0]
    full = jnp.concatenate([strip] * S5_TG, axis=0)
    row_g = lax.broadcasted_iota(jnp.int32, full.shape, 0) // rpg
    lane_g = (lax.broadcasted_iota(jnp.int32, full.shape, 1) % (S5_TG * lanes_per_group)) // lanes_per_group
    return jnp.where(row_g == lane_g, full, 0.0).astype(BF16)


def _s5_build_tables(k_ref, bre_ref, bim_ref, cre_ref, cim_ref, big, bsre, bsim, core, coim):
    blocks = [_group_block(k_ref[0, j], S5_GROUP) for j in range(S5_CHUNK)]
    zero = jnp.zeros((LANES, LANES), BF16)
    for s in range(S5_CHUNK):
        for t in range(S5_CHUNK):
            big[s * LANES:(s + 1) * LANES, t * LANES:(t + 1) * LANES] = blocks[t - s] if t >= s else zero
    for src, dst in ((bre_ref, bsre), (bim_ref, bsim)):
        for s in range(S5_CHUNK):
            dst[s * LANES:(s + 1) * LANES, :] = _group_block(src[0, s], S5_STATE)
    for src, dst in ((cre_ref, core), (cim_ref, coim)):
        for t in range(S5_CHUNK):
            cols = slice(t * LANES, (t + 1) * LANES)
            dst[:, cols] = _group_block(src[0, :, cols], S5_GROUP)


def _s5_kernel(u_ref, k_ref, bre_ref, bim_ref, cre_ref, cim_ref, are_ref, aim_ref, y_ref,
               big, bsre, bsim, core, coim, sre, sim, hre, him):
    @pl.when(pl.program_id(1) == 0)
    def _():
        _s5_build_tables(k_ref, bre_ref, bim_ref, cre_ref, cim_ref, big, bsre, bsim, core, coim)

    nck = u_ref.shape[0] // S5_CHUNK
    xcat = jnp.concatenate([u_ref[pl.ds(s, nck, stride=S5_CHUNK), :].astype(BF16) for s in range(S5_CHUNK)],
                           axis=1)
    sre[...] = _dot(xcat, bsre[...])
    sim[...] = _dot(xcat, bsim[...])
    a_re = are_ref[0]
    a_im = aim_ref[0]

    def step(i, carry):
        h_re, h_im = carry
        base = pl.multiple_of(i * SUBLANES, SUBLANES)
        s_re = sre[pl.ds(base, SUBLANES), :]
        s_im = sim[pl.ds(base, SUBLANES), :]
        ent_re, ent_im = [], []
        for j in range(SUBLANES):
            ent_re.append(h_re)
            ent_im.append(h_im)
            h_re, h_im = (a_re * h_re - a_im * h_im + s_re[j:j + 1], a_re * h_im + a_im * h_re + s_im[j:j + 1])
        hre[pl.ds(base, SUBLANES), :] = jnp.concatenate(ent_re, axis=0)
        him[pl.ds(base, SUBLANES), :] = jnp.concatenate(ent_im, axis=0)
        return h_re, h_im

    zero = jnp.zeros((1, _S5_SW), F32)
    lax.fori_loop(0, nck // SUBLANES, step, (zero, zero))
    y = (_dot(xcat, big[...]) + _dot(hre[...].astype(BF16), core[...])
         + _dot(him[...].astype(BF16), coim[...]))
    for t in range(S5_CHUNK):
        y_ref[pl.ds(t, nck, stride=S5_CHUNK), :] = y[:, t * LANES:(t + 1) * LANES]


def _s5_mixer(u, tabs, bsz):
    t = u.shape[0]
    seq = t // bsz
    nck = seq // S5_CHUNK
    nt = S5_GROUPS // S5_TG
    strip = lambda a: pl.BlockSpec((1,) + a.shape[1:], lambda x, b: (x,) + (0,) * (a.ndim - 1))
    strips = [tabs['k'], tabs['bs_re'], tabs['bs_im'], tabs['co_re'], tabs['co_im'], tabs['at_re'], tabs['at_im']]
    return pl.pallas_call(
        _s5_kernel,
        grid=(nt, bsz),
        in_specs=[pl.BlockSpec((seq, LANES), lambda x, b: (b, x))] + [strip(a) for a in strips],
        out_specs=pl.BlockSpec((seq, LANES), lambda x, b: (b, x)),
        out_shape=jax.ShapeDtypeStruct((t, S5_WIDTH), F32),
        scratch_shapes=[pltpu.VMEM((_S5_XW, _S5_XW), BF16),
                        pltpu.VMEM((_S5_XW, _S5_SW), BF16), pltpu.VMEM((_S5_XW, _S5_SW), BF16),
                        pltpu.VMEM((_S5_SW, _S5_XW), BF16), pltpu.VMEM((_S5_SW, _S5_XW), BF16)]
                       + [pltpu.VMEM((nck, _S5_SW), F32)] * 4,
        compiler_params=_cparams(("arbitrary", "arbitrary")),
        name="s5_mixer",
    )(u, *strips)


def _swap_rope_halves(y):
    w = y.shape[1]
    lane = lax.broadcasted_iota(jnp.int32, y.shape, 1)
    lower = (lane % HEAD_DIM) < (HEAD_DIM // 2)
    return jnp.where(lower, pltpu.roll(y, w - HEAD_DIM // 2, 1), pltpu.roll(y, HEAD_DIM // 2, 1))


def _norm_rope(x, gain, head_mean, cos2, sin2):
    reps = x.shape[1] // LANES
    ms = _dot_f32_lhs(x * x, head_mean)
    y = x * lax.rsqrt(ms + NORM_EPS) * gain
    if reps > 1:
        cos2 = jnp.concatenate([cos2] * reps, axis=1)
        sin2 = jnp.concatenate([sin2] * reps, axis=1)
    return y * cos2 + _swap_rope_halves(y) * sin2


def _attn_kernel(q_ref, kc_ref, kp_ref, vc_ref, vp_ref, cosc_ref, sinc_ref, cosp_ref, sinp_ref,
                 qg_ref, kg_ref, sink_ref, hmq_ref, hmk_ref, o_ref):
    seq_start = pl.program_id(1) == 0
    blk = ATTN_BLOCK
    nblk = q_ref.shape[0] // blk
    q = _norm_rope(q_ref[...].astype(F32), qg_ref[...], hmq_ref[...], cosc_ref[...], sinc_ref[...])
    q = (q * (HEAD_DIM ** -0.5)).astype(BF16)
    kc = _norm_rope(kc_ref[...].astype(F32), kg_ref[...], hmk_ref[...], cosc_ref[...], sinc_ref[...]).astype(BF16)
    kp = _norm_rope(kp_ref[...].astype(F32), kg_ref[...], hmk_ref[...], cosp_ref[...], sinp_ref[...]).astype(BF16)
    k_all = jnp.concatenate([kp, kc], axis=0)
    v_all = jnp.concatenate([vp_ref[...], vc_ref[...]], axis=0)
    shape = (ATTN_REP * blk, 2 * blk)
    qi = lax.broadcasted_iota(jnp.int32, shape, 0) % blk + blk
    kj = lax.broadcasted_iota(jnp.int32, shape, 1)
    band = (kj <= qi) & (qi - kj < blk)
    band_first = band & ((kj >= blk) | jnp.logical_not(seq_start))
    sinks = sink_ref[...]
    for j in range(ATTN_KV_HEADS):
        sl = slice(j * HEAD_DIM, (j + 1) * HEAD_DIM)
        heads = [j * ATTN_REP + r for r in range(ATTN_REP)]
        sink = jnp.concatenate([jnp.broadcast_to(sinks[:, h:h + 1], (blk, 1)) for h in heads], axis=0)
        for n in range(nblk):
            rows = slice(n * blk, (n + 1) * blk)
            kb = k_all[n * blk:(n + 2) * blk, sl]
            vb = v_all[n * blk:(n + 2) * blk, sl]
            q4 = jnp.concatenate([q[rows, h * HEAD_DIM:(h + 1) * HEAD_DIM] for h in heads], axis=0)
            s = lax.dot_general(q4, kb, (((1,), (1,)), ((), ())), preferred_element_type=F32)
            s = jnp.where(band_first if n == 0 else band, s, -jnp.inf)
            m = jnp.maximum(jnp.max(s, axis=-1, keepdims=True), sink)
            p = jnp.exp(s - m)
            denom = jnp.sum(p, axis=-1, keepdims=True) + jnp.exp(sink - m)
            out = _dot((p / denom).astype(BF16), vb)
            for r, h in enumerate(heads):
                o_ref[rows, h * HEAD_DIM:(h + 1) * HEAD_DIM] = out[r * blk:(r + 1) * blk].astype(o_ref.dtype)


def _rope_tables(seq):
    half = HEAD_DIM // 2
    inv = jnp.power(ROPE_THETA, -jnp.arange(half, dtype=F32) * 2.0 / HEAD_DIM)
    ang = jnp.arange(seq, dtype=F32)[:, None] * inv[None, :]
    cos, sin = jnp.cos(ang), jnp.sin(ang)
    cos2 = jnp.concatenate([cos, cos, cos, cos], axis=1)
    sin2 = jnp.concatenate([-sin, sin, -sin, sin], axis=1)
    return cos2, sin2


def _head_mean_matrix(width):
    i = jnp.arange(width)
    return jnp.where((i[:, None] // HEAD_DIM) == (i[None, :] // HEAD_DIM), 1.0 / HEAD_DIM, 0.0).astype(BF16)


def _swa_attention(q, k, v, q_g, k_g, sinks, bsz):
    t = q.shape[0]
    seq = t // bsz
    tile = min(ATTN_TILE, seq)
    nt = seq // tile
    bpt = tile // ATTN_BLOCK
    nb = seq // ATTN_BLOCK
    cos2, sin2 = _rope_tables(seq)
    qg = jnp.tile(q_g.astype(F32), ATTN_HEADS)[None]
    kg = jnp.tile(k_g.astype(F32), ATTN_KV_HEADS)[None]
    cur = lambda b, n: (b * nt + n, 0)
    prev = lambda b, n: (b * nb + jnp.maximum(n * bpt - 1, 0), 0)
    tcur = lambda b, n: (n, 0)
    tprev = lambda b, n: (jnp.maximum(n * bpt - 1, 0), 0)
    blk = ATTN_BLOCK
    return pl.pallas_call(
        _attn_kernel,
        grid=(bsz, nt),
        in_specs=[pl.BlockSpec((tile, ATTN_Q), cur),
                  pl.BlockSpec((tile, ATTN_KV), cur), pl.BlockSpec((blk, ATTN_KV), prev),
                  pl.BlockSpec((tile, ATTN_KV), cur), pl.BlockSpec((blk, ATTN_KV), prev),
                  pl.BlockSpec((tile, LANES), tcur), pl.BlockSpec((tile, LANES), tcur),
                  pl.BlockSpec((blk, LANES), tprev), pl.BlockSpec((blk, LANES), tprev),
                  _const_spec((1, ATTN_Q)), _const_spec((1, ATTN_KV)), _const_spec((1, ATTN_HEADS)),
                  _const_spec((ATTN_Q, ATTN_Q)), _const_spec((ATTN_KV, ATTN_KV))],
        out_specs=pl.BlockSpec((tile, ATTN_Q), cur),
        out_shape=jax.ShapeDtypeStruct((t, ATTN_Q), BF16),
        compiler_params=_cparams(("parallel", "parallel")),
        name="swa_attention",
    )(q, k, k, v, v, cos2, sin2, cos2, sin2, qg, kg, sinks.astype(F32)[None],
      _head_mean_matrix(ATTN_Q), _head_mean_matrix(ATTN_KV))


_CONV_TAIL = SUBLANES


def _ssd_kernel(z_ref, xbc_ref, dt_ref, cw_ref, cb_ref, dtb_ref, alog_ref, dskip_ref, ng_ref, exp_ref,
                o_ref, xext, state, ybuf):
    c = pl.program_id(1)
    ch = SSD_CHUNK

    @pl.when(c == 0)
    def _():
        xext[0:_CONV_TAIL, :] = jnp.zeros((_CONV_TAIL, SSD_CONV_CH), F32)
        state[...] = jnp.zeros_like(state)

    xext[_CONV_TAIL:_CONV_TAIL + ch, :] = xbc_ref[...].astype(F32)
    conv = cb_ref[...]
    for j in range(SSD_CONV):
        conv = conv + cw_ref[j:j + 1, :] * xext[pl.ds(_CONV_TAIL - (SSD_CONV - 1) + j, ch), :]
    xext[0:_CONV_TAIL, :] = xext[ch:ch + _CONV_TAIL, :]
    act = _silu(conv)
    xs = act[:, :SSD_WIDTH]
    bm = act[:, SSD_WIDTH:SSD_WIDTH + SSD_BC].astype(BF16)
    cm = act[:, SSD_WIDTH + SSD_BC:].astype(BF16)

    lane = lax.broadcasted_iota(jnp.int32, (ch, LANES), 1)
    xdt = dt_ref[...] + dtb_ref[...]
    dt = jnp.maximum(xdt, 0.0) + jnp.log1p(jnp.exp(-jnp.abs(xdt)))
    dt = jnp.where(lane < SSD_HEADS, dt, 0.0)
    a = dt * (-jnp.exp(alog_ref[...]))
    row = lax.broadcasted_iota(jnp.int32, (ch, ch), 0)
    col = lax.broadcasted_iota(jnp.int32, (ch, ch), 1)
    causal = row >= col
    hi, mid, lo = _split3(a)
    tril = causal.astype(BF16)
    cs = _dot(tril, hi) + _dot(tril, mid) + _dot(tril, lo)
    cs_t = cs.T
    expand = exp_ref[...]
    dt_x = _dot_f32_lhs(dt, expand)
    cs_x = _dot_f32_lhs(cs, expand)
    cs_last_x = cs_x[ch - 1:ch, :]
    xdt_full = xs * dt_x
    in_decay = jnp.exp(cs_x)
    out_decay = jnp.exp(cs_last_x - cs_x)
    chunk_decay = jnp.exp(cs_last_x)
    xw = (xdt_full * out_decay).astype(BF16)
    xdt_b = xdt_full.astype(BF16)
    hpg = SSD_HEADS // SSD_GROUPS
    for g in range(SSD_GROUPS):
        gs = slice(g * SSD_STATE, (g + 1) * SSD_STATE)
        ws = slice(g * SSD_GROUP_W, (g + 1) * SSD_GROUP_W)
        bg = bm[:, gs]
        cg = cm[:, gs]
        cb = lax.dot_general(cg, bg, (((1,), (1,)), ((), ())), preferred_element_type=F32)
        prev = state[g]
        ybuf[:, ws] = _dot(cg, prev.astype(BF16)) * in_decay[:, ws]
        for h in range(hpg):
            hh = g * hpg + h
            seg = cs[:, hh:hh + 1] - cs_t[hh:hh + 1, :]
            lmat = jnp.where(causal, jnp.exp(seg), 0.0)
            hs = slice(hh * SSD_HEAD_DIM, (hh + 1) * SSD_HEAD_DIM)
            ybuf[:, hs] += _dot((cb * lmat).astype(BF16), xdt_b[:, hs])
        upd = lax.dot_general(bg, xw[:, ws], (((0,), (0,)), ((), ())), preferred_element_type=F32)
        state[g] = prev * chunk_decay[:, ws] + upd

    y = ybuf[...] + xs * dskip_ref[...]
    y = y * _silu(z_ref[...].astype(F32))
    parts = []
    for g in range(SSD_GROUPS):
        ws = slice(g * SSD_GROUP_W, (g + 1) * SSD_GROUP_W)
        parts.append(_rms_scale(y[:, ws]))
    o_ref[...] = (jnp.concatenate(parts, axis=1) * ng_ref[...]).astype(o_ref.dtype)


def _ssd_mixer(z, xbc, dt_raw, conv_w, conv_b, dt_bias, a_log, d_skip, norm_g, bsz):
    t = z.shape[0]
    seq = t // bsz
    nc = seq // SSD_CHUNK
    pad = lambda v: jnp.pad(v.astype(F32), (0, LANES - SSD_HEADS))[None]
    d_x = jnp.repeat(d_skip.astype(F32), SSD_HEAD_DIM)[None]
    hid = jnp.arange(SSD_WIDTH) // SSD_HEAD_DIM
    expand = (jnp.arange(LANES)[:, None] == hid[None, :]).astype(BF16)
    blk = lambda b, c: (b * nc + c, 0)
    return pl.pallas_call(
        _ssd_kernel,
        grid=(bsz, nc),
        in_specs=[pl.BlockSpec((SSD_CHUNK, SSD_WIDTH), blk),
                  pl.BlockSpec((SSD_CHUNK, SSD_CONV_CH), blk),
                  pl.BlockSpec((SSD_CHUNK, DT_PAD), blk),
                  _const_spec((SSD_CONV, SSD_CONV_CH)), _const_spec((1, SSD_CONV_CH)),
                  _const_spec((1, LANES)), _const_spec((1, LANES)),
                  _const_spec((1, SSD_WIDTH)), _const_spec((1, SSD_WIDTH)),
                  _const_spec((LANES, SSD_WIDTH))],
        out_specs=pl.BlockSpec((SSD_CHUNK, SSD_WIDTH), blk),
        out_shape=jax.ShapeDtypeStruct((t, SSD_WIDTH), BF16),
        scratch_shapes=[pltpu.VMEM((SSD_CHUNK + 2 * _CONV_TAIL, SSD_CONV_CH), F32),
                        pltpu.VMEM((SSD_GROUPS, SSD_STATE, SSD_GROUP_W), F32),
                        pltpu.VMEM((SSD_CHUNK, SSD_WIDTH), F32)],
        compiler_params=_cparams(("arbitrary", "arbitrary")),
        name="ssd_mixer",
    )(z, xbc, dt_raw, conv_w.astype(F32), conv_b.astype(F32)[None], pad(dt_bias), pad(a_log),
      d_x, norm_g.astype(F32)[None], expand)


def _route(logits):
    lane = lax.broadcasted_iota(jnp.int32, logits.shape, 1).astype(F32)
    big = float(ROUTE_PAD)
    is_g = lane < MOE_GROUPS
    gl = jnp.where(is_g, logits, -jnp.inf)
    gmax = jnp.max(gl, axis=1, keepdims=True)
    gsel = jnp.min(jnp.where(gl == gmax, lane, big), axis=1, keepdims=True)
    pg = 1.0 / jnp.sum(jnp.where(is_g, jnp.exp(logits - gmax), 0.0), axis=1, keepdims=True)
    lo = MOE_GROUPS + MOE_PER_GROUP * gsel
    ev = jnp.where((lane >= lo) & (lane < lo + MOE_PER_GROUP), logits, -jnp.inf)
    m1 = jnp.max(ev, axis=1, keepdims=True)
    i1 = jnp.min(jnp.where(ev == m1, lane, big), axis=1, keepdims=True)
    ev2 = jnp.where(lane == i1, -jnp.inf, ev)
    m2 = jnp.max(ev2, axis=1, keepdims=True)
    i2 = jnp.min(jnp.where(ev2 == m2, lane, big), axis=1, keepdims=True)
    e21 = jnp.exp(m2 - m1)
    w1 = pg / (1.0 + e21)
    w2 = pg * e21 / (1.0 + e21)
    out = jnp.where(lane == 0, i1 - MOE_GROUPS, 0.0)
    out = jnp.where(lane == 1, i2 - MOE_GROUPS, out)
    out = jnp.where(lane == 2, w1, out)
    return jnp.where(lane == 3, w2, out)


def _merge_kernel(*refs, n_x):
    x_refs = refs[:n_x]
    (ys5_ref, u_ref, yb_ref, yc_ref, n1g_ref, wg_ref, bg_ref, d_ref, w1_ref, w2_ref, ps5_ref, pat_ref,
     pssd_ref, wo_ref, n2g_ref, wrh_ref, wrm_ref, br_ref, x1_o, h2_o, route_o) = refs[n_x:]
    x = _x_sum(x_refs)
    hb = (_rms_scale(x) * n1g_ref[...]).astype(BF16)
    ya = ys5_ref[...].astype(F32) + d_ref[...] * u_ref[...].astype(F32)
    yab = jax.nn.gelu(ya).astype(BF16)
    ya = _dot(yab, w1_ref[...]) * jax.nn.sigmoid(_dot(yab, w2_ref[...]))
    branches = ((ya.astype(BF16), ps5_ref), (yb_ref[...], pat_ref), (yc_ref[...], pssd_ref))
    merged = None
    for b, (yv, p_ref) in enumerate(branches):
        gate = jax.nn.sigmoid(_dot(hb, wg_ref[:, b * D_MODEL:(b + 1) * D_MODEL]) + bg_ref[b:b + 1, :])
        term = gate * _dot(yv, p_ref[...])
        merged = term if merged is None else merged + term
    x1 = x + _dot(merged.astype(BF16), wo_ref[...])
    x1_o[...] = x1
    h2 = _rms_scale(x1) * n2g_ref[...]
    h2_o[...] = h2.astype(h2_o.dtype)
    h_hi = h2.astype(BF16)
    h_mid = (h2 - h_hi.astype(F32)).astype(BF16)
    logits = _dot(h_hi, wrh_ref[...]) + _dot(h_hi, wrm_ref[...]) + _dot(h_mid, wrh_ref[...]) + br_ref[...]
    route_o[...] = _route(logits)


def _merge(xs, t, ys5, u, yb, yc, lw):
    tm = min(TM_PROJ, t)
    row = lambda w: pl.BlockSpec((tm, w), lambda i: (i, 0))
    consts = [lw['n1g'], lw['w_gate'], lw['b_gate'], lw['s5_d'], lw['glu_w1'], lw['glu_w2'], lw['p_s5'],
              lw['p_attn'], lw['p_ssd'], lw['w_out'], lw['n2g'], lw['w_router_hi'], lw['w_router_mid'],
              lw['b_router']]
    x_arrs, x_specs = _x_parts(xs, tm)
    return pl.pallas_call(
        functools.partial(_merge_kernel, n_x=len(x_arrs)),
        grid=(t // tm,),
        in_specs=x_specs + [row(S5_WIDTH), row(S5_WIDTH), row(ATTN_Q), row(SSD_WIDTH)]
                 + [_const_spec(c.shape) for c in consts],
        out_specs=[row(D_MODEL), row(D_MODEL), row(ROUTE_PAD)],
        out_shape=[jax.ShapeDtypeStruct((t, D_MODEL), F32), jax.ShapeDtypeStruct((t, D_MODEL), F32),
                   jax.ShapeDtypeStruct((t, ROUTE_PAD), F32)],
        compiler_params=_cparams(("parallel",)),
        name="merge_router",
    )(*x_arrs, ys5, u, yb, yc, *consts)


def _moe_plan(route, t):
    n_exp = MOE_EXPERTS
    e = route[:, 0:2].astype(jnp.int32).reshape(-1)
    onehot = (e[:, None] == jnp.arange(n_exp, dtype=jnp.int32)[None, :]).astype(jnp.int32)
    csum = jnp.cumsum(onehot, axis=0)
    rank = jnp.sum(csum * onehot, axis=1) - 1
    counts = csum[-1]
    padded = ((counts + TM_G - 1) // TM_G) * TM_G
    pend = jnp.cumsum(padded)
    dest = (pend - padded)[e] + rank
    n_tiles = -(-(2 * t + n_exp * (TM_G - 1)) // TM_G)
    n_rows = n_tiles * TM_G
    pair = jnp.full((n_rows,), -1, jnp.int32).at[dest].set(jnp.arange(2 * t, dtype=jnp.int32),
                                                            unique_indices=True)
    valid = pair >= 0
    tok, k = pair // 2, pair % 2
    src = jnp.where(valid, tok, 0)
    spare = t + jnp.arange(n_rows, dtype=jnp.int32) % _MOE_SPARE
    dst = jnp.where(valid, k * (t + _MOE_SPARE) + tok, spare)
    tile_start = jnp.arange(n_tiles, dtype=jnp.int32) * TM_G
    tile_e = jnp.sum((pend[None, :] <= tile_start[:, None]).astype(jnp.int32), axis=1)
    tile_e = jnp.minimum(tile_e, n_exp - 1)
    return src.reshape(n_tiles, 1, TM_G), dst.reshape(n_tiles, 1, TM_G), tile_e


_MOE_SPARE = 2 * TM_G


def _gmm_kernel(te_ref, src_ref, srcn_ref, dst_ref, h2_hbm, wg_ref, wu_ref, wd_ref, out_hbm,
                xbuf, ybuf, gsem, ssem):
    del te_ref
    i = pl.program_id(0)
    last = pl.num_programs(0) - 1
    slot = i % 2

    def gather_start(idx_ref, s):
        for r in range(TM_G):
            pltpu.make_async_copy(h2_hbm.at[pl.ds(idx_ref[0, 0, r], 1), :],
                                  xbuf.at[s, pl.ds(r, 1), :], gsem.at[s]).start()

    def gather_wait(s):
        pltpu.make_async_copy(h2_hbm.at[pl.ds(0, TM_G), :], xbuf.at[s], gsem.at[s]).wait()

    def scatter_wait(s):
        pltpu.make_async_copy(ybuf.at[s], out_hbm.at[pl.ds(0, TM_G), :], ssem.at[s]).wait()

    @pl.when(i == 0)
    def _():
        gather_start(src_ref, 0)
        ybuf[...] = jnp.zeros_like(ybuf)
        plane = out_hbm.shape[0] // 2
        fills = [pltpu.make_async_copy(ybuf.at[s],
                                       out_hbm.at[pl.ds(k * plane + plane - _MOE_SPARE + s * TM_G, TM_G), :],
                                       ssem.at[s]) for k in range(2) for s in range(2)]
        for f in fills:
            f.start()
        for f in fills:
            f.wait()

    gather_wait(slot)

    @pl.when(i >= 2)
    def _():
        scatter_wait(slot)

    gather_start(srcn_ref, 1 - slot)

    xb = xbuf[slot].astype(BF16)
    hid = _silu(_dot(xb, wg_ref[0].astype(BF16))) * _dot(xb, wu_ref[0].astype(BF16))
    ybuf[slot] = _dot(hid.astype(BF16), wd_ref[0].astype(BF16))
    for r in range(TM_G):
        pltpu.make_async_copy(ybuf.at[slot, pl.ds(r, 1), :],
                              out_hbm.at[pl.ds(dst_ref[0, 0, r], 1), :], ssem.at[slot]).start()

    @pl.when(i == last)
    def _():
        scatter_wait(slot)
        gather_wait(1 - slot)

    @pl.when((i == last) & (i >= 1))
    def _():
        scatter_wait(1 - slot)


def _moe_sparse(h2, route, w_gate, w_up, w_down):
    t = h2.shape[0]
    src, dst, tile_e = _moe_plan(route, t)
    n_tiles = src.shape[0]
    smem = lambda imap: pl.BlockSpec((1, 1, TM_G), imap, memory_space=pltpu.SMEM)
    grid_spec = pltpu.PrefetchScalarGridSpec(
        num_scalar_prefetch=1,
        grid=(n_tiles,),
        in_specs=[smem(lambda i, te: (i, 0, 0)),
                  smem(lambda i, te: (jnp.minimum(i + 1, n_tiles - 1), 0, 0)),
                  smem(lambda i, te: (i, 0, 0)),
                  pl.BlockSpec(memory_space=pl.ANY),
                  pl.BlockSpec((1, D_MODEL, MOE_FF), lambda i, te: (te[i], 0, 0)),
                  pl.BlockSpec((1, D_MODEL, MOE_FF), lambda i, te: (te[i], 0, 0)),
                  pl.BlockSpec((1, MOE_FF, D_MODEL), lambda i, te: (te[i], 0, 0))],
        out_specs=pl.BlockSpec(memory_space=pl.ANY),
        scratch_shapes=[pltpu.VMEM((2, TM_G, D_MODEL), F32), pltpu.VMEM((2, TM_G, D_MODEL), F32),
                        pltpu.SemaphoreType.DMA((2,)), pltpu.SemaphoreType.DMA((2,))])
    plane = t + _MOE_SPARE
    out = pl.pallas_call(
        _gmm_kernel,
        grid_spec=grid_spec,
        out_shape=jax.ShapeDtypeStruct((2 * plane, D_MODEL), F32),
        compiler_params=_cparams(("arbitrary",)),
        name="moe_gmm",
    )(tile_e, src, src, dst, h2, w_gate, w_up, w_down)
    return out.reshape(2, plane, D_MODEL)


def _sum_kernel(*refs):
    refs[-1][...] = _x_sum(refs[:-1])


def _residual_sum(xs, t):
    tm = min(TM_PROJ, t)
    x_arrs, x_specs = _x_parts(xs, tm)
    return pl.pallas_call(
        _sum_kernel,
        grid=(t // tm,),
        in_specs=x_specs,
        out_specs=pl.BlockSpec((tm, D_MODEL), lambda i: (i, 0)),
        out_shape=jax.ShapeDtypeStruct((t, D_MODEL), F32),
        compiler_params=_cparams(("parallel",)),
        name="residual_sum",
    )(*x_arrs)


def _layer_weights(l, p):
    w_in = p['w_in'][l]
    w_router = jnp.concatenate([p['w_router_group'][l], p['w_router_expert'][l]], axis=1).astype(F32)
    npad = ROUTE_PAD - w_router.shape[1]
    b_router = jnp.concatenate([p['b_router_group'][l], p['b_router_expert'][l]]).astype(F32)
    w_router = jnp.pad(w_router, ((0, 0), (0, npad)))
    w_router_hi = w_router.astype(BF16)
    return dict(
        n1g=p['norm1_g'][l].astype(F32)[None],
        w_gate=w_in[:, N_MIX:].astype(BF16), b_gate=p['b_gate'][l].astype(F32),
        s5_d=p['s5_d'][l].astype(F32)[None],
        glu_w1=p['s5_glu_w1'][l].astype(BF16), glu_w2=p['s5_glu_w2'][l].astype(BF16),
        p_s5=p['p_s5'][l].astype(BF16), p_attn=p['p_attn'][l].astype(BF16), p_ssd=p['p_ssd'][l].astype(BF16),
        w_out=p['w_out'][l].astype(BF16), n2g=p['norm2_g'][l].astype(F32)[None],
        w_router_hi=w_router_hi, w_router_mid=(w_router - w_router_hi.astype(F32)).astype(BF16),
        b_router=jnp.pad(b_router, (0, npad))[None])


def _layer(xs, t, l, p, bsz):
    lw = _layer_weights(l, p)
    u, q, k, v, z, xbc, dt_raw = _inproj(xs, t, lw['n1g'], p['w_in'].astype(F32), l)
    tabs = _s5_tables(p['s5_lambda_re'][l], p['s5_lambda_im'][l], p['s5_b_re'][l], p['s5_b_im'][l],
                      p['s5_c_re'][l], p['s5_c_im'][l], p['s5_log_dt'][l])
    ys5 = _s5_mixer(u, tabs, bsz)
    yb = _swa_attention(q, k, v, p['q_norm_g'][l], p['k_norm_g'][l], p['attn_sinks'][l], bsz)
    yc = _ssd_mixer(z, xbc, dt_raw, p['ssd_conv_w'][l], p['ssd_conv_b'][l], p['ssd_dt_bias'][l],
                    p['ssd_a_log'][l], p['ssd_d'][l], p['ssd_norm_g'][l], bsz)
    x1, h2, route = _merge(xs, t, ys5, u, yb, yc, lw)
    moe = _moe_sparse(h2, route, p['w_exp_gate'][l], p['w_exp_up'][l], p['w_exp_down'][l])
    return [x1, route, moe]


def kernel(x, norm1_g, w_in, b_gate, s5_lambda_re, s5_lambda_im, s5_b_re, s5_b_im, s5_c_re, s5_c_im, s5_d, s5_log_dt, s5_glu_w1, s5_glu_w2, q_norm_g, k_norm_g, attn_sinks, ssd_conv_w, ssd_conv_b, ssd_dt_bias, ssd_a_log, ssd_d, ssd_norm_g, p_s5, p_attn, p_ssd, w_out, norm2_g, w_router_group, b_router_group, w_router_expert, b_router_expert, w_exp_gate, w_exp_up, w_exp_down):
    p = dict(norm1_g=norm1_g, w_in=w_in, b_gate=b_gate, s5_lambda_re=s5_lambda_re, s5_lambda_im=s5_lambda_im,
             s5_b_re=s5_b_re, s5_b_im=s5_b_im, s5_c_re=s5_c_re, s5_c_im=s5_c_im, s5_d=s5_d,
             s5_log_dt=s5_log_dt, s5_glu_w1=s5_glu_w1, s5_glu_w2=s5_glu_w2, q_norm_g=q_norm_g,
             k_norm_g=k_norm_g, attn_sinks=attn_sinks, ssd_conv_w=ssd_conv_w, ssd_conv_b=ssd_conv_b,
             ssd_dt_bias=ssd_dt_bias, ssd_a_log=ssd_a_log, ssd_d=ssd_d, ssd_norm_g=ssd_norm_g, p_s5=p_s5,
             p_attn=p_attn, p_ssd=p_ssd, w_out=w_out, norm2_g=norm2_g, w_router_group=w_router_group,
             b_router_group=b_router_group, w_router_expert=w_router_expert, b_router_expert=b_router_expert,
             w_exp_gate=w_exp_gate, w_exp_up=w_exp_up, w_exp_down=w_exp_down)
    bsz, seq, dm = x.shape
    depth = w_in.shape[0]
    t = bsz * seq
    xs = [x.reshape(t, dm)]
    for l in range(depth):
        xs = _layer(xs, t, l, p, bsz)
    return _residual_sum(xs, t).reshape(bsz, seq, dm)
```

```python
import functools
import math

import jax
import jax.numpy as jnp
from jax import lax
from jax.experimental import pallas as pl
from jax.experimental.pallas import tpu as pltpu

F32 = jnp.float32
BF16 = jnp.bfloat16

D_MODEL = 1024
NORM_EPS = 1e-6
S5_WIDTH = 512
S5_GROUP = 16
S5_GROUPS = 32
S5_STATE = 64
HEAD_DIM = 64
ATTN_HEADS = 8
ATTN_KV_HEADS = 2
ATTN_REP = ATTN_HEADS // ATTN_KV_HEADS
ATTN_Q = ATTN_HEADS * HEAD_DIM
ATTN_KV = ATTN_KV_HEADS * HEAD_DIM
ATTN_BLOCK = 128
ROPE_THETA = 10000.0
SSD_WIDTH = 1024
SSD_HEAD_DIM = 64
SSD_HEADS = 16
SSD_GROUPS = 2
SSD_STATE = 64
SSD_CONV = 4
SSD_CHUNK = 128
SSD_BC = SSD_GROUPS * SSD_STATE
SSD_CONV_CH = SSD_WIDTH + 2 * SSD_BC
SSD_GROUP_W = SSD_WIDTH // SSD_GROUPS
N_BRANCH = 3
MOE_GROUPS = 4
MOE_PER_GROUP = 8
MOE_EXPERTS = 32
MOE_FF = 512
N_MIX = S5_WIDTH + ATTN_Q + 2 * ATTN_KV + SSD_WIDTH + SSD_CONV_CH + SSD_HEADS

LANES = 128
SUBLANES = 8
VMEM_LIMIT_BYTES = 56 * 1024 * 1024

S5_CHUNK = 16
S5_TG = 8
ATTN_TILE = 512
TM_PROJ = 512
TM_G = 256
DT_PAD = LANES
N_MIX_PAD = N_MIX - SSD_HEADS + DT_PAD
ROUTE_PAD = LANES


def _cparams(semantics):
    return pltpu.CompilerParams(dimension_semantics=semantics, vmem_limit_bytes=VMEM_LIMIT_BYTES)


def _const_spec(shape):
    zeros = (0,) * len(shape)
    return pl.BlockSpec(shape, lambda *_: zeros, pipeline_mode=pl.Buffered(1))


def _dot(a, b):
    return jnp.dot(a, b, preferred_element_type=F32)


def _split3(a):
    hi = a.astype(BF16)
    r1 = a - hi.astype(F32)
    mid = r1.astype(BF16)
    lo = (r1 - mid.astype(F32)).astype(BF16)
    return hi, mid, lo


def _dot_f32_lhs(a, b_bf16):
    hi, mid, lo = _split3(a)
    return _dot(hi, b_bf16) + _dot(mid, b_bf16) + _dot(lo, b_bf16)


def _rms_scale(x):
    return x * lax.rsqrt(jnp.mean(x * x, axis=-1, keepdims=True) + NORM_EPS)


def _silu(x):
    return x * jax.nn.sigmoid(x)


_OFF_U = 0
_OFF_Q = _OFF_U + S5_WIDTH
_OFF_K = _OFF_Q + ATTN_Q
_OFF_V = _OFF_K + ATTN_KV
_OFF_Z = _OFF_V + ATTN_KV
_OFF_XBC = _OFF_Z + SSD_WIDTH
_OFF_DT = _OFF_XBC + SSD_CONV_CH


ROW_TILE = D_MODEL // LANES


def _store_row_tiles(ref, start, val):
    n = val.shape[0]
    for j in range(ROW_TILE):
        ref[pl.ds(start * ROW_TILE + j, n, stride=ROW_TILE), :] = val[:, j * LANES:(j + 1) * LANES]


def _load_row_tiles(ref, start, n):
    return jnp.concatenate([ref[pl.ds(start * ROW_TILE + j, n, stride=ROW_TILE), :] for j in range(ROW_TILE)],
                           axis=1)


def _x_parts(xs, tm):
    row = lambda w: pl.BlockSpec((tm, w), lambda i: (i, 0))
    if len(xs) == 1:
        return list(xs), [row(D_MODEL)]
    x1, route, moe = xs
    plane = lambda k: pl.BlockSpec((None, tm * ROW_TILE, LANES), lambda i: (k, i, 0))
    return [x1, route, moe, moe], [row(D_MODEL), row(ROUTE_PAD), plane(0), plane(1)]


def _x_sum(x_refs):
    if len(x_refs) == 1:
        return x_refs[0][...]
    x1_ref, route_ref, m0_ref, m1_ref = x_refs
    r = route_ref[...]
    n = x1_ref.shape[0]
    return x1_ref[...] + r[:, 2:3] * _load_row_tiles(m0_ref, 0, n) + r[:, 3:4] * _load_row_tiles(m1_ref, 0, n)


def _inproj_kernel(*refs, n_x):
    x_refs, (g_ref, w_ref), outs, wb = refs[:n_x], refs[n_x:n_x + 2], refs[n_x + 2:-1], refs[-1]

    @pl.when(pl.program_id(0) == 0)
    def _():
        wb[...] = w_ref[...].astype(BF16)

    hb = (_rms_scale(_x_sum(x_refs)) * g_ref[...]).astype(BF16)
    offs = (_OFF_U, _OFF_Q, _OFF_K, _OFF_V, _OFF_Z, _OFF_XBC, _OFF_DT)
    for o_ref, off in zip(outs, offs):
        width = o_ref.shape[1]
        o_ref[...] = _dot(hb, wb[:, off:off + width]).astype(o_ref.dtype)


def _inproj(xs, t, g, w_in, l):
    tm = min(TM_PROJ, t)
    widths = (S5_WIDTH, ATTN_Q, ATTN_KV, ATTN_KV, SSD_WIDTH, SSD_CONV_CH, DT_PAD)
    dtypes = (F32, BF16, BF16, BF16, BF16, BF16, F32)
    x_arrs, x_specs = _x_parts(xs, tm)
    w_spec = pl.BlockSpec((None, D_MODEL, N_MIX_PAD), lambda i: (l, 0, 0), pipeline_mode=pl.Buffered(1))
    return pl.pallas_call(
        functools.partial(_inproj_kernel, n_x=len(x_arrs)),
        grid=(t // tm,),
        in_specs=x_specs + [_const_spec((1, D_MODEL)), w_spec],
        out_specs=[pl.BlockSpec((tm, w), lambda i: (i, 0)) for w in widths],
        out_shape=[jax.ShapeDtypeStruct((t, w), d) for w, d in zip(widths, dtypes)],
        scratch_shapes=[pltpu.VMEM((D_MODEL, N_MIX_PAD), BF16)],
        compiler_params=_cparams(("arbitrary",)),
        name="inproj",
    )(*x_arrs, g, w_in)


def _s5_tables(lam_re, lam_im, b_re, b_im, c_re, c_im, log_dt):
    hp = lax.Precision.HIGHEST
    g_n, p_n = lam_re.shape
    c_n = b_re.shape[-1]
    t_n = S5_CHUNK
    lr, li = lam_re.astype(F32), lam_im.astype(F32)
    dt = jnp.exp(log_dt.astype(F32))[:, None]
    mag = jnp.exp(lr * dt)
    ab_re = mag * jnp.cos(li * dt)
    ab_im = mag * jnp.sin(li * dt)
    nr = ab_re - 1.0
    den = lr * lr + li * li
    f_re = (nr * lr + ab_im * li) / den
    f_im = (ab_im * lr - nr * li) / den
    br, bi = b_re.astype(F32), b_im.astype(F32)
    bb_re = f_re[..., None] * br - f_im[..., None] * bi
    bb_im = f_re[..., None] * bi + f_im[..., None] * br
    j = jnp.arange(t_n + 1, dtype=F32)[:, None, None]
    pmag = jnp.exp(lr * dt * j)
    ang = li * dt * j
    p_re = pmag * jnp.cos(ang)
    p_im = pmag * jnp.sin(ang)
    cr, ci = c_re.astype(F32), c_im.astype(F32)
    ca_re = cr[None] * p_re[:, :, None, :] - ci[None] * p_im[:, :, None, :]
    ca_im = cr[None] * p_im[:, :, None, :] + ci[None] * p_re[:, :, None, :]
    kern = (jnp.einsum('jgcp,gpd->jgcd', ca_re[:t_n], bb_re, precision=hp)
            - jnp.einsum('jgcp,gpd->jgcd', ca_im[:t_n], bb_im, precision=hp))
    q_re = p_re[t_n - 1 - jnp.arange(t_n)]
    q_im = p_im[t_n - 1 - jnp.arange(t_n)]
    bs_re = q_re[..., None] * bb_re[None] - q_im[..., None] * bb_im[None]
    bs_im = q_re[..., None] * bb_im[None] + q_im[..., None] * bb_re[None]

    nt = g_n // S5_TG
    k_strip = kern.reshape(t_n, nt, S5_TG, c_n, c_n).transpose(1, 0, 4, 2, 3)
    k_strip = k_strip.reshape(nt, t_n, c_n, S5_TG * c_n)

    def in_strip(a):
        a = a.reshape(t_n, nt, S5_TG, p_n, c_n).transpose(1, 0, 4, 2, 3)
        return a.reshape(nt, t_n, c_n, S5_TG * p_n)

    def out_strip(a):
        a = a.reshape(t_n, nt, S5_TG, c_n, p_n).transpose(1, 4, 0, 2, 3)
        return a.reshape(nt, p_n, t_n * S5_TG * c_n)

    return dict(
        k=k_strip, bs_re=in_strip(bs_re), bs_im=in_strip(bs_im),
        co_re=out_strip(ca_re[1:]), co_im=out_strip(-ca_im[1:]),
        at_re=p_re[t_n].reshape(nt, 1, S5_TG * p_n), at_im=p_im[t_n].reshape(nt, 1, S5_TG * p_n))


_S5_XW = S5_CHUNK * LANES
_S5_SW = S5_TG * S5_STATE


def _group_block(strip, lanes_per_group):
    rpg = strip.shape[0]
    full = jnp.concatenate([strip] * S5_TG, axis=0)
    row_g = lax.broadcasted_iota(jnp.int32, full.shape, 0) // rpg
    lane_g = (lax.broadcasted_iota(jnp.int32, full.shape, 1) % (S5_TG * lanes_per_group)) // lanes_per_group
    return jnp.where(row_g == lane_g, full, 0.0).astype(BF16)


def _s5_build_tables(k_ref, bre_ref, bim_ref, cre_ref, cim_ref, big, bsre, bsim, core, coim):
    blocks = [_group_block(k_ref[0, j], S5_GROUP) for j in range(S5_CHUNK)]
    zero = jnp.zeros((LANES, LANES), BF16)
    for s in range(S5_CHUNK):
        for t in range(S5_CHUNK):
            big[s * LANES:(s + 1) * LANES, t * LANES:(t + 1) * LANES] = blocks[t - s] if t >= s else zero
    for src, dst in ((bre_ref, bsre), (bim_ref, bsim)):
        for s in range(S5_CHUNK):
            dst[s * LANES:(s + 1) * LANES, :] = _group_block(src[0, s], S5_STATE)
    for src, dst in ((cre_ref, core), (cim_ref, coim)):
        for t in range(S5_CHUNK):
            cols = slice(t * LANES, (t + 1) * LANES)
            dst[:, cols] = _group_block(src[0, :, cols], S5_GROUP)


def _s5_kernel(u_ref, k_ref, bre_ref, bim_ref, cre_ref, cim_ref, are_ref, aim_ref, y_ref,
               big, bsre, bsim, core, coim, sre, sim, hre, him):
    @pl.when(pl.program_id(1) == 0)
    def _():
        _s5_build_tables(k_ref, bre_ref, bim_ref, cre_ref, cim_ref, big, bsre, bsim, core, coim)

    nck = u_ref.shape[0] // S5_CHUNK
    xcat = jnp.concatenate([u_ref[pl.ds(s, nck, stride=S5_CHUNK), :].astype(BF16) for s in range(S5_CHUNK)],
                           axis=1)
    sre[...] = _dot(xcat, bsre[...])
    sim[...] = _dot(xcat, bsim[...])
    a_re = are_ref[0]
    a_im = aim_ref[0]

    def step(i, carry):
        h_re, h_im = carry
        base = pl.multiple_of(i * SUBLANES, SUBLANES)
        s_re = sre[pl.ds(base, SUBLANES), :]
        s_im = sim[pl.ds(base, SUBLANES), :]
        ent_re, ent_im = [], []
        for j in range(SUBLANES):
            ent_re.append(h_re)
            ent_im.append(h_im)
            h_re, h_im = (a_re * h_re - a_im * h_im + s_re[j:j + 1], a_re * h_im + a_im * h_re + s_im[j:j + 1])
        hre[pl.ds(base, SUBLANES), :] = jnp.concatenate(ent_re, axis=0)
        him[pl.ds(base, SUBLANES), :] = jnp.concatenate(ent_im, axis=0)
        return h_re, h_im

    zero = jnp.zeros((1, _S5_SW), F32)
    lax.fori_loop(0, nck // SUBLANES, step, (zero, zero))
    y = (_dot(xcat, big[...]) + _dot(hre[...].astype(BF16), core[...])
         + _dot(him[...].astype(BF16), coim[...]))
    for t in range(S5_CHUNK):
        y_ref[pl.ds(t, nck, stride=S5_CHUNK), :] = y[:, t * LANES:(t + 1) * LANES]


def _s5_mixer(u, tabs, bsz):
    t = u.shape[0]
    seq = t // bsz
    nck = seq // S5_CHUNK
    nt = S5_GROUPS // S5_TG
    strip = lambda a: pl.BlockSpec((1,) + a.shape[1:], lambda x, b: (x,) + (0,) * (a.ndim - 1))
    strips = [tabs['k'], tabs['bs_re'], tabs['bs_im'], tabs['co_re'], tabs['co_im'], tabs['at_re'], tabs['at_im']]
    return pl.pallas_call(
        _s5_kernel,
        grid=(nt, bsz),
        in_specs=[pl.BlockSpec((seq, LANES), lambda x, b: (b, x))] + [strip(a) for a in strips],
        out_specs=pl.BlockSpec((seq, LANES), lambda x, b: (b, x)),
        out_shape=jax.ShapeDtypeStruct((t, S5_WIDTH), F32),
        scratch_shapes=[pltpu.VMEM((_S5_XW, _S5_XW), BF16),
                        pltpu.VMEM((_S5_XW, _S5_SW), BF16), pltpu.VMEM((_S5_XW, _S5_SW), BF16),
                        pltpu.VMEM((_S5_SW, _S5_XW), BF16), pltpu.VMEM((_S5_SW, _S5_XW), BF16)]
                       + [pltpu.VMEM((nck, _S5_SW), F32)] * 4,
        compiler_params=_cparams(("arbitrary", "arbitrary")),
        name="s5_mixer",
    )(u, *strips)


def _swap_rope_halves(y):
    w = y.shape[1]
    lane = lax.broadcasted_iota(jnp.int32, y.shape, 1)
    lower = (lane % HEAD_DIM) < (HEAD_DIM // 2)
    return jnp.where(lower, pltpu.roll(y, w - HEAD_DIM // 2, 1), pltpu.roll(y, HEAD_DIM // 2, 1))


def _norm_rope(x, gain, head_mean, cos2, sin2):
    reps = x.shape[1] // LANES
    ms = _dot_f32_lhs(x * x, head_mean)
    y = x * lax.rsqrt(ms + NORM_EPS) * gain
    if reps > 1:
        cos2 = jnp.concatenate([cos2] * reps, axis=1)
        sin2 = jnp.concatenate([sin2] * reps, axis=1)
    return y * cos2 + _swap_rope_halves(y) * sin2


def _attn_kernel(q_ref, kc_ref, kp_ref, vc_ref, vp_ref, cosc_ref, sinc_ref, cosp_ref, sinp_ref,
                 qg_ref, kg_ref, sink_ref, hmq_ref, hmk_ref, o_ref):
    seq_start = pl.program_id(1) == 0
    blk = ATTN_BLOCK
    nblk = q_ref.shape[0] // blk
    q = _norm_rope(q_ref[...].astype(F32), qg_ref[...], hmq_ref[...], cosc_ref[...], sinc_ref[...])
    q = (q * (HEAD_DIM ** -0.5)).astype(BF16)
    kc = _norm_rope(kc_ref[...].astype(F32), kg_ref[...], hmk_ref[...], cosc_ref[...], sinc_ref[...]).astype(BF16)
    kp = _norm_rope(kp_ref[...].astype(F32), kg_ref[...], hmk_ref[...], cosp_ref[...], sinp_ref[...]).astype(BF16)
    k_all = jnp.concatenate([kp, kc], axis=0)
    v_all = jnp.concatenate([vp_ref[...], vc_ref[...]], axis=0)
    shape = (2 * blk, ATTN_REP * blk)
    kj = lax.broadcasted_iota(jnp.int32, shape, 0)
    qi = lax.broadcasted_iota(jnp.int32, shape, 1) % blk + blk
    band = (kj <= qi) & (qi - kj < blk)
    band_first = band & ((kj >= blk) | jnp.logical_not(seq_start))
    sinks = sink_ref[...]
    for j in range(ATTN_KV_HEADS):
        sl = slice(j * HEAD_DIM, (j + 1) * HEAD_DIM)
        heads = [j * ATTN_REP + r for r in range(ATTN_REP)]
        sink = jnp.concatenate([jnp.broadcast_to(sinks[:, h:h + 1], (1, blk)) for h in heads], axis=1)
        for n in range(nblk):
            rows = slice(n * blk, (n + 1) * blk)
            kb = k_all[n * blk:(n + 2) * blk, sl]
            vb = v_all[n * blk:(n + 2) * blk, sl]
            q4 = jnp.concatenate([q[rows, h * HEAD_DIM:(h + 1) * HEAD_DIM] for h in heads], axis=0)
            s = lax.dot_general(kb, q4, (((1,), (1,)), ((), ())), preferred_element_type=F32)
            s = jnp.where(band_first if n == 0 else band, s, -jnp.inf)
            m = jnp.maximum(jnp.max(s, axis=0, keepdims=True), sink)
            p = jnp.exp(s - m)
            denom = jnp.sum(p, axis=0, keepdims=True) + jnp.exp(sink - m)
            pn = (p * (1.0 / denom)).astype(BF16)
            out = lax.dot_general(pn, vb, (((0,), (0,)), ((), ())), preferred_element_type=F32)
            for r, h in enumerate(heads):
                o_ref[rows, h * HEAD_DIM:(h + 1) * HEAD_DIM] = out[r * blk:(r + 1) * blk].astype(o_ref.dtype)


def _rope_tables(seq):
    half = HEAD_DIM // 2
    inv = jnp.power(ROPE_THETA, -jnp.arange(half, dtype=F32) * 2.0 / HEAD_DIM)
    ang = jnp.arange(seq, dtype=F32)[:, None] * inv[None, :]
    cos, sin = jnp.cos(ang), jnp.sin(ang)
    cos2 = jnp.concatenate([cos, cos, cos, cos], axis=1)
    sin2 = jnp.concatenate([-sin, sin, -sin, sin], axis=1)
    return cos2, sin2


def _head_mean_matrix(width):
    i = jnp.arange(width)
    return jnp.where((i[:, None] // HEAD_DIM) == (i[None, :] // HEAD_DIM), 1.0 / HEAD_DIM, 0.0).astype(BF16)


def _swa_attention(q, k, v, q_g, k_g, sinks, bsz):
    t = q.shape[0]
    seq = t // bsz
    tile = min(ATTN_TILE, seq)
    nt = seq // tile
    bpt = tile // ATTN_BLOCK
    nb = seq // ATTN_BLOCK
    cos2, sin2 = _rope_tables(seq)
    qg = jnp.tile(q_g.astype(F32), ATTN_HEADS)[None]
    kg = jnp.tile(k_g.astype(F32), ATTN_KV_HEADS)[None]
    cur = lambda b, n: (b * nt + n, 0)
    prev = lambda b, n: (b * nb + jnp.maximum(n * bpt - 1, 0), 0)
    tcur = lambda b, n: (n, 0)
    tprev = lambda b, n: (jnp.maximum(n * bpt - 1, 0), 0)
    blk = ATTN_BLOCK
    return pl.pallas_call(
        _attn_kernel,
        grid=(bsz, nt),
        in_specs=[pl.BlockSpec((tile, ATTN_Q), cur),
                  pl.BlockSpec((tile, ATTN_KV), cur), pl.BlockSpec((blk, ATTN_KV), prev),
                  pl.BlockSpec((tile, ATTN_KV), cur), pl.BlockSpec((blk, ATTN_KV), prev),
                  pl.BlockSpec((tile, LANES), tcur), pl.BlockSpec((tile, LANES), tcur),
                  pl.BlockSpec((blk, LANES), tprev), pl.BlockSpec((blk, LANES), tprev),
                  _const_spec((1, ATTN_Q)), _const_spec((1, ATTN_KV)), _const_spec((1, ATTN_HEADS)),
                  _const_spec((ATTN_Q, ATTN_Q)), _const_spec((ATTN_KV, ATTN_KV))],
        out_specs=pl.BlockSpec((tile, ATTN_Q), cur),
        out_shape=jax.ShapeDtypeStruct((t, ATTN_Q), BF16),
        compiler_params=_cparams(("parallel", "parallel")),
        name="swa_attention",
    )(q, k, k, v, v, cos2, sin2, cos2, sin2, qg, kg, sinks.astype(F32)[None],
      _head_mean_matrix(ATTN_Q), _head_mean_matrix(ATTN_KV))


_CONV_TAIL = SUBLANES


def _ssd_kernel(z_ref, xbc_ref, dt_ref, cw_ref, cb_ref, dtb_ref, alog_ref, dskip_ref, ng_ref, exp_ref,
                o_ref, xext, state, ybuf):
    c = pl.program_id(1)
    ch = SSD_CHUNK

    @pl.when(c == 0)
    def _():
        xext[0:_CONV_TAIL, :] = jnp.zeros((_CONV_TAIL, SSD_CONV_CH), F32)
        state[...] = jnp.zeros_like(state)

    xext[_CONV_TAIL:_CONV_TAIL + ch, :] = xbc_ref[...].astype(F32)
    conv = cb_ref[...]
    for j in range(SSD_CONV):
        conv = conv + cw_ref[j:j + 1, :] * xext[pl.ds(_CONV_TAIL - (SSD_CONV - 1) + j, ch), :]
    xext[0:_CONV_TAIL, :] = xext[ch:ch + _CONV_TAIL, :]
    act = _silu(conv)
    xs = act[:, :SSD_WIDTH]
    bm = act[:, SSD_WIDTH:SSD_WIDTH + SSD_BC].astype(BF16)
    cm = act[:, SSD_WIDTH + SSD_BC:].astype(BF16)

    lane = lax.broadcasted_iota(jnp.int32, (ch, LANES), 1)
    xdt = dt_ref[...] + dtb_ref[...]
    dt = jnp.maximum(xdt, 0.0) + jnp.log1p(jnp.exp(-jnp.abs(xdt)))
    dt = jnp.where(lane < SSD_HEADS, dt, 0.0)
    a = dt * (-jnp.exp(alog_ref[...]))
    row = lax.broadcasted_iota(jnp.int32, (ch, ch), 0)
    col = lax.broadcasted_iota(jnp.int32, (ch, ch), 1)
    causal = row >= col
    hi, mid, lo = _split3(a)
    tril = causal.astype(BF16)
    cs = _dot(tril, hi) + _dot(tril, mid) + _dot(tril, lo)
    cs_t = cs.T
    expand = exp_ref[...]
    dt_x = _dot_f32_lhs(dt, expand)
    cs_x = _dot_f32_lhs(cs, expand)
    cs_last_x = cs_x[ch - 1:ch, :]
    xdt_full = xs * dt_x
    in_decay = jnp.exp(cs_x)
    out_decay = jnp.exp(cs_last_x - cs_x)
    chunk_decay = jnp.exp(cs_last_x)
    xw = (xdt_full * out_decay).astype(BF16)
    xdt_b = xdt_full.astype(BF16)
    hpg = SSD_HEADS // SSD_GROUPS
    for g in range(SSD_GROUPS):
        gs = slice(g * SSD_STATE, (g + 1) * SSD_STATE)
        ws = slice(g * SSD_GROUP_W, (g + 1) * SSD_GROUP_W)
        bg = bm[:, gs]
        cg = cm[:, gs]
        cb = lax.dot_general(cg, bg, (((1,), (1,)), ((), ())), preferred_element_type=F32)
        prev = state[g]
        ybuf[:, ws] = _dot(cg, prev.astype(BF16)) * in_decay[:, ws]
        for h in range(hpg):
            hh = g * hpg + h
            seg = cs[:, hh:hh + 1] - cs_t[hh:hh + 1, :]
            lmat = jnp.where(causal, jnp.exp(seg), 0.0)
            hs = slice(hh * SSD_HEAD_DIM, (hh + 1) * SSD_HEAD_DIM)
            ybuf[:, hs] += _dot((cb * lmat).astype(BF16), xdt_b[:, hs])
        upd = lax.dot_general(bg, xw[:, ws], (((0,), (0,)), ((), ())), preferred_element_type=F32)
        state[g] = prev * chunk_decay[:, ws] + upd

    y = ybuf[...] + xs * dskip_ref[...]
    y = y * _silu(z_ref[...].astype(F32))
    parts = []
    for g in range(SSD_GROUPS):
        ws = slice(g * SSD_GROUP_W, (g + 1) * SSD_GROUP_W)
        parts.append(_rms_scale(y[:, ws]))
    o_ref[...] = (jnp.concatenate(parts, axis=1) * ng_ref[...]).astype(o_ref.dtype)


def _ssd_mixer(z, xbc, dt_raw, conv_w, conv_b, dt_bias, a_log, d_skip, norm_g, bsz):
    t = z.shape[0]
    seq = t // bsz
    nc = seq // SSD_CHUNK
    pad = lambda v: jnp.pad(v.astype(F32), (0, LANES - SSD_HEADS))[None]
    d_x = jnp.repeat(d_skip.astype(F32), SSD_HEAD_DIM)[None]
    hid = jnp.arange(SSD_WIDTH) // SSD_HEAD_DIM
    expand = (jnp.arange(LANES)[:, None] == hid[None, :]).astype(BF16)
    blk = lambda b, c: (b * nc + c, 0)
    return pl.pallas_call(
        _ssd_kernel,
        grid=(bsz, nc),
        in_specs=[pl.BlockSpec((SSD_CHUNK, SSD_WIDTH), blk),
                  pl.BlockSpec((SSD_CHUNK, SSD_CONV_CH), blk),
                  pl.BlockSpec((SSD_CHUNK, DT_PAD), blk),
                  _const_spec((SSD_CONV, SSD_CONV_CH)), _const_spec((1, SSD_CONV_CH)),
                  _const_spec((1, LANES)), _const_spec((1, LANES)),
                  _const_spec((1, SSD_WIDTH)), _const_spec((1, SSD_WIDTH)),
                  _const_spec((LANES, SSD_WIDTH))],
        out_specs=pl.BlockSpec((SSD_CHUNK, SSD_WIDTH), blk),
        out_shape=jax.ShapeDtypeStruct((t, SSD_WIDTH), BF16),
        scratch_shapes=[pltpu.VMEM((SSD_CHUNK + 2 * _CONV_TAIL, SSD_CONV_CH), F32),
                        pltpu.VMEM((SSD_GROUPS, SSD_STATE, SSD_GROUP_W), F32),
                        pltpu.VMEM((SSD_CHUNK, SSD_WIDTH), F32)],
        compiler_params=_cparams(("arbitrary", "arbitrary")),
        name="ssd_mixer",
    )(z, xbc, dt_raw, conv_w.astype(F32), conv_b.astype(F32)[None], pad(dt_bias), pad(a_log),
      d_x, norm_g.astype(F32)[None], expand)


def _route(logits):
    lane = lax.broadcasted_iota(jnp.int32, logits.shape, 1).astype(F32)
    big = float(ROUTE_PAD)
    is_g = lane < MOE_GROUPS
    gl = jnp.where(is_g, logits, -jnp.inf)
    gmax = jnp.max(gl, axis=1, keepdims=True)
    gsel = jnp.min(jnp.where(gl == gmax, lane, big), axis=1, keepdims=True)
    pg = 1.0 / jnp.sum(jnp.where(is_g, jnp.exp(logits - gmax), 0.0), axis=1, keepdims=True)
    lo = MOE_GROUPS + MOE_PER_GROUP * gsel
    ev = jnp.where((lane >= lo) & (lane < lo + MOE_PER_GROUP), logits, -jnp.inf)
    m1 = jnp.max(ev, axis=1, keepdims=True)
    i1 = jnp.min(jnp.where(ev == m1, lane, big), axis=1, keepdims=True)
    ev2 = jnp.where(lane == i1, -jnp.inf, ev)
    m2 = jnp.max(ev2, axis=1, keepdims=True)
    i2 = jnp.min(jnp.where(ev2 == m2, lane, big), axis=1, keepdims=True)
    e21 = jnp.exp(m2 - m1)
    w1 = pg / (1.0 + e21)
    w2 = pg * e21 / (1.0 + e21)
    out = jnp.where(lane == 0, i1 - MOE_GROUPS, 0.0)
    out = jnp.where(lane == 1, i2 - MOE_GROUPS, out)
    out = jnp.where(lane == 2, w1, out)
    return jnp.where(lane == 3, w2, out)


def _merge_kernel(*refs, n_x):
    x_refs = refs[:n_x]
    (ys5_ref, u_ref, yb_ref, yc_ref, n1g_ref, wg_ref, bg_ref, d_ref, w1_ref, w2_ref, ps5_ref, pat_ref,
     pssd_ref, wo_ref, n2g_ref, wrh_ref, wrm_ref, br_ref, x1_o, h2_o, route_o) = refs[n_x:]
    x = _x_sum(x_refs)
    hb = (_rms_scale(x) * n1g_ref[...]).astype(BF16)
    ya = ys5_ref[...].astype(F32) + d_ref[...] * u_ref[...].astype(F32)
    yab = jax.nn.gelu(ya).astype(BF16)
    ya = _dot(yab, w1_ref[...]) * jax.nn.sigmoid(_dot(yab, w2_ref[...]))
    branches = ((ya.astype(BF16), ps5_ref), (yb_ref[...], pat_ref), (yc_ref[...], pssd_ref))
    merged = None
    for b, (yv, p_ref) in enumerate(branches):
        gate = jax.nn.sigmoid(_dot(hb, wg_ref[:, b * D_MODEL:(b + 1) * D_MODEL]) + bg_ref[b:b + 1, :])
        term = gate * _dot(yv, p_ref[...])
        merged = term if merged is None else merged + term
    x1 = x + _dot(merged.astype(BF16), wo_ref[...])
    x1_o[...] = x1
    h2 = _rms_scale(x1) * n2g_ref[...]
    _store_row_tiles(h2_o, 0, h2)
    h_hi = h2.astype(BF16)
    h_mid = (h2 - h_hi.astype(F32)).astype(BF16)
    logits = _dot(h_hi, wrh_ref[...]) + _dot(h_hi, wrm_ref[...]) + _dot(h_mid, wrh_ref[...]) + br_ref[...]
    route_o[...] = _route(logits)


def _merge(xs, t, ys5, u, yb, yc, lw):
    tm = min(TM_PROJ, t)
    row = lambda w: pl.BlockSpec((tm, w), lambda i: (i, 0))
    consts = [lw['n1g'], lw['w_gate'], lw['b_gate'], lw['s5_d'], lw['glu_w1'], lw['glu_w2'], lw['p_s5'],
              lw['p_attn'], lw['p_ssd'], lw['w_out'], lw['n2g'], lw['w_router_hi'], lw['w_router_mid'],
              lw['b_router']]
    x_arrs, x_specs = _x_parts(xs, tm)
    return pl.pallas_call(
        functools.partial(_merge_kernel, n_x=len(x_arrs)),
        grid=(t // tm,),
        in_specs=x_specs + [row(S5_WIDTH), row(S5_WIDTH), row(ATTN_Q), row(SSD_WIDTH)]
                 + [_const_spec(c.shape) for c in consts],
        out_specs=[row(D_MODEL), pl.BlockSpec((tm * ROW_TILE, LANES), lambda i: (i, 0)), row(ROUTE_PAD)],
        out_shape=[jax.ShapeDtypeStruct((t, D_MODEL), F32), jax.ShapeDtypeStruct((t * ROW_TILE, LANES), F32),
                   jax.ShapeDtypeStruct((t, ROUTE_PAD), F32)],
        compiler_params=_cparams(("parallel",)),
        name="merge_router",
    )(*x_arrs, ys5, u, yb, yc, *consts)


def _moe_plan(route, t):
    n_exp = MOE_EXPERTS
    e = route[:, 0:2].astype(jnp.int32).reshape(-1)
    onehot = (e[:, None] == jnp.arange(n_exp, dtype=jnp.int32)[None, :]).astype(jnp.int32)
    csum = jnp.cumsum(onehot, axis=0)
    rank = jnp.sum(csum * onehot, axis=1) - 1
    counts = csum[-1]
    padded = ((counts + TM_G - 1) // TM_G) * TM_G
    pend = jnp.cumsum(padded)
    dest = (pend - padded)[e] + rank
    n_tiles = -(-(2 * t + n_exp * (TM_G - 1)) // TM_G)
    n_rows = n_tiles * TM_G
    pair = jnp.full((n_rows,), -1, jnp.int32).at[dest].set(jnp.arange(2 * t, dtype=jnp.int32),
                                                            unique_indices=True)
    valid = pair >= 0
    tok, k = pair // 2, pair % 2
    src = jnp.where(valid, tok, 0)
    spare = t + jnp.arange(n_rows, dtype=jnp.int32) % _MOE_SPARE
    dst = jnp.where(valid, k * (t + _MOE_SPARE) + tok, spare)
    tile_start = jnp.arange(n_tiles, dtype=jnp.int32) * TM_G
    tile_e = jnp.sum((pend[None, :] <= tile_start[:, None]).astype(jnp.int32), axis=1)
    tile_e = jnp.minimum(tile_e, n_exp - 1)
    return src.reshape(n_tiles, 1, TM_G), dst.reshape(n_tiles, 1, TM_G), tile_e


_MOE_SPARE = 2 * TM_G


def _gmm_kernel(te_ref, src_ref, srcn_ref, dst_ref, h2_hbm, wg_ref, wu_ref, wd_ref, out_hbm,
                xbuf, ybuf, gsem, ssem):
    del te_ref
    i = pl.program_id(0)
    last = pl.num_programs(0) - 1
    slot = i % 2

    tile_rows = TM_G * ROW_TILE

    def row_tile(ref, s, r):
        return ref.at[pl.ds((s * TM_G + r) * ROW_TILE, ROW_TILE), :]

    def slot_rows(ref, s):
        return ref.at[pl.ds(s * tile_rows, tile_rows), :]

    def gather_start(idx_ref, s):
        for r in range(TM_G):
            src = h2_hbm.at[pl.ds(pl.multiple_of(idx_ref[0, 0, r], ROW_TILE), ROW_TILE), :]
            pltpu.make_async_copy(src, row_tile(xbuf, s, r), gsem.at[s]).start()

    def gather_wait(s):
        pltpu.make_async_copy(h2_hbm.at[pl.ds(0, tile_rows), :], slot_rows(xbuf, s), gsem.at[s]).wait()

    def scatter_wait(s):
        pltpu.make_async_copy(slot_rows(ybuf, s), out_hbm.at[pl.ds(0, tile_rows), :], ssem.at[s]).wait()

    @pl.when(i == 0)
    def _():
        gather_start(src_ref, 0)
        ybuf[...] = jnp.zeros_like(ybuf)
        plane = out_hbm.shape[0] // 2
        spare0 = plane - _MOE_SPARE * ROW_TILE
        fills = [pltpu.make_async_copy(slot_rows(ybuf, s),
                                       out_hbm.at[pl.ds(k * plane + spare0 + s * tile_rows, tile_rows), :],
                                       ssem.at[s]) for k in range(2) for s in range(2)]
        for f in fills:
            f.start()
        for f in fills:
            f.wait()

    gather_wait(slot)

    @pl.when(i >= 2)
    def _():
        scatter_wait(slot)

    gather_start(srcn_ref, 1 - slot)

    xb = _load_row_tiles(xbuf, slot * TM_G, TM_G).astype(BF16)
    hid = _silu(_dot(xb, wg_ref[0].astype(BF16))) * _dot(xb, wu_ref[0].astype(BF16))
    _store_row_tiles(ybuf, slot * TM_G, _dot(hid.astype(BF16), wd_ref[0].astype(BF16)))
    for r in range(TM_G):
        dst = out_hbm.at[pl.ds(pl.multiple_of(dst_ref[0, 0, r], ROW_TILE), ROW_TILE), :]
        pltpu.make_async_copy(row_tile(ybuf, slot, r), dst, ssem.at[slot]).start()

    @pl.when(i == last)
    def _():
        scatter_wait(slot)
        gather_wait(1 - slot)

    @pl.when((i == last) & (i >= 1))
    def _():
        scatter_wait(1 - slot)


def _moe_sparse(h2, route, w_gate, w_up, w_down):
    t = h2.shape[0] // ROW_TILE
    src, dst, tile_e = _moe_plan(route, t)
    src, dst = src * ROW_TILE, dst * ROW_TILE
    n_tiles = src.shape[0]
    smem = lambda imap: pl.BlockSpec((1, 1, TM_G), imap, memory_space=pltpu.SMEM)
    grid_spec = pltpu.PrefetchScalarGridSpec(
        num_scalar_prefetch=1,
        grid=(n_tiles,),
        in_specs=[smem(lambda i, te: (i, 0, 0)),
                  smem(lambda i, te: (jnp.minimum(i + 1, n_tiles - 1), 0, 0)),
                  smem(lambda i, te: (i, 0, 0)),
                  pl.BlockSpec(memory_space=pl.ANY),
                  pl.BlockSpec((1, D_MODEL, MOE_FF), lambda i, te: (te[i], 0, 0)),
                  pl.BlockSpec((1, D_MODEL, MOE_FF), lambda i, te: (te[i], 0, 0)),
                  pl.BlockSpec((1, MOE_FF, D_MODEL), lambda i, te: (te[i], 0, 0))],
        out_specs=pl.BlockSpec(memory_space=pl.ANY),
        scratch_shapes=[pltpu.VMEM((2 * TM_G * ROW_TILE, LANES), F32), pltpu.VMEM((2 * TM_G * ROW_TILE, LANES), F32),
                        pltpu.SemaphoreType.DMA((2,)), pltpu.SemaphoreType.DMA((2,))])
    plane = (t + _MOE_SPARE) * ROW_TILE
    out = pl.pallas_call(
        _gmm_kernel,
        grid_spec=grid_spec,
        out_shape=jax.ShapeDtypeStruct((2 * plane, LANES), F32),
        compiler_params=_cparams(("arbitrary",)),
        name="moe_gmm",
    )(tile_e, src, src, dst, h2, w_gate, w_up, w_down)
    return out.reshape(2, plane, LANES)


def _sum_kernel(*refs):
    refs[-1][...] = _x_sum(refs[:-1])


def _residual_sum(xs, t):
    tm = min(TM_PROJ, t)
    x_arrs, x_specs = _x_parts(xs, tm)
    return pl.pallas_call(
        _sum_kernel,
        grid=(t // tm,),
        in_specs=x_specs,
        out_specs=pl.BlockSpec((tm, D_MODEL), lambda i: (i, 0)),
        out_shape=jax.ShapeDtypeStruct((t, D_MODEL), F32),
        compiler_params=_cparams(("parallel",)),
        name="residual_sum",
    )(*x_arrs)


def _cast_kernel(*refs):
    n = len(refs) // 2
    for x_ref, o_ref in zip(refs[:n], refs[n:]):
        o_ref[...] = x_ref[...].astype(o_ref.dtype)


def _cast_layer_bf16(ws, l):
    halves = 2
    in_specs = [pl.BlockSpec((None, w.shape[1] // halves, w.shape[2]), lambda i: (l, i, 0)) for w in ws]
    out_specs = [pl.BlockSpec((w.shape[1] // halves, w.shape[2]), lambda i: (i, 0)) for w in ws]
    return pl.pallas_call(
        _cast_kernel,
        grid=(halves,),
        in_specs=in_specs,
        out_specs=out_specs,
        out_shape=[jax.ShapeDtypeStruct(w.shape[1:], BF16) for w in ws],
        compiler_params=_cparams(("parallel",)),
        name="cast_weights",
    )(*[w.astype(F32) for w in ws])


_GATE_SHIFT = N_MIX % LANES
_GATE_BLOCK = 512
assert (N_MIX - _GATE_SHIFT) % _GATE_BLOCK == 0 and (N_BRANCH * D_MODEL) % _GATE_BLOCK == 0


def _gate_w_kernel(a_ref, b_ref, o_ref):
    full = jnp.concatenate([a_ref[...], b_ref[...]], axis=1)
    shifted = pltpu.roll(full, full.shape[1] - _GATE_SHIFT, 1)
    o_ref[...] = shifted[:, :_GATE_BLOCK].astype(o_ref.dtype)


def _gate_weights(w_in, l):
    base = (N_MIX - _GATE_SHIFT) // _GATE_BLOCK
    per = _GATE_BLOCK // LANES
    return pl.pallas_call(
        _gate_w_kernel,
        grid=(N_BRANCH * D_MODEL // _GATE_BLOCK,),
        in_specs=[pl.BlockSpec((None, D_MODEL, _GATE_BLOCK), lambda k: (l, 0, base + k)),
                  pl.BlockSpec((None, D_MODEL, LANES), lambda k: (l, 0, (base + k + 1) * per))],
        out_specs=pl.BlockSpec((D_MODEL, _GATE_BLOCK), lambda k: (0, k)),
        out_shape=jax.ShapeDtypeStruct((D_MODEL, N_BRANCH * D_MODEL), BF16),
        compiler_params=_cparams(("parallel",)),
        name="gate_weights",
    )(w_in, w_in)


def _layer_weights(l, p):
    glu_w1, glu_w2, p_s5, p_attn, p_ssd, w_out = _cast_layer_bf16(
        [p['s5_glu_w1'], p['s5_glu_w2'], p['p_s5'], p['p_attn'], p['p_ssd'], p['w_out']], l)
    w_router = jnp.concatenate([p['w_router_group'][l], p['w_router_expert'][l]], axis=1).astype(F32)
    npad = ROUTE_PAD - w_router.shape[1]
    b_router = jnp.concatenate([p['b_router_group'][l], p['b_router_expert'][l]]).astype(F32)
    w_router = jnp.pad(w_router, ((0, 0), (0, npad)))
    w_router_hi = w_router.astype(BF16)
    return dict(
        n1g=p['norm1_g'][l].astype(F32)[None],
        w_gate=_gate_weights(p['w_in'].astype(F32), l), b_gate=p['b_gate'][l].astype(F32),
        s5_d=p['s5_d'][l].astype(F32)[None],
        glu_w1=glu_w1, glu_w2=glu_w2, p_s5=p_s5, p_attn=p_attn, p_ssd=p_ssd, w_out=w_out,
        n2g=p['norm2_g'][l].astype(F32)[None],
        w_router_hi=w_router_hi, w_router_mid=(w_router - w_router_hi.astype(F32)).astype(BF16),
        b_router=jnp.pad(b_router, (0, npad))[None])


def _layer(xs, t, l, p, bsz):
    lw = _layer_weights(l, p)
    u, q, k, v, z, xbc, dt_raw = _inproj(xs, t, lw['n1g'], p['w_in'].astype(F32), l)
    tabs = _s5_tables(p['s5_lambda_re'][l], p['s5_lambda_im'][l], p['s5_b_re'][l], p['s5_b_im'][l],
                      p['s5_c_re'][l], p['s5_c_im'][l], p['s5_log_dt'][l])
    ys5 = _s5_mixer(u, tabs, bsz)
    yb = _swa_attention(q, k, v, p['q_norm_g'][l], p['k_norm_g'][l], p['attn_sinks'][l], bsz)
    yc = _ssd_mixer(z, xbc, dt_raw, p['ssd_conv_w'][l], p['ssd_conv_b'][l], p['ssd_dt_bias'][l],
                    p['ssd_a_log'][l], p['ssd_d'][l], p['ssd_norm_g'][l], bsz)
    x1, h2, route = _merge(xs, t, ys5, u, yb, yc, lw)
    moe = _moe_sparse(h2, route, p['w_exp_gate'][l], p['w_exp_up'][l], p['w_exp_down'][l])
    return [x1, route, moe]


def kernel(x, norm1_g, w_in, b_gate, s5_lambda_re, s5_lambda_im, s5_b_re, s5_b_im, s5_c_re, s5_c_im, s5_d, s5_log_dt, s5_glu_w1, s5_glu_w2, q_norm_g, k_norm_g, attn_sinks, ssd_conv_w, ssd_conv_b, ssd_dt_bias, ssd_a_log, ssd_d, ssd_norm_g, p_s5, p_attn, p_ssd, w_out, norm2_g, w_router_group, b_router_group, w_router_expert, b_router_expert, w_exp_gate, w_exp_up, w_exp_down):
    p = dict(norm1_g=norm1_g, w_in=w_in, b_gate=b_gate, s5_lambda_re=s5_lambda_re, s5_lambda_im=s5_lambda_im,
             s5_b_re=s5_b_re, s5_b_im=s5_b_im, s5_c_re=s5_c_re, s5_c_im=s5_c_im, s5_d=s5_d,
             s5_log_dt=s5_log_dt, s5_glu_w1=s5_glu_w1, s5_glu_w2=s5_glu_w2, q_norm_g=q_norm_g,
             k_norm_g=k_norm_g, attn_sinks=attn_sinks, ssd_conv_w=ssd_conv_w, ssd_conv_b=ssd_conv_b,
             ssd_dt_bias=ssd_dt_bias, ssd_a_log=ssd_a_log, ssd_d=ssd_d, ssd_norm_g=ssd_norm_g, p_s5=p_s5,
             p_attn=p_attn, p_ssd=p_ssd, w_out=w_out, norm2_g=norm2_g, w_router_group=w_router_group,
             b_router_group=b_router_group, w_router_expert=w_router_expert, b_router_expert=b_router_expert,
             w_exp_gate=w_exp_gate, w_exp_up=w_exp_up, w_exp_down=w_exp_down)
    bsz, seq, dm = x.shape
    depth = w_in.shape[0]
    t = bsz * seq
    xs = [x.reshape(t, dm)]
    for l in range(depth):
        xs = _layer(xs, t, l, p, bsz)
    return _residual_sum(xs, t).reshape(bsz, seq, dm)
```

```python
import functools
import math

import jax
import jax.numpy as jnp
from jax import lax
from jax.experimental import pallas as pl
from jax.experimental.pallas import tpu as pltpu

F32 = jnp.float32
BF16 = jnp.bfloat16

D_MODEL = 1024
NORM_EPS = 1e-6
S5_WIDTH = 512
S5_GROUP = 16
S5_GROUPS = 32
S5_STATE = 64
HEAD_DIM = 64
ATTN_HEADS = 8
ATTN_KV_HEADS = 2
ATTN_REP = ATTN_HEADS // ATTN_KV_HEADS
ATTN_Q = ATTN_HEADS * HEAD_DIM
ATTN_KV = ATTN_KV_HEADS * HEAD_DIM
ATTN_BLOCK = 128
ROPE_THETA = 10000.0
SSD_WIDTH = 1024
SSD_HEAD_DIM = 64
SSD_HEADS = 16
SSD_GROUPS = 2
SSD_STATE = 64
SSD_CONV = 4
SSD_CHUNK = 128
SSD_BC = SSD_GROUPS * SSD_STATE
SSD_CONV_CH = SSD_WIDTH + 2 * SSD_BC
SSD_GROUP_W = SSD_WIDTH // SSD_GROUPS
N_BRANCH = 3
MOE_GROUPS = 4
MOE_PER_GROUP = 8
MOE_EXPERTS = 32
MOE_FF = 512
N_MIX = S5_WIDTH + ATTN_Q + 2 * ATTN_KV + SSD_WIDTH + SSD_CONV_CH + SSD_HEADS

LANES = 128
SUBLANES = 8
VMEM_LIMIT_BYTES = 56 * 1024 * 1024

S5_CHUNK = 16
S5_TG = 8
ATTN_TILE = 512
TM_PROJ = 512
TM_G = 256
DT_PAD = LANES
N_MIX_PAD = N_MIX - SSD_HEADS + DT_PAD
ROUTE_PAD = LANES


def _cparams(semantics):
    return pltpu.CompilerParams(dimension_semantics=semantics, vmem_limit_bytes=VMEM_LIMIT_BYTES)


def _const_spec(shape):
    zeros = (0,) * len(shape)
    return pl.BlockSpec(shape, lambda *_: zeros, pipeline_mode=pl.Buffered(1))


def _dot(a, b):
    return jnp.dot(a, b, preferred_element_type=F32)


def _split3(a):
    hi = a.astype(BF16)
    r1 = a - hi.astype(F32)
    mid = r1.astype(BF16)
    lo = (r1 - mid.astype(F32)).astype(BF16)
    return hi, mid, lo


def _dot_f32_lhs(a, b_bf16):
    hi, mid, lo = _split3(a)
    return _dot(hi, b_bf16) + _dot(mid, b_bf16) + _dot(lo, b_bf16)


def _rms_scale(x):
    return x * lax.rsqrt(jnp.mean(x * x, axis=-1, keepdims=True) + NORM_EPS)


def _silu(x):
    return x * jax.nn.sigmoid(x)


_OFF_U = 0
_OFF_Q = _OFF_U + S5_WIDTH
_OFF_K = _OFF_Q + ATTN_Q
_OFF_V = _OFF_K + ATTN_KV
_OFF_Z = _OFF_V + ATTN_KV
_OFF_XBC = _OFF_Z + SSD_WIDTH
_OFF_DT = _OFF_XBC + SSD_CONV_CH


ROW_TILE = D_MODEL // LANES


def _store_row_tiles(ref, start, val):
    n = val.shape[0]
    for j in range(ROW_TILE):
        ref[pl.ds(start * ROW_TILE + j, n, stride=ROW_TILE), :] = val[:, j * LANES:(j + 1) * LANES]


def _load_row_tiles(ref, start, n):
    return jnp.concatenate([ref[pl.ds(start * ROW_TILE + j, n, stride=ROW_TILE), :] for j in range(ROW_TILE)],
                           axis=1)


def _x_parts(xs, tm):
    row = lambda w: pl.BlockSpec((tm, w), lambda i: (i, 0))
    if len(xs) == 1:
        return list(xs), [row(D_MODEL)]
    x1, route, moe = xs
    plane = lambda k: pl.BlockSpec((None, tm * ROW_TILE, LANES), lambda i: (k, i, 0))
    return [x1, route, moe, moe], [row(D_MODEL), row(ROUTE_PAD), plane(0), plane(1)]


def _x_sum(x_refs):
    if len(x_refs) == 1:
        return x_refs[0][...]
    x1_ref, route_ref, m0_ref, m1_ref = x_refs
    r = route_ref[...]
    n = x1_ref.shape[0]
    return x1_ref[...] + r[:, 2:3] * _load_row_tiles(m0_ref, 0, n) + r[:, 3:4] * _load_row_tiles(m1_ref, 0, n)


def _inproj_kernel(*refs, n_x):
    x_refs, (g_ref, w_ref), outs, wb = refs[:n_x], refs[n_x:n_x + 2], refs[n_x + 2:-1], refs[-1]

    @pl.when(pl.program_id(0) == 0)
    def _():
        wb[...] = w_ref[...].astype(BF16)

    hb = (_rms_scale(_x_sum(x_refs)) * g_ref[...]).astype(BF16)
    offs = (_OFF_U, _OFF_Q, _OFF_K, _OFF_V, _OFF_Z, _OFF_XBC, _OFF_DT)
    for o_ref, off in zip(outs, offs):
        width = o_ref.shape[1]
        o_ref[...] = _dot(hb, wb[:, off:off + width]).astype(o_ref.dtype)


def _inproj(xs, t, g, w_in, l):
    tm = min(TM_PROJ, t)
    widths = (S5_WIDTH, ATTN_Q, ATTN_KV, ATTN_KV, SSD_WIDTH, SSD_CONV_CH, DT_PAD)
    dtypes = (F32, BF16, BF16, BF16, BF16, BF16, F32)
    x_arrs, x_specs = _x_parts(xs, tm)
    w_spec = pl.BlockSpec((None, D_MODEL, N_MIX_PAD), lambda i: (l, 0, 0), pipeline_mode=pl.Buffered(1))
    return pl.pallas_call(
        functools.partial(_inproj_kernel, n_x=len(x_arrs)),
        grid=(t // tm,),
        in_specs=x_specs + [_const_spec((1, D_MODEL)), w_spec],
        out_specs=[pl.BlockSpec((tm, w), lambda i: (i, 0)) for w in widths],
        out_shape=[jax.ShapeDtypeStruct((t, w), d) for w, d in zip(widths, dtypes)],
        scratch_shapes=[pltpu.VMEM((D_MODEL, N_MIX_PAD), BF16)],
        compiler_params=_cparams(("arbitrary",)),
        name="inproj",
    )(*x_arrs, g, w_in)


def _s5_tables(lam_re, lam_im, b_re, b_im, c_re, c_im, log_dt):
    hp = lax.Precision.HIGHEST
    g_n, p_n = lam_re.shape
    c_n = b_re.shape[-1]
    t_n = S5_CHUNK
    lr, li = lam_re.astype(F32), lam_im.astype(F32)
    dt = jnp.exp(log_dt.astype(F32))[:, None]
    mag = jnp.exp(lr * dt)
    ab_re = mag * jnp.cos(li * dt)
    ab_im = mag * jnp.sin(li * dt)
    nr = ab_re - 1.0
    den = lr * lr + li * li
    f_re = (nr * lr + ab_im * li) / den
    f_im = (ab_im * lr - nr * li) / den
    br, bi = b_re.astype(F32), b_im.astype(F32)
    bb_re = f_re[..., None] * br - f_im[..., None] * bi
    bb_im = f_re[..., None] * bi + f_im[..., None] * br
    j = jnp.arange(t_n + 1, dtype=F32)[:, None, None]
    pmag = jnp.exp(lr * dt * j)
    ang = li * dt * j
    p_re = pmag * jnp.cos(ang)
    p_im = pmag * jnp.sin(ang)
    cr, ci = c_re.astype(F32), c_im.astype(F32)
    ca_re = cr[None] * p_re[:, :, None, :] - ci[None] * p_im[:, :, None, :]
    ca_im = cr[None] * p_im[:, :, None, :] + ci[None] * p_re[:, :, None, :]
    kern = (jnp.einsum('jgcp,gpd->jgcd', ca_re[:t_n], bb_re, precision=hp)
            - jnp.einsum('jgcp,gpd->jgcd', ca_im[:t_n], bb_im, precision=hp))
    q_re = p_re[t_n - 1 - jnp.arange(t_n)]
    q_im = p_im[t_n - 1 - jnp.arange(t_n)]
    bs_re = q_re[..., None] * bb_re[None] - q_im[..., None] * bb_im[None]
    bs_im = q_re[..., None] * bb_im[None] + q_im[..., None] * bb_re[None]

    nt = g_n // S5_TG
    k_strip = kern.reshape(t_n, nt, S5_TG, c_n, c_n).transpose(1, 0, 4, 2, 3)
    k_strip = k_strip.reshape(nt, t_n, c_n, S5_TG * c_n)

    def in_strip(a):
        a = a.reshape(t_n, nt, S5_TG, p_n, c_n).transpose(1, 0, 4, 2, 3)
        return a.reshape(nt, t_n, c_n, S5_TG * p_n)

    def out_strip(a):
        a = a.reshape(t_n, nt, S5_TG, c_n, p_n).transpose(1, 4, 0, 2, 3)
        return a.reshape(nt, p_n, t_n * S5_TG * c_n)

    return dict(
        k=k_strip, bs_re=in_strip(bs_re), bs_im=in_strip(bs_im),
        co_re=out_strip(ca_re[1:]), co_im=out_strip(-ca_im[1:]),
        at_re=p_re[t_n].reshape(nt, 1, S5_TG * p_n), at_im=p_im[t_n].reshape(nt, 1, S5_TG * p_n))


_S5_XW = S5_CHUNK * LANES
_S5_SW = S5_TG * S5_STATE


def _group_block(strip, lanes_per_group):
    rpg = strip.shape[0]
    full = jnp.concatenate([strip] * S5_TG, axis=0)
    row_g = lax.broadcasted_iota(jnp.int32, full.shape, 0) // rpg
    lane_g = (lax.broadcasted_iota(jnp.int32, full.shape, 1) % (S5_TG * lanes_per_group)) // lanes_per_group
    return jnp.where(row_g == lane_g, full, 0.0).astype(BF16)


def _s5_build_tables(k_ref, bre_ref, bim_ref, cre_ref, cim_ref, big, bsre, bsim, core, coim):
    blocks = [_group_block(k_ref[0, j], S5_GROUP) for j in range(S5_CHUNK)]
    zero = jnp.zeros((LANES, LANES), BF16)
    for s in range(S5_CHUNK):
        for t in range(S5_CHUNK):
            big[s * LANES:(s + 1) * LANES, t * LANES:(t + 1) * LANES] = blocks[t - s] if t >= s else zero
    for src, dst in ((bre_ref, bsre), (bim_ref, bsim)):
        for s in range(S5_CHUNK):
            dst[s * LANES:(s + 1) * LANES, :] = _group_block(src[0, s], S5_STATE)
    for src, dst in ((cre_ref, core), (cim_ref, coim)):
        for t in range(S5_CHUNK):
            cols = slice(t * LANES, (t + 1) * LANES)
            dst[:, cols] = _group_block(src[0, :, cols], S5_GROUP)


def _s5_kernel(u_ref, k_ref, bre_ref, bim_ref, cre_ref, cim_ref, are_ref, aim_ref, y_ref,
               big, bsre, bsim, core, coim, sre, sim, hre, him):
    @pl.when(pl.program_id(1) == 0)
    def _():
        _s5_build_tables(k_ref, bre_ref, bim_ref, cre_ref, cim_ref, big, bsre, bsim, core, coim)

    nck = u_ref.shape[0] // S5_CHUNK
    xcat = jnp.concatenate([u_ref[pl.ds(s, nck, stride=S5_CHUNK), :].astype(BF16) for s in range(S5_CHUNK)],
                           axis=1)
    sre[...] = _dot(xcat, bsre[...])
    sim[...] = _dot(xcat, bsim[...])
    a_re = are_ref[0]
    a_im = aim_ref[0]

    def step(i, carry):
        h_re, h_im = carry
        base = pl.multiple_of(i * SUBLANES, SUBLANES)
        s_re = sre[pl.ds(base, SUBLANES), :]
        s_im = sim[pl.ds(base, SUBLANES), :]
        ent_re, ent_im = [], []
        for j in range(SUBLANES):
            ent_re.append(h_re)
            ent_im.append(h_im)
            h_re, h_im = (a_re * h_re - a_im * h_im + s_re[j:j + 1], a_re * h_im + a_im * h_re + s_im[j:j + 1])
        hre[pl.ds(base, SUBLANES), :] = jnp.concatenate(ent_re, axis=0)
        him[pl.ds(base, SUBLANES), :] = jnp.concatenate(ent_im, axis=0)
        return h_re, h_im

    zero = jnp.zeros((1, _S5_SW), F32)
    lax.fori_loop(0, nck // SUBLANES, step, (zero, zero))
    y = (_dot(xcat, big[...]) + _dot(hre[...].astype(BF16), core[...])
         + _dot(him[...].astype(BF16), coim[...]))
    for t in range(S5_CHUNK):
        y_ref[pl.ds(t, nck, stride=S5_CHUNK), :] = y[:, t * LANES:(t + 1) * LANES]


def _s5_mixer(u, tabs, bsz):
    t = u.shape[0]
    seq = t // bsz
    nck = seq // S5_CHUNK
    nt = S5_GROUPS // S5_TG
    strip = lambda a: pl.BlockSpec((1,) + a.shape[1:], lambda x, b: (x,) + (0,) * (a.ndim - 1))
    strips = [tabs['k'], tabs['bs_re'], tabs['bs_im'], tabs['co_re'], tabs['co_im'], tabs['at_re'], tabs['at_im']]
    return pl.pallas_call(
        _s5_kernel,
        grid=(nt, bsz),
        in_specs=[pl.BlockSpec((seq, LANES), lambda x, b: (b, x))] + [strip(a) for a in strips],
        out_specs=pl.BlockSpec((seq, LANES), lambda x, b: (b, x)),
        out_shape=jax.ShapeDtypeStruct((t, S5_WIDTH), F32),
        scratch_shapes=[pltpu.VMEM((_S5_XW, _S5_XW), BF16),
                        pltpu.VMEM((_S5_XW, _S5_SW), BF16), pltpu.VMEM((_S5_XW, _S5_SW), BF16),
                        pltpu.VMEM((_S5_SW, _S5_XW), BF16), pltpu.VMEM((_S5_SW, _S5_XW), BF16)]
                       + [pltpu.VMEM((nck, _S5_SW), F32)] * 4,
        compiler_params=_cparams(("arbitrary", "arbitrary")),
        name="s5_mixer",
    )(u, *strips)


def _swap_rope_halves(y):
    w = y.shape[1]
    lane = lax.broadcasted_iota(jnp.int32, y.shape, 1)
    lower = (lane % HEAD_DIM) < (HEAD_DIM // 2)
    return jnp.where(lower, pltpu.roll(y, w - HEAD_DIM // 2, 1), pltpu.roll(y, HEAD_DIM // 2, 1))


def _norm_rope(x, gain, head_mean, cos2, sin2):
    reps = x.shape[1] // LANES
    ms = _dot_f32_lhs(x * x, head_mean)
    y = x * lax.rsqrt(ms + NORM_EPS) * gain
    if reps > 1:
        cos2 = jnp.concatenate([cos2] * reps, axis=1)
        sin2 = jnp.concatenate([sin2] * reps, axis=1)
    return y * cos2 + _swap_rope_halves(y) * sin2


def _attn_kernel(q_ref, kc_ref, kp_ref, vc_ref, vp_ref, cosc_ref, sinc_ref, cosp_ref, sinp_ref,
                 qg_ref, kg_ref, sink_ref, hmq_ref, hmk_ref, o_ref):
    seq_start = pl.program_id(1) == 0
    blk = ATTN_BLOCK
    nblk = q_ref.shape[0] // blk
    q = _norm_rope(q_ref[...].astype(F32), qg_ref[...], hmq_ref[...], cosc_ref[...], sinc_ref[...])
    q = (q * (HEAD_DIM ** -0.5)).astype(BF16)
    kc = _norm_rope(kc_ref[...].astype(F32), kg_ref[...], hmk_ref[...], cosc_ref[...], sinc_ref[...]).astype(BF16)
    kp = _norm_rope(kp_ref[...].astype(F32), kg_ref[...], hmk_ref[...], cosp_ref[...], sinp_ref[...]).astype(BF16)
    k_all = jnp.concatenate([kp, kc], axis=0)
    v_all = jnp.concatenate([vp_ref[...], vc_ref[...]], axis=0)
    shape = (2 * blk, ATTN_REP * blk)
    kj = lax.broadcasted_iota(jnp.int32, shape, 0)
    qi = lax.broadcasted_iota(jnp.int32, shape, 1) % blk + blk
    band = (kj <= qi) & (qi - kj < blk)
    band_first = band & ((kj >= blk) | jnp.logical_not(seq_start))
    sinks = sink_ref[...]
    for j in range(ATTN_KV_HEADS):
        sl = slice(j * HEAD_DIM, (j + 1) * HEAD_DIM)
        heads = [j * ATTN_REP + r for r in range(ATTN_REP)]
        sink = jnp.concatenate([jnp.broadcast_to(sinks[:, h:h + 1], (1, blk)) for h in heads], axis=1)
        for n in range(nblk):
            rows = slice(n * blk, (n + 1) * blk)
            kb = k_all[n * blk:(n + 2) * blk, sl]
            vb = v_all[n * blk:(n + 2) * blk, sl]
            q4 = jnp.concatenate([q[rows, h * HEAD_DIM:(h + 1) * HEAD_DIM] for h in heads], axis=0)
            s = lax.dot_general(kb, q4, (((1,), (1,)), ((), ())), preferred_element_type=F32)
            s = jnp.where(band_first if n == 0 else band, s, -jnp.inf)
            m = jnp.maximum(jnp.max(s, axis=0, keepdims=True), sink)
            p = jnp.exp(s - m)
            denom = jnp.sum(p, axis=0, keepdims=True) + jnp.exp(sink - m)
            pn = (p * (1.0 / denom)).astype(BF16)
            out = lax.dot_general(pn, vb, (((0,), (0,)), ((), ())), preferred_element_type=F32)
            for r, h in enumerate(heads):
                o_ref[rows, h * HEAD_DIM:(h + 1) * HEAD_DIM] = out[r * blk:(r + 1) * blk].astype(o_ref.dtype)


def _rope_tables(seq):
    half = HEAD_DIM // 2
    inv = jnp.power(ROPE_THETA, -jnp.arange(half, dtype=F32) * 2.0 / HEAD_DIM)
    ang = jnp.arange(seq, dtype=F32)[:, None] * inv[None, :]
    cos, sin = jnp.cos(ang), jnp.sin(ang)
    cos2 = jnp.concatenate([cos, cos, cos, cos], axis=1)
    sin2 = jnp.concatenate([-sin, sin, -sin, sin], axis=1)
    return cos2, sin2


def _head_mean_matrix(width):
    i = jnp.arange(width)
    return jnp.where((i[:, None] // HEAD_DIM) == (i[None, :] // HEAD_DIM), 1.0 / HEAD_DIM, 0.0).astype(BF16)


def _swa_attention(q, k, v, q_g, k_g, sinks, bsz):
    t = q.shape[0]
    seq = t // bsz
    tile = min(ATTN_TILE, seq)
    nt = seq // tile
    bpt = tile // ATTN_BLOCK
    nb = seq // ATTN_BLOCK
    cos2, sin2 = _rope_tables(seq)
    qg = jnp.tile(q_g.astype(F32), ATTN_HEADS)[None]
    kg = jnp.tile(k_g.astype(F32), ATTN_KV_HEADS)[None]
    cur = lambda b, n: (b * nt + n, 0)
    prev = lambda b, n: (b * nb + jnp.maximum(n * bpt - 1, 0), 0)
    tcur = lambda b, n: (n, 0)
    tprev = lambda b, n: (jnp.maximum(n * bpt - 1, 0), 0)
    blk = ATTN_BLOCK
    return pl.pallas_call(
        _attn_kernel,
        grid=(bsz, nt),
        in_specs=[pl.BlockSpec((tile, ATTN_Q), cur),
                  pl.BlockSpec((tile, ATTN_KV), cur), pl.BlockSpec((blk, ATTN_KV), prev),
                  pl.BlockSpec((tile, ATTN_KV), cur), pl.BlockSpec((blk, ATTN_KV), prev),
                  pl.BlockSpec((tile, LANES), tcur), pl.BlockSpec((tile, LANES), tcur),
                  pl.BlockSpec((blk, LANES), tprev), pl.BlockSpec((blk, LANES), tprev),
                  _const_spec((1, ATTN_Q)), _const_spec((1, ATTN_KV)), _const_spec((1, ATTN_HEADS)),
                  _const_spec((ATTN_Q, ATTN_Q)), _const_spec((ATTN_KV, ATTN_KV))],
        out_specs=pl.BlockSpec((tile, ATTN_Q), cur),
        out_shape=jax.ShapeDtypeStruct((t, ATTN_Q), BF16),
        compiler_params=_cparams(("parallel", "parallel")),
        name="swa_attention",
    )(q, k, k, v, v, cos2, sin2, cos2, sin2, qg, kg, sinks.astype(F32)[None],
      _head_mean_matrix(ATTN_Q), _head_mean_matrix(ATTN_KV))


_CONV_TAIL = SUBLANES


def _ssd_kernel(z_ref, xbc_ref, dt_ref, cw_ref, cb_ref, dtb_ref, alog_ref, dskip_ref, ng_ref, exp_ref,
                o_ref, xext, state, ybuf):
    c = pl.program_id(1)
    ch = SSD_CHUNK

    @pl.when(c == 0)
    def _():
        xext[0:_CONV_TAIL, :] = jnp.zeros((_CONV_TAIL, SSD_CONV_CH), F32)
        state[...] = jnp.zeros_like(state)

    xext[_CONV_TAIL:_CONV_TAIL + ch, :] = xbc_ref[...].astype(F32)
    conv = cb_ref[...]
    for j in range(SSD_CONV):
        conv = conv + cw_ref[j:j + 1, :] * xext[pl.ds(_CONV_TAIL - (SSD_CONV - 1) + j, ch), :]
    xext[0:_CONV_TAIL, :] = xext[ch:ch + _CONV_TAIL, :]
    act = _silu(conv)
    xs = act[:, :SSD_WIDTH]
    bm = act[:, SSD_WIDTH:SSD_WIDTH + SSD_BC].astype(BF16)
    cm = act[:, SSD_WIDTH + SSD_BC:].astype(BF16)

    lane = lax.broadcasted_iota(jnp.int32, (ch, LANES), 1)
    xdt = dt_ref[...] + dtb_ref[...]
    dt = jnp.maximum(xdt, 0.0) + jnp.log1p(jnp.exp(-jnp.abs(xdt)))
    dt = jnp.where(lane < SSD_HEADS, dt, 0.0)
    a = dt * (-jnp.exp(alog_ref[...]))
    row = lax.broadcasted_iota(jnp.int32, (ch, ch), 0)
    col = lax.broadcasted_iota(jnp.int32, (ch, ch), 1)
    causal = row >= col
    hi, mid, lo = _split3(a)
    tril = causal.astype(BF16)
    cs = _dot(tril, hi) + _dot(tril, mid) + _dot(tril, lo)
    cs_t = cs.T
    expand = exp_ref[...]
    dt_x = _dot_f32_lhs(dt, expand)
    cs_x = _dot_f32_lhs(cs, expand)
    cs_last_x = cs_x[ch - 1:ch, :]
    xdt_full = xs * dt_x
    in_decay = jnp.exp(cs_x)
    out_decay = jnp.exp(cs_last_x - cs_x)
    chunk_decay = jnp.exp(cs_last_x)
    xw = (xdt_full * out_decay).astype(BF16)
    xdt_b = xdt_full.astype(BF16)
    hpg = SSD_HEADS // SSD_GROUPS
    for g in range(SSD_GROUPS):
        gs = slice(g * SSD_STATE, (g + 1) * SSD_STATE)
        ws = slice(g * SSD_GROUP_W, (g + 1) * SSD_GROUP_W)
        bg = bm[:, gs]
        cg = cm[:, gs]
        cb = lax.dot_general(cg, bg, (((1,), (1,)), ((), ())), preferred_element_type=F32)
        prev = state[g]
        ybuf[:, ws] = _dot(cg, prev.astype(BF16)) * in_decay[:, ws]
        for h in range(hpg):
            hh = g * hpg + h
            seg = cs[:, hh:hh + 1] - cs_t[hh:hh + 1, :]
            lmat = jnp.where(causal, jnp.exp(seg), 0.0)
            hs = slice(hh * SSD_HEAD_DIM, (hh + 1) * SSD_HEAD_DIM)
            ybuf[:, hs] += _dot((cb * lmat).astype(BF16), xdt_b[:, hs])
        upd = lax.dot_general(bg, xw[:, ws], (((0,), (0,)), ((), ())), preferred_element_type=F32)
        state[g] = prev * chunk_decay[:, ws] + upd

    y = ybuf[...] + xs * dskip_ref[...]
    y = y * _silu(z_ref[...].astype(F32))
    parts = []
    for g in range(SSD_GROUPS):
        ws = slice(g * SSD_GROUP_W, (g + 1) * SSD_GROUP_W)
        parts.append(_rms_scale(y[:, ws]))
    o_ref[...] = (jnp.concatenate(parts, axis=1) * ng_ref[...]).astype(o_ref.dtype)


def _ssd_mixer(z, xbc, dt_raw, conv_w, conv_b, dt_bias, a_log, d_skip, norm_g, bsz):
    t = z.shape[0]
    seq = t // bsz
    nc = seq // SSD_CHUNK
    pad = lambda v: jnp.pad(v.astype(F32), (0, LANES - SSD_HEADS))[None]
    d_x = jnp.repeat(d_skip.astype(F32), SSD_HEAD_DIM)[None]
    hid = jnp.arange(SSD_WIDTH) // SSD_HEAD_DIM
    expand = (jnp.arange(LANES)[:, None] == hid[None, :]).astype(BF16)
    blk = lambda b, c: (b * nc + c, 0)
    return pl.pallas_call(
        _ssd_kernel,
        grid=(bsz, nc),
        in_specs=[pl.BlockSpec((SSD_CHUNK, SSD_WIDTH), blk),
                  pl.BlockSpec((SSD_CHUNK, SSD_CONV_CH), blk),
                  pl.BlockSpec((SSD_CHUNK, DT_PAD), blk),
                  _const_spec((SSD_CONV, SSD_CONV_CH)), _const_spec((1, SSD_CONV_CH)),
                  _const_spec((1, LANES)), _const_spec((1, LANES)),
                  _const_spec((1, SSD_WIDTH)), _const_spec((1, SSD_WIDTH)),
                  _const_spec((LANES, SSD_WIDTH))],
        out_specs=pl.BlockSpec((SSD_CHUNK, SSD_WIDTH), blk),
        out_shape=jax.ShapeDtypeStruct((t, SSD_WIDTH), BF16),
        scratch_shapes=[pltpu.VMEM((SSD_CHUNK + 2 * _CONV_TAIL, SSD_CONV_CH), F32),
                        pltpu.VMEM((SSD_GROUPS, SSD_STATE, SSD_GROUP_W), F32),
                        pltpu.VMEM((SSD_CHUNK, SSD_WIDTH), F32)],
        compiler_params=_cparams(("arbitrary", "arbitrary")),
        name="ssd_mixer",
    )(z, xbc, dt_raw, conv_w.astype(F32), conv_b.astype(F32)[None], pad(dt_bias), pad(a_log),
      d_x, norm_g.astype(F32)[None], expand)


def _route(logits):
    lane = lax.broadcasted_iota(jnp.int32, logits.shape, 1).astype(F32)
    big = float(ROUTE_PAD)
    is_g = lane < MOE_GROUPS
    gl = jnp.where(is_g, logits, -jnp.inf)
    gmax = jnp.max(gl, axis=1, keepdims=True)
    gsel = jnp.min(jnp.where(gl == gmax, lane, big), axis=1, keepdims=True)
    pg = 1.0 / jnp.sum(jnp.where(is_g, jnp.exp(logits - gmax), 0.0), axis=1, keepdims=True)
    lo = MOE_GROUPS + MOE_PER_GROUP * gsel
    ev = jnp.where((lane >= lo) & (lane < lo + MOE_PER_GROUP), logits, -jnp.inf)
    m1 = jnp.max(ev, axis=1, keepdims=True)
    i1 = jnp.min(jnp.where(ev == m1, lane, big), axis=1, keepdims=True)
    ev2 = jnp.where(lane == i1, -jnp.inf, ev)
    m2 = jnp.max(ev2, axis=1, keepdims=True)
    i2 = jnp.min(jnp.where(ev2 == m2, lane, big), axis=1, keepdims=True)
    e21 = jnp.exp(m2 - m1)
    w1 = pg / (1.0 + e21)
    w2 = pg * e21 / (1.0 + e21)
    out = jnp.where(lane == 0, i1 - MOE_GROUPS, 0.0)
    out = jnp.where(lane == 1, i2 - MOE_GROUPS, out)
    out = jnp.where(lane == 2, w1, out)
    return jnp.where(lane == 3, w2, out)


def _merge_kernel(*refs, n_x):
    x_refs = refs[:n_x]
    (ys5_ref, u_ref, yb_ref, yc_ref, n1g_ref, wg_ref, bg_ref, d_ref, w1_ref, w2_ref, ps5_ref, pat_ref,
     pssd_ref, wo_ref, n2g_ref, wrh_ref, wrm_ref, br_ref, x1_o, h2_o, route_o) = refs[n_x:]
    x = _x_sum(x_refs)
    hb = (_rms_scale(x) * n1g_ref[...]).astype(BF16)
    ya = ys5_ref[...].astype(F32) + d_ref[...] * u_ref[...].astype(F32)
    yab = jax.nn.gelu(ya).astype(BF16)
    ya = _dot(yab, w1_ref[...]) * jax.nn.sigmoid(_dot(yab, w2_ref[...]))
    branches = ((ya.astype(BF16), ps5_ref), (yb_ref[...], pat_ref), (yc_ref[...], pssd_ref))
    merged = None
    for b, (yv, p_ref) in enumerate(branches):
        gate = jax.nn.sigmoid(_dot(hb, wg_ref[:, b * D_MODEL:(b + 1) * D_MODEL]) + bg_ref[b:b + 1, :])
        term = gate * _dot(yv, p_ref[...])
        merged = term if merged is None else merged + term
    x1 = x + _dot(merged.astype(BF16), wo_ref[...])
    x1_o[...] = x1
    h2 = _rms_scale(x1) * n2g_ref[...]
    _store_row_tiles(h2_o, 0, h2)
    h_hi = h2.astype(BF16)
    h_mid = (h2 - h_hi.astype(F32)).astype(BF16)
    logits = _dot(h_hi, wrh_ref[...]) + _dot(h_hi, wrm_ref[...]) + _dot(h_mid, wrh_ref[...]) + br_ref[...]
    route_o[...] = _route(logits)


def _merge(xs, t, ys5, u, yb, yc, lw):
    tm = min(TM_PROJ, t)
    row = lambda w: pl.BlockSpec((tm, w), lambda i: (i, 0))
    consts = [lw['n1g'], lw['w_gate'], lw['b_gate'], lw['s5_d'], lw['glu_w1'], lw['glu_w2'], lw['p_s5'],
              lw['p_attn'], lw['p_ssd'], lw['w_out'], lw['n2g'], lw['w_router_hi'], lw['w_router_mid'],
              lw['b_router']]
    x_arrs, x_specs = _x_parts(xs, tm)
    return pl.pallas_call(
        functools.partial(_merge_kernel, n_x=len(x_arrs)),
        grid=(t // tm,),
        in_specs=x_specs + [row(S5_WIDTH), row(S5_WIDTH), row(ATTN_Q), row(SSD_WIDTH)]
                 + [_const_spec(c.shape) for c in consts],
        out_specs=[row(D_MODEL), pl.BlockSpec((tm * ROW_TILE, LANES), lambda i: (i, 0)), row(ROUTE_PAD)],
        out_shape=[jax.ShapeDtypeStruct((t, D_MODEL), F32), jax.ShapeDtypeStruct((t * ROW_TILE, LANES), F32),
                   jax.ShapeDtypeStruct((t, ROUTE_PAD), F32)],
        compiler_params=_cparams(("parallel",)),
        name="merge_router",
    )(*x_arrs, ys5, u, yb, yc, *consts)


def _moe_plan(route, t):
    n_exp = MOE_EXPERTS
    e = route[:, 0:2].astype(jnp.int32).reshape(-1)
    onehot = (e[:, None] == jnp.arange(n_exp, dtype=jnp.int32)[None, :]).astype(jnp.int32)
    csum = jnp.cumsum(onehot, axis=0)
    rank = jnp.sum(csum * onehot, axis=1) - 1
    counts = csum[-1]
    padded = ((counts + TM_G - 1) // TM_G) * TM_G
    pend = jnp.cumsum(padded)
    dest = (pend - padded)[e] + rank
    n_tiles = -(-(2 * t + n_exp * (TM_G - 1)) // TM_G)
    n_rows = n_tiles * TM_G
    pair = jnp.full((n_rows,), -1, jnp.int32).at[dest].set(jnp.arange(2 * t, dtype=jnp.int32),
                                                            unique_indices=True)
    valid = pair >= 0
    tok, k = pair // 2, pair % 2
    src = jnp.where(valid, tok, 0)
    spare = t + jnp.arange(n_rows, dtype=jnp.int32) % _MOE_SPARE
    dst = jnp.where(valid, k * (t + _MOE_SPARE) + tok, spare)
    tile_start = jnp.arange(n_tiles, dtype=jnp.int32) * TM_G
    tile_e = jnp.sum((pend[None, :] <= tile_start[:, None]).astype(jnp.int32), axis=1)
    n_used = pend[-1:] // TM_G
    tile_e = jnp.minimum(tile_e, n_exp - 1)
    tile_e = jnp.minimum(tile_e, tile_e[n_used[0] - 1])
    return src.reshape(n_tiles, 1, TM_G), dst.reshape(n_tiles, 1, TM_G), tile_e, n_used.astype(jnp.int32)


_MOE_SPARE = 2 * TM_G


def _gmm_kernel(te_ref, nu_ref, src_ref, srcn_ref, dst_ref, h2_hbm, wg_ref, wu_ref, wd_ref, out_hbm,
                xbuf, ybuf, gsem, ssem):
    del te_ref
    i = pl.program_id(0)
    last = nu_ref[0] - 1
    slot = i % 2

    tile_rows = TM_G * ROW_TILE

    def row_tile(ref, s, r):
        return ref.at[pl.ds((s * TM_G + r) * ROW_TILE, ROW_TILE), :]

    def slot_rows(ref, s):
        return ref.at[pl.ds(s * tile_rows, tile_rows), :]

    def gather_start(idx_ref, s):
        for r in range(TM_G):
            src = h2_hbm.at[pl.ds(pl.multiple_of(idx_ref[0, 0, r], ROW_TILE), ROW_TILE), :]
            pltpu.make_async_copy(src, row_tile(xbuf, s, r), gsem.at[s]).start(priority=r % 2)

    def gather_wait(s):
        pltpu.make_async_copy(h2_hbm.at[pl.ds(0, tile_rows), :], slot_rows(xbuf, s), gsem.at[s]).wait()

    def scatter_wait(s):
        pltpu.make_async_copy(slot_rows(ybuf, s), out_hbm.at[pl.ds(0, tile_rows), :], ssem.at[s]).wait()

    @pl.when(i == 0)
    def _():
        gather_start(src_ref, 0)
        ybuf[...] = jnp.zeros_like(ybuf)
        plane = out_hbm.shape[0] // 2
        spare0 = plane - _MOE_SPARE * ROW_TILE
        fills = [pltpu.make_async_copy(slot_rows(ybuf, s),
                                       out_hbm.at[pl.ds(k * plane + spare0 + s * tile_rows, tile_rows), :],
                                       ssem.at[s]) for k in range(2) for s in range(2)]
        for f in fills:
            f.start()
        for f in fills:
            f.wait()

    @pl.when(i <= last)
    def _():
        gather_wait(slot)

        @pl.when(i >= 2)
        def _():
            scatter_wait(slot)

        gather_start(srcn_ref, 1 - slot)

        xb = _load_row_tiles(xbuf, slot * TM_G, TM_G).astype(BF16)
        hid = _silu(_dot(xb, wg_ref[0, 0].astype(BF16))) * _dot(xb, wu_ref[0, 0].astype(BF16))
        _store_row_tiles(ybuf, slot * TM_G, _dot(hid.astype(BF16), wd_ref[0, 0].astype(BF16)))
        for r in range(TM_G):
            dst = out_hbm.at[pl.ds(pl.multiple_of(dst_ref[0, 0, r], ROW_TILE), ROW_TILE), :]
            pltpu.make_async_copy(row_tile(ybuf, slot, r), dst, ssem.at[slot]).start(priority=r % 2)

        @pl.when(i == last)
        def _():
            scatter_wait(slot)
            gather_wait(1 - slot)

        @pl.when((i == last) & (i >= 1))
        def _():
            scatter_wait(1 - slot)


def _moe_sparse(h2, route, w_gate, w_up, w_down, l):
    t = h2.shape[0] // ROW_TILE
    src, dst, tile_e, n_used = _moe_plan(route, t)
    src, dst = src * ROW_TILE, dst * ROW_TILE
    n_tiles = src.shape[0]
    smem = lambda imap: pl.BlockSpec((1, 1, TM_G), imap, memory_space=pltpu.SMEM)
    grid_spec = pltpu.PrefetchScalarGridSpec(
        num_scalar_prefetch=2,
        grid=(n_tiles,),
        in_specs=[smem(lambda i, te, nu: (i, 0, 0)),
                  smem(lambda i, te, nu: (jnp.minimum(i + 1, n_tiles - 1), 0, 0)),
                  smem(lambda i, te, nu: (i, 0, 0)),
                  pl.BlockSpec(memory_space=pl.ANY),
                  pl.BlockSpec((1, 1, D_MODEL, MOE_FF), lambda i, te, nu: (l, te[i], 0, 0)),
                  pl.BlockSpec((1, 1, D_MODEL, MOE_FF), lambda i, te, nu: (l, te[i], 0, 0)),
                  pl.BlockSpec((1, 1, MOE_FF, D_MODEL), lambda i, te, nu: (l, te[i], 0, 0))],
        out_specs=pl.BlockSpec(memory_space=pl.ANY),
        scratch_shapes=[pltpu.VMEM((2 * TM_G * ROW_TILE, LANES), F32), pltpu.VMEM((2 * TM_G * ROW_TILE, LANES), F32),
                        pltpu.SemaphoreType.DMA((2,)), pltpu.SemaphoreType.DMA((2,))])
    plane = (t + _MOE_SPARE) * ROW_TILE
    out = pl.pallas_call(
        _gmm_kernel,
        grid_spec=grid_spec,
        out_shape=jax.ShapeDtypeStruct((2 * plane, LANES), F32),
        compiler_params=_cparams(("arbitrary",)),
        name="moe_gmm",
    )(tile_e, n_used, src, src, dst, h2, w_gate, w_up, w_down)
    return out.reshape(2, plane, LANES)


def _sum_kernel(*refs):
    refs[-1][...] = _x_sum(refs[:-1])


def _residual_sum(xs, t):
    tm = min(TM_PROJ, t)
    x_arrs, x_specs = _x_parts(xs, tm)
    return pl.pallas_call(
        _sum_kernel,
        grid=(t // tm,),
        in_specs=x_specs,
        out_specs=pl.BlockSpec((tm, D_MODEL), lambda i: (i, 0)),
        out_shape=jax.ShapeDtypeStruct((t, D_MODEL), F32),
        compiler_params=_cparams(("parallel",)),
        name="residual_sum",
    )(*x_arrs)


def _cast_kernel(*refs):
    n = len(refs) // 2
    for x_ref, o_ref in zip(refs[:n], refs[n:]):
        o_ref[...] = x_ref[...].astype(o_ref.dtype)


def _cast_layer_bf16(ws, l):
    halves = 2
    in_specs = [pl.BlockSpec((None, w.shape[1] // halves, w.shape[2]), lambda i: (l, i, 0)) for w in ws]
    out_specs = [pl.BlockSpec((w.shape[1] // halves, w.shape[2]), lambda i: (i, 0)) for w in ws]
    return pl.pallas_call(
        _cast_kernel,
        grid=(halves,),
        in_specs=in_specs,
        out_specs=out_specs,
        out_shape=[jax.ShapeDtypeStruct(w.shape[1:], BF16) for w in ws],
        compiler_params=_cparams(("parallel",)),
        name="cast_weights",
    )(*[w.astype(F32) for w in ws])


_GATE_SHIFT = N_MIX % LANES
_GATE_BLOCK = 512
assert (N_MIX - _GATE_SHIFT) % _GATE_BLOCK == 0 and (N_BRANCH * D_MODEL) % _GATE_BLOCK == 0


def _gate_w_kernel(a_ref, b_ref, o_ref):
    full = jnp.concatenate([a_ref[...], b_ref[...]], axis=1)
    shifted = pltpu.roll(full, full.shape[1] - _GATE_SHIFT, 1)
    o_ref[...] = shifted[:, :_GATE_BLOCK].astype(o_ref.dtype)


def _gate_weights(w_in, l):
    base = (N_MIX - _GATE_SHIFT) // _GATE_BLOCK
    per = _GATE_BLOCK // LANES
    return pl.pallas_call(
        _gate_w_kernel,
        grid=(N_BRANCH * D_MODEL // _GATE_BLOCK,),
        in_specs=[pl.BlockSpec((None, D_MODEL, _GATE_BLOCK), lambda k: (l, 0, base + k)),
                  pl.BlockSpec((None, D_MODEL, LANES), lambda k: (l, 0, (base + k + 1) * per))],
        out_specs=pl.BlockSpec((D_MODEL, _GATE_BLOCK), lambda k: (0, k)),
        out_shape=jax.ShapeDtypeStruct((D_MODEL, N_BRANCH * D_MODEL), BF16),
        compiler_params=_cparams(("parallel",)),
        name="gate_weights",
    )(w_in, w_in)


def _layer_weights(l, p):
    glu_w1, glu_w2, p_s5, p_attn, p_ssd, w_out = _cast_layer_bf16(
        [p['s5_glu_w1'], p['s5_glu_w2'], p['p_s5'], p['p_attn'], p['p_ssd'], p['w_out']], l)
    w_router = jnp.concatenate([p['w_router_group'][l], p['w_router_expert'][l]], axis=1).astype(F32)
    npad = ROUTE_PAD - w_router.shape[1]
    b_router = jnp.concatenate([p['b_router_group'][l], p['b_router_expert'][l]]).astype(F32)
    w_router = jnp.pad(w_router, ((0, 0), (0, npad)))
    w_router_hi = w_router.astype(BF16)
    return dict(
        n1g=p['norm1_g'][l].astype(F32)[None],
        w_gate=_gate_weights(p['w_in'].astype(F32), l), b_gate=p['b_gate'][l].astype(F32),
        s5_d=p['s5_d'][l].astype(F32)[None],
        glu_w1=glu_w1, glu_w2=glu_w2, p_s5=p_s5, p_attn=p_attn, p_ssd=p_ssd, w_out=w_out,
        n2g=p['norm2_g'][l].astype(F32)[None],
        w_router_hi=w_router_hi, w_router_mid=(w_router - w_router_hi.astype(F32)).astype(BF16),
        b_router=jnp.pad(b_router, (0, npad))[None])


def _layer(xs, t, l, p, bsz):
    lw = _layer_weights(l, p)
    u, q, k, v, z, xbc, dt_raw = _inproj(xs, t, lw['n1g'], p['w_in'].astype(F32), l)
    tabs = _s5_tables(p['s5_lambda_re'][l], p['s5_lambda_im'][l], p['s5_b_re'][l], p['s5_b_im'][l],
                      p['s5_c_re'][l], p['s5_c_im'][l], p['s5_log_dt'][l])
    ys5 = _s5_mixer(u, tabs, bsz)
    yb = _swa_attention(q, k, v, p['q_norm_g'][l], p['k_norm_g'][l], p['attn_sinks'][l], bsz)
    yc = _ssd_mixer(z, xbc, dt_raw, p['ssd_conv_w'][l], p['ssd_conv_b'][l], p['ssd_dt_bias'][l],
                    p['ssd_a_log'][l], p['ssd_d'][l], p['ssd_norm_g'][l], bsz)
    x1, h2, route = _merge(xs, t, ys5, u, yb, yc, lw)
    moe = _moe_sparse(h2, route, p['w_exp_gate'], p['w_exp_up'], p['w_exp_down'], l)
    return [x1, route, moe]


def kernel(x, norm1_g, w_in, b_gate, s5_lambda_re, s5_lambda_im, s5_b_re, s5_b_im, s5_c_re, s5_c_im, s5_d, s5_log_dt, s5_glu_w1, s5_glu_w2, q_norm_g, k_norm_g, attn_sinks, ssd_conv_w, ssd_conv_b, ssd_dt_bias, ssd_a_log, ssd_d, ssd_norm_g, p_s5, p_attn, p_ssd, w_out, norm2_g, w_router_group, b_router_group, w_router_expert, b_router_expert, w_exp_gate, w_exp_up, w_exp_down):
    p = dict(norm1_g=norm1_g, w_in=w_in, b_gate=b_gate, s5_lambda_re=s5_lambda_re, s5_lambda_im=s5_lambda_im,
             s5_b_re=s5_b_re, s5_b_im=s5_b_im, s5_c_re=s5_c_re, s5_c_im=s5_c_im, s5_d=s5_d,
             s5_log_dt=s5_log_dt, s5_glu_w1=s5_glu_w1, s5_glu_w2=s5_glu_w2, q_norm_g=q_norm_g,
             k_norm_g=k_norm_g, attn_sinks=attn_sinks, ssd_conv_w=ssd_conv_w, ssd_conv_b=ssd_conv_b,
             ssd_dt_bias=ssd_dt_bias, ssd_a_log=ssd_a_log, ssd_d=ssd_d, ssd_norm_g=ssd_norm_g, p_s5=p_s5,
             p_attn=p_attn, p_ssd=p_ssd, w_out=w_out, norm2_g=norm2_g, w_router_group=w_router_group,
             b_router_group=b_router_group, w_router_expert=w_router_expert, b_router_expert=b_router_expert,
             w_exp_gate=w_exp_gate, w_exp_up=w_exp_up, w_exp_down=w_exp_down)
    bsz, seq, dm = x.shape
    depth = w_in.shape[0]
    t = bsz * seq
    xs = [x.reshape(t, dm)]
    for l in range(depth):
        xs = _layer(xs, t, l, p, bsz)
    return _residual_sum(xs, t).reshape(bsz, seq, dm)
```

```python
import functools
import math

import jax
import jax.numpy as jnp
from jax import lax
from jax.experimental import pallas as pl
from jax.experimental.pallas import tpu as pltpu

F32 = jnp.float32
BF16 = jnp.bfloat16

D_MODEL = 1024
NORM_EPS = 1e-6
S5_WIDTH = 512
S5_GROUP = 16
S5_GROUPS = 32
S5_STATE = 64
HEAD_DIM = 64
ATTN_HEADS = 8
ATTN_KV_HEADS = 2
ATTN_REP = ATTN_HEADS // ATTN_KV_HEADS
ATTN_Q = ATTN_HEADS * HEAD_DIM
ATTN_KV = ATTN_KV_HEADS * HEAD_DIM
ATTN_BLOCK = 128
ROPE_THETA = 10000.0
SSD_WIDTH = 1024
SSD_HEAD_DIM = 64
SSD_HEADS = 16
SSD_GROUPS = 2
SSD_STATE = 64
SSD_CONV = 4
SSD_CHUNK = 128
SSD_BC = SSD_GROUPS * SSD_STATE
SSD_CONV_CH = SSD_WIDTH + 2 * SSD_BC
SSD_GROUP_W = SSD_WIDTH // SSD_GROUPS
N_BRANCH = 3
MOE_GROUPS = 4
MOE_PER_GROUP = 8
MOE_EXPERTS = 32
MOE_FF = 512
N_MIX = S5_WIDTH + ATTN_Q + 2 * ATTN_KV + SSD_WIDTH + SSD_CONV_CH + SSD_HEADS

LANES = 128
SUBLANES = 8
VMEM_LIMIT_BYTES = 56 * 1024 * 1024

S5_CHUNK = 16
S5_TG = 8
ATTN_TILE = 512
TM_PROJ = 512
TM_G = 256
DT_PAD = LANES
N_MIX_PAD = N_MIX - SSD_HEADS + DT_PAD
ROUTE_PAD = LANES


def _cparams(semantics):
    return pltpu.CompilerParams(dimension_semantics=semantics, vmem_limit_bytes=VMEM_LIMIT_BYTES)


def _const_spec(shape):
    zeros = (0,) * len(shape)
    return pl.BlockSpec(shape, lambda *_: zeros, pipeline_mode=pl.Buffered(1))


def _dot(a, b):
    return jnp.dot(a, b, preferred_element_type=F32)


def _split3(a):
    hi = a.astype(BF16)
    r1 = a - hi.astype(F32)
    mid = r1.astype(BF16)
    lo = (r1 - mid.astype(F32)).astype(BF16)
    return hi, mid, lo


def _dot_f32_lhs(a, b_bf16):
    hi, mid, lo = _split3(a)
    return _dot(hi, b_bf16) + _dot(mid, b_bf16) + _dot(lo, b_bf16)


def _rms_scale(x):
    return x * lax.rsqrt(jnp.mean(x * x, axis=-1, keepdims=True) + NORM_EPS)


def _silu(x):
    return x * jax.nn.sigmoid(x)


_OFF_U = 0
_OFF_Q = _OFF_U + S5_WIDTH
_OFF_K = _OFF_Q + ATTN_Q
_OFF_V = _OFF_K + ATTN_KV
_OFF_Z = _OFF_V + ATTN_KV
_OFF_XBC = _OFF_Z + SSD_WIDTH
_OFF_DT = _OFF_XBC + SSD_CONV_CH


ROW_TILE = D_MODEL // LANES


def _store_row_tiles(ref, start, val):
    n = val.shape[0]
    for j in range(ROW_TILE):
        ref[pl.ds(start * ROW_TILE + j, n, stride=ROW_TILE), :] = val[:, j * LANES:(j + 1) * LANES]


def _load_row_tiles(ref, start, n):
    return jnp.concatenate([ref[pl.ds(start * ROW_TILE + j, n, stride=ROW_TILE), :] for j in range(ROW_TILE)],
                           axis=1)


def _x_parts(xs, tm):
    row = lambda w: pl.BlockSpec((tm, w), lambda i: (i, 0))
    if len(xs) == 1:
        return list(xs), [row(D_MODEL)]
    x1, route, moe = xs
    plane = lambda k: pl.BlockSpec((None, tm * ROW_TILE, LANES), lambda i: (k, i, 0))
    return [x1, route, moe, moe], [row(D_MODEL), row(ROUTE_PAD), plane(0), plane(1)]


def _x_sum(x_refs):
    if len(x_refs) == 1:
        return x_refs[0][...]
    x1_ref, route_ref, m0_ref, m1_ref = x_refs
    r = route_ref[...]
    n = x1_ref.shape[0]
    return x1_ref[...] + r[:, 2:3] * _load_row_tiles(m0_ref, 0, n) + r[:, 3:4] * _load_row_tiles(m1_ref, 0, n)


def _inproj_kernel(*refs, n_x):
    x_refs, (g_ref, w_ref), outs, wb = refs[:n_x], refs[n_x:n_x + 2], refs[n_x + 2:-1], refs[-1]

    @pl.when(pl.program_id(0) == 0)
    def _():
        wb[...] = w_ref[...].astype(BF16)

    hb = (_rms_scale(_x_sum(x_refs)) * g_ref[...]).astype(BF16)
    offs = (_OFF_U, _OFF_Q, _OFF_K, _OFF_V, _OFF_Z, _OFF_XBC, _OFF_DT)
    for o_ref, off in zip(outs, offs):
        width = o_ref.shape[1]
        o_ref[...] = _dot(hb, wb[:, off:off + width]).astype(o_ref.dtype)


def _inproj(xs, t, g, w_in, l):
    tm = min(TM_PROJ, t)
    widths = (S5_WIDTH, ATTN_Q, ATTN_KV, ATTN_KV, SSD_WIDTH, SSD_CONV_CH, DT_PAD)
    dtypes = (F32, BF16, BF16, BF16, BF16, BF16, F32)
    x_arrs, x_specs = _x_parts(xs, tm)
    w_spec = pl.BlockSpec((None, D_MODEL, N_MIX_PAD), lambda i: (l, 0, 0), pipeline_mode=pl.Buffered(1))
    return pl.pallas_call(
        functools.partial(_inproj_kernel, n_x=len(x_arrs)),
        grid=(t // tm,),
        in_specs=x_specs + [_const_spec((1, D_MODEL)), w_spec],
        out_specs=[pl.BlockSpec((tm, w), lambda i: (i, 0)) for w in widths],
        out_shape=[jax.ShapeDtypeStruct((t, w), d) for w, d in zip(widths, dtypes)],
        scratch_shapes=[pltpu.VMEM((D_MODEL, N_MIX_PAD), BF16)],
        compiler_params=_cparams(("arbitrary",)),
        name="inproj",
    )(*x_arrs, g, w_in)


def _s5_tables(lam_re, lam_im, b_re, b_im, c_re, c_im, log_dt):
    hp = lax.Precision.HIGHEST
    g_n, p_n = lam_re.shape
    c_n = b_re.shape[-1]
    t_n = S5_CHUNK
    lr, li = lam_re.astype(F32), lam_im.astype(F32)
    dt = jnp.exp(log_dt.astype(F32))[:, None]
    mag = jnp.exp(lr * dt)
    ab_re = mag * jnp.cos(li * dt)
    ab_im = mag * jnp.sin(li * dt)
    nr = ab_re - 1.0
    den = lr * lr + li * li
    f_re = (nr * lr + ab_im * li) / den
    f_im = (ab_im * lr - nr * li) / den
    br, bi = b_re.astype(F32), b_im.astype(F32)
    bb_re = f_re[..., None] * br - f_im[..., None] * bi
    bb_im = f_re[..., None] * bi + f_im[..., None] * br
    j = jnp.arange(t_n + 1, dtype=F32)[:, None, None]
    pmag = jnp.exp(lr * dt * j)
    ang = li * dt * j
    p_re = pmag * jnp.cos(ang)
    p_im = pmag * jnp.sin(ang)
    cr, ci = c_re.astype(F32), c_im.astype(F32)
    ca_re = cr[None] * p_re[:, :, None, :] - ci[None] * p_im[:, :, None, :]
    ca_im = cr[None] * p_im[:, :, None, :] + ci[None] * p_re[:, :, None, :]
    kern = (jnp.einsum('jgcp,gpd->jgcd', ca_re[:t_n], bb_re, precision=hp)
            - jnp.einsum('jgcp,gpd->jgcd', ca_im[:t_n], bb_im, precision=hp))
    q_re = p_re[t_n - 1 - jnp.arange(t_n)]
    q_im = p_im[t_n - 1 - jnp.arange(t_n)]
    bs_re = q_re[..., None] * bb_re[None] - q_im[..., None] * bb_im[None]
    bs_im = q_re[..., None] * bb_im[None] + q_im[..., None] * bb_re[None]

    nt = g_n // S5_TG
    k_strip = kern.reshape(t_n, nt, S5_TG, c_n, c_n).transpose(1, 0, 4, 2, 3)
    k_strip = k_strip.reshape(nt, t_n, c_n, S5_TG * c_n)

    def in_strip(a):
        a = a.reshape(t_n, nt, S5_TG, p_n, c_n).transpose(1, 0, 4, 2, 3)
        return a.reshape(nt, t_n, c_n, S5_TG * p_n)

    def out_strip(a):
        a = a.reshape(t_n, nt, S5_TG, c_n, p_n).transpose(1, 4, 0, 2, 3)
        return a.reshape(nt, p_n, t_n * S5_TG * c_n)

    return dict(
        k=k_strip, bs_re=in_strip(bs_re), bs_im=in_strip(bs_im),
        co_re=out_strip(ca_re[1:]), co_im=out_strip(-ca_im[1:]),
        at_re=p_re[t_n].reshape(nt, 1, S5_TG * p_n), at_im=p_im[t_n].reshape(nt, 1, S5_TG * p_n))


_S5_XW = S5_CHUNK * LANES
_S5_SW = S5_TG * S5_STATE


def _group_block(strip, lanes_per_group):
    rpg = strip.shape[0]
    full = jnp.concatenate([strip] * S5_TG, axis=0)
    row_g = lax.broadcasted_iota(jnp.int32, full.shape, 0) // rpg
    lane_g = (lax.broadcasted_iota(jnp.int32, full.shape, 1) % (S5_TG * lanes_per_group)) // lanes_per_group
    return jnp.where(row_g == lane_g, full, 0.0).astype(BF16)


def _s5_build_tables(k_ref, bre_ref, bim_ref, cre_ref, cim_ref, big, bsre, bsim, core, coim):
    blocks = [_group_block(k_ref[0, j], S5_GROUP) for j in range(S5_CHUNK)]
    zero = jnp.zeros((LANES, LANES), BF16)
    for s in range(S5_CHUNK):
        for t in range(S5_CHUNK):
            big[s * LANES:(s + 1) * LANES, t * LANES:(t + 1) * LANES] = blocks[t - s] if t >= s else zero
    for src, dst in ((bre_ref, bsre), (bim_ref, bsim)):
        for s in range(S5_CHUNK):
            dst[s * LANES:(s + 1) * LANES, :] = _group_block(src[0, s], S5_STATE)
    for src, dst in ((cre_ref, core), (cim_ref, coim)):
        for t in range(S5_CHUNK):
            cols = slice(t * LANES, (t + 1) * LANES)
            dst[:, cols] = _group_block(src[0, :, cols], S5_GROUP)


def _s5_kernel(u_ref, k_ref, bre_ref, bim_ref, cre_ref, cim_ref, are_ref, aim_ref, y_ref,
               big, bsre, bsim, core, coim, sre, sim, hre, him):
    @pl.when(pl.program_id(1) == 0)
    def _():
        _s5_build_tables(k_ref, bre_ref, bim_ref, cre_ref, cim_ref, big, bsre, bsim, core, coim)

    nck = u_ref.shape[0] // S5_CHUNK
    xcat = jnp.concatenate([u_ref[pl.ds(s, nck, stride=S5_CHUNK), :].astype(BF16) for s in range(S5_CHUNK)],
                           axis=1)
    sre[...] = _dot(xcat, bsre[...])
    sim[...] = _dot(xcat, bsim[...])
    a_re = are_ref[0]
    a_im = aim_ref[0]

    def step(i, carry):
        h_re, h_im = carry
        base = pl.multiple_of(i * SUBLANES, SUBLANES)
        s_re = sre[pl.ds(base, SUBLANES), :]
        s_im = sim[pl.ds(base, SUBLANES), :]
        ent_re, ent_im = [], []
        for j in range(SUBLANES):
            ent_re.append(h_re)
            ent_im.append(h_im)
            h_re, h_im = (a_re * h_re - a_im * h_im + s_re[j:j + 1], a_re * h_im + a_im * h_re + s_im[j:j + 1])
        hre[pl.ds(base, SUBLANES), :] = jnp.concatenate(ent_re, axis=0)
        him[pl.ds(base, SUBLANES), :] = jnp.concatenate(ent_im, axis=0)
        return h_re, h_im

    zero = jnp.zeros((1, _S5_SW), F32)
    lax.fori_loop(0, nck // SUBLANES, step, (zero, zero))
    y = (_dot(xcat, big[...]) + _dot(hre[...].astype(BF16), core[...])
         + _dot(him[...].astype(BF16), coim[...]))
    for t in range(S5_CHUNK):
        y_ref[pl.ds(t, nck, stride=S5_CHUNK), :] = y[:, t * LANES:(t + 1) * LANES]


def _s5_mixer(u, tabs, l, bsz):
    t = u.shape[0]
    seq = t // bsz
    nck = seq // S5_CHUNK
    nt = S5_GROUPS // S5_TG
    strip = lambda a: pl.BlockSpec((None, 1) + a.shape[2:], lambda x, b: (l, x) + (0,) * (a.ndim - 2))
    strips = [tabs['k'], tabs['bs_re'], tabs['bs_im'], tabs['co_re'], tabs['co_im'], tabs['at_re'], tabs['at_im']]
    return pl.pallas_call(
        _s5_kernel,
        grid=(nt, bsz),
        in_specs=[pl.BlockSpec((seq, LANES), lambda x, b: (b, x))] + [strip(a) for a in strips],
        out_specs=pl.BlockSpec((seq, LANES), lambda x, b: (b, x)),
        out_shape=jax.ShapeDtypeStruct((t, S5_WIDTH), F32),
        scratch_shapes=[pltpu.VMEM((_S5_XW, _S5_XW), BF16),
                        pltpu.VMEM((_S5_XW, _S5_SW), BF16), pltpu.VMEM((_S5_XW, _S5_SW), BF16),
                        pltpu.VMEM((_S5_SW, _S5_XW), BF16), pltpu.VMEM((_S5_SW, _S5_XW), BF16)]
                       + [pltpu.VMEM((nck, _S5_SW), F32)] * 4,
        compiler_params=_cparams(("arbitrary", "arbitrary")),
        name="s5_mixer",
    )(u, *strips)


def _swap_rope_halves(y):
    w = y.shape[1]
    lane = lax.broadcasted_iota(jnp.int32, y.shape, 1)
    lower = (lane % HEAD_DIM) < (HEAD_DIM // 2)
    return jnp.where(lower, pltpu.roll(y, w - HEAD_DIM // 2, 1), pltpu.roll(y, HEAD_DIM // 2, 1))


def _norm_rope(x, gain, head_mean, cos2, sin2):
    reps = x.shape[1] // LANES
    ms = _dot_f32_lhs(x * x, head_mean)
    y = x * lax.rsqrt(ms + NORM_EPS) * gain
    if reps > 1:
        cos2 = jnp.concatenate([cos2] * reps, axis=1)
        sin2 = jnp.concatenate([sin2] * reps, axis=1)
    return y * cos2 + _swap_rope_halves(y) * sin2


def _attn_kernel(q_ref, kc_ref, kp_ref, vc_ref, vp_ref, cosc_ref, sinc_ref, cosp_ref, sinp_ref,
                 qg_ref, kg_ref, sink_ref, hmq_ref, hmk_ref, o_ref):
    seq_start = pl.program_id(1) == 0
    blk = ATTN_BLOCK
    nblk = q_ref.shape[0] // blk
    q = _norm_rope(q_ref[...].astype(F32), qg_ref[...], hmq_ref[...], cosc_ref[...], sinc_ref[...])
    q = (q * (HEAD_DIM ** -0.5)).astype(BF16)
    kc = _norm_rope(kc_ref[...].astype(F32), kg_ref[...], hmk_ref[...], cosc_ref[...], sinc_ref[...]).astype(BF16)
    kp = _norm_rope(kp_ref[...].astype(F32), kg_ref[...], hmk_ref[...], cosp_ref[...], sinp_ref[...]).astype(BF16)
    k_all = jnp.concatenate([kp, kc], axis=0)
    v_all = jnp.concatenate([vp_ref[...], vc_ref[...]], axis=0)
    shape = (2 * blk, ATTN_REP * blk)
    kj = lax.broadcasted_iota(jnp.int32, shape, 0)
    qi = lax.broadcasted_iota(jnp.int32, shape, 1) % blk + blk
    band = (kj <= qi) & (qi - kj < blk)
    band_first = band & ((kj >= blk) | jnp.logical_not(seq_start))
    sinks = sink_ref[...]
    for j in range(ATTN_KV_HEADS):
        sl = slice(j * HEAD_DIM, (j + 1) * HEAD_DIM)
        heads = [j * ATTN_REP + r for r in range(ATTN_REP)]
        sink = jnp.concatenate([jnp.broadcast_to(sinks[:, h:h + 1], (1, blk)) for h in heads], axis=1)
        for n in range(nblk):
            rows = slice(n * blk, (n + 1) * blk)
            kb = k_all[n * blk:(n + 2) * blk, sl]
            vb = v_all[n * blk:(n + 2) * blk, sl]
            q4 = jnp.concatenate([q[rows, h * HEAD_DIM:(h + 1) * HEAD_DIM] for h in heads], axis=0)
            s = lax.dot_general(kb, q4, (((1,), (1,)), ((), ())), preferred_element_type=F32)
            s = jnp.where(band_first if n == 0 else band, s, -jnp.inf)
            m = jnp.maximum(jnp.max(s, axis=0, keepdims=True), sink)
            p = jnp.exp(s - m)
            denom = jnp.sum(p, axis=0, keepdims=True) + jnp.exp(sink - m)
            pn = (p * (1.0 / denom)).astype(BF16)
            out = lax.dot_general(pn, vb, (((0,), (0,)), ((), ())), preferred_element_type=F32)
            for r, h in enumerate(heads):
                o_ref[rows, h * HEAD_DIM:(h + 1) * HEAD_DIM] = out[r * blk:(r + 1) * blk].astype(o_ref.dtype)


def _rope_tables(seq):
    half = HEAD_DIM // 2
    inv = jnp.power(ROPE_THETA, -jnp.arange(half, dtype=F32) * 2.0 / HEAD_DIM)
    ang = jnp.arange(seq, dtype=F32)[:, None] * inv[None, :]
    cos, sin = jnp.cos(ang), jnp.sin(ang)
    cos2 = jnp.concatenate([cos, cos, cos, cos], axis=1)
    sin2 = jnp.concatenate([-sin, sin, -sin, sin], axis=1)
    return cos2, sin2


def _head_mean_matrix(width):
    i = jnp.arange(width)
    return jnp.where((i[:, None] // HEAD_DIM) == (i[None, :] // HEAD_DIM), 1.0 / HEAD_DIM, 0.0).astype(BF16)


def _swa_attention(q, k, v, q_g, k_g, sinks, bsz):
    t = q.shape[0]
    seq = t // bsz
    tile = min(ATTN_TILE, seq)
    nt = seq // tile
    bpt = tile // ATTN_BLOCK
    nb = seq // ATTN_BLOCK
    cos2, sin2 = _rope_tables(seq)
    qg = jnp.tile(q_g.astype(F32), ATTN_HEADS)[None]
    kg = jnp.tile(k_g.astype(F32), ATTN_KV_HEADS)[None]
    cur = lambda b, n: (b * nt + n, 0)
    prev = lambda b, n: (b * nb + jnp.maximum(n * bpt - 1, 0), 0)
    tcur = lambda b, n: (n, 0)
    tprev = lambda b, n: (jnp.maximum(n * bpt - 1, 0), 0)
    blk = ATTN_BLOCK
    return pl.pallas_call(
        _attn_kernel,
        grid=(bsz, nt),
        in_specs=[pl.BlockSpec((tile, ATTN_Q), cur),
                  pl.BlockSpec((tile, ATTN_KV), cur), pl.BlockSpec((blk, ATTN_KV), prev),
                  pl.BlockSpec((tile, ATTN_KV), cur), pl.BlockSpec((blk, ATTN_KV), prev),
                  pl.BlockSpec((tile, LANES), tcur), pl.BlockSpec((tile, LANES), tcur),
                  pl.BlockSpec((blk, LANES), tprev), pl.BlockSpec((blk, LANES), tprev),
                  _const_spec((1, ATTN_Q)), _const_spec((1, ATTN_KV)), _const_spec((1, ATTN_HEADS)),
                  _const_spec((ATTN_Q, ATTN_Q)), _const_spec((ATTN_KV, ATTN_KV))],
        out_specs=pl.BlockSpec((tile, ATTN_Q), cur),
        out_shape=jax.ShapeDtypeStruct((t, ATTN_Q), BF16),
        compiler_params=_cparams(("parallel", "parallel")),
        name="swa_attention",
    )(q, k, k, v, v, cos2, sin2, cos2, sin2, qg, kg, sinks.astype(F32)[None],
      _head_mean_matrix(ATTN_Q), _head_mean_matrix(ATTN_KV))


_CONV_TAIL = 2 * SUBLANES


def _ssd_kernel(z_ref, xbc_ref, dt_ref, cw_ref, cb_ref, dtb_ref, alog_ref, dskip_ref, ng_ref, exp_ref, shift_ref,
                o_ref, tail, state, ybuf):
    c = pl.program_id(1)
    ch = SSD_CHUNK

    @pl.when(c == 0)
    def _():
        tail[...] = jnp.zeros_like(tail)
        state[...] = jnp.zeros_like(state)

    cur = xbc_ref[...]
    shifted = _dot(shift_ref[...], jnp.concatenate([tail[...], cur], axis=0))
    tail[...] = cur[ch - _CONV_TAIL:, :]
    conv = cb_ref[...] + cw_ref[SSD_CONV - 1:SSD_CONV, :] * cur.astype(F32)
    for d in range(1, SSD_CONV):
        conv = conv + cw_ref[SSD_CONV - 1 - d:SSD_CONV - d, :] * shifted[(d - 1) * ch:d * ch]
    act = _silu(conv)
    xs = act[:, :SSD_WIDTH]
    bm = act[:, SSD_WIDTH:SSD_WIDTH + SSD_BC].astype(BF16)
    cm = act[:, SSD_WIDTH + SSD_BC:].astype(BF16)

    lane = lax.broadcasted_iota(jnp.int32, (ch, LANES), 1)
    xdt = dt_ref[...] + dtb_ref[...]
    dt = jnp.maximum(xdt, 0.0) + jnp.log1p(jnp.exp(-jnp.abs(xdt)))
    dt = jnp.where(lane < SSD_HEADS, dt, 0.0)
    a = dt * (-jnp.exp(alog_ref[...]))
    row = lax.broadcasted_iota(jnp.int32, (ch, ch), 0)
    col = lax.broadcasted_iota(jnp.int32, (ch, ch), 1)
    causal = row >= col
    hi, mid, lo = _split3(a)
    tril = causal.astype(BF16)
    cs = _dot(tril, hi) + _dot(tril, mid) + _dot(tril, lo)
    cs_t = cs.T
    expand = exp_ref[...]
    dt_x = _dot_f32_lhs(dt, expand)
    cs_x = _dot_f32_lhs(cs, expand)
    cs_last_x = cs_x[ch - 1:ch, :]
    xdt_full = xs * dt_x
    in_decay = jnp.exp(cs_x)
    out_decay = jnp.exp(cs_last_x - cs_x)
    chunk_decay = jnp.exp(cs_last_x)
    xw = (xdt_full * out_decay).astype(BF16)
    xdt_b = xdt_full.astype(BF16)
    hpg = SSD_HEADS // SSD_GROUPS
    for g in range(SSD_GROUPS):
        gs = slice(g * SSD_STATE, (g + 1) * SSD_STATE)
        ws = slice(g * SSD_GROUP_W, (g + 1) * SSD_GROUP_W)
        bg = bm[:, gs]
        cg = cm[:, gs]
        cb = lax.dot_general(cg, bg, (((1,), (1,)), ((), ())), preferred_element_type=F32)
        prev = state[g]
        ybuf[:, ws] = _dot(cg, prev.astype(BF16)) * in_decay[:, ws]
        for h in range(hpg):
            hh = g * hpg + h
            seg = cs[:, hh:hh + 1] - cs_t[hh:hh + 1, :]
            lmat = jnp.where(causal, jnp.exp(seg), 0.0)
            hs = slice(hh * SSD_HEAD_DIM, (hh + 1) * SSD_HEAD_DIM)
            ybuf[:, hs] += _dot((cb * lmat).astype(BF16), xdt_b[:, hs])
        upd = lax.dot_general(bg, xw[:, ws], (((0,), (0,)), ((), ())), preferred_element_type=F32)
        state[g] = prev * chunk_decay[:, ws] + upd

    y = ybuf[...] + xs * dskip_ref[...]
    y = y * _silu(z_ref[...].astype(F32))
    parts = []
    for g in range(SSD_GROUPS):
        ws = slice(g * SSD_GROUP_W, (g + 1) * SSD_GROUP_W)
        parts.append(_rms_scale(y[:, ws]))
    o_ref[...] = (jnp.concatenate(parts, axis=1) * ng_ref[...]).astype(o_ref.dtype)


def _ssd_mixer(z, xbc, dt_raw, conv_w, conv_b, dt_bias, a_log, d_skip, norm_g, bsz):
    t = z.shape[0]
    seq = t // bsz
    nc = seq // SSD_CHUNK
    pad = lambda v: jnp.pad(v.astype(F32), (0, LANES - SSD_HEADS))[None]
    d_x = jnp.repeat(d_skip.astype(F32), SSD_HEAD_DIM)[None]
    hid = jnp.arange(SSD_WIDTH) // SSD_HEAD_DIM
    expand = (jnp.arange(LANES)[:, None] == hid[None, :]).astype(BF16)
    r = jnp.arange((SSD_CONV - 1) * SSD_CHUNK)
    pick = r % SSD_CHUNK + _CONV_TAIL - (r // SSD_CHUNK + 1)
    shift = (jnp.arange(_CONV_TAIL + SSD_CHUNK)[None, :] == pick[:, None]).astype(BF16)
    blk = lambda b, c: (b * nc + c, 0)
    return pl.pallas_call(
        _ssd_kernel,
        grid=(bsz, nc),
        in_specs=[pl.BlockSpec((SSD_CHUNK, SSD_WIDTH), blk),
                  pl.BlockSpec((SSD_CHUNK, SSD_CONV_CH), blk),
                  pl.BlockSpec((SSD_CHUNK, DT_PAD), blk),
                  _const_spec((SSD_CONV, SSD_CONV_CH)), _const_spec((1, SSD_CONV_CH)),
                  _const_spec((1, LANES)), _const_spec((1, LANES)),
                  _const_spec((1, SSD_WIDTH)), _const_spec((1, SSD_WIDTH)),
                  _const_spec((LANES, SSD_WIDTH)), _const_spec(shift.shape)],
        out_specs=pl.BlockSpec((SSD_CHUNK, SSD_WIDTH), blk),
        out_shape=jax.ShapeDtypeStruct((t, SSD_WIDTH), BF16),
        scratch_shapes=[pltpu.VMEM((_CONV_TAIL, SSD_CONV_CH), BF16),
                        pltpu.VMEM((SSD_GROUPS, SSD_STATE, SSD_GROUP_W), F32),
                        pltpu.VMEM((SSD_CHUNK, SSD_WIDTH), F32)],
        compiler_params=_cparams(("arbitrary", "arbitrary")),
        name="ssd_mixer",
    )(z, xbc, dt_raw, conv_w.astype(F32), conv_b.astype(F32)[None], pad(dt_bias), pad(a_log),
      d_x, norm_g.astype(F32)[None], expand, shift)


def _route(logits):
    lane = lax.broadcasted_iota(jnp.int32, logits.shape, 1).astype(F32)
    big = float(ROUTE_PAD)
    is_g = lane < MOE_GROUPS
    gl = jnp.where(is_g, logits, -jnp.inf)
    gmax = jnp.max(gl, axis=1, keepdims=True)
    gsel = jnp.min(jnp.where(gl == gmax, lane, big), axis=1, keepdims=True)
    pg = 1.0 / jnp.sum(jnp.where(is_g, jnp.exp(logits - gmax), 0.0), axis=1, keepdims=True)
    lo = MOE_GROUPS + MOE_PER_GROUP * gsel
    ev = jnp.where((lane >= lo) & (lane < lo + MOE_PER_GROUP), logits, -jnp.inf)
    m1 = jnp.max(ev, axis=1, keepdims=True)
    i1 = jnp.min(jnp.where(ev == m1, lane, big), axis=1, keepdims=True)
    ev2 = jnp.where(lane == i1, -jnp.inf, ev)
    m2 = jnp.max(ev2, axis=1, keepdims=True)
    i2 = jnp.min(jnp.where(ev2 == m2, lane, big), axis=1, keepdims=True)
    e21 = jnp.exp(m2 - m1)
    w1 = pg / (1.0 + e21)
    w2 = pg * e21 / (1.0 + e21)
    out = jnp.where(lane == 0, i1 - MOE_GROUPS, 0.0)
    out = jnp.where(lane == 1, i2 - MOE_GROUPS, out)
    out = jnp.where(lane == 2, w1, out)
    return jnp.where(lane == 3, w2, out)


def _merge_kernel(*refs, n_x):
    x_refs = refs[:n_x]
    (ys5_ref, u_ref, yb_ref, yc_ref, n1g_ref, wg_ref, bg_ref, d_ref, w1_ref, w2_ref, ps5_ref, pat_ref,
     pssd_ref, wo_ref, n2g_ref, wrh_ref, wrm_ref, br_ref, x1_o, h2_o, route_o) = refs[n_x:]
    x = _x_sum(x_refs)
    hb = (_rms_scale(x) * n1g_ref[...]).astype(BF16)
    ya = ys5_ref[...].astype(F32) + d_ref[...] * u_ref[...].astype(F32)
    yab = jax.nn.gelu(ya).astype(BF16)
    ya = _dot(yab, w1_ref[...]) * jax.nn.sigmoid(_dot(yab, w2_ref[...]))
    branches = ((ya.astype(BF16), ps5_ref), (yb_ref[...], pat_ref), (yc_ref[...], pssd_ref))
    merged = None
    for b, (yv, p_ref) in enumerate(branches):
        gate = jax.nn.sigmoid(_dot(hb, wg_ref[:, b * D_MODEL:(b + 1) * D_MODEL]) + bg_ref[b:b + 1, :])
        term = gate * _dot(yv, p_ref[...])
        merged = term if merged is None else merged + term
    x1 = x + _dot(merged.astype(BF16), wo_ref[...])
    x1_o[...] = x1
    h2 = _rms_scale(x1) * n2g_ref[...]
    _store_row_tiles(h2_o, 0, h2)
    h_hi = h2.astype(BF16)
    h_mid = (h2 - h_hi.astype(F32)).astype(BF16)
    logits = _dot(h_hi, wrh_ref[...]) + _dot(h_hi, wrm_ref[...]) + _dot(h_mid, wrh_ref[...]) + br_ref[...]
    route_o[...] = _route(logits)


def _merge(xs, t, ys5, u, yb, yc, lw):
    tm = min(TM_PROJ, t)
    row = lambda w: pl.BlockSpec((tm, w), lambda i: (i, 0))
    consts = [lw['n1g'], lw['w_gate'], lw['b_gate'], lw['s5_d'], lw['glu_w1'], lw['glu_w2'], lw['p_s5'],
              lw['p_attn'], lw['p_ssd'], lw['w_out'], lw['n2g'], lw['w_router_hi'], lw['w_router_mid'],
              lw['b_router']]
    x_arrs, x_specs = _x_parts(xs, tm)
    return pl.pallas_call(
        functools.partial(_merge_kernel, n_x=len(x_arrs)),
        grid=(t // tm,),
        in_specs=x_specs + [row(S5_WIDTH), row(S5_WIDTH), row(ATTN_Q), row(SSD_WIDTH)]
                 + [_const_spec(c.shape) for c in consts],
        out_specs=[row(D_MODEL), pl.BlockSpec((tm * ROW_TILE, LANES), lambda i: (i, 0)), row(ROUTE_PAD)],
        out_shape=[jax.ShapeDtypeStruct((t, D_MODEL), F32), jax.ShapeDtypeStruct((t * ROW_TILE, LANES), F32),
                   jax.ShapeDtypeStruct((t, ROUTE_PAD), F32)],
        compiler_params=_cparams(("parallel",)),
        name="merge_router",
    )(*x_arrs, ys5, u, yb, yc, *consts)


def _moe_plan(route, t):
    n_exp = MOE_EXPERTS
    e = route[:, 0:2].astype(jnp.int32).reshape(-1)
    onehot = (e[:, None] == jnp.arange(n_exp, dtype=jnp.int32)[None, :]).astype(jnp.int32)
    csum = jnp.cumsum(onehot, axis=0)
    rank = jnp.sum(csum * onehot, axis=1) - 1
    counts = csum[-1]
    padded = ((counts + TM_G - 1) // TM_G) * TM_G
    pend = jnp.cumsum(padded)
    dest = (pend - padded)[e] + rank
    n_tiles = -(-(2 * t + n_exp * (TM_G - 1)) // TM_G)
    n_rows = n_tiles * TM_G
    pair = jnp.full((n_rows,), -1, jnp.int32).at[dest].set(jnp.arange(2 * t, dtype=jnp.int32),
                                                            unique_indices=True)
    valid = pair >= 0
    tok, k = pair // 2, pair % 2
    src = jnp.where(valid, tok, 0)
    spare = t + jnp.arange(n_rows, dtype=jnp.int32) % _MOE_SPARE
    dst = jnp.where(valid, k * (t + _MOE_SPARE) + tok, spare)
    tile_start = jnp.arange(n_tiles, dtype=jnp.int32) * TM_G
    tile_e = jnp.sum((pend[None, :] <= tile_start[:, None]).astype(jnp.int32), axis=1)
    n_used = pend[-1:] // TM_G
    tile_e = jnp.minimum(tile_e, n_exp - 1)
    tile_e = jnp.minimum(tile_e, tile_e[n_used[0] - 1])
    return src.reshape(n_tiles, 1, TM_G), dst.reshape(n_tiles, 1, TM_G), tile_e, n_used.astype(jnp.int32)


_MOE_SPARE = 2 * TM_G


def _gmm_kernel(te_ref, nu_ref, src_ref, srcn_ref, dst_ref, h2_hbm, wg_ref, wu_ref, wd_ref, out_hbm,
                xbuf, ybuf, gsem, ssem):
    del te_ref
    i = pl.program_id(0)
    last = nu_ref[0] - 1
    slot = i % 2

    tile_rows = TM_G * ROW_TILE

    def row_tile(ref, s, r):
        return ref.at[pl.ds((s * TM_G + r) * ROW_TILE, ROW_TILE), :]

    def slot_rows(ref, s):
        return ref.at[pl.ds(s * tile_rows, tile_rows), :]

    def gather_start(idx_ref, s):
        for r in range(TM_G):
            src = h2_hbm.at[pl.ds(pl.multiple_of(idx_ref[0, 0, r], ROW_TILE), ROW_TILE), :]
            pltpu.make_async_copy(src, row_tile(xbuf, s, r), gsem.at[s]).start(priority=r % 2)

    def gather_wait(s):
        pltpu.make_async_copy(h2_hbm.at[pl.ds(0, tile_rows), :], slot_rows(xbuf, s), gsem.at[s]).wait()

    def scatter_wait(s):
        pltpu.make_async_copy(slot_rows(ybuf, s), out_hbm.at[pl.ds(0, tile_rows), :], ssem.at[s]).wait()

    @pl.when(i == 0)
    def _():
        gather_start(src_ref, 0)
        ybuf[...] = jnp.zeros_like(ybuf)
        plane = out_hbm.shape[0] // 2
        spare0 = plane - _MOE_SPARE * ROW_TILE
        fills = [pltpu.make_async_copy(slot_rows(ybuf, s),
                                       out_hbm.at[pl.ds(k * plane + spare0 + s * tile_rows, tile_rows), :],
                                       ssem.at[s]) for k in range(2) for s in range(2)]
        for f in fills:
            f.start()
        for f in fills:
            f.wait()

    @pl.when(i <= last)
    def _():
        gather_wait(slot)

        @pl.when(i >= 2)
        def _():
            scatter_wait(slot)

        gather_start(srcn_ref, 1 - slot)

        xb = _load_row_tiles(xbuf, slot * TM_G, TM_G).astype(BF16)
        hid = _silu(_dot(xb, wg_ref[0, 0].astype(BF16))) * _dot(xb, wu_ref[0, 0].astype(BF16))
        _store_row_tiles(ybuf, slot * TM_G, _dot(hid.astype(BF16), wd_ref[0, 0].astype(BF16)))
        for r in range(TM_G):
            dst = out_hbm.at[pl.ds(pl.multiple_of(dst_ref[0, 0, r], ROW_TILE), ROW_TILE), :]
            pltpu.make_async_copy(row_tile(ybuf, slot, r), dst, ssem.at[slot]).start(priority=r % 2)

        @pl.when(i == last)
        def _():
            scatter_wait(slot)
            gather_wait(1 - slot)

        @pl.when((i == last) & (i >= 1))
        def _():
            scatter_wait(1 - slot)


def _moe_sparse(h2, route, w_gate, w_up, w_down, l):
    t = h2.shape[0] // ROW_TILE
    src, dst, tile_e, n_used = _moe_plan(route, t)
    src, dst = src * ROW_TILE, dst * ROW_TILE
    n_tiles = src.shape[0]
    smem = lambda imap: pl.BlockSpec((1, 1, TM_G), imap, memory_space=pltpu.SMEM)
    grid_spec = pltpu.PrefetchScalarGridSpec(
        num_scalar_prefetch=2,
        grid=(n_tiles,),
        in_specs=[smem(lambda i, te, nu: (i, 0, 0)),
                  smem(lambda i, te, nu: (jnp.minimum(i + 1, n_tiles - 1), 0, 0)),
                  smem(lambda i, te, nu: (i, 0, 0)),
                  pl.BlockSpec(memory_space=pl.ANY),
                  pl.BlockSpec((1, 1, D_MODEL, MOE_FF), lambda i, te, nu: (l, te[i], 0, 0)),
                  pl.BlockSpec((1, 1, D_MODEL, MOE_FF), lambda i, te, nu: (l, te[i], 0, 0)),
                  pl.BlockSpec((1, 1, MOE_FF, D_MODEL), lambda i, te, nu: (l, te[i], 0, 0))],
        out_specs=pl.BlockSpec(memory_space=pl.ANY),
        scratch_shapes=[pltpu.VMEM((2 * TM_G * ROW_TILE, LANES), F32), pltpu.VMEM((2 * TM_G * ROW_TILE, LANES), F32),
                        pltpu.SemaphoreType.DMA((2,)), pltpu.SemaphoreType.DMA((2,))])
    plane = (t + _MOE_SPARE) * ROW_TILE
    out = pl.pallas_call(
        _gmm_kernel,
        grid_spec=grid_spec,
        out_shape=jax.ShapeDtypeStruct((2 * plane, LANES), F32),
        compiler_params=_cparams(("arbitrary",)),
        name="moe_gmm",
    )(tile_e, n_used, src, src, dst, h2, w_gate, w_up, w_down)
    return out.reshape(2, plane, LANES)


def _sum_kernel(*refs):
    refs[-1][...] = _x_sum(refs[:-1])


def _residual_sum(xs, t):
    tm = min(TM_PROJ, t)
    x_arrs, x_specs = _x_parts(xs, tm)
    return pl.pallas_call(
        _sum_kernel,
        grid=(t // tm,),
        in_specs=x_specs,
        out_specs=pl.BlockSpec((tm, D_MODEL), lambda i: (i, 0)),
        out_shape=jax.ShapeDtypeStruct((t, D_MODEL), F32),
        compiler_params=_cparams(("parallel",)),
        name="residual_sum",
    )(*x_arrs)


def _cast_kernel(*refs):
    n = len(refs) // 2
    for x_ref, o_ref in zip(refs[:n], refs[n:]):
        o_ref[...] = x_ref[...].astype(o_ref.dtype)


def _cast_layer_bf16(ws, l):
    halves = 2
    in_specs = [pl.BlockSpec((None, w.shape[1] // halves, w.shape[2]), lambda i: (l, i, 0)) for w in ws]
    out_specs = [pl.BlockSpec((w.shape[1] // halves, w.shape[2]), lambda i: (i, 0)) for w in ws]
    return pl.pallas_call(
        _cast_kernel,
        grid=(halves,),
        in_specs=in_specs,
        out_specs=out_specs,
        out_shape=[jax.ShapeDtypeStruct(w.shape[1:], BF16) for w in ws],
        compiler_params=_cparams(("parallel",)),
        name="cast_weights",
    )(*[w.astype(F32) for w in ws])


_GATE_SHIFT = N_MIX % LANES
_GATE_BLOCK = 512
assert (N_MIX - _GATE_SHIFT) % _GATE_BLOCK == 0 and (N_BRANCH * D_MODEL) % _GATE_BLOCK == 0


def _gate_w_kernel(a_ref, b_ref, o_ref):
    full = jnp.concatenate([a_ref[...], b_ref[...]], axis=1)
    shifted = pltpu.roll(full, full.shape[1] - _GATE_SHIFT, 1)
    o_ref[...] = shifted[:, :_GATE_BLOCK].astype(o_ref.dtype)


def _gate_weights(w_in, l):
    base = (N_MIX - _GATE_SHIFT) // _GATE_BLOCK
    per = _GATE_BLOCK // LANES
    return pl.pallas_call(
        _gate_w_kernel,
        grid=(N_BRANCH * D_MODEL // _GATE_BLOCK,),
        in_specs=[pl.BlockSpec((None, D_MODEL, _GATE_BLOCK), lambda k: (l, 0, base + k)),
                  pl.BlockSpec((None, D_MODEL, LANES), lambda k: (l, 0, (base + k + 1) * per))],
        out_specs=pl.BlockSpec((D_MODEL, _GATE_BLOCK), lambda k: (0, k)),
        out_shape=jax.ShapeDtypeStruct((D_MODEL, N_BRANCH * D_MODEL), BF16),
        compiler_params=_cparams(("parallel",)),
        name="gate_weights",
    )(w_in, w_in)


def _layer_weights(l, p):
    glu_w1, glu_w2, p_s5, p_attn, p_ssd, w_out = _cast_layer_bf16(
        [p['s5_glu_w1'], p['s5_glu_w2'], p['p_s5'], p['p_attn'], p['p_ssd'], p['w_out']], l)
    w_router = jnp.concatenate([p['w_router_group'][l], p['w_router_expert'][l]], axis=1).astype(F32)
    npad = ROUTE_PAD - w_router.shape[1]
    b_router = jnp.concatenate([p['b_router_group'][l], p['b_router_expert'][l]]).astype(F32)
    w_router = jnp.pad(w_router, ((0, 0), (0, npad)))
    w_router_hi = w_router.astype(BF16)
    return dict(
        n1g=p['norm1_g'][l].astype(F32)[None],
        w_gate=_gate_weights(p['w_in'].astype(F32), l), b_gate=p['b_gate'][l].astype(F32),
        s5_d=p['s5_d'][l].astype(F32)[None],
        glu_w1=glu_w1, glu_w2=glu_w2, p_s5=p_s5, p_attn=p_attn, p_ssd=p_ssd, w_out=w_out,
        n2g=p['norm2_g'][l].astype(F32)[None],
        w_router_hi=w_router_hi, w_router_mid=(w_router - w_router_hi.astype(F32)).astype(BF16),
        b_router=jnp.pad(b_router, (0, npad))[None])


def _layer(xs, t, l, p, s5_tabs, bsz):
    lw = _layer_weights(l, p)
    u, q, k, v, z, xbc, dt_raw = _inproj(xs, t, lw['n1g'], p['w_in'].astype(F32), l)
    ys5 = _s5_mixer(u, s5_tabs, l, bsz)
    yb = _swa_attention(q, k, v, p['q_norm_g'][l], p['k_norm_g'][l], p['attn_sinks'][l], bsz)
    yc = _ssd_mixer(z, xbc, dt_raw, p['ssd_conv_w'][l], p['ssd_conv_b'][l], p['ssd_dt_bias'][l],
                    p['ssd_a_log'][l], p['ssd_d'][l], p['ssd_norm_g'][l], bsz)
    x1, h2, route = _merge(xs, t, ys5, u, yb, yc, lw)
    moe = _moe_sparse(h2, route, p['w_exp_gate'], p['w_exp_up'], p['w_exp_down'], l)
    return [x1, route, moe]


def kernel(x, norm1_g, w_in, b_gate, s5_lambda_re, s5_lambda_im, s5_b_re, s5_b_im, s5_c_re, s5_c_im, s5_d, s5_log_dt, s5_glu_w1, s5_glu_w2, q_norm_g, k_norm_g, attn_sinks, ssd_conv_w, ssd_conv_b, ssd_dt_bias, ssd_a_log, ssd_d, ssd_norm_g, p_s5, p_attn, p_ssd, w_out, norm2_g, w_router_group, b_router_group, w_router_expert, b_router_expert, w_exp_gate, w_exp_up, w_exp_down):
    p = dict(norm1_g=norm1_g, w_in=w_in, b_gate=b_gate, s5_lambda_re=s5_lambda_re, s5_lambda_im=s5_lambda_im,
             s5_b_re=s5_b_re, s5_b_im=s5_b_im, s5_c_re=s5_c_re, s5_c_im=s5_c_im, s5_d=s5_d,
             s5_log_dt=s5_log_dt, s5_glu_w1=s5_glu_w1, s5_glu_w2=s5_glu_w2, q_norm_g=q_norm_g,
             k_norm_g=k_norm_g, attn_sinks=attn_sinks, ssd_conv_w=ssd_conv_w, ssd_conv_b=ssd_conv_b,
             ssd_dt_bias=ssd_dt_bias, ssd_a_log=ssd_a_log, ssd_d=ssd_d, ssd_norm_g=ssd_norm_g, p_s5=p_s5,
             p_attn=p_attn, p_ssd=p_ssd, w_out=w_out, norm2_g=norm2_g, w_router_group=w_router_group,
             b_router_group=b_router_group, w_router_expert=w_router_expert, b_router_expert=b_router_expert,
             w_exp_gate=w_exp_gate, w_exp_up=w_exp_up, w_exp_down=w_exp_down)
    bsz, seq, dm = x.shape
    depth = w_in.shape[0]
    t = bsz * seq
    xs = [x.reshape(t, dm)]
    s5_tabs = jax.vmap(_s5_tables)(s5_lambda_re, s5_lambda_im, s5_b_re, s5_b_im, s5_c_re, s5_c_im, s5_log_dt)
    for l in range(depth):
        xs = _layer(xs, t, l, p, s5_tabs, bsz)
    return _residual_sum(xs, t).reshape(bsz, seq, dm)
```

```python
import functools
import math

import jax
import jax.numpy as jnp
from jax import lax
from jax.experimental import pallas as pl
from jax.experimental.pallas import tpu as pltpu

F32 = jnp.float32
BF16 = jnp.bfloat16

D_MODEL = 1024
NORM_EPS = 1e-6
S5_WIDTH = 512
S5_GROUP = 16
S5_GROUPS = 32
S5_STATE = 64
HEAD_DIM = 64
ATTN_HEADS = 8
ATTN_KV_HEADS = 2
ATTN_REP = ATTN_HEADS // ATTN_KV_HEADS
ATTN_Q = ATTN_HEADS * HEAD_DIM
ATTN_KV = ATTN_KV_HEADS * HEAD_DIM
ATTN_BLOCK = 128
ROPE_THETA = 10000.0
SSD_WIDTH = 1024
SSD_HEAD_DIM = 64
SSD_HEADS = 16
SSD_GROUPS = 2
SSD_STATE = 64
SSD_CONV = 4
SSD_CHUNK = 128
SSD_BC = SSD_GROUPS * SSD_STATE
SSD_CONV_CH = SSD_WIDTH + 2 * SSD_BC
SSD_GROUP_W = SSD_WIDTH // SSD_GROUPS
N_BRANCH = 3
MOE_GROUPS = 4
MOE_PER_GROUP = 8
MOE_EXPERTS = 32
MOE_FF = 512
N_MIX = S5_WIDTH + ATTN_Q + 2 * ATTN_KV + SSD_WIDTH + SSD_CONV_CH + SSD_HEADS

LANES = 128
SUBLANES = 8
VMEM_LIMIT_BYTES = 56 * 1024 * 1024

S5_CHUNK = 16
S5_TG = 8
ATTN_TILE = 512
TM_PROJ = 512
TM_G = 256
DT_PAD = LANES
N_MIX_PAD = N_MIX - SSD_HEADS + DT_PAD
ROUTE_PAD = LANES


def _cparams(semantics):
    return pltpu.CompilerParams(dimension_semantics=semantics, vmem_limit_bytes=VMEM_LIMIT_BYTES)


def _const_spec(shape):
    zeros = (0,) * len(shape)
    return pl.BlockSpec(shape, lambda *_: zeros, pipeline_mode=pl.Buffered(1))


def _dot(a, b):
    return jnp.dot(a, b, preferred_element_type=F32)


def _split3(a):
    hi = a.astype(BF16)
    r1 = a - hi.astype(F32)
    mid = r1.astype(BF16)
    lo = (r1 - mid.astype(F32)).astype(BF16)
    return hi, mid, lo


def _dot_f32_lhs(a, b_bf16):
    hi, mid, lo = _split3(a)
    return _dot(hi, b_bf16) + _dot(mid, b_bf16) + _dot(lo, b_bf16)


def _rms_scale(x):
    return x * lax.rsqrt(jnp.mean(x * x, axis=-1, keepdims=True) + NORM_EPS)


def _silu(x):
    return x * jax.nn.sigmoid(x)


_OFF_U = 0
_OFF_Q = _OFF_U + S5_WIDTH
_OFF_K = _OFF_Q + ATTN_Q
_OFF_V = _OFF_K + ATTN_KV
_OFF_Z = _OFF_V + ATTN_KV
_OFF_XBC = _OFF_Z + SSD_WIDTH
_OFF_DT = _OFF_XBC + SSD_CONV_CH


ROW_TILE = D_MODEL // LANES


def _store_row_tiles(ref, start, val):
    n = val.shape[0]
    for j in range(ROW_TILE):
        ref[pl.ds(start * ROW_TILE + j, n, stride=ROW_TILE), :] = val[:, j * LANES:(j + 1) * LANES]


def _load_row_tiles(ref, start, n):
    return jnp.concatenate([ref[pl.ds(start * ROW_TILE + j, n, stride=ROW_TILE), :] for j in range(ROW_TILE)],
                           axis=1)


def _x_parts(xs, tm):
    row = lambda w: pl.BlockSpec((tm, w), lambda i: (i, 0))
    if len(xs) == 1:
        return list(xs), [row(D_MODEL)]
    x1, route, moe = xs
    plane = lambda k: pl.BlockSpec((None, tm * ROW_TILE, LANES), lambda i: (k, i, 0))
    return [x1, route, moe, moe], [row(D_MODEL), row(ROUTE_PAD), plane(0), plane(1)]


def _x_sum(x_refs):
    if len(x_refs) == 1:
        return x_refs[0][...]
    x1_ref, route_ref, m0_ref, m1_ref = x_refs
    r = route_ref[...]
    n = x1_ref.shape[0]
    return x1_ref[...] + r[:, 2:3] * _load_row_tiles(m0_ref, 0, n) + r[:, 3:4] * _load_row_tiles(m1_ref, 0, n)


def _inproj_kernel(*refs, n_x):
    x_refs, (g_ref, w_ref), outs, wb = refs[:n_x], refs[n_x:n_x + 2], refs[n_x + 2:-1], refs[-1]

    @pl.when(pl.program_id(0) == 0)
    def _():
        wb[...] = w_ref[...].astype(BF16)

    x = _x_sum(x_refs)
    hb = (_rms_scale(x) * g_ref[...]).astype(BF16)
    offs = (_OFF_U, _OFF_Q, _OFF_K, _OFF_V, _OFF_Z, _OFF_XBC, _OFF_DT)
    for o_ref, off in zip(outs, offs):
        width = o_ref.shape[1]
        o_ref[...] = _dot(hb, wb[:, off:off + width]).astype(o_ref.dtype)
    if len(outs) > len(offs):
        outs[-1][...] = x


def _inproj(xs, t, g, w_in, l):
    tm = min(TM_PROJ, t)
    widths = (S5_WIDTH, ATTN_Q, ATTN_KV, ATTN_KV, SSD_WIDTH, SSD_CONV_CH, DT_PAD)
    dtypes = (F32, BF16, BF16, BF16, BF16, BF16, F32)
    if len(xs) > 1:
        widths, dtypes = widths + (D_MODEL,), dtypes + (F32,)
    x_arrs, x_specs = _x_parts(xs, tm)
    w_spec = pl.BlockSpec((None, D_MODEL, N_MIX_PAD), lambda i: (l, 0, 0), pipeline_mode=pl.Buffered(1))
    return pl.pallas_call(
        functools.partial(_inproj_kernel, n_x=len(x_arrs)),
        grid=(t // tm,),
        in_specs=x_specs + [_const_spec((1, D_MODEL)), w_spec],
        out_specs=[pl.BlockSpec((tm, w), lambda i: (i, 0)) for w in widths],
        out_shape=[jax.ShapeDtypeStruct((t, w), d) for w, d in zip(widths, dtypes)],
        scratch_shapes=[pltpu.VMEM((D_MODEL, N_MIX_PAD), BF16)],
        compiler_params=_cparams(("arbitrary",)),
        name="inproj",
    )(*x_arrs, g, w_in)


def _s5_tables(lam_re, lam_im, b_re, b_im, c_re, c_im, log_dt):
    hp = lax.Precision.HIGHEST
    g_n, p_n = lam_re.shape
    c_n = b_re.shape[-1]
    t_n = S5_CHUNK
    lr, li = lam_re.astype(F32), lam_im.astype(F32)
    dt = jnp.exp(log_dt.astype(F32))[:, None]
    mag = jnp.exp(lr * dt)
    ab_re = mag * jnp.cos(li * dt)
    ab_im = mag * jnp.sin(li * dt)
    nr = ab_re - 1.0
    den = lr * lr + li * li
    f_re = (nr * lr + ab_im * li) / den
    f_im = (ab_im * lr - nr * li) / den
    br, bi = b_re.astype(F32), b_im.astype(F32)
    bb_re = f_re[..., None] * br - f_im[..., None] * bi
    bb_im = f_re[..., None] * bi + f_im[..., None] * br
    j = jnp.arange(t_n + 1, dtype=F32)[:, None, None]
    pmag = jnp.exp(lr * dt * j)
    ang = li * dt * j
    p_re = pmag * jnp.cos(ang)
    p_im = pmag * jnp.sin(ang)
    cr, ci = c_re.astype(F32), c_im.astype(F32)
    ca_re = cr[None] * p_re[:, :, None, :] - ci[None] * p_im[:, :, None, :]
    ca_im = cr[None] * p_im[:, :, None, :] + ci[None] * p_re[:, :, None, :]
    kern = (jnp.einsum('jgcp,gpd->jgcd', ca_re[:t_n], bb_re, precision=hp)
            - jnp.einsum('jgcp,gpd->jgcd', ca_im[:t_n], bb_im, precision=hp))
    q_re = p_re[t_n - 1 - jnp.arange(t_n)]
    q_im = p_im[t_n - 1 - jnp.arange(t_n)]
    bs_re = q_re[..., None] * bb_re[None] - q_im[..., None] * bb_im[None]
    bs_im = q_re[..., None] * bb_im[None] + q_im[..., None] * bb_re[None]

    nt = g_n // S5_TG
    k_strip = kern.reshape(t_n, nt, S5_TG, c_n, c_n).transpose(1, 0, 4, 2, 3)
    k_strip = k_strip.reshape(nt, t_n, c_n, S5_TG * c_n)

    def in_strip(a):
        a = a.reshape(t_n, nt, S5_TG, p_n, c_n).transpose(1, 0, 4, 2, 3)
        return a.reshape(nt, t_n, c_n, S5_TG * p_n)

    def out_strip(a):
        a = a.reshape(t_n, nt, S5_TG, c_n, p_n).transpose(1, 4, 0, 2, 3)
        return a.reshape(nt, p_n, t_n * S5_TG * c_n)

    return dict(
        k=k_strip, bs_re=in_strip(bs_re), bs_im=in_strip(bs_im),
        co_re=out_strip(ca_re[1:]), co_im=out_strip(-ca_im[1:]),
        at_re=p_re[t_n].reshape(nt, 1, S5_TG * p_n), at_im=p_im[t_n].reshape(nt, 1, S5_TG * p_n))


_S5_XW = S5_CHUNK * LANES
_S5_SW = S5_TG * S5_STATE


def _group_block(strip, lanes_per_group):
    rpg = strip.shape[0]
    full = jnp.concatenate([strip] * S5_TG, axis=0)
    row_g = lax.broadcasted_iota(jnp.int32, full.shape, 0) // rpg
    lane_g = (lax.broadcasted_iota(jnp.int32, full.shape, 1) % (S5_TG * lanes_per_group)) // lanes_per_group
    return jnp.where(row_g == lane_g, full, 0.0).astype(BF16)


def _s5_build_tables(k_ref, bre_ref, bim_ref, cre_ref, cim_ref, big, bsre, bsim, core, coim):
    blocks = [_group_block(k_ref[0, j], S5_GROUP) for j in range(S5_CHUNK)]
    zero = jnp.zeros((LANES, LANES), BF16)
    for s in range(S5_CHUNK):
        for t in range(S5_CHUNK):
            big[s * LANES:(s + 1) * LANES, t * LANES:(t + 1) * LANES] = blocks[t - s] if t >= s else zero
    for src, dst in ((bre_ref, bsre), (bim_ref, bsim)):
        for s in range(S5_CHUNK):
            dst[s * LANES:(s + 1) * LANES, :] = _group_block(src[0, s], S5_STATE)
    for src, dst in ((cre_ref, core), (cim_ref, coim)):
        for t in range(S5_CHUNK):
            cols = slice(t * LANES, (t + 1) * LANES)
            dst[:, cols] = _group_block(src[0, :, cols], S5_GROUP)


def _s5_kernel(u_ref, k_ref, bre_ref, bim_ref, cre_ref, cim_ref, are_ref, aim_ref, y_ref,
               big, bsre, bsim, core, coim, sre, sim, hre, him):
    @pl.when(pl.program_id(1) == 0)
    def _():
        _s5_build_tables(k_ref, bre_ref, bim_ref, cre_ref, cim_ref, big, bsre, bsim, core, coim)

    nck = u_ref.shape[0] // S5_CHUNK
    xcat = jnp.concatenate([u_ref[pl.ds(s, nck, stride=S5_CHUNK), :].astype(BF16) for s in range(S5_CHUNK)],
                           axis=1)
    sre[...] = _dot(xcat, bsre[...])
    sim[...] = _dot(xcat, bsim[...])
    a_re = are_ref[0]
    a_im = aim_ref[0]

    def step(i, carry):
        h_re, h_im = carry
        base = pl.multiple_of(i * SUBLANES, SUBLANES)
        s_re = sre[pl.ds(base, SUBLANES), :]
        s_im = sim[pl.ds(base, SUBLANES), :]
        ent_re, ent_im = [], []
        for j in range(SUBLANES):
            ent_re.append(h_re)
            ent_im.append(h_im)
            h_re, h_im = (a_re * h_re - a_im * h_im + s_re[j:j + 1], a_re * h_im + a_im * h_re + s_im[j:j + 1])
        hre[pl.ds(base, SUBLANES), :] = jnp.concatenate(ent_re, axis=0)
        him[pl.ds(base, SUBLANES), :] = jnp.concatenate(ent_im, axis=0)
        return h_re, h_im

    zero = jnp.zeros((1, _S5_SW), F32)
    lax.fori_loop(0, nck // SUBLANES, step, (zero, zero))
    y = (_dot(xcat, big[...]) + _dot(hre[...].astype(BF16), core[...])
         + _dot(him[...].astype(BF16), coim[...]))
    for t in range(S5_CHUNK):
        y_ref[pl.ds(t, nck, stride=S5_CHUNK), :] = y[:, t * LANES:(t + 1) * LANES]


def _s5_mixer(u, tabs, l, bsz):
    t = u.shape[0]
    seq = t // bsz
    nck = seq // S5_CHUNK
    nt = S5_GROUPS // S5_TG
    strip = lambda a: pl.BlockSpec((None, 1) + a.shape[2:], lambda x, b: (l, x) + (0,) * (a.ndim - 2))
    strips = [tabs['k'], tabs['bs_re'], tabs['bs_im'], tabs['co_re'], tabs['co_im'], tabs['at_re'], tabs['at_im']]
    return pl.pallas_call(
        _s5_kernel,
        grid=(nt, bsz),
        in_specs=[pl.BlockSpec((seq, LANES), lambda x, b: (b, x))] + [strip(a) for a in strips],
        out_specs=pl.BlockSpec((seq, LANES), lambda x, b: (b, x)),
        out_shape=jax.ShapeDtypeStruct((t, S5_WIDTH), F32),
        scratch_shapes=[pltpu.VMEM((_S5_XW, _S5_XW), BF16),
                        pltpu.VMEM((_S5_XW, _S5_SW), BF16), pltpu.VMEM((_S5_XW, _S5_SW), BF16),
                        pltpu.VMEM((_S5_SW, _S5_XW), BF16), pltpu.VMEM((_S5_SW, _S5_XW), BF16)]
                       + [pltpu.VMEM((nck, _S5_SW), F32)] * 4,
        compiler_params=_cparams(("arbitrary", "arbitrary")),
        name="s5_mixer",
    )(u, *strips)


def _swap_rope_halves(y):
    w = y.shape[1]
    lane = lax.broadcasted_iota(jnp.int32, y.shape, 1)
    lower = (lane % HEAD_DIM) < (HEAD_DIM // 2)
    return jnp.where(lower, pltpu.roll(y, w - HEAD_DIM // 2, 1), pltpu.roll(y, HEAD_DIM // 2, 1))


def _norm_rope(x, gain, head_mean, cos2, sin2):
    reps = x.shape[1] // LANES
    ms = _dot_f32_lhs(x * x, head_mean)
    y = x * lax.rsqrt(ms + NORM_EPS) * gain
    if reps > 1:
        cos2 = jnp.concatenate([cos2] * reps, axis=1)
        sin2 = jnp.concatenate([sin2] * reps, axis=1)
    return y * cos2 + _swap_rope_halves(y) * sin2


def _attn_kernel(q_ref, kc_ref, kp_ref, vc_ref, vp_ref, cosc_ref, sinc_ref, cosp_ref, sinp_ref,
                 qg_ref, kg_ref, sink_ref, hmq_ref, hmk_ref, o_ref):
    seq_start = pl.program_id(1) == 0
    blk = ATTN_BLOCK
    nblk = q_ref.shape[0] // blk
    q = _norm_rope(q_ref[...].astype(F32), qg_ref[...], hmq_ref[...], cosc_ref[...], sinc_ref[...])
    q = (q * (HEAD_DIM ** -0.5)).astype(BF16)
    kc = _norm_rope(kc_ref[...].astype(F32), kg_ref[...], hmk_ref[...], cosc_ref[...], sinc_ref[...]).astype(BF16)
    kp = _norm_rope(kp_ref[...].astype(F32), kg_ref[...], hmk_ref[...], cosp_ref[...], sinp_ref[...]).astype(BF16)
    k_all = jnp.concatenate([kp, kc], axis=0)
    v_all = jnp.concatenate([vp_ref[...], vc_ref[...]], axis=0)
    shape = (2 * blk, ATTN_REP * blk)
    kj = lax.broadcasted_iota(jnp.int32, shape, 0)
    qi = lax.broadcasted_iota(jnp.int32, shape, 1) % blk + blk
    band = (kj <= qi) & (qi - kj < blk)
    band_first = band & ((kj >= blk) | jnp.logical_not(seq_start))
    sinks = sink_ref[...]
    for j in range(ATTN_KV_HEADS):
        sl = slice(j * HEAD_DIM, (j + 1) * HEAD_DIM)
        heads = [j * ATTN_REP + r for r in range(ATTN_REP)]
        sink = jnp.concatenate([jnp.broadcast_to(sinks[:, h:h + 1], (1, blk)) for h in heads], axis=1)
        for n in range(nblk):
            rows = slice(n * blk, (n + 1) * blk)
            kb = k_all[n * blk:(n + 2) * blk, sl]
            vb = v_all[n * blk:(n + 2) * blk, sl]
            q4 = jnp.concatenate([q[rows, h * HEAD_DIM:(h + 1) * HEAD_DIM] for h in heads], axis=0)
            s = lax.dot_general(kb, q4, (((1,), (1,)), ((), ())), preferred_element_type=F32)
            s = jnp.where(band_first if n == 0 else band, s, -jnp.inf)
            m = jnp.maximum(jnp.max(s, axis=0, keepdims=True), sink)
            p = jnp.exp(s - m)
            denom = jnp.sum(p, axis=0, keepdims=True) + jnp.exp(sink - m)
            pn = (p * (1.0 / denom)).astype(BF16)
            out = lax.dot_general(pn, vb, (((0,), (0,)), ((), ())), preferred_element_type=F32)
            for r, h in enumerate(heads):
                o_ref[rows, h * HEAD_DIM:(h + 1) * HEAD_DIM] = out[r * blk:(r + 1) * blk].astype(o_ref.dtype)


def _rope_tables(seq):
    half = HEAD_DIM // 2
    inv = jnp.power(ROPE_THETA, -jnp.arange(half, dtype=F32) * 2.0 / HEAD_DIM)
    ang = jnp.arange(seq, dtype=F32)[:, None] * inv[None, :]
    cos, sin = jnp.cos(ang), jnp.sin(ang)
    cos2 = jnp.concatenate([cos, cos, cos, cos], axis=1)
    sin2 = jnp.concatenate([-sin, sin, -sin, sin], axis=1)
    return cos2, sin2


def _head_mean_matrix(width):
    i = jnp.arange(width)
    return jnp.where((i[:, None] // HEAD_DIM) == (i[None, :] // HEAD_DIM), 1.0 / HEAD_DIM, 0.0).astype(BF16)


def _swa_attention(q, k, v, q_g, k_g, sinks, bsz):
    t = q.shape[0]
    seq = t // bsz
    tile = min(ATTN_TILE, seq)
    nt = seq // tile
    bpt = tile // ATTN_BLOCK
    nb = seq // ATTN_BLOCK
    cos2, sin2 = _rope_tables(seq)
    qg = jnp.tile(q_g.astype(F32), ATTN_HEADS)[None]
    kg = jnp.tile(k_g.astype(F32), ATTN_KV_HEADS)[None]
    cur = lambda b, n: (b * nt + n, 0)
    prev = lambda b, n: (b * nb + jnp.maximum(n * bpt - 1, 0), 0)
    tcur = lambda b, n: (n, 0)
    tprev = lambda b, n: (jnp.maximum(n * bpt - 1, 0), 0)
    blk = ATTN_BLOCK
    return pl.pallas_call(
        _attn_kernel,
        grid=(bsz, nt),
        in_specs=[pl.BlockSpec((tile, ATTN_Q), cur),
                  pl.BlockSpec((tile, ATTN_KV), cur), pl.BlockSpec((blk, ATTN_KV), prev),
                  pl.BlockSpec((tile, ATTN_KV), cur), pl.BlockSpec((blk, ATTN_KV), prev),
                  pl.BlockSpec((tile, LANES), tcur), pl.BlockSpec((tile, LANES), tcur),
                  pl.BlockSpec((blk, LANES), tprev), pl.BlockSpec((blk, LANES), tprev),
                  _const_spec((1, ATTN_Q)), _const_spec((1, ATTN_KV)), _const_spec((1, ATTN_HEADS)),
                  _const_spec((ATTN_Q, ATTN_Q)), _const_spec((ATTN_KV, ATTN_KV))],
        out_specs=pl.BlockSpec((tile, ATTN_Q), cur),
        out_shape=jax.ShapeDtypeStruct((t, ATTN_Q), BF16),
        compiler_params=_cparams(("parallel", "parallel")),
        name="swa_attention",
    )(q, k, k, v, v, cos2, sin2, cos2, sin2, qg, kg, sinks.astype(F32)[None],
      _head_mean_matrix(ATTN_Q), _head_mean_matrix(ATTN_KV))


_CONV_TAIL = SSD_CHUNK


def _ssd_kernel(z_ref, xbc_ref, dt_ref, cw_ref, cb_ref, dtb_ref, alog_ref, dskip_ref, ng_ref, exp_ref, shift_ref,
                o_ref, tail, state, ybuf):
    c = pl.program_id(1)
    ch = SSD_CHUNK

    @pl.when(c == 0)
    def _():
        tail[0] = jnp.zeros(tail.shape[1:], tail.dtype)
        state[...] = jnp.zeros_like(state)

    cur = xbc_ref[...]
    shifted = _dot(shift_ref[...], jnp.concatenate([tail[c % 2], cur], axis=0))
    tail[1 - c % 2] = cur
    conv = cb_ref[...] + cw_ref[SSD_CONV - 1:SSD_CONV, :] * cur.astype(F32)
    for d in range(1, SSD_CONV):
        conv = conv + cw_ref[SSD_CONV - 1 - d:SSD_CONV - d, :] * shifted[(d - 1) * ch:d * ch]
    act = _silu(conv)
    xs = act[:, :SSD_WIDTH]
    bm = act[:, SSD_WIDTH:SSD_WIDTH + SSD_BC].astype(BF16)
    cm = act[:, SSD_WIDTH + SSD_BC:].astype(BF16)

    lane = lax.broadcasted_iota(jnp.int32, (ch, LANES), 1)
    xdt = dt_ref[...] + dtb_ref[...]
    dt = jnp.maximum(xdt, 0.0) + jnp.log1p(jnp.exp(-jnp.abs(xdt)))
    dt = jnp.where(lane < SSD_HEADS, dt, 0.0)
    a = dt * (-jnp.exp(alog_ref[...]))
    row = lax.broadcasted_iota(jnp.int32, (ch, ch), 0)
    col = lax.broadcasted_iota(jnp.int32, (ch, ch), 1)
    causal = row >= col
    hi, mid, lo = _split3(a)
    tril = causal.astype(BF16)
    cs = _dot(tril, hi) + _dot(tril, mid) + _dot(tril, lo)
    cs_t = cs.T
    expand = exp_ref[...]
    dt_x = _dot_f32_lhs(dt, expand)
    cs_x = _dot_f32_lhs(cs, expand)
    cs_last_x = cs_x[ch - 1:ch, :]
    xdt_full = xs * dt_x
    in_decay = jnp.exp(cs_x)
    out_decay = jnp.exp(cs_last_x - cs_x)
    chunk_decay = jnp.exp(cs_last_x)
    xw = (xdt_full * out_decay).astype(BF16)
    xdt_b = xdt_full.astype(BF16)
    hpg = SSD_HEADS // SSD_GROUPS
    for g in range(SSD_GROUPS):
        gs = slice(g * SSD_STATE, (g + 1) * SSD_STATE)
        ws = slice(g * SSD_GROUP_W, (g + 1) * SSD_GROUP_W)
        bg = bm[:, gs]
        cg = cm[:, gs]
        cb = lax.dot_general(cg, bg, (((1,), (1,)), ((), ())), preferred_element_type=F32)
        prev = state[g]
        ybuf[:, ws] = _dot(cg, prev.astype(BF16)) * in_decay[:, ws]
        for h in range(hpg):
            hh = g * hpg + h
            seg = cs[:, hh:hh + 1] - cs_t[hh:hh + 1, :]
            lmat = jnp.where(causal, jnp.exp(seg), 0.0)
            hs = slice(hh * SSD_HEAD_DIM, (hh + 1) * SSD_HEAD_DIM)
            ybuf[:, hs] += _dot((cb * lmat).astype(BF16), xdt_b[:, hs])
        upd = lax.dot_general(bg, xw[:, ws], (((0,), (0,)), ((), ())), preferred_element_type=F32)
        state[g] = prev * chunk_decay[:, ws] + upd

    y = ybuf[...] + xs * dskip_ref[...]
    y = y * _silu(z_ref[...].astype(F32))
    parts = []
    for g in range(SSD_GROUPS):
        ws = slice(g * SSD_GROUP_W, (g + 1) * SSD_GROUP_W)
        parts.append(_rms_scale(y[:, ws]))
    o_ref[...] = (jnp.concatenate(parts, axis=1) * ng_ref[...]).astype(o_ref.dtype)


def _ssd_mixer(z, xbc, dt_raw, conv_w, conv_b, dt_bias, a_log, d_skip, norm_g, bsz):
    t = z.shape[0]
    seq = t // bsz
    nc = seq // SSD_CHUNK
    pad = lambda v: jnp.pad(v.astype(F32), (0, LANES - SSD_HEADS))[None]
    d_x = jnp.repeat(d_skip.astype(F32), SSD_HEAD_DIM)[None]
    hid = jnp.arange(SSD_WIDTH) // SSD_HEAD_DIM
    expand = (jnp.arange(LANES)[:, None] == hid[None, :]).astype(BF16)
    r = jnp.arange((SSD_CONV - 1) * SSD_CHUNK)
    pick = r % SSD_CHUNK + _CONV_TAIL - (r // SSD_CHUNK + 1)
    shift = (jnp.arange(_CONV_TAIL + SSD_CHUNK)[None, :] == pick[:, None]).astype(BF16)
    blk = lambda b, c: (b * nc + c, 0)
    return pl.pallas_call(
        _ssd_kernel,
        grid=(bsz, nc),
        in_specs=[pl.BlockSpec((SSD_CHUNK, SSD_WIDTH), blk),
                  pl.BlockSpec((SSD_CHUNK, SSD_CONV_CH), blk),
                  pl.BlockSpec((SSD_CHUNK, DT_PAD), blk),
                  _const_spec((SSD_CONV, SSD_CONV_CH)), _const_spec((1, SSD_CONV_CH)),
                  _const_spec((1, LANES)), _const_spec((1, LANES)),
                  _const_spec((1, SSD_WIDTH)), _const_spec((1, SSD_WIDTH)),
                  _const_spec((LANES, SSD_WIDTH)), _const_spec(shift.shape)],
        out_specs=pl.BlockSpec((SSD_CHUNK, SSD_WIDTH), blk),
        out_shape=jax.ShapeDtypeStruct((t, SSD_WIDTH), BF16),
        scratch_shapes=[pltpu.VMEM((2, _CONV_TAIL, SSD_CONV_CH), BF16),
                        pltpu.VMEM((SSD_GROUPS, SSD_STATE, SSD_GROUP_W), F32),
                        pltpu.VMEM((SSD_CHUNK, SSD_WIDTH), F32)],
        compiler_params=_cparams(("arbitrary", "arbitrary")),
        name="ssd_mixer",
    )(z, xbc, dt_raw, conv_w.astype(F32), conv_b.astype(F32)[None], pad(dt_bias), pad(a_log),
      d_x, norm_g.astype(F32)[None], expand, shift)


def _route(logits):
    lane = lax.broadcasted_iota(jnp.int32, logits.shape, 1).astype(F32)
    big = float(ROUTE_PAD)
    is_g = lane < MOE_GROUPS
    gl = jnp.where(is_g, logits, -jnp.inf)
    gmax = jnp.max(gl, axis=1, keepdims=True)
    gsel = jnp.min(jnp.where(gl == gmax, lane, big), axis=1, keepdims=True)
    pg = 1.0 / jnp.sum(jnp.where(is_g, jnp.exp(logits - gmax), 0.0), axis=1, keepdims=True)
    lo = MOE_GROUPS + MOE_PER_GROUP * gsel
    ev = jnp.where((lane >= lo) & (lane < lo + MOE_PER_GROUP), logits, -jnp.inf)
    m1 = jnp.max(ev, axis=1, keepdims=True)
    i1 = jnp.min(jnp.where(ev == m1, lane, big), axis=1, keepdims=True)
    ev2 = jnp.where(lane == i1, -jnp.inf, ev)
    m2 = jnp.max(ev2, axis=1, keepdims=True)
    i2 = jnp.min(jnp.where(ev2 == m2, lane, big), axis=1, keepdims=True)
    e21 = jnp.exp(m2 - m1)
    w1 = pg / (1.0 + e21)
    w2 = pg * e21 / (1.0 + e21)
    out = jnp.where(lane == 0, i1 - MOE_GROUPS, 0.0)
    out = jnp.where(lane == 1, i2 - MOE_GROUPS, out)
    out = jnp.where(lane == 2, w1, out)
    return jnp.where(lane == 3, w2, out)


def _merge_kernel(*refs, n_x):
    x_refs = refs[:n_x]
    (ys5_ref, u_ref, yb_ref, yc_ref, n1g_ref, wg_ref, bg_ref, d_ref, w1_ref, w2_ref, ps5_ref, pat_ref,
     pssd_ref, wo_ref, n2g_ref, wrh_ref, wrm_ref, br_ref, x1_o, h2_o, route_o) = refs[n_x:]
    x = _x_sum(x_refs)
    hb = (_rms_scale(x) * n1g_ref[...]).astype(BF16)
    ya = ys5_ref[...].astype(F32) + d_ref[...] * u_ref[...].astype(F32)
    yab = jax.nn.gelu(ya).astype(BF16)
    ya = _dot(yab, w1_ref[...]) * jax.nn.sigmoid(_dot(yab, w2_ref[...]))
    branches = ((ya.astype(BF16), ps5_ref), (yb_ref[...], pat_ref), (yc_ref[...], pssd_ref))
    merged = None
    for b, (yv, p_ref) in enumerate(branches):
        gate = jax.nn.sigmoid(_dot(hb, wg_ref[:, b * D_MODEL:(b + 1) * D_MODEL]) + bg_ref[b:b + 1, :])
        term = gate * _dot(yv, p_ref[...])
        merged = term if merged is None else merged + term
    x1 = x + _dot(merged.astype(BF16), wo_ref[...])
    x1_o[...] = x1
    h2 = _rms_scale(x1) * n2g_ref[...]
    _store_row_tiles(h2_o, 0, h2)
    h_hi = h2.astype(BF16)
    h_mid = (h2 - h_hi.astype(F32)).astype(BF16)
    logits = _dot(h_hi, wrh_ref[...]) + _dot(h_hi, wrm_ref[...]) + _dot(h_mid, wrh_ref[...]) + br_ref[...]
    route_o[...] = _route(logits)


def _merge(xs, t, ys5, u, yb, yc, lw):
    tm = min(TM_PROJ, t)
    row = lambda w: pl.BlockSpec((tm, w), lambda i: (i, 0))
    consts = [lw['n1g'], lw['w_gate'], lw['b_gate'], lw['s5_d'], lw['glu_w1'], lw['glu_w2'], lw['p_s5'],
              lw['p_attn'], lw['p_ssd'], lw['w_out'], lw['n2g'], lw['w_router_hi'], lw['w_router_mid'],
              lw['b_router']]
    x_arrs, x_specs = _x_parts(xs, tm)
    return pl.pallas_call(
        functools.partial(_merge_kernel, n_x=len(x_arrs)),
        grid=(t // tm,),
        in_specs=x_specs + [row(S5_WIDTH), row(S5_WIDTH), row(ATTN_Q), row(SSD_WIDTH)]
                 + [_const_spec(c.shape) for c in consts],
        out_specs=[row(D_MODEL), pl.BlockSpec((tm * ROW_TILE, LANES), lambda i: (i, 0)), row(ROUTE_PAD)],
        out_shape=[jax.ShapeDtypeStruct((t, D_MODEL), F32), jax.ShapeDtypeStruct((t * ROW_TILE, LANES), F32),
                   jax.ShapeDtypeStruct((t, ROUTE_PAD), F32)],
        compiler_params=_cparams(("parallel",)),
        name="merge_router",
    )(*x_arrs, ys5, u, yb, yc, *consts)


def _moe_plan(route, t):
    n_exp = MOE_EXPERTS
    e = route[:, 0:2].astype(jnp.int32).reshape(-1)
    onehot = (e[:, None] == jnp.arange(n_exp, dtype=jnp.int32)[None, :]).astype(jnp.int32)
    csum = jnp.cumsum(onehot, axis=0)
    rank = jnp.sum(csum * onehot, axis=1) - 1
    counts = csum[-1]
    padded = ((counts + TM_G - 1) // TM_G) * TM_G
    pend = jnp.cumsum(padded)
    dest = (pend - padded)[e] + rank
    n_tiles = -(-(2 * t + n_exp * (TM_G - 1)) // TM_G)
    n_rows = n_tiles * TM_G
    pair = jnp.full((n_rows,), -1, jnp.int32).at[dest].set(jnp.arange(2 * t, dtype=jnp.int32),
                                                            unique_indices=True)
    valid = pair >= 0
    tok, k = pair // 2, pair % 2
    src = jnp.where(valid, tok, 0)
    spare = t + jnp.arange(n_rows, dtype=jnp.int32) % _MOE_SPARE
    dst = jnp.where(valid, k * (t + _MOE_SPARE) + tok, spare)
    tile_start = jnp.arange(n_tiles, dtype=jnp.int32) * TM_G
    tile_e = jnp.sum((pend[None, :] <= tile_start[:, None]).astype(jnp.int32), axis=1)
    n_used = pend[-1:] // TM_G
    tile_e = jnp.minimum(tile_e, n_exp - 1)
    tile_e = jnp.minimum(tile_e, tile_e[n_used[0] - 1])
    return src.reshape(n_tiles, 1, TM_G), dst.reshape(n_tiles, 1, TM_G), tile_e, n_used.astype(jnp.int32)


_MOE_SPARE = 2 * TM_G


def _gmm_kernel(te_ref, nu_ref, src_ref, srcn_ref, dst_ref, h2_hbm, wg_ref, wu_ref, wd_ref, out_hbm,
                xbuf, ybuf, gsem, ssem):
    del te_ref
    i = pl.program_id(0)
    last = nu_ref[0] - 1
    slot = i % 2

    tile_rows = TM_G * ROW_TILE

    def row_tile(ref, s, r):
        return ref.at[pl.ds((s * TM_G + r) * ROW_TILE, ROW_TILE), :]

    def slot_rows(ref, s):
        return ref.at[pl.ds(s * tile_rows, tile_rows), :]

    def gather_start(idx_ref, s):
        for r in range(TM_G):
            src = h2_hbm.at[pl.ds(pl.multiple_of(idx_ref[0, 0, r], ROW_TILE), ROW_TILE), :]
            pltpu.make_async_copy(src, row_tile(xbuf, s, r), gsem.at[s]).start(priority=r % 2)

    def gather_wait(s):
        pltpu.make_async_copy(h2_hbm.at[pl.ds(0, tile_rows), :], slot_rows(xbuf, s), gsem.at[s]).wait()

    def scatter_wait(s):
        pltpu.make_async_copy(slot_rows(ybuf, s), out_hbm.at[pl.ds(0, tile_rows), :], ssem.at[s]).wait()

    @pl.when(i == 0)
    def _():
        gather_start(src_ref, 0)
        ybuf[...] = jnp.zeros_like(ybuf)
        plane = out_hbm.shape[0] // 2
        spare0 = plane - _MOE_SPARE * ROW_TILE
        fills = [pltpu.make_async_copy(slot_rows(ybuf, s),
                                       out_hbm.at[pl.ds(k * plane + spare0 + s * tile_rows, tile_rows), :],
                                       ssem.at[s]) for k in range(2) for s in range(2)]
        for f in fills:
            f.start()
        for f in fills:
            f.wait()

    @pl.when(i <= last)
    def _():
        gather_wait(slot)

        @pl.when(i >= 2)
        def _():
            scatter_wait(slot)

        gather_start(srcn_ref, 1 - slot)

        xb = _load_row_tiles(xbuf, slot * TM_G, TM_G).astype(BF16)
        hid = _silu(_dot(xb, wg_ref[0, 0].astype(BF16))) * _dot(xb, wu_ref[0, 0].astype(BF16))
        _store_row_tiles(ybuf, slot * TM_G, _dot(hid.astype(BF16), wd_ref[0, 0].astype(BF16)))
        for r in range(TM_G):
            dst = out_hbm.at[pl.ds(pl.multiple_of(dst_ref[0, 0, r], ROW_TILE), ROW_TILE), :]
            pltpu.make_async_copy(row_tile(ybuf, slot, r), dst, ssem.at[slot]).start(priority=r % 2)

        @pl.when(i == last)
        def _():
            scatter_wait(slot)
            gather_wait(1 - slot)

        @pl.when((i == last) & (i >= 1))
        def _():
            scatter_wait(1 - slot)


def _moe_sparse(h2, route, w_gate, w_up, w_down, l):
    t = h2.shape[0] // ROW_TILE
    src, dst, tile_e, n_used = _moe_plan(route, t)
    src, dst = src * ROW_TILE, dst * ROW_TILE
    n_tiles = src.shape[0]
    smem = lambda imap: pl.BlockSpec((1, 1, TM_G), imap, memory_space=pltpu.SMEM)
    grid_spec = pltpu.PrefetchScalarGridSpec(
        num_scalar_prefetch=2,
        grid=(n_tiles,),
        in_specs=[smem(lambda i, te, nu: (i, 0, 0)),
                  smem(lambda i, te, nu: (jnp.minimum(i + 1, n_tiles - 1), 0, 0)),
                  smem(lambda i, te, nu: (i, 0, 0)),
                  pl.BlockSpec(memory_space=pl.ANY),
                  pl.BlockSpec((1, 1, D_MODEL, MOE_FF), lambda i, te, nu: (l, te[i], 0, 0)),
                  pl.BlockSpec((1, 1, D_MODEL, MOE_FF), lambda i, te, nu: (l, te[i], 0, 0)),
                  pl.BlockSpec((1, 1, MOE_FF, D_MODEL), lambda i, te, nu: (l, te[i], 0, 0))],
        out_specs=pl.BlockSpec(memory_space=pl.ANY),
        scratch_shapes=[pltpu.VMEM((2 * TM_G * ROW_TILE, LANES), F32), pltpu.VMEM((2 * TM_G * ROW_TILE, LANES), F32),
                        pltpu.SemaphoreType.DMA((2,)), pltpu.SemaphoreType.DMA((2,))])
    plane = (t + _MOE_SPARE) * ROW_TILE
    out = pl.pallas_call(
        _gmm_kernel,
        grid_spec=grid_spec,
        out_shape=jax.ShapeDtypeStruct((2 * plane, LANES), F32),
        compiler_params=_cparams(("arbitrary",)),
        name="moe_gmm",
    )(tile_e, n_used, src, src, dst, h2, w_gate, w_up, w_down)
    return out.reshape(2, plane, LANES)


def _sum_kernel(*refs):
    refs[-1][...] = _x_sum(refs[:-1])


def _residual_sum(xs, t):
    tm = min(TM_PROJ, t)
    x_arrs, x_specs = _x_parts(xs, tm)
    return pl.pallas_call(
        _sum_kernel,
        grid=(t // tm,),
        in_specs=x_specs,
        out_specs=pl.BlockSpec((tm, D_MODEL), lambda i: (i, 0)),
        out_shape=jax.ShapeDtypeStruct((t, D_MODEL), F32),
        compiler_params=_cparams(("parallel",)),
        name="residual_sum",
    )(*x_arrs)


def _cast_kernel(*refs):
    n = len(refs) // 2
    for x_ref, o_ref in zip(refs[:n], refs[n:]):
        o_ref[...] = x_ref[...].astype(o_ref.dtype)


def _cast_layer_bf16(ws, l):
    halves = 2
    in_specs = [pl.BlockSpec((None, w.shape[1] // halves, w.shape[2]), lambda i: (l, i, 0)) for w in ws]
    out_specs = [pl.BlockSpec((w.shape[1] // halves, w.shape[2]), lambda i: (i, 0)) for w in ws]
    return pl.pallas_call(
        _cast_kernel,
        grid=(halves,),
        in_specs=in_specs,
        out_specs=out_specs,
        out_shape=[jax.ShapeDtypeStruct(w.shape[1:], BF16) for w in ws],
        compiler_params=_cparams(("parallel",)),
        name="cast_weights",
    )(*[w.astype(F32) for w in ws])


_GATE_SHIFT = N_MIX % LANES
_GATE_BLOCK = 512
assert (N_MIX - _GATE_SHIFT) % _GATE_BLOCK == 0 and (N_BRANCH * D_MODEL) % _GATE_BLOCK == 0


def _gate_w_kernel(a_ref, b_ref, o_ref):
    full = jnp.concatenate([a_ref[...], b_ref[...]], axis=1)
    shifted = pltpu.roll(full, full.shape[1] - _GATE_SHIFT, 1)
    o_ref[...] = shifted[:, :_GATE_BLOCK].astype(o_ref.dtype)


def _gate_weights(w_in, l):
    base = (N_MIX - _GATE_SHIFT) // _GATE_BLOCK
    per = _GATE_BLOCK // LANES
    return pl.pallas_call(
        _gate_w_kernel,
        grid=(N_BRANCH * D_MODEL // _GATE_BLOCK,),
        in_specs=[pl.BlockSpec((None, D_MODEL, _GATE_BLOCK), lambda k: (l, 0, base + k)),
                  pl.BlockSpec((None, D_MODEL, LANES), lambda k: (l, 0, (base + k + 1) * per))],
        out_specs=pl.BlockSpec((D_MODEL, _GATE_BLOCK), lambda k: (0, k)),
        out_shape=jax.ShapeDtypeStruct((D_MODEL, N_BRANCH * D_MODEL), BF16),
        compiler_params=_cparams(("parallel",)),
        name="gate_weights",
    )(w_in, w_in)


def _layer_weights(l, p):
    glu_w1, glu_w2, p_s5, p_attn, p_ssd, w_out = _cast_layer_bf16(
        [p['s5_glu_w1'], p['s5_glu_w2'], p['p_s5'], p['p_attn'], p['p_ssd'], p['w_out']], l)
    w_router = jnp.concatenate([p['w_router_group'][l], p['w_router_expert'][l]], axis=1).astype(F32)
    npad = ROUTE_PAD - w_router.shape[1]
    b_router = jnp.concatenate([p['b_router_group'][l], p['b_router_expert'][l]]).astype(F32)
    w_router = jnp.pad(w_router, ((0, 0), (0, npad)))
    w_router_hi = w_router.astype(BF16)
    return dict(
        n1g=p['norm1_g'][l].astype(F32)[None],
        w_gate=_gate_weights(p['w_in'].astype(F32), l), b_gate=p['b_gate'][l].astype(F32),
        s5_d=p['s5_d'][l].astype(F32)[None],
        glu_w1=glu_w1, glu_w2=glu_w2, p_s5=p_s5, p_attn=p_attn, p_ssd=p_ssd, w_out=w_out,
        n2g=p['norm2_g'][l].astype(F32)[None],
        w_router_hi=w_router_hi, w_router_mid=(w_router - w_router_hi.astype(F32)).astype(BF16),
        b_router=jnp.pad(b_router, (0, npad))[None])


def _layer(xs, t, l, p, s5_tabs, bsz):
    lw = _layer_weights(l, p)
    u, q, k, v, z, xbc, dt_raw, *x_sum = _inproj(xs, t, lw['n1g'], p['w_in'].astype(F32), l)
    if x_sum:
        xs = x_sum
    ys5 = _s5_mixer(u, s5_tabs, l, bsz)
    yb = _swa_attention(q, k, v, p['q_norm_g'][l], p['k_norm_g'][l], p['attn_sinks'][l], bsz)
    yc = _ssd_mixer(z, xbc, dt_raw, p['ssd_conv_w'][l], p['ssd_conv_b'][l], p['ssd_dt_bias'][l],
                    p['ssd_a_log'][l], p['ssd_d'][l], p['ssd_norm_g'][l], bsz)
    x1, h2, route = _merge(xs, t, ys5, u, yb, yc, lw)
    moe = _moe_sparse(h2, route, p['w_exp_gate'], p['w_exp_up'], p['w_exp_down'], l)
    return [x1, route, moe]


def kernel(x, norm1_g, w_in, b_gate, s5_lambda_re, s5_lambda_im, s5_b_re, s5_b_im, s5_c_re, s5_c_im, s5_d, s5_log_dt, s5_glu_w1, s5_glu_w2, q_norm_g, k_norm_g, attn_sinks, ssd_conv_w, ssd_conv_b, ssd_dt_bias, ssd_a_log, ssd_d, ssd_norm_g, p_s5, p_attn, p_ssd, w_out, norm2_g, w_router_group, b_router_group, w_router_expert, b_router_expert, w_exp_gate, w_exp_up, w_exp_down):
    p = dict(norm1_g=norm1_g, w_in=w_in, b_gate=b_gate, s5_lambda_re=s5_lambda_re, s5_lambda_im=s5_lambda_im,
             s5_b_re=s5_b_re, s5_b_im=s5_b_im, s5_c_re=s5_c_re, s5_c_im=s5_c_im, s5_d=s5_d,
             s5_log_dt=s5_log_dt, s5_glu_w1=s5_glu_w1, s5_glu_w2=s5_glu_w2, q_norm_g=q_norm_g,
             k_norm_g=k_norm_g, attn_sinks=attn_sinks, ssd_conv_w=ssd_conv_w, ssd_conv_b=ssd_conv_b,
             ssd_dt_bias=ssd_dt_bias, ssd_a_log=ssd_a_log, ssd_d=ssd_d, ssd_norm_g=ssd_norm_g, p_s5=p_s5,
             p_attn=p_attn, p_ssd=p_ssd, w_out=w_out, norm2_g=norm2_g, w_router_group=w_router_group,
             b_router_group=b_router_group, w_router_expert=w_router_expert, b_router_expert=b_router_expert,
             w_exp_gate=w_exp_gate, w_exp_up=w_exp_up, w_exp_down=w_exp_down)
    bsz, seq, dm = x.shape
    depth = w_in.shape[0]
    t = bsz * seq
    xs = [x.reshape(t, dm)]
    per_layer = [_s5_tables(s5_lambda_re[l], s5_lambda_im[l], s5_b_re[l], s5_b_im[l], s5_c_re[l], s5_c_im[l],
                            s5_log_dt[l]) for l in range(depth)]
    s5_tabs = {name: jnp.stack([tab[name] for tab in per_layer]) for name in per_layer[0]}
    for l in range(depth):
        xs = _layer(xs, t, l, p, s5_tabs, bsz)
    return _residual_sum(xs, t).reshape(bsz, seq, dm)
```

```python
import functools
import math

import jax
import jax.numpy as jnp
from jax import lax
from jax.experimental import pallas as pl
from jax.experimental.pallas import tpu as pltpu

F32 = jnp.float32
BF16 = jnp.bfloat16

D_MODEL = 1024
NORM_EPS = 1e-6
S5_WIDTH = 512
S5_GROUP = 16
S5_GROUPS = 32
S5_STATE = 64
HEAD_DIM = 64
ATTN_HEADS = 8
ATTN_KV_HEADS = 2
ATTN_REP = ATTN_HEADS // ATTN_KV_HEADS
ATTN_Q = ATTN_HEADS * HEAD_DIM
ATTN_KV = ATTN_KV_HEADS * HEAD_DIM
ATTN_BLOCK = 128
ROPE_THETA = 10000.0
SSD_WIDTH = 1024
SSD_HEAD_DIM = 64
SSD_HEADS = 16
SSD_GROUPS = 2
SSD_STATE = 64
SSD_CONV = 4
SSD_CHUNK = 128
SSD_BC = SSD_GROUPS * SSD_STATE
SSD_CONV_CH = SSD_WIDTH + 2 * SSD_BC
SSD_GROUP_W = SSD_WIDTH // SSD_GROUPS
N_BRANCH = 3
MOE_GROUPS = 4
MOE_PER_GROUP = 8
MOE_EXPERTS = 32
MOE_FF = 512
N_MIX = S5_WIDTH + ATTN_Q + 2 * ATTN_KV + SSD_WIDTH + SSD_CONV_CH + SSD_HEADS

LANES = 128
SUBLANES = 8
VMEM_LIMIT_BYTES = 56 * 1024 * 1024

S5_CHUNK = 16
S5_TG = 8
ATTN_TILE = 512
TM_PROJ = 512
TM_G = 256
SSD_STEP_CHUNKS = 2
DT_PAD = LANES
N_MIX_PAD = N_MIX - SSD_HEADS + DT_PAD
ROUTE_PAD = LANES


def _cparams(semantics):
    return pltpu.CompilerParams(dimension_semantics=semantics, vmem_limit_bytes=VMEM_LIMIT_BYTES)


def _const_spec(shape):
    zeros = (0,) * len(shape)
    return pl.BlockSpec(shape, lambda *_: zeros, pipeline_mode=pl.Buffered(1))


def _dot(a, b):
    return jnp.dot(a, b, preferred_element_type=F32)


def _split3(a):
    hi = a.astype(BF16)
    r1 = a - hi.astype(F32)
    mid = r1.astype(BF16)
    lo = (r1 - mid.astype(F32)).astype(BF16)
    return hi, mid, lo


def _dot_f32_lhs(a, b_bf16):
    hi, mid, lo = _split3(a)
    return _dot(hi, b_bf16) + _dot(mid, b_bf16) + _dot(lo, b_bf16)


def _rms_scale(x):
    return x * lax.rsqrt(jnp.mean(x * x, axis=-1, keepdims=True) + NORM_EPS)


def _silu(x):
    return x * jax.nn.sigmoid(x)


_OFF_U = 0
_OFF_Q = _OFF_U + S5_WIDTH
_OFF_K = _OFF_Q + ATTN_Q
_OFF_V = _OFF_K + ATTN_KV
_OFF_Z = _OFF_V + ATTN_KV
_OFF_XBC = _OFF_Z + SSD_WIDTH
_OFF_DT = _OFF_XBC + SSD_CONV_CH


ROW_TILE = D_MODEL // LANES


def _store_row_tiles(ref, start, val):
    n = val.shape[0]
    for j in range(ROW_TILE):
        ref[pl.ds(start * ROW_TILE + j, n, stride=ROW_TILE), :] = val[:, j * LANES:(j + 1) * LANES]


def _load_row_tiles(ref, start, n):
    return jnp.concatenate([ref[pl.ds(start * ROW_TILE + j, n, stride=ROW_TILE), :] for j in range(ROW_TILE)],
                           axis=1)


def _x_parts(xs, tm):
    row = lambda w: pl.BlockSpec((tm, w), lambda i: (i, 0))
    if len(xs) == 1:
        return list(xs), [row(D_MODEL)]
    x1, route, moe = xs
    plane = lambda k: pl.BlockSpec((None, tm * ROW_TILE, LANES), lambda i: (k, i, 0))
    return [x1, route, moe, moe], [row(D_MODEL), row(ROUTE_PAD), plane(0), plane(1)]


def _x_sum(x_refs):
    if len(x_refs) == 1:
        return x_refs[0][...]
    x1_ref, route_ref, m0_ref, m1_ref = x_refs
    r = route_ref[...]
    n = x1_ref.shape[0]
    return x1_ref[...] + r[:, 2:3] * _load_row_tiles(m0_ref, 0, n) + r[:, 3:4] * _load_row_tiles(m1_ref, 0, n)


def _inproj_kernel(*refs, n_x):
    x_refs, (g_ref, w_ref), outs, wb = refs[:n_x], refs[n_x:n_x + 2], refs[n_x + 2:-1], refs[-1]

    @pl.when(pl.program_id(0) == 0)
    def _():
        for j in range(N_MIX_PAD // LANES):
            wb[:, j * LANES:(j + 1) * LANES] = w_ref[j * LANES:(j + 1) * LANES, :].T.astype(BF16)

    x = _x_sum(x_refs)
    hb = (_rms_scale(x) * g_ref[...]).astype(BF16)
    offs = (_OFF_U, _OFF_Q, _OFF_K, _OFF_V, _OFF_Z, _OFF_XBC, _OFF_DT)
    for o_ref, off in zip(outs, offs):
        width = o_ref.shape[1]
        o_ref[...] = _dot(hb, wb[:, off:off + width]).astype(o_ref.dtype)
    if len(outs) > len(offs):
        outs[-1][...] = x


def _inproj(xs, t, g, w_in, l):
    tm = min(TM_PROJ, t)
    widths = (S5_WIDTH, ATTN_Q, ATTN_KV, ATTN_KV, SSD_WIDTH, SSD_CONV_CH, DT_PAD)
    dtypes = (F32, BF16, BF16, BF16, BF16, BF16, F32)
    if len(xs) > 1:
        widths, dtypes = widths + (D_MODEL,), dtypes + (F32,)
    x_arrs, x_specs = _x_parts(xs, tm)
    w_spec = pl.BlockSpec((None, N_MIX_PAD, D_MODEL), lambda i: (l, 0, 0), pipeline_mode=pl.Buffered(1))
    return pl.pallas_call(
        functools.partial(_inproj_kernel, n_x=len(x_arrs)),
        grid=(t // tm,),
        in_specs=x_specs + [_const_spec((1, D_MODEL)), w_spec],
        out_specs=[pl.BlockSpec((tm, w), lambda i: (i, 0)) for w in widths],
        out_shape=[jax.ShapeDtypeStruct((t, w), d) for w, d in zip(widths, dtypes)],
        scratch_shapes=[pltpu.VMEM((D_MODEL, N_MIX_PAD), BF16)],
        compiler_params=_cparams(("arbitrary",)),
        name="inproj",
    )(*x_arrs, g, w_in)


def _s5_tables(lam_re, lam_im, b_re, b_im, c_re, c_im, log_dt):
    hp = lax.Precision.HIGHEST
    g_n, p_n = lam_re.shape
    c_n = b_re.shape[-1]
    t_n = S5_CHUNK
    lr, li = lam_re.astype(F32), lam_im.astype(F32)
    dt = jnp.exp(log_dt.astype(F32))[:, None]
    mag = jnp.exp(lr * dt)
    ab_re = mag * jnp.cos(li * dt)
    ab_im = mag * jnp.sin(li * dt)
    nr = ab_re - 1.0
    den = lr * lr + li * li
    f_re = (nr * lr + ab_im * li) / den
    f_im = (ab_im * lr - nr * li) / den
    br, bi = b_re.astype(F32), b_im.astype(F32)
    bb_re = f_re[..., None] * br - f_im[..., None] * bi
    bb_im = f_re[..., None] * bi + f_im[..., None] * br
    j = jnp.arange(t_n + 1, dtype=F32)[:, None, None]
    pmag = jnp.exp(lr * dt * j)
    ang = li * dt * j
    p_re = pmag * jnp.cos(ang)
    p_im = pmag * jnp.sin(ang)
    cr, ci = c_re.astype(F32), c_im.astype(F32)
    ca_re = cr[None] * p_re[:, :, None, :] - ci[None] * p_im[:, :, None, :]
    ca_im = cr[None] * p_im[:, :, None, :] + ci[None] * p_re[:, :, None, :]
    kern = (jnp.einsum('jgcp,gpd->jgcd', ca_re[:t_n], bb_re, precision=hp)
            - jnp.einsum('jgcp,gpd->jgcd', ca_im[:t_n], bb_im, precision=hp))
    q_re = p_re[t_n - 1 - jnp.arange(t_n)]
    q_im = p_im[t_n - 1 - jnp.arange(t_n)]
    bs_re = q_re[..., None] * bb_re[None] - q_im[..., None] * bb_im[None]
    bs_im = q_re[..., None] * bb_im[None] + q_im[..., None] * bb_re[None]

    nt = g_n // S5_TG
    k_strip = kern.reshape(t_n, nt, S5_TG, c_n, c_n).transpose(1, 0, 4, 2, 3)
    k_strip = k_strip.reshape(nt, t_n, c_n, S5_TG * c_n)

    def in_strip(a):
        a = a.reshape(t_n, nt, S5_TG, p_n, c_n).transpose(1, 0, 4, 2, 3)
        return a.reshape(nt, t_n, c_n, S5_TG * p_n)

    def out_strip(a):
        a = a.reshape(t_n, nt, S5_TG, c_n, p_n).transpose(1, 4, 0, 2, 3)
        return a.reshape(nt, p_n, t_n * S5_TG * c_n)

    return dict(
        k=k_strip, bs_re=in_strip(bs_re), bs_im=in_strip(bs_im),
        co_re=out_strip(ca_re[1:]), co_im=out_strip(-ca_im[1:]),
        at_re=p_re[t_n].reshape(nt, 1, S5_TG * p_n), at_im=p_im[t_n].reshape(nt, 1, S5_TG * p_n))


_S5_XW = S5_CHUNK * LANES
_S5_SW = S5_TG * S5_STATE


def _group_block(strip, lanes_per_group):
    rpg = strip.shape[0]
    full = jnp.concatenate([strip] * S5_TG, axis=0)
    row_g = lax.broadcasted_iota(jnp.int32, full.shape, 0) // rpg
    lane_g = (lax.broadcasted_iota(jnp.int32, full.shape, 1) % (S5_TG * lanes_per_group)) // lanes_per_group
    return jnp.where(row_g == lane_g, full, 0.0).astype(BF16)


def _s5_build_tables(k_ref, bre_ref, bim_ref, cre_ref, cim_ref, big, bsre, bsim, core, coim):
    blocks = [_group_block(k_ref[0, j], S5_GROUP) for j in range(S5_CHUNK)]
    zero = jnp.zeros((LANES, LANES), BF16)
    for s in range(S5_CHUNK):
        for t in range(S5_CHUNK):
            big[s * LANES:(s + 1) * LANES, t * LANES:(t + 1) * LANES] = blocks[t - s] if t >= s else zero
    for src, dst in ((bre_ref, bsre), (bim_ref, bsim)):
        for s in range(S5_CHUNK):
            dst[s * LANES:(s + 1) * LANES, :] = _group_block(src[0, s], S5_STATE)
    for src, dst in ((cre_ref, core), (cim_ref, coim)):
        for t in range(S5_CHUNK):
            cols = slice(t * LANES, (t + 1) * LANES)
            dst[:, cols] = _group_block(src[0, :, cols], S5_GROUP)


def _s5_kernel(u_ref, k_ref, bre_ref, bim_ref, cre_ref, cim_ref, are_ref, aim_ref, y_ref,
               big, bsre, bsim, core, coim, sre, sim, hre, him):
    @pl.when(pl.program_id(1) == 0)
    def _():
        _s5_build_tables(k_ref, bre_ref, bim_ref, cre_ref, cim_ref, big, bsre, bsim, core, coim)

    nck = u_ref.shape[0] // S5_CHUNK
    xcat = jnp.concatenate([u_ref[pl.ds(s, nck, stride=S5_CHUNK), :].astype(BF16) for s in range(S5_CHUNK)],
                           axis=1)
    sre[...] = _dot(xcat, bsre[...])
    sim[...] = _dot(xcat, bsim[...])
    a_re = are_ref[0]
    a_im = aim_ref[0]

    def step(i, carry):
        h_re, h_im = carry
        base = pl.multiple_of(i * SUBLANES, SUBLANES)
        s_re = sre[pl.ds(base, SUBLANES), :]
        s_im = sim[pl.ds(base, SUBLANES), :]
        ent_re, ent_im = [], []
        for j in range(SUBLANES):
            ent_re.append(h_re)
            ent_im.append(h_im)
            h_re, h_im = (a_re * h_re - a_im * h_im + s_re[j:j + 1], a_re * h_im + a_im * h_re + s_im[j:j + 1])
        hre[pl.ds(base, SUBLANES), :] = jnp.concatenate(ent_re, axis=0)
        him[pl.ds(base, SUBLANES), :] = jnp.concatenate(ent_im, axis=0)
        return h_re, h_im

    zero = jnp.zeros((1, _S5_SW), F32)
    lax.fori_loop(0, nck // SUBLANES, step, (zero, zero))
    y = (_dot(xcat, big[...]) + _dot(hre[...].astype(BF16), core[...])
         + _dot(him[...].astype(BF16), coim[...]))
    for t in range(S5_CHUNK):
        y_ref[pl.ds(t, nck, stride=S5_CHUNK), :] = y[:, t * LANES:(t + 1) * LANES]


def _s5_mixer(u, tabs, bsz):
    t = u.shape[0]
    seq = t // bsz
    nck = seq // S5_CHUNK
    nt = S5_GROUPS // S5_TG
    strip = lambda a: pl.BlockSpec((1,) + a.shape[1:], lambda x, b: (x,) + (0,) * (a.ndim - 1))
    strips = [tabs['k'], tabs['bs_re'], tabs['bs_im'], tabs['co_re'], tabs['co_im'], tabs['at_re'], tabs['at_im']]
    return pl.pallas_call(
        _s5_kernel,
        grid=(nt, bsz),
        in_specs=[pl.BlockSpec((seq, LANES), lambda x, b: (b, x))] + [strip(a) for a in strips],
        out_specs=pl.BlockSpec((seq, LANES), lambda x, b: (b, x)),
        out_shape=jax.ShapeDtypeStruct((t, S5_WIDTH), F32),
        scratch_shapes=[pltpu.VMEM((_S5_XW, _S5_XW), BF16),
                        pltpu.VMEM((_S5_XW, _S5_SW), BF16), pltpu.VMEM((_S5_XW, _S5_SW), BF16),
                        pltpu.VMEM((_S5_SW, _S5_XW), BF16), pltpu.VMEM((_S5_SW, _S5_XW), BF16)]
                       + [pltpu.VMEM((nck, _S5_SW), F32)] * 4,
        compiler_params=_cparams(("arbitrary", "arbitrary")),
        name="s5_mixer",
    )(u, *strips)


def _swap_rope_halves(y):
    w = y.shape[1]
    lane = lax.broadcasted_iota(jnp.int32, y.shape, 1)
    lower = (lane % HEAD_DIM) < (HEAD_DIM // 2)
    return jnp.where(lower, pltpu.roll(y, w - HEAD_DIM // 2, 1), pltpu.roll(y, HEAD_DIM // 2, 1))


def _norm_rope(x, gain, head_mean, cos2, sin2):
    reps = x.shape[1] // LANES
    ms = _dot_f32_lhs(x * x, head_mean)
    y = x * lax.rsqrt(ms + NORM_EPS) * gain
    if reps > 1:
        cos2 = jnp.concatenate([cos2] * reps, axis=1)
        sin2 = jnp.concatenate([sin2] * reps, axis=1)
    return y * cos2 + _swap_rope_halves(y) * sin2


def _attn_kernel(q_ref, kc_ref, kp_ref, vc_ref, vp_ref, cosc_ref, sinc_ref, cosp_ref, sinp_ref,
                 qg_ref, kg_ref, sink_ref, hmq_ref, hmk_ref, o_ref):
    seq_start = pl.program_id(1) == 0
    blk = ATTN_BLOCK
    nblk = q_ref.shape[0] // blk
    q = _norm_rope(q_ref[...].astype(F32), qg_ref[...], hmq_ref[...], cosc_ref[...], sinc_ref[...])
    q = (q * (HEAD_DIM ** -0.5)).astype(BF16)
    kc = _norm_rope(kc_ref[...].astype(F32), kg_ref[...], hmk_ref[...], cosc_ref[...], sinc_ref[...]).astype(BF16)
    kp = _norm_rope(kp_ref[...].astype(F32), kg_ref[...], hmk_ref[...], cosp_ref[...], sinp_ref[...]).astype(BF16)
    k_all = jnp.concatenate([kp, kc], axis=0)
    v_all = jnp.concatenate([vp_ref[...], vc_ref[...]], axis=0)
    shape = (2 * blk, ATTN_REP * blk)
    kj = lax.broadcasted_iota(jnp.int32, shape, 0)
    qi = lax.broadcasted_iota(jnp.int32, shape, 1) % blk + blk
    band = (kj <= qi) & (qi - kj < blk)
    band_first = band & ((kj >= blk) | jnp.logical_not(seq_start))
    sinks = sink_ref[...]
    for j in range(ATTN_KV_HEADS):
        sl = slice(j * HEAD_DIM, (j + 1) * HEAD_DIM)
        heads = [j * ATTN_REP + r for r in range(ATTN_REP)]
        sink = jnp.concatenate([jnp.broadcast_to(sinks[:, h:h + 1], (1, blk)) for h in heads], axis=1)
        for n in range(nblk):
            rows = slice(n * blk, (n + 1) * blk)
            kb = k_all[n * blk:(n + 2) * blk, sl]
            vb = v_all[n * blk:(n + 2) * blk, sl]
            q4 = jnp.concatenate([q[rows, h * HEAD_DIM:(h + 1) * HEAD_DIM] for h in heads], axis=0)
            s = lax.dot_general(kb, q4, (((1,), (1,)), ((), ())), preferred_element_type=F32)
            s = jnp.where(band_first if n == 0 else band, s, -jnp.inf)
            m = jnp.maximum(jnp.max(s, axis=0, keepdims=True), sink)
            p = jnp.exp(s - m)
            denom = jnp.sum(p, axis=0, keepdims=True) + jnp.exp(sink - m)
            pn = (p * (1.0 / denom)).astype(BF16)
            out = lax.dot_general(pn, vb, (((0,), (0,)), ((), ())), preferred_element_type=F32)
            for r, h in enumerate(heads):
                o_ref[rows, h * HEAD_DIM:(h + 1) * HEAD_DIM] = out[r * blk:(r + 1) * blk].astype(o_ref.dtype)


def _rope_tables(seq):
    half = HEAD_DIM // 2
    inv = jnp.power(ROPE_THETA, -jnp.arange(half, dtype=F32) * 2.0 / HEAD_DIM)
    ang = jnp.arange(seq, dtype=F32)[:, None] * inv[None, :]
    cos, sin = jnp.cos(ang), jnp.sin(ang)
    cos2 = jnp.concatenate([cos, cos, cos, cos], axis=1)
    sin2 = jnp.concatenate([-sin, sin, -sin, sin], axis=1)
    return cos2, sin2


def _head_mean_matrix(width):
    i = jnp.arange(width)
    return jnp.where((i[:, None] // HEAD_DIM) == (i[None, :] // HEAD_DIM), 1.0 / HEAD_DIM, 0.0).astype(BF16)


def _swa_attention(q, k, v, q_g, k_g, sinks, bsz):
    t = q.shape[0]
    seq = t // bsz
    tile = min(ATTN_TILE, seq)
    nt = seq // tile
    bpt = tile // ATTN_BLOCK
    nb = seq // ATTN_BLOCK
    cos2, sin2 = _rope_tables(seq)
    qg = jnp.tile(q_g.astype(F32), ATTN_HEADS)[None]
    kg = jnp.tile(k_g.astype(F32), ATTN_KV_HEADS)[None]
    cur = lambda b, n: (b * nt + n, 0)
    prev = lambda b, n: (b * nb + jnp.maximum(n * bpt - 1, 0), 0)
    tcur = lambda b, n: (n, 0)
    tprev = lambda b, n: (jnp.maximum(n * bpt - 1, 0), 0)
    blk = ATTN_BLOCK
    return pl.pallas_call(
        _attn_kernel,
        grid=(bsz, nt),
        in_specs=[pl.BlockSpec((tile, ATTN_Q), cur),
                  pl.BlockSpec((tile, ATTN_KV), cur), pl.BlockSpec((blk, ATTN_KV), prev),
                  pl.BlockSpec((tile, ATTN_KV), cur), pl.BlockSpec((blk, ATTN_KV), prev),
                  pl.BlockSpec((tile, LANES), tcur), pl.BlockSpec((tile, LANES), tcur),
                  pl.BlockSpec((blk, LANES), tprev), pl.BlockSpec((blk, LANES), tprev),
                  _const_spec((1, ATTN_Q)), _const_spec((1, ATTN_KV)), _const_spec((1, ATTN_HEADS)),
                  _const_spec((ATTN_Q, ATTN_Q)), _const_spec((ATTN_KV, ATTN_KV))],
        out_specs=pl.BlockSpec((tile, ATTN_Q), cur),
        out_shape=jax.ShapeDtypeStruct((t, ATTN_Q), BF16),
        compiler_params=_cparams(("parallel", "parallel")),
        name="swa_attention",
    )(q, k, k, v, v, cos2, sin2, cos2, sin2, qg, kg, sinks.astype(F32)[None],
      _head_mean_matrix(ATTN_Q), _head_mean_matrix(ATTN_KV))


_CONV_TAIL = SSD_CHUNK


def _ssd_kernel(z_ref, xbc_ref, dt_ref, cw_ref, cb_ref, dtb_ref, alog_ref, dskip_ref, ng_ref, exp_ref, shift_ref,
                o_ref, tail, state, ybufs):
    @pl.when(pl.program_id(1) == 0)
    def _():
        tail[0] = jnp.zeros(tail.shape[1:], tail.dtype)
        state[...] = jnp.zeros_like(state)

    first = pl.program_id(1) * SSD_STEP_CHUNKS
    for i in range(SSD_STEP_CHUNKS):
        _ssd_chunk(i, (first + i) % 3, (first + i + 1) % 3, z_ref, xbc_ref, dt_ref, cw_ref, cb_ref, dtb_ref,
                   alog_ref, dskip_ref, ng_ref, exp_ref, shift_ref, o_ref, tail, state, ybufs.at[i])


def _ssd_chunk(i, prev_slot, next_slot, z_ref, xbc_ref, dt_ref, cw_ref, cb_ref, dtb_ref, alog_ref, dskip_ref,
               ng_ref, exp_ref, shift_ref, o_ref, tail, state, ybuf):
    ch = SSD_CHUNK
    rows = slice(i * ch, (i + 1) * ch)
    cur = xbc_ref[rows, :]
    shifted = _dot(shift_ref[...], jnp.concatenate([tail[prev_slot], cur], axis=0))
    tail[next_slot] = cur
    conv = cb_ref[...] + cw_ref[SSD_CONV - 1:SSD_CONV, :] * cur.astype(F32)
    for d in range(1, SSD_CONV):
        conv = conv + cw_ref[SSD_CONV - 1 - d:SSD_CONV - d, :] * shifted[(d - 1) * ch:d * ch]
    act = _silu(conv)
    xs = act[:, :SSD_WIDTH]
    bm = act[:, SSD_WIDTH:SSD_WIDTH + SSD_BC].astype(BF16)
    cm = act[:, SSD_WIDTH + SSD_BC:].astype(BF16)

    lane = lax.broadcasted_iota(jnp.int32, (ch, LANES), 1)
    xdt = dt_ref[rows, :] + dtb_ref[...]
    dt = jnp.maximum(xdt, 0.0) + jnp.log1p(jnp.exp(-jnp.abs(xdt)))
    dt = jnp.where(lane < SSD_HEADS, dt, 0.0)
    a = dt * (-jnp.exp(alog_ref[...]))
    row = lax.broadcasted_iota(jnp.int32, (ch, ch), 0)
    col = lax.broadcasted_iota(jnp.int32, (ch, ch), 1)
    causal = row >= col
    hi, mid, lo = _split3(a)
    tril = causal.astype(BF16)
    cs = _dot(tril, hi) + _dot(tril, mid) + _dot(tril, lo)
    cs_t = cs.T
    expand = exp_ref[...]
    dt_x = _dot_f32_lhs(dt, expand)
    cs_x = _dot_f32_lhs(cs, expand)
    cs_last_x = cs_x[ch - 1:ch, :]
    xdt_full = xs * dt_x
    in_decay = jnp.exp(cs_x)
    out_decay = jnp.exp(cs_last_x - cs_x)
    chunk_decay = jnp.exp(cs_last_x)
    xw = (xdt_full * out_decay).astype(BF16)
    xdt_b = xdt_full.astype(BF16)
    hpg = SSD_HEADS // SSD_GROUPS
    for g in range(SSD_GROUPS):
        gs = slice(g * SSD_STATE, (g + 1) * SSD_STATE)
        ws = slice(g * SSD_GROUP_W, (g + 1) * SSD_GROUP_W)
        bg = bm[:, gs]
        cg = cm[:, gs]
        cb = lax.dot_general(cg, bg, (((1,), (1,)), ((), ())), preferred_element_type=F32)
        prev = state[g]
        ybuf[:, ws] = _dot(cg, prev.astype(BF16)) * in_decay[:, ws]
        for h in range(hpg):
            hh = g * hpg + h
            seg = cs[:, hh:hh + 1] - cs_t[hh:hh + 1, :]
            lmat = jnp.where(causal, jnp.exp(seg), 0.0)
            hs = slice(hh * SSD_HEAD_DIM, (hh + 1) * SSD_HEAD_DIM)
            ybuf[:, hs] += _dot((cb * lmat).astype(BF16), xdt_b[:, hs])
        upd = lax.dot_general(bg, xw[:, ws], (((0,), (0,)), ((), ())), preferred_element_type=F32)
        state[g] = prev * chunk_decay[:, ws] + upd

    y = ybuf[...] + xs * dskip_ref[...]
    y = y * _silu(z_ref[rows, :].astype(F32))
    parts = []
    for g in range(SSD_GROUPS):
        ws = slice(g * SSD_GROUP_W, (g + 1) * SSD_GROUP_W)
        parts.append(_rms_scale(y[:, ws]))
    o_ref[rows, :] = (jnp.concatenate(parts, axis=1) * ng_ref[...]).astype(o_ref.dtype)


def _ssd_mixer(z, xbc, dt_raw, conv_w, conv_b, dt_bias, a_log, d_skip, norm_g, bsz):
    t = z.shape[0]
    seq = t // bsz
    nc = seq // SSD_CHUNK
    pad = lambda v: jnp.pad(v.astype(F32), (0, LANES - SSD_HEADS))[None]
    d_x = jnp.repeat(d_skip.astype(F32), SSD_HEAD_DIM)[None]
    hid = jnp.arange(SSD_WIDTH) // SSD_HEAD_DIM
    expand = (jnp.arange(LANES)[:, None] == hid[None, :]).astype(BF16)
    r = jnp.arange((SSD_CONV - 1) * SSD_CHUNK)
    pick = r % SSD_CHUNK + _CONV_TAIL - (r // SSD_CHUNK + 1)
    shift = (jnp.arange(_CONV_TAIL + SSD_CHUNK)[None, :] == pick[:, None]).astype(BF16)
    assert nc % SSD_STEP_CHUNKS == 0
    ns = nc // SSD_STEP_CHUNKS
    rows = SSD_STEP_CHUNKS * SSD_CHUNK
    blk = lambda b, c: (b * ns + c, 0)
    return pl.pallas_call(
        _ssd_kernel,
        grid=(bsz, ns),
        in_specs=[pl.BlockSpec((rows, SSD_WIDTH), blk),
                  pl.BlockSpec((rows, SSD_CONV_CH), blk),
                  pl.BlockSpec((rows, DT_PAD), blk),
                  _const_spec((SSD_CONV, SSD_CONV_CH)), _const_spec((1, SSD_CONV_CH)),
                  _const_spec((1, LANES)), _const_spec((1, LANES)),
                  _const_spec((1, SSD_WIDTH)), _const_spec((1, SSD_WIDTH)),
                  _const_spec((LANES, SSD_WIDTH)), _const_spec(shift.shape)],
        out_specs=pl.BlockSpec((rows, SSD_WIDTH), blk),
        out_shape=jax.ShapeDtypeStruct((t, SSD_WIDTH), BF16),
        scratch_shapes=[pltpu.VMEM((3, _CONV_TAIL, SSD_CONV_CH), BF16),
                        pltpu.VMEM((SSD_GROUPS, SSD_STATE, SSD_GROUP_W), F32),
                        pltpu.VMEM((SSD_STEP_CHUNKS, SSD_CHUNK, SSD_WIDTH), F32)],
        compiler_params=_cparams(("arbitrary", "arbitrary")),
        name="ssd_mixer",
    )(z, xbc, dt_raw, conv_w.astype(F32), conv_b.astype(F32)[None], pad(dt_bias), pad(a_log),
      d_x, norm_g.astype(F32)[None], expand, shift)


def _route(logits):
    lane = lax.broadcasted_iota(jnp.int32, logits.shape, 1).astype(F32)
    big = float(ROUTE_PAD)
    is_g = lane < MOE_GROUPS
    gl = jnp.where(is_g, logits, -jnp.inf)
    gmax = jnp.max(gl, axis=1, keepdims=True)
    gsel = jnp.min(jnp.where(gl == gmax, lane, big), axis=1, keepdims=True)
    pg = 1.0 / jnp.sum(jnp.where(is_g, jnp.exp(logits - gmax), 0.0), axis=1, keepdims=True)
    lo = MOE_GROUPS + MOE_PER_GROUP * gsel
    ev = jnp.where((lane >= lo) & (lane < lo + MOE_PER_GROUP), logits, -jnp.inf)
    m1 = jnp.max(ev, axis=1, keepdims=True)
    i1 = jnp.min(jnp.where(ev == m1, lane, big), axis=1, keepdims=True)
    ev2 = jnp.where(lane == i1, -jnp.inf, ev)
    m2 = jnp.max(ev2, axis=1, keepdims=True)
    i2 = jnp.min(jnp.where(ev2 == m2, lane, big), axis=1, keepdims=True)
    e21 = jnp.exp(m2 - m1)
    w1 = pg / (1.0 + e21)
    w2 = pg * e21 / (1.0 + e21)
    out = jnp.where(lane == 0, i1 - MOE_GROUPS, 0.0)
    out = jnp.where(lane == 1, i2 - MOE_GROUPS, out)
    out = jnp.where(lane == 2, w1, out)
    return jnp.where(lane == 3, w2, out)


def _merge_kernel(*refs, n_x):
    x_refs = refs[:n_x]
    (ys5_ref, u_ref, yb_ref, yc_ref, n1g_ref, wg_ref, bg_ref, d_ref, w1_ref, w2_ref, ps5_ref, pat_ref,
     pssd_ref, wo_ref, n2g_ref, wrh_ref, wrm_ref, br_ref, x1_o, h2_o, route_o) = refs[n_x:]
    x = _x_sum(x_refs)
    hb = (_rms_scale(x) * n1g_ref[...]).astype(BF16)
    ya = ys5_ref[...].astype(F32) + d_ref[...] * u_ref[...].astype(F32)
    yab = jax.nn.gelu(ya).astype(BF16)
    ya = _dot(yab, w1_ref[...]) * jax.nn.sigmoid(_dot(yab, w2_ref[...]))
    branches = ((ya.astype(BF16), ps5_ref), (yb_ref[...], pat_ref), (yc_ref[...], pssd_ref))
    merged = None
    for b, (yv, p_ref) in enumerate(branches):
        gate = jax.nn.sigmoid(_dot(hb, wg_ref[:, b * D_MODEL:(b + 1) * D_MODEL]) + bg_ref[b:b + 1, :])
        term = gate * _dot(yv, p_ref[...])
        merged = term if merged is None else merged + term
    x1 = x + _dot(merged.astype(BF16), wo_ref[...])
    x1_o[...] = x1
    h2 = _rms_scale(x1) * n2g_ref[...]
    _store_row_tiles(h2_o, 0, h2)
    h_hi = h2.astype(BF16)
    h_mid = (h2 - h_hi.astype(F32)).astype(BF16)
    logits = _dot(h_hi, wrh_ref[...]) + _dot(h_hi, wrm_ref[...]) + _dot(h_mid, wrh_ref[...]) + br_ref[...]
    route_o[...] = _route(logits)


def _merge(xs, t, ys5, u, yb, yc, lw):
    tm = min(TM_PROJ, t)
    row = lambda w: pl.BlockSpec((tm, w), lambda i: (i, 0))
    consts = [lw['n1g'], lw['w_gate'], lw['b_gate'], lw['s5_d'], lw['glu_w1'], lw['glu_w2'], lw['p_s5'],
              lw['p_attn'], lw['p_ssd'], lw['w_out'], lw['n2g'], lw['w_router_hi'], lw['w_router_mid'],
              lw['b_router']]
    x_arrs, x_specs = _x_parts(xs, tm)
    return pl.pallas_call(
        functools.partial(_merge_kernel, n_x=len(x_arrs)),
        grid=(t // tm,),
        in_specs=x_specs + [row(S5_WIDTH), row(S5_WIDTH), row(ATTN_Q), row(SSD_WIDTH)]
                 + [_const_spec(c.shape) for c in consts],
        out_specs=[row(D_MODEL), pl.BlockSpec((tm * ROW_TILE, LANES), lambda i: (i, 0)), row(ROUTE_PAD)],
        out_shape=[jax.ShapeDtypeStruct((t, D_MODEL), F32), jax.ShapeDtypeStruct((t * ROW_TILE, LANES), F32),
                   jax.ShapeDtypeStruct((t, ROUTE_PAD), F32)],
        compiler_params=_cparams(("parallel",)),
        name="merge_router",
    )(*x_arrs, ys5, u, yb, yc, *consts)


def _moe_plan(route, t):
    n_exp = MOE_EXPERTS
    e = route[:, 0:2].astype(jnp.int32).reshape(-1)
    onehot = (e[:, None] == jnp.arange(n_exp, dtype=jnp.int32)[None, :]).astype(jnp.int32)
    csum = jnp.cumsum(onehot, axis=0)
    rank = jnp.sum(csum * onehot, axis=1) - 1
    counts = csum[-1]
    padded = ((counts + TM_G - 1) // TM_G) * TM_G
    pend = jnp.cumsum(padded)
    dest = (pend - padded)[e] + rank
    n_tiles = -(-(2 * t + n_exp * (TM_G - 1)) // TM_G)
    n_rows = n_tiles * TM_G
    pair = jnp.full((n_rows,), -1, jnp.int32).at[dest].set(jnp.arange(2 * t, dtype=jnp.int32),
                                                            unique_indices=True)
    valid = pair >= 0
    tok, k = pair // 2, pair % 2
    src = jnp.where(valid, tok, 0)
    spare = t + jnp.arange(n_rows, dtype=jnp.int32) % _MOE_SPARE
    dst = jnp.where(valid, k * (t + _MOE_SPARE) + tok, spare)
    tile_start = jnp.arange(n_tiles, dtype=jnp.int32) * TM_G
    tile_e = jnp.sum((pend[None, :] <= tile_start[:, None]).astype(jnp.int32), axis=1)
    n_used = pend[-1:] // TM_G
    tile_e = jnp.minimum(tile_e, n_exp - 1)
    tile_e = jnp.minimum(tile_e, tile_e[n_used[0] - 1])
    return src.reshape(n_tiles, 1, TM_G), dst.reshape(n_tiles, 1, TM_G), tile_e, n_used.astype(jnp.int32)


_MOE_SPARE = 2 * TM_G


def _gmm_kernel(te_ref, nu_ref, src_ref, srcn_ref, dst_ref, h2_hbm, wg_ref, wu_ref, wd_ref, out_hbm,
                xbuf, ybuf, gsem, ssem):
    del te_ref
    i = pl.program_id(0)
    last = nu_ref[0] - 1
    slot = i % 2

    tile_rows = TM_G * ROW_TILE

    def row_tile(ref, s, r):
        return ref.at[pl.ds((s * TM_G + r) * ROW_TILE, ROW_TILE), :]

    def slot_rows(ref, s):
        return ref.at[pl.ds(s * tile_rows, tile_rows), :]

    def gather_start(idx_ref, s):
        for r in range(TM_G):
            src = h2_hbm.at[pl.ds(pl.multiple_of(idx_ref[0, 0, r], ROW_TILE), ROW_TILE), :]
            pltpu.make_async_copy(src, row_tile(xbuf, s, r), gsem.at[s]).start(priority=r % 2)

    def gather_wait(s):
        pltpu.make_async_copy(h2_hbm.at[pl.ds(0, tile_rows), :], slot_rows(xbuf, s), gsem.at[s]).wait()

    def scatter_wait(s):
        pltpu.make_async_copy(slot_rows(ybuf, s), out_hbm.at[pl.ds(0, tile_rows), :], ssem.at[s]).wait()

    @pl.when(i == 0)
    def _():
        gather_start(src_ref, 0)
        ybuf[...] = jnp.zeros_like(ybuf)
        plane = out_hbm.shape[0] // 2
        spare0 = plane - _MOE_SPARE * ROW_TILE
        fills = [pltpu.make_async_copy(slot_rows(ybuf, s),
                                       out_hbm.at[pl.ds(k * plane + spare0 + s * tile_rows, tile_rows), :],
                                       ssem.at[s]) for k in range(2) for s in range(2)]
        for f in fills:
            f.start()
        for f in fills:
            f.wait()

    @pl.when(i <= last)
    def _():
        gather_wait(slot)

        @pl.when(i >= 2)
        def _():
            scatter_wait(slot)

        gather_start(srcn_ref, 1 - slot)

        xb = _load_row_tiles(xbuf, slot * TM_G, TM_G).astype(BF16)
        hid = _silu(_dot(xb, wg_ref[0, 0].astype(BF16))) * _dot(xb, wu_ref[0, 0].astype(BF16))
        _store_row_tiles(ybuf, slot * TM_G, _dot(hid.astype(BF16), wd_ref[0, 0].astype(BF16)))
        for r in range(TM_G):
            dst = out_hbm.at[pl.ds(pl.multiple_of(dst_ref[0, 0, r], ROW_TILE), ROW_TILE), :]
            pltpu.make_async_copy(row_tile(ybuf, slot, r), dst, ssem.at[slot]).start(priority=r % 2)

        @pl.when(i == last)
        def _():
            scatter_wait(slot)
            gather_wait(1 - slot)

        @pl.when((i == last) & (i >= 1))
        def _():
            scatter_wait(1 - slot)


def _moe_sparse(h2, route, w_gate, w_up, w_down, l):
    t = h2.shape[0] // ROW_TILE
    src, dst, tile_e, n_used = _moe_plan(route, t)
    src, dst = src * ROW_TILE, dst * ROW_TILE
    n_tiles = src.shape[0]
    smem = lambda imap: pl.BlockSpec((1, 1, TM_G), imap, memory_space=pltpu.SMEM)
    grid_spec = pltpu.PrefetchScalarGridSpec(
        num_scalar_prefetch=2,
        grid=(n_tiles,),
        in_specs=[smem(lambda i, te, nu: (i, 0, 0)),
                  smem(lambda i, te, nu: (jnp.minimum(i + 1, n_tiles - 1), 0, 0)),
                  smem(lambda i, te, nu: (i, 0, 0)),
                  pl.BlockSpec(memory_space=pl.ANY),
                  pl.BlockSpec((1, 1, D_MODEL, MOE_FF), lambda i, te, nu: (l, te[i], 0, 0)),
                  pl.BlockSpec((1, 1, D_MODEL, MOE_FF), lambda i, te, nu: (l, te[i], 0, 0)),
                  pl.BlockSpec((1, 1, MOE_FF, D_MODEL), lambda i, te, nu: (l, te[i], 0, 0))],
        out_specs=pl.BlockSpec(memory_space=pl.ANY),
        scratch_shapes=[pltpu.VMEM((2 * TM_G * ROW_TILE, LANES), F32), pltpu.VMEM((2 * TM_G * ROW_TILE, LANES), F32),
                        pltpu.SemaphoreType.DMA((2,)), pltpu.SemaphoreType.DMA((2,))])
    plane = (t + _MOE_SPARE) * ROW_TILE
    out = pl.pallas_call(
        _gmm_kernel,
        grid_spec=grid_spec,
        out_shape=jax.ShapeDtypeStruct((2 * plane, LANES), F32),
        compiler_params=_cparams(("arbitrary",)),
        name="moe_gmm",
    )(tile_e, n_used, src, src, dst, h2, w_gate, w_up, w_down)
    return out.reshape(2, plane, LANES)


def _sum_kernel(*refs):
    refs[-1][...] = _x_sum(refs[:-1])


def _residual_sum(xs, t):
    tm = min(TM_PROJ, t)
    x_arrs, x_specs = _x_parts(xs, tm)
    return pl.pallas_call(
        _sum_kernel,
        grid=(t // tm,),
        in_specs=x_specs,
        out_specs=pl.BlockSpec((tm, D_MODEL), lambda i: (i, 0)),
        out_shape=jax.ShapeDtypeStruct((t, D_MODEL), F32),
        compiler_params=_cparams(("parallel",)),
        name="residual_sum",
    )(*x_arrs)


def _cast_kernel(*refs):
    n = len(refs) // 2
    for x_ref, o_ref in zip(refs[:n], refs[n:]):
        o_ref[...] = x_ref[...].astype(o_ref.dtype)


def _cast_layer_bf16(ws, l):
    halves = 2
    in_specs = [pl.BlockSpec((None, w.shape[1] // halves, w.shape[2]), lambda i: (l, i, 0)) for w in ws]
    out_specs = [pl.BlockSpec((w.shape[1] // halves, w.shape[2]), lambda i: (i, 0)) for w in ws]
    return pl.pallas_call(
        _cast_kernel,
        grid=(halves,),
        in_specs=in_specs,
        out_specs=out_specs,
        out_shape=[jax.ShapeDtypeStruct(w.shape[1:], BF16) for w in ws],
        compiler_params=_cparams(("parallel",)),
        name="cast_weights",
    )(*[w.astype(F32) for w in ws])


_GATE_SHIFT = N_MIX % LANES
assert _GATE_SHIFT % SUBLANES == 0


def _gate_w_kernel(a_ref, b_ref, o_ref):
    full = jnp.concatenate([a_ref[...], b_ref[...]], axis=0)
    o_ref[...] = full[_GATE_SHIFT:_GATE_SHIFT + LANES, :].T.astype(o_ref.dtype)


def _gate_weights(w_in_t, l):
    base = (N_MIX - _GATE_SHIFT) // LANES
    return pl.pallas_call(
        _gate_w_kernel,
        grid=(N_BRANCH * D_MODEL // LANES,),
        in_specs=[pl.BlockSpec((None, LANES, D_MODEL), lambda k: (l, base + k, 0)),
                  pl.BlockSpec((None, LANES, D_MODEL), lambda k: (l, base + k + 1, 0))],
        out_specs=pl.BlockSpec((D_MODEL, LANES), lambda k: (0, k)),
        out_shape=jax.ShapeDtypeStruct((D_MODEL, N_BRANCH * D_MODEL), BF16),
        compiler_params=_cparams(("parallel",)),
        name="gate_weights",
    )(w_in_t, w_in_t)


def _layer_weights(l, p):
    glu_w1, glu_w2, p_s5, p_attn, p_ssd, w_out = _cast_layer_bf16(
        [p['s5_glu_w1'], p['s5_glu_w2'], p['p_s5'], p['p_attn'], p['p_ssd'], p['w_out']], l)
    w_router = jnp.concatenate([p['w_router_group'][l], p['w_router_expert'][l]], axis=1).astype(F32)
    npad = ROUTE_PAD - w_router.shape[1]
    b_router = jnp.concatenate([p['b_router_group'][l], p['b_router_expert'][l]]).astype(F32)
    w_router = jnp.pad(w_router, ((0, 0), (0, npad)))
    w_router_hi = w_router.astype(BF16)
    return dict(
        n1g=p['norm1_g'][l].astype(F32)[None],
        w_gate=_gate_weights(p['w_in_t'], l), b_gate=p['b_gate'][l].astype(F32),
        s5_d=p['s5_d'][l].astype(F32)[None],
        glu_w1=glu_w1, glu_w2=glu_w2, p_s5=p_s5, p_attn=p_attn, p_ssd=p_ssd, w_out=w_out,
        n2g=p['norm2_g'][l].astype(F32)[None],
        w_router_hi=w_router_hi, w_router_mid=(w_router - w_router_hi.astype(F32)).astype(BF16),
        b_router=jnp.pad(b_router, (0, npad))[None])


def _layer(xs, t, l, p, s5_tabs, bsz):
    lw = _layer_weights(l, p)
    u, q, k, v, z, xbc, dt_raw, *x_sum = _inproj(xs, t, lw['n1g'], p['w_in_t'], l)
    if x_sum:
        xs = x_sum
    ys5 = _s5_mixer(u, s5_tabs, bsz)
    yb = _swa_attention(q, k, v, p['q_norm_g'][l], p['k_norm_g'][l], p['attn_sinks'][l], bsz)
    yc = _ssd_mixer(z, xbc, dt_raw, p['ssd_conv_w'][l], p['ssd_conv_b'][l], p['ssd_dt_bias'][l],
                    p['ssd_a_log'][l], p['ssd_d'][l], p['ssd_norm_g'][l], bsz)
    x1, h2, route = _merge(xs, t, ys5, u, yb, yc, lw)
    moe = _moe_sparse(h2, route, p['w_exp_gate'], p['w_exp_up'], p['w_exp_down'], l)
    return [x1, route, moe]


def kernel(x, norm1_g, w_in, b_gate, s5_lambda_re, s5_lambda_im, s5_b_re, s5_b_im, s5_c_re, s5_c_im, s5_d, s5_log_dt, s5_glu_w1, s5_glu_w2, q_norm_g, k_norm_g, attn_sinks, ssd_conv_w, ssd_conv_b, ssd_dt_bias, ssd_a_log, ssd_d, ssd_norm_g, p_s5, p_attn, p_ssd, w_out, norm2_g, w_router_group, b_router_group, w_router_expert, b_router_expert, w_exp_gate, w_exp_up, w_exp_down):
    p = dict(norm1_g=norm1_g, w_in=w_in, b_gate=b_gate, s5_lambda_re=s5_lambda_re, s5_lambda_im=s5_lambda_im,
             s5_b_re=s5_b_re, s5_b_im=s5_b_im, s5_c_re=s5_c_re, s5_c_im=s5_c_im, s5_d=s5_d,
             s5_log_dt=s5_log_dt, s5_glu_w1=s5_glu_w1, s5_glu_w2=s5_glu_w2, q_norm_g=q_norm_g,
             k_norm_g=k_norm_g, attn_sinks=attn_sinks, ssd_conv_w=ssd_conv_w, ssd_conv_b=ssd_conv_b,
             ssd_dt_bias=ssd_dt_bias, ssd_a_log=ssd_a_log, ssd_d=ssd_d, ssd_norm_g=ssd_norm_g, p_s5=p_s5,
             p_attn=p_attn, p_ssd=p_ssd, w_out=w_out, norm2_g=norm2_g, w_router_group=w_router_group,
             b_router_group=b_router_group, w_router_expert=w_router_expert, b_router_expert=b_router_expert,
             w_exp_gate=w_exp_gate, w_exp_up=w_exp_up, w_exp_down=w_exp_down)
    p['w_in_t'] = jnp.swapaxes(w_in.astype(F32), 1, 2)
    bsz, seq, dm = x.shape
    depth = w_in.shape[0]
    t = bsz * seq
    xs = [x.reshape(t, dm)]
    for l in range(depth):
        s5_tabs = _s5_tables(s5_lambda_re[l], s5_lambda_im[l], s5_b_re[l], s5_b_im[l], s5_c_re[l], s5_c_im[l],
                             s5_log_dt[l])
        xs = _layer(xs, t, l, p, s5_tabs, bsz)
    return _residual_sum(xs, t).reshape(bsz, seq, dm)
```

```python
import functools
import math

import jax
import jax.numpy as jnp
from jax import lax
from jax.experimental import pallas as pl
from jax.experimental.pallas import tpu as pltpu

F32 = jnp.float32
BF16 = jnp.bfloat16

D_MODEL = 1024
NORM_EPS = 1e-6
S5_WIDTH = 512
S5_GROUP = 16
S5_GROUPS = 32
S5_STATE = 64
HEAD_DIM = 64
ATTN_HEADS = 8
ATTN_KV_HEADS = 2
ATTN_REP = ATTN_HEADS // ATTN_KV_HEADS
ATTN_Q = ATTN_HEADS * HEAD_DIM
ATTN_KV = ATTN_KV_HEADS * HEAD_DIM
ATTN_BLOCK = 128
ROPE_THETA = 10000.0
SSD_WIDTH = 1024
SSD_HEAD_DIM = 64
SSD_HEADS = 16
SSD_GROUPS = 2
SSD_STATE = 64
SSD_CONV = 4
SSD_CHUNK = 128
SSD_BC = SSD_GROUPS * SSD_STATE
SSD_CONV_CH = SSD_WIDTH + 2 * SSD_BC
SSD_GROUP_W = SSD_WIDTH // SSD_GROUPS
N_BRANCH = 3
MOE_GROUPS = 4
MOE_PER_GROUP = 8
MOE_EXPERTS = 32
MOE_FF = 512
N_MIX = S5_WIDTH + ATTN_Q + 2 * ATTN_KV + SSD_WIDTH + SSD_CONV_CH + SSD_HEADS

LANES = 128
SUBLANES = 8
VMEM_LIMIT_BYTES = 56 * 1024 * 1024

S5_CHUNK = 16
S5_TG = 8
ATTN_TILE = 512
TM_PROJ = 512
TM_G = 256
SSD_STEP_CHUNKS = 2
DT_PAD = LANES
N_MIX_PAD = N_MIX - SSD_HEADS + DT_PAD
ROUTE_PAD = LANES


def _cparams(semantics):
    return pltpu.CompilerParams(dimension_semantics=semantics, vmem_limit_bytes=VMEM_LIMIT_BYTES)


def _const_spec(shape):
    zeros = (0,) * len(shape)
    return pl.BlockSpec(shape, lambda *_: zeros, pipeline_mode=pl.Buffered(1))


def _dot(a, b):
    return jnp.dot(a, b, preferred_element_type=F32)


def _split3(a):
    hi = a.astype(BF16)
    r1 = a - hi.astype(F32)
    mid = r1.astype(BF16)
    lo = (r1 - mid.astype(F32)).astype(BF16)
    return hi, mid, lo


def _dot_f32_lhs(a, b_bf16):
    hi, mid, lo = _split3(a)
    return _dot(hi, b_bf16) + _dot(mid, b_bf16) + _dot(lo, b_bf16)


def _rms_scale(x):
    return x * lax.rsqrt(jnp.mean(x * x, axis=-1, keepdims=True) + NORM_EPS)


def _silu(x):
    return x * jax.nn.sigmoid(x)


_OFF_U = 0
_OFF_Q = _OFF_U + S5_WIDTH
_OFF_K = _OFF_Q + ATTN_Q
_OFF_V = _OFF_K + ATTN_KV
_OFF_Z = _OFF_V + ATTN_KV
_OFF_XBC = _OFF_Z + SSD_WIDTH
_OFF_DT = _OFF_XBC + SSD_CONV_CH


ROW_TILE = D_MODEL // LANES


def _store_row_tiles(ref, start, val):
    n = val.shape[0]
    for j in range(ROW_TILE):
        ref[pl.ds(start * ROW_TILE + j, n, stride=ROW_TILE), :] = val[:, j * LANES:(j + 1) * LANES]


def _load_row_tiles(ref, start, n):
    return jnp.concatenate([ref[pl.ds(start * ROW_TILE + j, n, stride=ROW_TILE), :] for j in range(ROW_TILE)],
                           axis=1)


def _x_parts(xs, tm):
    row = lambda w: pl.BlockSpec((tm, w), lambda i: (i, 0))
    if len(xs) == 1:
        return list(xs), [row(D_MODEL)]
    x1, route, moe = xs
    plane = lambda k: pl.BlockSpec((None, tm * ROW_TILE, LANES), lambda i: (k, i, 0))
    return [x1, route, moe, moe], [row(D_MODEL), row(ROUTE_PAD), plane(0), plane(1)]


def _x_sum(x_refs):
    if len(x_refs) == 1:
        return x_refs[0][...]
    x1_ref, route_ref, m0_ref, m1_ref = x_refs
    r = route_ref[...]
    n = x1_ref.shape[0]
    return x1_ref[...] + r[:, 2:3] * _load_row_tiles(m0_ref, 0, n) + r[:, 3:4] * _load_row_tiles(m1_ref, 0, n)


def _inproj_kernel(*refs, n_x):
    x_refs, (g_ref, w_ref), outs, wb = refs[:n_x], refs[n_x:n_x + 2], refs[n_x + 2:-1], refs[-1]

    @pl.when(pl.program_id(0) == 0)
    def _():
        for j in range(N_MIX_PAD // LANES):
            wb[:, j * LANES:(j + 1) * LANES] = w_ref[j * LANES:(j + 1) * LANES, :].T.astype(BF16)

    x = _x_sum(x_refs)
    hb = (_rms_scale(x) * g_ref[...]).astype(BF16)
    offs = (_OFF_U, _OFF_Q, _OFF_K, _OFF_V, _OFF_Z, _OFF_XBC, _OFF_DT)
    for o_ref, off in zip(outs, offs):
        width = o_ref.shape[1]
        o_ref[...] = _dot(hb, wb[:, off:off + width]).astype(o_ref.dtype)
    if len(outs) > len(offs):
        outs[-1][...] = x


def _inproj(xs, t, g, w_in, l):
    tm = min(TM_PROJ, t)
    widths = (S5_WIDTH, ATTN_Q, ATTN_KV, ATTN_KV, SSD_WIDTH, SSD_CONV_CH, DT_PAD)
    dtypes = (F32, BF16, BF16, BF16, BF16, BF16, F32)
    if len(xs) > 1:
        widths, dtypes = widths + (D_MODEL,), dtypes + (F32,)
    x_arrs, x_specs = _x_parts(xs, tm)
    w_spec = pl.BlockSpec((None, N_MIX_PAD, D_MODEL), lambda i: (l, 0, 0), pipeline_mode=pl.Buffered(1))
    return pl.pallas_call(
        functools.partial(_inproj_kernel, n_x=len(x_arrs)),
        grid=(t // tm,),
        in_specs=x_specs + [_const_spec((1, D_MODEL)), w_spec],
        out_specs=[pl.BlockSpec((tm, w), lambda i: (i, 0)) for w in widths],
        out_shape=[jax.ShapeDtypeStruct((t, w), d) for w, d in zip(widths, dtypes)],
        scratch_shapes=[pltpu.VMEM((D_MODEL, N_MIX_PAD), BF16)],
        compiler_params=_cparams(("arbitrary",)),
        name="inproj",
    )(*x_arrs, g, w_in)


def _s5_tables(lam_re, lam_im, b_re, b_im, c_re, c_im, log_dt):
    hp = lax.Precision.HIGHEST
    g_n, p_n = lam_re.shape
    c_n = b_re.shape[-1]
    t_n = S5_CHUNK
    lr, li = lam_re.astype(F32), lam_im.astype(F32)
    dt = jnp.exp(log_dt.astype(F32))[:, None]
    mag = jnp.exp(lr * dt)
    ab_re = mag * jnp.cos(li * dt)
    ab_im = mag * jnp.sin(li * dt)
    nr = ab_re - 1.0
    den = lr * lr + li * li
    f_re = (nr * lr + ab_im * li) / den
    f_im = (ab_im * lr - nr * li) / den
    br, bi = b_re.astype(F32), b_im.astype(F32)
    bb_re = f_re[..., None] * br - f_im[..., None] * bi
    bb_im = f_re[..., None] * bi + f_im[..., None] * br
    j = jnp.arange(t_n + 1, dtype=F32)[:, None, None]
    pmag = jnp.exp(lr * dt * j)
    ang = li * dt * j
    p_re = pmag * jnp.cos(ang)
    p_im = pmag * jnp.sin(ang)
    cr, ci = c_re.astype(F32), c_im.astype(F32)
    ca_re = cr[None] * p_re[:, :, None, :] - ci[None] * p_im[:, :, None, :]
    ca_im = cr[None] * p_im[:, :, None, :] + ci[None] * p_re[:, :, None, :]
    q_re = p_re[t_n - 1 - jnp.arange(t_n)]
    q_im = p_im[t_n - 1 - jnp.arange(t_n)]
    bs_re = q_re[..., None] * bb_re[None] - q_im[..., None] * bb_im[None]
    bs_im = q_re[..., None] * bb_im[None] + q_im[..., None] * bb_re[None]

    nt = g_n // S5_TG

    def in_strip(a):
        a = a.reshape(t_n, nt, S5_TG, p_n, c_n).transpose(1, 0, 4, 2, 3)
        return a.reshape(nt, t_n, c_n, S5_TG * p_n)

    def out_strip(a):
        a = a.reshape(t_n + 1, nt, S5_TG, c_n, p_n).transpose(1, 4, 0, 2, 3)
        return a.reshape(nt, p_n, (t_n + 1) * S5_TG * c_n)

    return dict(
        bs_re=in_strip(bs_re), bs_im=in_strip(bs_im), co_re=out_strip(ca_re), co_im=out_strip(-ca_im),
        at_re=p_re[t_n].reshape(nt, 1, S5_TG * p_n), at_im=p_im[t_n].reshape(nt, 1, S5_TG * p_n))


_S5_XW = S5_CHUNK * LANES
_S5_SW = S5_TG * S5_STATE


def _group_block(strip, lanes_per_group):
    rpg = strip.shape[0]
    full = jnp.concatenate([strip] * S5_TG, axis=0)
    row_g = lax.broadcasted_iota(jnp.int32, full.shape, 0) // rpg
    lane_g = (lax.broadcasted_iota(jnp.int32, full.shape, 1) % (S5_TG * lanes_per_group)) // lanes_per_group
    return jnp.where(row_g == lane_g, full, 0.0)


def _dot_hi_mid(a, b):
    a_hi, a_mid, _ = _split3(a)
    b_hi, b_mid, _ = _split3(b)
    return _dot(a_hi, b_hi) + _dot(a_hi, b_mid) + _dot(a_mid, b_hi)


def _s5_build_tables(bre_ref, bim_ref, cre_ref, cim_ref, big, bsre, bsim, core, coim):
    lag = lambda ref, j: _group_block(ref[0, :, j * LANES:(j + 1) * LANES], S5_GROUP)
    b0_re = _group_block(bre_ref[0, S5_CHUNK - 1], S5_STATE)
    b0_im = _group_block(bim_ref[0, S5_CHUNK - 1], S5_STATE)
    blocks = [(_dot_hi_mid(b0_re, lag(cre_ref, j)) + _dot_hi_mid(b0_im, lag(cim_ref, j))).astype(BF16)
              for j in range(S5_CHUNK)]
    zero = jnp.zeros((LANES, LANES), BF16)
    for s in range(S5_CHUNK):
        for t in range(S5_CHUNK):
            big[s * LANES:(s + 1) * LANES, t * LANES:(t + 1) * LANES] = blocks[t - s] if t >= s else zero
    for src, dst in ((bre_ref, bsre), (bim_ref, bsim)):
        for s in range(S5_CHUNK):
            dst[s * LANES:(s + 1) * LANES, :] = _group_block(src[0, s], S5_STATE).astype(BF16)
    for src, dst in ((cre_ref, core), (cim_ref, coim)):
        for t in range(S5_CHUNK):
            dst[:, t * LANES:(t + 1) * LANES] = lag(src, t + 1).astype(BF16)


def _s5_kernel(u_ref, bre_ref, bim_ref, cre_ref, cim_ref, are_ref, aim_ref, y_ref,
               big, bsre, bsim, core, coim, sre, sim, hre, him):
    @pl.when(pl.program_id(1) == 0)
    def _():
        _s5_build_tables(bre_ref, bim_ref, cre_ref, cim_ref, big, bsre, bsim, core, coim)

    nck = u_ref.shape[0] // S5_CHUNK
    xcat = jnp.concatenate([u_ref[pl.ds(s, nck, stride=S5_CHUNK), :].astype(BF16) for s in range(S5_CHUNK)],
                           axis=1)
    sre[...] = _dot(xcat, bsre[...])
    sim[...] = _dot(xcat, bsim[...])
    a_re = are_ref[0]
    a_im = aim_ref[0]

    def step(i, carry):
        h_re, h_im = carry
        base = pl.multiple_of(i * SUBLANES, SUBLANES)
        s_re = sre[pl.ds(base, SUBLANES), :]
        s_im = sim[pl.ds(base, SUBLANES), :]
        ent_re, ent_im = [], []
        for j in range(SUBLANES):
            ent_re.append(h_re)
            ent_im.append(h_im)
            h_re, h_im = (a_re * h_re - a_im * h_im + s_re[j:j + 1], a_re * h_im + a_im * h_re + s_im[j:j + 1])
        hre[pl.ds(base, SUBLANES), :] = jnp.concatenate(ent_re, axis=0)
        him[pl.ds(base, SUBLANES), :] = jnp.concatenate(ent_im, axis=0)
        return h_re, h_im

    zero = jnp.zeros((1, _S5_SW), F32)
    lax.fori_loop(0, nck // SUBLANES, step, (zero, zero))
    y = (_dot(xcat, big[...]) + _dot(hre[...].astype(BF16), core[...])
         + _dot(him[...].astype(BF16), coim[...]))
    for t in range(S5_CHUNK):
        y_ref[pl.ds(t, nck, stride=S5_CHUNK), :] = y[:, t * LANES:(t + 1) * LANES]


def _s5_mixer(u, tabs, bsz):
    t = u.shape[0]
    seq = t // bsz
    nck = seq // S5_CHUNK
    nt = S5_GROUPS // S5_TG
    strip = lambda a: pl.BlockSpec((1,) + a.shape[1:], lambda x, b: (x,) + (0,) * (a.ndim - 1))
    strips = [tabs['bs_re'], tabs['bs_im'], tabs['co_re'], tabs['co_im'], tabs['at_re'], tabs['at_im']]
    return pl.pallas_call(
        _s5_kernel,
        grid=(nt, bsz),
        in_specs=[pl.BlockSpec((seq, LANES), lambda x, b: (b, x))] + [strip(a) for a in strips],
        out_specs=pl.BlockSpec((seq, LANES), lambda x, b: (b, x)),
        out_shape=jax.ShapeDtypeStruct((t, S5_WIDTH), F32),
        scratch_shapes=[pltpu.VMEM((_S5_XW, _S5_XW), BF16),
                        pltpu.VMEM((_S5_XW, _S5_SW), BF16), pltpu.VMEM((_S5_XW, _S5_SW), BF16),
                        pltpu.VMEM((_S5_SW, _S5_XW), BF16), pltpu.VMEM((_S5_SW, _S5_XW), BF16)]
                       + [pltpu.VMEM((nck, _S5_SW), F32)] * 4,
        compiler_params=_cparams(("arbitrary", "arbitrary")),
        name="s5_mixer",
    )(u, *strips)


def _swap_rope_halves(y):
    w = y.shape[1]
    lane = lax.broadcasted_iota(jnp.int32, y.shape, 1)
    lower = (lane % HEAD_DIM) < (HEAD_DIM // 2)
    return jnp.where(lower, pltpu.roll(y, w - HEAD_DIM // 2, 1), pltpu.roll(y, HEAD_DIM // 2, 1))


def _norm_rope(x, gain, head_mean, cos2, sin2):
    reps = x.shape[1] // LANES
    ms = _dot_f32_lhs(x * x, head_mean)
    y = x * lax.rsqrt(ms + NORM_EPS) * gain
    if reps > 1:
        cos2 = jnp.concatenate([cos2] * reps, axis=1)
        sin2 = jnp.concatenate([sin2] * reps, axis=1)
    return y * cos2 + _swap_rope_halves(y) * sin2


def _attn_kernel(q_ref, kc_ref, kp_ref, vc_ref, vp_ref, cosc_ref, sinc_ref, cosp_ref, sinp_ref,
                 qg_ref, kg_ref, sink_ref, hmq_ref, hmk_ref, o_ref):
    seq_start = pl.program_id(1) == 0
    blk = ATTN_BLOCK
    nblk = q_ref.shape[0] // blk
    q = _norm_rope(q_ref[...].astype(F32), qg_ref[...], hmq_ref[...], cosc_ref[...], sinc_ref[...])
    q = (q * (HEAD_DIM ** -0.5)).astype(BF16)
    kc = _norm_rope(kc_ref[...].astype(F32), kg_ref[...], hmk_ref[...], cosc_ref[...], sinc_ref[...]).astype(BF16)
    kp = _norm_rope(kp_ref[...].astype(F32), kg_ref[...], hmk_ref[...], cosp_ref[...], sinp_ref[...]).astype(BF16)
    k_all = jnp.concatenate([kp, kc], axis=0)
    v_all = jnp.concatenate([vp_ref[...], vc_ref[...]], axis=0)
    shape = (2 * blk, ATTN_REP * blk)
    kj = lax.broadcasted_iota(jnp.int32, shape, 0)
    qi = lax.broadcasted_iota(jnp.int32, shape, 1) % blk + blk
    band = (kj <= qi) & (qi - kj < blk)
    band_first = band & ((kj >= blk) | jnp.logical_not(seq_start))
    sinks = sink_ref[...]
    for j in range(ATTN_KV_HEADS):
        sl = slice(j * HEAD_DIM, (j + 1) * HEAD_DIM)
        heads = [j * ATTN_REP + r for r in range(ATTN_REP)]
        sink = jnp.concatenate([jnp.broadcast_to(sinks[:, h:h + 1], (1, blk)) for h in heads], axis=1)
        for n in range(nblk):
            rows = slice(n * blk, (n + 1) * blk)
            kb = k_all[n * blk:(n + 2) * blk, sl]
            vb = v_all[n * blk:(n + 2) * blk, sl]
            q4 = jnp.concatenate([q[rows, h * HEAD_DIM:(h + 1) * HEAD_DIM] for h in heads], axis=0)
            s = lax.dot_general(kb, q4, (((1,), (1,)), ((), ())), preferred_element_type=F32)
            s = jnp.where(band_first if n == 0 else band, s, -jnp.inf)
            m = jnp.maximum(jnp.max(s, axis=0, keepdims=True), sink)
            p = jnp.exp(s - m)
            denom = jnp.sum(p, axis=0, keepdims=True) + jnp.exp(sink - m)
            pn = (p * (1.0 / denom)).astype(BF16)
            out = lax.dot_general(pn, vb, (((0,), (0,)), ((), ())), preferred_element_type=F32)
            for r, h in enumerate(heads):
                o_ref[rows, h * HEAD_DIM:(h + 1) * HEAD_DIM] = out[r * blk:(r + 1) * blk].astype(o_ref.dtype)


def _rope_tables(seq):
    half = HEAD_DIM // 2
    inv = jnp.power(ROPE_THETA, -jnp.arange(half, dtype=F32) * 2.0 / HEAD_DIM)
    ang = jnp.arange(seq, dtype=F32)[:, None] * inv[None, :]
    cos, sin = jnp.cos(ang), jnp.sin(ang)
    cos2 = jnp.concatenate([cos, cos, cos, cos], axis=1)
    sin2 = jnp.concatenate([-sin, sin, -sin, sin], axis=1)
    return cos2, sin2


def _head_mean_matrix(width):
    i = jnp.arange(width)
    return jnp.where((i[:, None] // HEAD_DIM) == (i[None, :] // HEAD_DIM), 1.0 / HEAD_DIM, 0.0).astype(BF16)


def _swa_attention(q, k, v, q_g, k_g, sinks, bsz):
    t = q.shape[0]
    seq = t // bsz
    tile = min(ATTN_TILE, seq)
    nt = seq // tile
    bpt = tile // ATTN_BLOCK
    nb = seq // ATTN_BLOCK
    cos2, sin2 = _rope_tables(seq)
    qg = jnp.tile(q_g.astype(F32), ATTN_HEADS)[None]
    kg = jnp.tile(k_g.astype(F32), ATTN_KV_HEADS)[None]
    cur = lambda b, n: (b * nt + n, 0)
    prev = lambda b, n: (b * nb + jnp.maximum(n * bpt - 1, 0), 0)
    tcur = lambda b, n: (n, 0)
    tprev = lambda b, n: (jnp.maximum(n * bpt - 1, 0), 0)
    blk = ATTN_BLOCK
    return pl.pallas_call(
        _attn_kernel,
        grid=(bsz, nt),
        in_specs=[pl.BlockSpec((tile, ATTN_Q), cur),
                  pl.BlockSpec((tile, ATTN_KV), cur), pl.BlockSpec((blk, ATTN_KV), prev),
                  pl.BlockSpec((tile, ATTN_KV), cur), pl.BlockSpec((blk, ATTN_KV), prev),
                  pl.BlockSpec((tile, LANES), tcur), pl.BlockSpec((tile, LANES), tcur),
                  pl.BlockSpec((blk, LANES), tprev), pl.BlockSpec((blk, LANES), tprev),
                  _const_spec((1, ATTN_Q)), _const_spec((1, ATTN_KV)), _const_spec((1, ATTN_HEADS)),
                  _const_spec((ATTN_Q, ATTN_Q)), _const_spec((ATTN_KV, ATTN_KV))],
        out_specs=pl.BlockSpec((tile, ATTN_Q), cur),
        out_shape=jax.ShapeDtypeStruct((t, ATTN_Q), BF16),
        compiler_params=_cparams(("parallel", "parallel")),
        name="swa_attention",
    )(q, k, k, v, v, cos2, sin2, cos2, sin2, qg, kg, sinks.astype(F32)[None],
      _head_mean_matrix(ATTN_Q), _head_mean_matrix(ATTN_KV))


_CONV_TAIL = SSD_CHUNK


def _ssd_kernel(z_ref, xbc_ref, dt_ref, cw_ref, cb_ref, dtb_ref, alog_ref, dskip_ref, ng_ref, exp_ref, shift_ref,
                o_ref, tail, state, ybufs):
    @pl.when(pl.program_id(1) == 0)
    def _():
        tail[0] = jnp.zeros(tail.shape[1:], tail.dtype)
        state[...] = jnp.zeros_like(state)

    first = pl.program_id(1) * SSD_STEP_CHUNKS
    for i in range(SSD_STEP_CHUNKS):
        _ssd_chunk(i, (first + i) % 3, (first + i + 1) % 3, z_ref, xbc_ref, dt_ref, cw_ref, cb_ref, dtb_ref,
                   alog_ref, dskip_ref, ng_ref, exp_ref, shift_ref, o_ref, tail, state, ybufs.at[i])


def _ssd_chunk(i, prev_slot, next_slot, z_ref, xbc_ref, dt_ref, cw_ref, cb_ref, dtb_ref, alog_ref, dskip_ref,
               ng_ref, exp_ref, shift_ref, o_ref, tail, state, ybuf):
    ch = SSD_CHUNK
    rows = slice(i * ch, (i + 1) * ch)
    cur = xbc_ref[rows, :]
    shifted = _dot(shift_ref[...], jnp.concatenate([tail[prev_slot], cur], axis=0))
    tail[next_slot] = cur
    conv = cb_ref[...] + cw_ref[SSD_CONV - 1:SSD_CONV, :] * cur.astype(F32)
    for d in range(1, SSD_CONV):
        conv = conv + cw_ref[SSD_CONV - 1 - d:SSD_CONV - d, :] * shifted[(d - 1) * ch:d * ch]
    act = _silu(conv)
    xs = act[:, :SSD_WIDTH]
    bm = act[:, SSD_WIDTH:SSD_WIDTH + SSD_BC].astype(BF16)
    cm = act[:, SSD_WIDTH + SSD_BC:].astype(BF16)

    lane = lax.broadcasted_iota(jnp.int32, (ch, LANES), 1)
    xdt = dt_ref[rows, :] + dtb_ref[...]
    dt = jnp.maximum(xdt, 0.0) + jnp.log1p(jnp.exp(-jnp.abs(xdt)))
    dt = jnp.where(lane < SSD_HEADS, dt, 0.0)
    a = dt * (-jnp.exp(alog_ref[...]))
    row = lax.broadcasted_iota(jnp.int32, (ch, ch), 0)
    col = lax.broadcasted_iota(jnp.int32, (ch, ch), 1)
    causal = row >= col
    hi, mid, lo = _split3(a)
    tril = causal.astype(BF16)
    cs = _dot(tril, hi) + _dot(tril, mid) + _dot(tril, lo)
    cs_t = cs.T
    expand = exp_ref[...]
    dt_x = _dot_f32_lhs(dt, expand)
    cs_x = _dot_f32_lhs(cs, expand)
    cs_last_x = cs_x[ch - 1:ch, :]
    xdt_full = xs * dt_x
    in_decay = jnp.exp(cs_x)
    out_decay = jnp.exp(cs_last_x - cs_x)
    chunk_decay = jnp.exp(cs_last_x)
    xw = (xdt_full * out_decay).astype(BF16)
    xdt_b = xdt_full.astype(BF16)
    hpg = SSD_HEADS // SSD_GROUPS
    for g in range(SSD_GROUPS):
        gs = slice(g * SSD_STATE, (g + 1) * SSD_STATE)
        ws = slice(g * SSD_GROUP_W, (g + 1) * SSD_GROUP_W)
        bg = bm[:, gs]
        cg = cm[:, gs]
        cb = lax.dot_general(cg, bg, (((1,), (1,)), ((), ())), preferred_element_type=F32)
        prev = state[g]
        ybuf[:, ws] = _dot(cg, prev.astype(BF16)) * in_decay[:, ws]
        for h in range(hpg):
            hh = g * hpg + h
            seg = cs[:, hh:hh + 1] - cs_t[hh:hh + 1, :]
            lmat = jnp.where(causal, jnp.exp(seg), 0.0)
            hs = slice(hh * SSD_HEAD_DIM, (hh + 1) * SSD_HEAD_DIM)
            ybuf[:, hs] += _dot((cb * lmat).astype(BF16), xdt_b[:, hs])
        upd = lax.dot_general(bg, xw[:, ws], (((0,), (0,)), ((), ())), preferred_element_type=F32)
        state[g] = prev * chunk_decay[:, ws] + upd

    y = ybuf[...] + xs * dskip_ref[...]
    y = y * _silu(z_ref[rows, :].astype(F32))
    parts = []
    for g in range(SSD_GROUPS):
        ws = slice(g * SSD_GROUP_W, (g + 1) * SSD_GROUP_W)
        parts.append(_rms_scale(y[:, ws]))
    o_ref[rows, :] = (jnp.concatenate(parts, axis=1) * ng_ref[...]).astype(o_ref.dtype)


def _ssd_mixer(z, xbc, dt_raw, conv_w, conv_b, dt_bias, a_log, d_skip, norm_g, bsz):
    t = z.shape[0]
    seq = t // bsz
    nc = seq // SSD_CHUNK
    pad = lambda v: jnp.pad(v.astype(F32), (0, LANES - SSD_HEADS))[None]
    d_x = jnp.repeat(d_skip.astype(F32), SSD_HEAD_DIM)[None]
    hid = jnp.arange(SSD_WIDTH) // SSD_HEAD_DIM
    expand = (jnp.arange(LANES)[:, None] == hid[None, :]).astype(BF16)
    r = jnp.arange((SSD_CONV - 1) * SSD_CHUNK)
    pick = r % SSD_CHUNK + _CONV_TAIL - (r // SSD_CHUNK + 1)
    shift = (jnp.arange(_CONV_TAIL + SSD_CHUNK)[None, :] == pick[:, None]).astype(BF16)
    assert nc % SSD_STEP_CHUNKS == 0
    ns = nc // SSD_STEP_CHUNKS
    rows = SSD_STEP_CHUNKS * SSD_CHUNK
    blk = lambda b, c: (b * ns + c, 0)
    return pl.pallas_call(
        _ssd_kernel,
        grid=(bsz, ns),
        in_specs=[pl.BlockSpec((rows, SSD_WIDTH), blk),
                  pl.BlockSpec((rows, SSD_CONV_CH), blk),
                  pl.BlockSpec((rows, DT_PAD), blk),
                  _const_spec((SSD_CONV, SSD_CONV_CH)), _const_spec((1, SSD_CONV_CH)),
                  _const_spec((1, LANES)), _const_spec((1, LANES)),
                  _const_spec((1, SSD_WIDTH)), _const_spec((1, SSD_WIDTH)),
                  _const_spec((LANES, SSD_WIDTH)), _const_spec(shift.shape)],
        out_specs=pl.BlockSpec((rows, SSD_WIDTH), blk),
        out_shape=jax.ShapeDtypeStruct((t, SSD_WIDTH), BF16),
        scratch_shapes=[pltpu.VMEM((3, _CONV_TAIL, SSD_CONV_CH), BF16),
                        pltpu.VMEM((SSD_GROUPS, SSD_STATE, SSD_GROUP_W), F32),
                        pltpu.VMEM((SSD_STEP_CHUNKS, SSD_CHUNK, SSD_WIDTH), F32)],
        compiler_params=_cparams(("arbitrary", "arbitrary")),
        name="ssd_mixer",
    )(z, xbc, dt_raw, conv_w.astype(F32), conv_b.astype(F32)[None], pad(dt_bias), pad(a_log),
      d_x, norm_g.astype(F32)[None], expand, shift)


def _route(logits):
    lane = lax.broadcasted_iota(jnp.int32, logits.shape, 1).astype(F32)
    big = float(ROUTE_PAD)
    is_g = lane < MOE_GROUPS
    gl = jnp.where(is_g, logits, -jnp.inf)
    gmax = jnp.max(gl, axis=1, keepdims=True)
    gsel = jnp.min(jnp.where(gl == gmax, lane, big), axis=1, keepdims=True)
    pg = 1.0 / jnp.sum(jnp.where(is_g, jnp.exp(logits - gmax), 0.0), axis=1, keepdims=True)
    lo = MOE_GROUPS + MOE_PER_GROUP * gsel
    ev = jnp.where((lane >= lo) & (lane < lo + MOE_PER_GROUP), logits, -jnp.inf)
    m1 = jnp.max(ev, axis=1, keepdims=True)
    i1 = jnp.min(jnp.where(ev == m1, lane, big), axis=1, keepdims=True)
    ev2 = jnp.where(lane == i1, -jnp.inf, ev)
    m2 = jnp.max(ev2, axis=1, keepdims=True)
    i2 = jnp.min(jnp.where(ev2 == m2, lane, big), axis=1, keepdims=True)
    e21 = jnp.exp(m2 - m1)
    w1 = pg / (1.0 + e21)
    w2 = pg * e21 / (1.0 + e21)
    out = jnp.where(lane == 0, i1 - MOE_GROUPS, 0.0)
    out = jnp.where(lane == 1, i2 - MOE_GROUPS, out)
    out = jnp.where(lane == 2, w1, out)
    return jnp.where(lane == 3, w2, out)


def _merge_kernel(*refs, n_x):
    x_refs = refs[:n_x]
    (ys5_ref, u_ref, yb_ref, yc_ref, n1g_ref, wg_ref, bg_ref, d_ref, w1_ref, w2_ref, ps5_ref, pat_ref,
     pssd_ref, wo_ref, n2g_ref, wrh_ref, wrm_ref, br_ref, x1_o, h2_o, route_o) = refs[n_x:]
    x = _x_sum(x_refs)
    hb = (_rms_scale(x) * n1g_ref[...]).astype(BF16)
    ya = ys5_ref[...].astype(F32) + d_ref[...] * u_ref[...].astype(F32)
    yab = jax.nn.gelu(ya).astype(BF16)
    ya = _dot(yab, w1_ref[...]) * jax.nn.sigmoid(_dot(yab, w2_ref[...]))
    branches = ((ya.astype(BF16), ps5_ref), (yb_ref[...], pat_ref), (yc_ref[...], pssd_ref))
    merged = None
    for b, (yv, p_ref) in enumerate(branches):
        gate = jax.nn.sigmoid(_dot(hb, wg_ref[:, b * D_MODEL:(b + 1) * D_MODEL]) + bg_ref[b:b + 1, :])
        term = gate * _dot(yv, p_ref[...])
        merged = term if merged is None else merged + term
    x1 = x + _dot(merged.astype(BF16), wo_ref[...])
    x1_o[...] = x1
    h2 = _rms_scale(x1) * n2g_ref[...]
    _store_row_tiles(h2_o, 0, h2)
    h_hi = h2.astype(BF16)
    h_mid = (h2 - h_hi.astype(F32)).astype(BF16)
    logits = _dot(h_hi, wrh_ref[...]) + _dot(h_hi, wrm_ref[...]) + _dot(h_mid, wrh_ref[...]) + br_ref[...]
    route_o[...] = _route(logits)


def _merge(xs, t, ys5, u, yb, yc, lw):
    tm = min(TM_PROJ, t)
    row = lambda w: pl.BlockSpec((tm, w), lambda i: (i, 0))
    consts = [lw['n1g'], lw['w_gate'], lw['b_gate'], lw['s5_d'], lw['glu_w1'], lw['glu_w2'], lw['p_s5'],
              lw['p_attn'], lw['p_ssd'], lw['w_out'], lw['n2g'], lw['w_router_hi'], lw['w_router_mid'],
              lw['b_router']]
    x_arrs, x_specs = _x_parts(xs, tm)
    return pl.pallas_call(
        functools.partial(_merge_kernel, n_x=len(x_arrs)),
        grid=(t // tm,),
        in_specs=x_specs + [row(S5_WIDTH), row(S5_WIDTH), row(ATTN_Q), row(SSD_WIDTH)]
                 + [_const_spec(c.shape) for c in consts],
        out_specs=[row(D_MODEL), pl.BlockSpec((tm * ROW_TILE, LANES), lambda i: (i, 0)), row(ROUTE_PAD)],
        out_shape=[jax.ShapeDtypeStruct((t, D_MODEL), F32), jax.ShapeDtypeStruct((t * ROW_TILE, LANES), F32),
                   jax.ShapeDtypeStruct((t, ROUTE_PAD), F32)],
        compiler_params=_cparams(("parallel",)),
        name="merge_router",
    )(*x_arrs, ys5, u, yb, yc, *consts)


def _moe_plan(route, t):
    n_exp = MOE_EXPERTS
    e = route[:, 0:2].astype(jnp.int32).reshape(-1)
    onehot = (e[:, None] == jnp.arange(n_exp, dtype=jnp.int32)[None, :]).astype(jnp.int32)
    csum = jnp.cumsum(onehot, axis=0)
    rank = jnp.sum(csum * onehot, axis=1) - 1
    counts = csum[-1]
    padded = ((counts + TM_G - 1) // TM_G) * TM_G
    pend = jnp.cumsum(padded)
    dest = (pend - padded)[e] + rank
    n_tiles = -(-(2 * t + n_exp * (TM_G - 1)) // TM_G)
    n_rows = n_tiles * TM_G
    pair = jnp.full((n_rows,), -1, jnp.int32).at[dest].set(jnp.arange(2 * t, dtype=jnp.int32),
                                                            unique_indices=True)
    valid = pair >= 0
    tok, k = pair // 2, pair % 2
    src = jnp.where(valid, tok, 0)
    spare = t + jnp.arange(n_rows, dtype=jnp.int32) % _MOE_SPARE
    dst = jnp.where(valid, k * (t + _MOE_SPARE) + tok, spare)
    tile_start = jnp.arange(n_tiles, dtype=jnp.int32) * TM_G
    tile_e = jnp.sum((pend[None, :] <= tile_start[:, None]).astype(jnp.int32), axis=1)
    n_used = pend[-1:] // TM_G
    tile_e = jnp.minimum(tile_e, n_exp - 1)
    tile_e = jnp.minimum(tile_e, tile_e[n_used[0] - 1])
    return src.reshape(n_tiles, 1, TM_G), dst.reshape(n_tiles, 1, TM_G), tile_e, n_used.astype(jnp.int32)


_MOE_SPARE = 2 * TM_G


def _gmm_kernel(te_ref, nu_ref, src_ref, srcn_ref, dst_ref, h2_hbm, wg_ref, wu_ref, wd_ref, out_hbm,
                xbuf, ybuf, gsem, ssem):
    del te_ref
    i = pl.program_id(0)
    last = nu_ref[0] - 1
    slot = i % 2

    tile_rows = TM_G * ROW_TILE

    def row_tile(ref, s, r):
        return ref.at[pl.ds((s * TM_G + r) * ROW_TILE, ROW_TILE), :]

    def slot_rows(ref, s):
        return ref.at[pl.ds(s * tile_rows, tile_rows), :]

    def gather_start(idx_ref, s):
        for r in range(TM_G):
            src = h2_hbm.at[pl.ds(pl.multiple_of(idx_ref[0, 0, r], ROW_TILE), ROW_TILE), :]
            pltpu.make_async_copy(src, row_tile(xbuf, s, r), gsem.at[s]).start(priority=r % 2)

    def gather_wait(s):
        pltpu.make_async_copy(h2_hbm.at[pl.ds(0, tile_rows), :], slot_rows(xbuf, s), gsem.at[s]).wait()

    def scatter_wait(s):
        pltpu.make_async_copy(slot_rows(ybuf, s), out_hbm.at[pl.ds(0, tile_rows), :], ssem.at[s]).wait()

    @pl.when(i == 0)
    def _():
        gather_start(src_ref, 0)
        ybuf[...] = jnp.zeros_like(ybuf)
        plane = out_hbm.shape[0] // 2
        spare0 = plane - _MOE_SPARE * ROW_TILE
        fills = [pltpu.make_async_copy(slot_rows(ybuf, s),
                                       out_hbm.at[pl.ds(k * plane + spare0 + s * tile_rows, tile_rows), :],
                                       ssem.at[s]) for k in range(2) for s in range(2)]
        for f in fills:
            f.start()
        for f in fills:
            f.wait()

    @pl.when(i <= last)
    def _():
        gather_wait(slot)

        @pl.when(i >= 2)
        def _():
            scatter_wait(slot)

        gather_start(srcn_ref, 1 - slot)

        xb = _load_row_tiles(xbuf, slot * TM_G, TM_G).astype(BF16)
        hid = _silu(_dot(xb, wg_ref[0, 0].astype(BF16))) * _dot(xb, wu_ref[0, 0].astype(BF16))
        _store_row_tiles(ybuf, slot * TM_G, _dot(hid.astype(BF16), wd_ref[0, 0].astype(BF16)))
        for r in range(TM_G):
            dst = out_hbm.at[pl.ds(pl.multiple_of(dst_ref[0, 0, r], ROW_TILE), ROW_TILE), :]
            pltpu.make_async_copy(row_tile(ybuf, slot, r), dst, ssem.at[slot]).start(priority=r % 2)

        @pl.when(i == last)
        def _():
            scatter_wait(slot)
            gather_wait(1 - slot)

        @pl.when((i == last) & (i >= 1))
        def _():
            scatter_wait(1 - slot)


def _moe_sparse(h2, route, w_gate, w_up, w_down, l):
    t = h2.shape[0] // ROW_TILE
    src, dst, tile_e, n_used = _moe_plan(route, t)
    src, dst = src * ROW_TILE, dst * ROW_TILE
    n_tiles = src.shape[0]
    smem = lambda imap: pl.BlockSpec((1, 1, TM_G), imap, memory_space=pltpu.SMEM)
    grid_spec = pltpu.PrefetchScalarGridSpec(
        num_scalar_prefetch=2,
        grid=(n_tiles,),
        in_specs=[smem(lambda i, te, nu: (i, 0, 0)),
                  smem(lambda i, te, nu: (jnp.minimum(i + 1, n_tiles - 1), 0, 0)),
                  smem(lambda i, te, nu: (i, 0, 0)),
                  pl.BlockSpec(memory_space=pl.ANY),
                  pl.BlockSpec((1, 1, D_MODEL, MOE_FF), lambda i, te, nu: (l, te[i], 0, 0)),
                  pl.BlockSpec((1, 1, D_MODEL, MOE_FF), lambda i, te, nu: (l, te[i], 0, 0)),
                  pl.BlockSpec((1, 1, MOE_FF, D_MODEL), lambda i, te, nu: (l, te[i], 0, 0))],
        out_specs=pl.BlockSpec(memory_space=pl.ANY),
        scratch_shapes=[pltpu.VMEM((2 * TM_G * ROW_TILE, LANES), F32), pltpu.VMEM((2 * TM_G * ROW_TILE, LANES), F32),
                        pltpu.SemaphoreType.DMA((2,)), pltpu.SemaphoreType.DMA((2,))])
    plane = (t + _MOE_SPARE) * ROW_TILE
    out = pl.pallas_call(
        _gmm_kernel,
        grid_spec=grid_spec,
        out_shape=jax.ShapeDtypeStruct((2 * plane, LANES), F32),
        compiler_params=_cparams(("arbitrary",)),
        name="moe_gmm",
    )(tile_e, n_used, src, src, dst, h2, w_gate, w_up, w_down)
    return out.reshape(2, plane, LANES)


def _sum_kernel(*refs):
    refs[-1][...] = _x_sum(refs[:-1])


def _residual_sum(xs, t):
    tm = min(TM_PROJ, t)
    x_arrs, x_specs = _x_parts(xs, tm)
    return pl.pallas_call(
        _sum_kernel,
        grid=(t // tm,),
        in_specs=x_specs,
        out_specs=pl.BlockSpec((tm, D_MODEL), lambda i: (i, 0)),
        out_shape=jax.ShapeDtypeStruct((t, D_MODEL), F32),
        compiler_params=_cparams(("parallel",)),
        name="residual_sum",
    )(*x_arrs)


def _cast_kernel(*refs):
    n = len(refs) // 2
    for x_ref, o_ref in zip(refs[:n], refs[n:]):
        o_ref[...] = x_ref[...].astype(o_ref.dtype)


def _cast_layer_bf16(ws, l):
    halves = 2
    in_specs = [pl.BlockSpec((None, w.shape[1] // halves, w.shape[2]), lambda i: (l, i, 0)) for w in ws]
    out_specs = [pl.BlockSpec((w.shape[1] // halves, w.shape[2]), lambda i: (i, 0)) for w in ws]
    return pl.pallas_call(
        _cast_kernel,
        grid=(halves,),
        in_specs=in_specs,
        out_specs=out_specs,
        out_shape=[jax.ShapeDtypeStruct(w.shape[1:], BF16) for w in ws],
        compiler_params=_cparams(("parallel",)),
        name="cast_weights",
    )(*[w.astype(F32) for w in ws])


_GATE_SHIFT = N_MIX % LANES
assert _GATE_SHIFT % SUBLANES == 0


def _gate_w_kernel(a_ref, b_ref, o_ref):
    full = jnp.concatenate([a_ref[...], b_ref[...]], axis=0)
    o_ref[...] = full[_GATE_SHIFT:_GATE_SHIFT + LANES, :].T.astype(o_ref.dtype)


def _gate_weights(w_in_t, l):
    base = (N_MIX - _GATE_SHIFT) // LANES
    return pl.pallas_call(
        _gate_w_kernel,
        grid=(N_BRANCH * D_MODEL // LANES,),
        in_specs=[pl.BlockSpec((None, LANES, D_MODEL), lambda k: (l, base + k, 0)),
                  pl.BlockSpec((None, LANES, D_MODEL), lambda k: (l, base + k + 1, 0))],
        out_specs=pl.BlockSpec((D_MODEL, LANES), lambda k: (0, k)),
        out_shape=jax.ShapeDtypeStruct((D_MODEL, N_BRANCH * D_MODEL), BF16),
        compiler_params=_cparams(("parallel",)),
        name="gate_weights",
    )(w_in_t, w_in_t)


def _layer_weights(l, p):
    glu_w1, glu_w2, p_s5, p_attn, p_ssd, w_out = _cast_layer_bf16(
        [p['s5_glu_w1'], p['s5_glu_w2'], p['p_s5'], p['p_attn'], p['p_ssd'], p['w_out']], l)
    w_router = jnp.concatenate([p['w_router_group'][l], p['w_router_expert'][l]], axis=1).astype(F32)
    npad = ROUTE_PAD - w_router.shape[1]
    b_router = jnp.concatenate([p['b_router_group'][l], p['b_router_expert'][l]]).astype(F32)
    w_router = jnp.pad(w_router, ((0, 0), (0, npad)))
    w_router_hi = w_router.astype(BF16)
    return dict(
        n1g=p['norm1_g'][l].astype(F32)[None],
        w_gate=_gate_weights(p['w_in_t'], l), b_gate=p['b_gate'][l].astype(F32),
        s5_d=p['s5_d'][l].astype(F32)[None],
        glu_w1=glu_w1, glu_w2=glu_w2, p_s5=p_s5, p_attn=p_attn, p_ssd=p_ssd, w_out=w_out,
        n2g=p['norm2_g'][l].astype(F32)[None],
        w_router_hi=w_router_hi, w_router_mid=(w_router - w_router_hi.astype(F32)).astype(BF16),
        b_router=jnp.pad(b_router, (0, npad))[None])


def _layer(xs, t, l, p, s5_tabs, bsz):
    lw = _layer_weights(l, p)
    u, q, k, v, z, xbc, dt_raw, *x_sum = _inproj(xs, t, lw['n1g'], p['w_in_t'], l)
    if x_sum:
        xs = x_sum
    ys5 = _s5_mixer(u, s5_tabs, bsz)
    yb = _swa_attention(q, k, v, p['q_norm_g'][l], p['k_norm_g'][l], p['attn_sinks'][l], bsz)
    yc = _ssd_mixer(z, xbc, dt_raw, p['ssd_conv_w'][l], p['ssd_conv_b'][l], p['ssd_dt_bias'][l],
                    p['ssd_a_log'][l], p['ssd_d'][l], p['ssd_norm_g'][l], bsz)
    x1, h2, route = _merge(xs, t, ys5, u, yb, yc, lw)
    moe = _moe_sparse(h2, route, p['w_exp_gate'], p['w_exp_up'], p['w_exp_down'], l)
    return [x1, route, moe]


def kernel(x, norm1_g, w_in, b_gate, s5_lambda_re, s5_lambda_im, s5_b_re, s5_b_im, s5_c_re, s5_c_im, s5_d, s5_log_dt, s5_glu_w1, s5_glu_w2, q_norm_g, k_norm_g, attn_sinks, ssd_conv_w, ssd_conv_b, ssd_dt_bias, ssd_a_log, ssd_d, ssd_norm_g, p_s5, p_attn, p_ssd, w_out, norm2_g, w_router_group, b_router_group, w_router_expert, b_router_expert, w_exp_gate, w_exp_up, w_exp_down):
    p = dict(norm1_g=norm1_g, w_in=w_in, b_gate=b_gate, s5_lambda_re=s5_lambda_re, s5_lambda_im=s5_lambda_im,
             s5_b_re=s5_b_re, s5_b_im=s5_b_im, s5_c_re=s5_c_re, s5_c_im=s5_c_im, s5_d=s5_d,
             s5_log_dt=s5_log_dt, s5_glu_w1=s5_glu_w1, s5_glu_w2=s5_glu_w2, q_norm_g=q_norm_g,
             k_norm_g=k_norm_g, attn_sinks=attn_sinks, ssd_conv_w=ssd_conv_w, ssd_conv_b=ssd_conv_b,
             ssd_dt_bias=ssd_dt_bias, ssd_a_log=ssd_a_log, ssd_d=ssd_d, ssd_norm_g=ssd_norm_g, p_s5=p_s5,
             p_attn=p_attn, p_ssd=p_ssd, w_out=w_out, norm2_g=norm2_g, w_router_group=w_router_group,
             b_router_group=b_router_group, w_router_expert=w_router_expert, b_router_expert=b_router_expert,
             w_exp_gate=w_exp_gate, w_exp_up=w_exp_up, w_exp_down=w_exp_down)
    p['w_in_t'] = jnp.swapaxes(w_in.astype(F32), 1, 2)
    bsz, seq, dm = x.shape
    depth = w_in.shape[0]
    t = bsz * seq
    xs = [x.reshape(t, dm)]
    for l in range(depth):
        s5_tabs = _s5_tables(s5_lambda_re[l], s5_lambda_im[l], s5_b_re[l], s5_b_im[l], s5_c_re[l], s5_c_im[l],
                             s5_log_dt[l])
        xs = _layer(xs, t, l, p, s5_tabs, bsz)
    return _residual_sum(xs, t).reshape(bsz, seq, dm)
```

```python
import functools
import math

import jax
import jax.numpy as jnp
from jax import lax
from jax.experimental import pallas as pl
from jax.experimental.pallas import tpu as pltpu

F32 = jnp.float32
BF16 = jnp.bfloat16

D_MODEL = 1024
NORM_EPS = 1e-6
S5_WIDTH = 512
S5_GROUP = 16
S5_GROUPS = 32
S5_STATE = 64
HEAD_DIM = 64
ATTN_HEADS = 8
ATTN_KV_HEADS = 2
ATTN_REP = ATTN_HEADS // ATTN_KV_HEADS
ATTN_Q = ATTN_HEADS * HEAD_DIM
ATTN_KV = ATTN_KV_HEADS * HEAD_DIM
ATTN_BLOCK = 128
ROPE_THETA = 10000.0
SSD_WIDTH = 1024
SSD_HEAD_DIM = 64
SSD_HEADS = 16
SSD_GROUPS = 2
SSD_STATE = 64
SSD_CONV = 4
SSD_CHUNK = 128
SSD_BC = SSD_GROUPS * SSD_STATE
SSD_CONV_CH = SSD_WIDTH + 2 * SSD_BC
SSD_GROUP_W = SSD_WIDTH // SSD_GROUPS
N_BRANCH = 3
MOE_GROUPS = 4
MOE_PER_GROUP = 8
MOE_EXPERTS = 32
MOE_FF = 512
N_MIX = S5_WIDTH + ATTN_Q + 2 * ATTN_KV + SSD_WIDTH + SSD_CONV_CH + SSD_HEADS

LANES = 128
SUBLANES = 8
VMEM_LIMIT_BYTES = 56 * 1024 * 1024

S5_CHUNK = 16
S5_TG = 8
ATTN_TILE = 512
TM_PROJ = 512
TM_G = 256
SSD_STEP_CHUNKS = 2
DT_PAD = LANES
N_MIX_PAD = N_MIX - SSD_HEADS + DT_PAD
ROUTE_PAD = LANES


def _cparams(semantics):
    return pltpu.CompilerParams(dimension_semantics=semantics, vmem_limit_bytes=VMEM_LIMIT_BYTES)


def _const_spec(shape):
    zeros = (0,) * len(shape)
    return pl.BlockSpec(shape, lambda *_: zeros, pipeline_mode=pl.Buffered(1))


def _dot(a, b):
    return jnp.dot(a, b, preferred_element_type=F32)


def _split3(a):
    hi = a.astype(BF16)
    r1 = a - hi.astype(F32)
    mid = r1.astype(BF16)
    lo = (r1 - mid.astype(F32)).astype(BF16)
    return hi, mid, lo


def _dot_f32_lhs(a, b_bf16):
    hi, mid, lo = _split3(a)
    return _dot(hi, b_bf16) + _dot(mid, b_bf16) + _dot(lo, b_bf16)


def _rms_scale(x):
    return x * lax.rsqrt(jnp.mean(x * x, axis=-1, keepdims=True) + NORM_EPS)


def _silu(x):
    return x * jax.nn.sigmoid(x)


_OFF_U = 0
_OFF_Q = _OFF_U + S5_WIDTH
_OFF_K = _OFF_Q + ATTN_Q
_OFF_V = _OFF_K + ATTN_KV
_OFF_Z = _OFF_V + ATTN_KV
_OFF_XBC = _OFF_Z + SSD_WIDTH
_OFF_DT = _OFF_XBC + SSD_CONV_CH


ROW_TILE = D_MODEL // LANES


def _store_row_tiles(ref, start, val):
    n = val.shape[0]
    for j in range(ROW_TILE):
        ref[pl.ds(start * ROW_TILE + j, n, stride=ROW_TILE), :] = val[:, j * LANES:(j + 1) * LANES]


def _load_row_tiles(ref, start, n):
    return jnp.concatenate([ref[pl.ds(start * ROW_TILE + j, n, stride=ROW_TILE), :] for j in range(ROW_TILE)],
                           axis=1)


def _x_parts(xs, tm):
    row = lambda w: pl.BlockSpec((tm, w), lambda i: (i, 0))
    if len(xs) == 1:
        return list(xs), [row(D_MODEL)]
    x1, route, moe = xs
    plane = lambda k: pl.BlockSpec((None, tm * ROW_TILE, LANES), lambda i: (k, i, 0))
    return [x1, route, moe, moe], [row(D_MODEL), row(ROUTE_PAD), plane(0), plane(1)]


def _x_sum(x_refs):
    if len(x_refs) == 1:
        return x_refs[0][...]
    x1_ref, route_ref, m0_ref, m1_ref = x_refs
    r = route_ref[...]
    n = x1_ref.shape[0]
    return x1_ref[...] + r[:, 2:3] * _load_row_tiles(m0_ref, 0, n) + r[:, 3:4] * _load_row_tiles(m1_ref, 0, n)


def _inproj_kernel(*refs, n_x):
    x_refs, (g_ref, w_ref), outs, wb = refs[:n_x], refs[n_x:n_x + 2], refs[n_x + 2:-1], refs[-1]

    @pl.when(pl.program_id(0) == 0)
    def _():
        for j in range(N_MIX_PAD // LANES):
            wb[:, j * LANES:(j + 1) * LANES] = w_ref[j * LANES:(j + 1) * LANES, :].T.astype(BF16)

    x = _x_sum(x_refs)
    hb = (_rms_scale(x) * g_ref[...]).astype(BF16)
    offs = (_OFF_U, _OFF_Q, _OFF_K, _OFF_V, _OFF_Z, _OFF_XBC, _OFF_DT)
    for o_ref, off in zip(outs, offs):
        width = o_ref.shape[1]
        o_ref[...] = _dot(hb, wb[:, off:off + width]).astype(o_ref.dtype)
    if len(outs) > len(offs):
        outs[-1][...] = x


def _inproj(xs, t, g, w_in, l):
    tm = min(TM_PROJ, t)
    widths = (S5_WIDTH, ATTN_Q, ATTN_KV, ATTN_KV, SSD_WIDTH, SSD_CONV_CH, DT_PAD)
    dtypes = (F32, BF16, BF16, BF16, BF16, BF16, F32)
    if len(xs) > 1:
        widths, dtypes = widths + (D_MODEL,), dtypes + (F32,)
    x_arrs, x_specs = _x_parts(xs, tm)
    w_spec = pl.BlockSpec((None, N_MIX_PAD, D_MODEL), lambda i: (l, 0, 0), pipeline_mode=pl.Buffered(1))
    return pl.pallas_call(
        functools.partial(_inproj_kernel, n_x=len(x_arrs)),
        grid=(t // tm,),
        in_specs=x_specs + [_const_spec((1, D_MODEL)), w_spec],
        out_specs=[pl.BlockSpec((tm, w), lambda i: (i, 0)) for w in widths],
        out_shape=[jax.ShapeDtypeStruct((t, w), d) for w, d in zip(widths, dtypes)],
        scratch_shapes=[pltpu.VMEM((D_MODEL, N_MIX_PAD), BF16)],
        compiler_params=_cparams(("arbitrary",)),
        name="inproj",
    )(*x_arrs, g, w_in)


def _s5_tables(lam_re, lam_im, b_re, b_im, c_re, c_im, log_dt):
    hp = lax.Precision.HIGHEST
    g_n, p_n = lam_re.shape
    c_n = b_re.shape[-1]
    t_n = S5_CHUNK
    lr, li = lam_re.astype(F32), lam_im.astype(F32)
    dt = jnp.exp(log_dt.astype(F32))[:, None]
    mag = jnp.exp(lr * dt)
    ab_re = mag * jnp.cos(li * dt)
    ab_im = mag * jnp.sin(li * dt)
    nr = ab_re - 1.0
    den = lr * lr + li * li
    f_re = (nr * lr + ab_im * li) / den
    f_im = (ab_im * lr - nr * li) / den
    br, bi = b_re.astype(F32), b_im.astype(F32)
    bb_re = f_re[..., None] * br - f_im[..., None] * bi
    bb_im = f_re[..., None] * bi + f_im[..., None] * br
    j = jnp.arange(t_n + 1, dtype=F32)[:, None, None]
    pmag = jnp.exp(lr * dt * j)
    ang = li * dt * j
    p_re = pmag * jnp.cos(ang)
    p_im = pmag * jnp.sin(ang)
    cr, ci = c_re.astype(F32), c_im.astype(F32)
    ca_re = cr[None] * p_re[:, :, None, :] - ci[None] * p_im[:, :, None, :]
    ca_im = cr[None] * p_im[:, :, None, :] + ci[None] * p_re[:, :, None, :]
    q_re = p_re[t_n - 1 - jnp.arange(t_n)]
    q_im = p_im[t_n - 1 - jnp.arange(t_n)]
    bs_re = q_re[..., None] * bb_re[None] - q_im[..., None] * bb_im[None]
    bs_im = q_re[..., None] * bb_im[None] + q_im[..., None] * bb_re[None]

    nt = g_n // S5_TG

    def in_strip(a):
        a = a.reshape(t_n, nt, S5_TG, p_n, c_n).transpose(1, 0, 4, 2, 3)
        return a.reshape(nt, t_n, c_n, S5_TG * p_n)

    def out_strip(a):
        a = a.reshape(t_n + 1, nt, S5_TG, c_n, p_n).transpose(1, 4, 0, 2, 3)
        return a.reshape(nt, p_n, (t_n + 1) * S5_TG * c_n)

    return dict(
        bs_re=in_strip(bs_re), bs_im=in_strip(bs_im), co_re=out_strip(ca_re), co_im=out_strip(-ca_im),
        at_re=p_re[t_n].reshape(nt, 1, S5_TG * p_n), at_im=p_im[t_n].reshape(nt, 1, S5_TG * p_n))


_S5_XW = S5_CHUNK * LANES
_S5_SW = S5_TG * S5_STATE


def _group_block(strip, lanes_per_group):
    rpg = strip.shape[0]
    full = jnp.concatenate([strip] * S5_TG, axis=0)
    row_g = lax.broadcasted_iota(jnp.int32, full.shape, 0) // rpg
    lane_g = (lax.broadcasted_iota(jnp.int32, full.shape, 1) % (S5_TG * lanes_per_group)) // lanes_per_group
    return jnp.where(row_g == lane_g, full, 0.0)


def _dot_hi_mid(a, b):
    a_hi, a_mid, _ = _split3(a)
    b_hi, b_mid, _ = _split3(b)
    return _dot(a_hi, b_hi) + _dot(a_hi, b_mid) + _dot(a_mid, b_hi)


def _s5_build_tables(bre_ref, bim_ref, cre_ref, cim_ref, big, bsre, bsim, core, coim):
    lag = lambda ref, j: _group_block(ref[0, :, j * LANES:(j + 1) * LANES], S5_GROUP)
    b0_re = _group_block(bre_ref[0, S5_CHUNK - 1], S5_STATE)
    b0_im = _group_block(bim_ref[0, S5_CHUNK - 1], S5_STATE)
    blocks = [(_dot_hi_mid(b0_re, lag(cre_ref, j)) + _dot_hi_mid(b0_im, lag(cim_ref, j))).astype(BF16)
              for j in range(S5_CHUNK)]
    zero = jnp.zeros((LANES, LANES), BF16)
    for s in range(S5_CHUNK):
        for t in range(S5_CHUNK):
            big[s * LANES:(s + 1) * LANES, t * LANES:(t + 1) * LANES] = blocks[t - s] if t >= s else zero
    for src, dst in ((bre_ref, bsre), (bim_ref, bsim)):
        for s in range(S5_CHUNK):
            dst[s * LANES:(s + 1) * LANES, :] = _group_block(src[0, s], S5_STATE).astype(BF16)
    for src, dst in ((cre_ref, core), (cim_ref, coim)):
        for t in range(S5_CHUNK):
            dst[:, t * LANES:(t + 1) * LANES] = lag(src, t + 1).astype(BF16)


def _s5_kernel(u_ref, bre_ref, bim_ref, cre_ref, cim_ref, are_ref, aim_ref, y_ref,
               big, bsre, bsim, core, coim, sre, sim, hre, him):
    @pl.when(pl.program_id(1) == 0)
    def _():
        _s5_build_tables(bre_ref, bim_ref, cre_ref, cim_ref, big, bsre, bsim, core, coim)

    nck = u_ref.shape[0] // S5_CHUNK
    xcat = jnp.concatenate([u_ref[pl.ds(s, nck, stride=S5_CHUNK), :].astype(BF16) for s in range(S5_CHUNK)],
                           axis=1)
    sre[...] = _dot(xcat, bsre[...])
    sim[...] = _dot(xcat, bsim[...])
    a_re = are_ref[0]
    a_im = aim_ref[0]

    def step(i, carry):
        h_re, h_im = carry
        base = pl.multiple_of(i * SUBLANES, SUBLANES)
        s_re = sre[pl.ds(base, SUBLANES), :]
        s_im = sim[pl.ds(base, SUBLANES), :]
        ent_re, ent_im = [], []
        for j in range(SUBLANES):
            ent_re.append(h_re)
            ent_im.append(h_im)
            h_re, h_im = (a_re * h_re - a_im * h_im + s_re[j:j + 1], a_re * h_im + a_im * h_re + s_im[j:j + 1])
        hre[pl.ds(base, SUBLANES), :] = jnp.concatenate(ent_re, axis=0)
        him[pl.ds(base, SUBLANES), :] = jnp.concatenate(ent_im, axis=0)
        return h_re, h_im

    zero = jnp.zeros((1, _S5_SW), F32)
    lax.fori_loop(0, nck // SUBLANES, step, (zero, zero))
    y = (_dot(xcat, big[...]) + _dot(hre[...].astype(BF16), core[...])
         + _dot(him[...].astype(BF16), coim[...]))
    for t in range(S5_CHUNK):
        y_ref[pl.ds(t, nck, stride=S5_CHUNK), :] = y[:, t * LANES:(t + 1) * LANES]


def _s5_mixer(u, tabs, bsz):
    t = u.shape[0]
    seq = t // bsz
    nck = seq // S5_CHUNK
    nt = S5_GROUPS // S5_TG
    strip = lambda a: pl.BlockSpec((1,) + a.shape[1:], lambda x, b: (x,) + (0,) * (a.ndim - 1))
    strips = [tabs['bs_re'], tabs['bs_im'], tabs['co_re'], tabs['co_im'], tabs['at_re'], tabs['at_im']]
    return pl.pallas_call(
        _s5_kernel,
        grid=(nt, bsz),
        in_specs=[pl.BlockSpec((seq, LANES), lambda x, b: (b, x))] + [strip(a) for a in strips],
        out_specs=pl.BlockSpec((seq, LANES), lambda x, b: (b, x)),
        out_shape=jax.ShapeDtypeStruct((t, S5_WIDTH), F32),
        scratch_shapes=[pltpu.VMEM((_S5_XW, _S5_XW), BF16),
                        pltpu.VMEM((_S5_XW, _S5_SW), BF16), pltpu.VMEM((_S5_XW, _S5_SW), BF16),
                        pltpu.VMEM((_S5_SW, _S5_XW), BF16), pltpu.VMEM((_S5_SW, _S5_XW), BF16)]
                       + [pltpu.VMEM((nck, _S5_SW), F32)] * 4,
        compiler_params=_cparams(("arbitrary", "arbitrary")),
        name="s5_mixer",
    )(u, *strips)


def _swap_rope_halves(y):
    w = y.shape[1]
    lane = lax.broadcasted_iota(jnp.int32, y.shape, 1)
    lower = (lane % HEAD_DIM) < (HEAD_DIM // 2)
    return jnp.where(lower, pltpu.roll(y, w - HEAD_DIM // 2, 1), pltpu.roll(y, HEAD_DIM // 2, 1))


def _norm_rope(x, gain, head_mean, cos2, sin2):
    reps = x.shape[1] // LANES
    ms = _dot_f32_lhs(x * x, head_mean)
    y = x * lax.rsqrt(ms + NORM_EPS) * gain
    if reps > 1:
        cos2 = jnp.concatenate([cos2] * reps, axis=1)
        sin2 = jnp.concatenate([sin2] * reps, axis=1)
    return y * cos2 + _swap_rope_halves(y) * sin2


def _attn_kernel(q_ref, kc_ref, kp_ref, vc_ref, vp_ref, cosc_ref, sinc_ref, cosp_ref, sinp_ref,
                 qg_ref, kg_ref, sink_ref, hmq_ref, hmk_ref, o_ref):
    seq_start = pl.program_id(1) == 0
    blk = ATTN_BLOCK
    nblk = q_ref.shape[0] // blk
    q = _norm_rope(q_ref[...].astype(F32), qg_ref[...], hmq_ref[...], cosc_ref[...], sinc_ref[...])
    q = (q * (HEAD_DIM ** -0.5)).astype(BF16)
    kc = _norm_rope(kc_ref[...].astype(F32), kg_ref[...], hmk_ref[...], cosc_ref[...], sinc_ref[...]).astype(BF16)
    kp = _norm_rope(kp_ref[...].astype(F32), kg_ref[...], hmk_ref[...], cosp_ref[...], sinp_ref[...]).astype(BF16)
    k_all = jnp.concatenate([kp, kc], axis=0)
    v_all = jnp.concatenate([vp_ref[...], vc_ref[...]], axis=0)
    shape = (2 * blk, ATTN_REP * blk)
    kj = lax.broadcasted_iota(jnp.int32, shape, 0)
    qi = lax.broadcasted_iota(jnp.int32, shape, 1) % blk + blk
    band = (kj <= qi) & (qi - kj < blk)
    band_first = band & ((kj >= blk) | jnp.logical_not(seq_start))
    sinks = sink_ref[...]
    for j in range(ATTN_KV_HEADS):
        sl = slice(j * HEAD_DIM, (j + 1) * HEAD_DIM)
        heads = [j * ATTN_REP + r for r in range(ATTN_REP)]
        sink = jnp.concatenate([jnp.broadcast_to(sinks[:, h:h + 1], (1, blk)) for h in heads], axis=1)
        for n in range(nblk):
            rows = slice(n * blk, (n + 1) * blk)
            kb = k_all[n * blk:(n + 2) * blk, sl]
            vb = v_all[n * blk:(n + 2) * blk, sl]
            q4 = jnp.concatenate([q[rows, h * HEAD_DIM:(h + 1) * HEAD_DIM] for h in heads], axis=0)
            s = lax.dot_general(kb, q4, (((1,), (1,)), ((), ())), preferred_element_type=F32)
            s = jnp.where(band_first if n == 0 else band, s, -jnp.inf)
            m = jnp.maximum(jnp.max(s, axis=0, keepdims=True), sink)
            p = jnp.exp(s - m)
            denom = jnp.sum(p, axis=0, keepdims=True) + jnp.exp(sink - m)
            pn = (p * (1.0 / denom)).astype(BF16)
            out = lax.dot_general(pn, vb, (((0,), (0,)), ((), ())), preferred_element_type=F32)
            for r, h in enumerate(heads):
                o_ref[rows, h * HEAD_DIM:(h + 1) * HEAD_DIM] = out[r * blk:(r + 1) * blk].astype(o_ref.dtype)


def _rope_tables(seq):
    half = HEAD_DIM // 2
    inv = jnp.power(ROPE_THETA, -jnp.arange(half, dtype=F32) * 2.0 / HEAD_DIM)
    ang = jnp.arange(seq, dtype=F32)[:, None] * inv[None, :]
    cos, sin = jnp.cos(ang), jnp.sin(ang)
    cos2 = jnp.concatenate([cos, cos, cos, cos], axis=1)
    sin2 = jnp.concatenate([-sin, sin, -sin, sin], axis=1)
    return cos2, sin2


def _head_mean_matrix(width):
    i = jnp.arange(width)
    return jnp.where((i[:, None] // HEAD_DIM) == (i[None, :] // HEAD_DIM), 1.0 / HEAD_DIM, 0.0).astype(BF16)


def _swa_attention(q, k, v, q_g, k_g, sinks, bsz):
    t = q.shape[0]
    seq = t // bsz
    tile = min(ATTN_TILE, seq)
    nt = seq // tile
    bpt = tile // ATTN_BLOCK
    nb = seq // ATTN_BLOCK
    cos2, sin2 = _rope_tables(seq)
    qg = jnp.tile(q_g.astype(F32), ATTN_HEADS)[None]
    kg = jnp.tile(k_g.astype(F32), ATTN_KV_HEADS)[None]
    cur = lambda b, n: (b * nt + n, 0)
    prev = lambda b, n: (b * nb + jnp.maximum(n * bpt - 1, 0), 0)
    tcur = lambda b, n: (n, 0)
    tprev = lambda b, n: (jnp.maximum(n * bpt - 1, 0), 0)
    blk = ATTN_BLOCK
    return pl.pallas_call(
        _attn_kernel,
        grid=(bsz, nt),
        in_specs=[pl.BlockSpec((tile, ATTN_Q), cur),
                  pl.BlockSpec((tile, ATTN_KV), cur), pl.BlockSpec((blk, ATTN_KV), prev),
                  pl.BlockSpec((tile, ATTN_KV), cur), pl.BlockSpec((blk, ATTN_KV), prev),
                  pl.BlockSpec((tile, LANES), tcur), pl.BlockSpec((tile, LANES), tcur),
                  pl.BlockSpec((blk, LANES), tprev), pl.BlockSpec((blk, LANES), tprev),
                  _const_spec((1, ATTN_Q)), _const_spec((1, ATTN_KV)), _const_spec((1, ATTN_HEADS)),
                  _const_spec((ATTN_Q, ATTN_Q)), _const_spec((ATTN_KV, ATTN_KV))],
        out_specs=pl.BlockSpec((tile, ATTN_Q), cur),
        out_shape=jax.ShapeDtypeStruct((t, ATTN_Q), BF16),
        compiler_params=_cparams(("parallel", "parallel")),
        name="swa_attention",
    )(q, k, k, v, v, cos2, sin2, cos2, sin2, qg, kg, sinks.astype(F32)[None],
      _head_mean_matrix(ATTN_Q), _head_mean_matrix(ATTN_KV))


_CONV_TAIL = SSD_CHUNK


def _ssd_kernel(z_ref, xbc_ref, dt_ref, cw_ref, cb_ref, dtb_ref, alog_ref, dskip_ref, ng_ref, exp_ref, shift_ref,
                o_ref, tail, state, ybufs):
    @pl.when(pl.program_id(1) == 0)
    def _():
        tail[0] = jnp.zeros(tail.shape[1:], tail.dtype)
        state[...] = jnp.zeros_like(state)

    first = pl.program_id(1) * SSD_STEP_CHUNKS
    for i in range(SSD_STEP_CHUNKS):
        _ssd_chunk(i, (first + i) % 3, (first + i + 1) % 3, z_ref, xbc_ref, dt_ref, cw_ref, cb_ref, dtb_ref,
                   alog_ref, dskip_ref, ng_ref, exp_ref, shift_ref, o_ref, tail, state, ybufs.at[i])


def _ssd_chunk(i, prev_slot, next_slot, z_ref, xbc_ref, dt_ref, cw_ref, cb_ref, dtb_ref, alog_ref, dskip_ref,
               ng_ref, exp_ref, shift_ref, o_ref, tail, state, ybuf):
    ch = SSD_CHUNK
    rows = slice(i * ch, (i + 1) * ch)
    cur = xbc_ref[rows, :]
    shifted = _dot(shift_ref[...], jnp.concatenate([tail[prev_slot], cur], axis=0))
    tail[next_slot] = cur
    conv = cb_ref[...] + cw_ref[SSD_CONV - 1:SSD_CONV, :] * cur.astype(F32)
    for d in range(1, SSD_CONV):
        conv = conv + cw_ref[SSD_CONV - 1 - d:SSD_CONV - d, :] * shifted[(d - 1) * ch:d * ch]
    act = _silu(conv)
    xs = act[:, :SSD_WIDTH]
    bm = act[:, SSD_WIDTH:SSD_WIDTH + SSD_BC].astype(BF16)
    cm = act[:, SSD_WIDTH + SSD_BC:].astype(BF16)

    lane = lax.broadcasted_iota(jnp.int32, (ch, LANES), 1)
    xdt = dt_ref[rows, :] + dtb_ref[...]
    dt = jnp.maximum(xdt, 0.0) + jnp.log1p(jnp.exp(-jnp.abs(xdt)))
    dt = jnp.where(lane < SSD_HEADS, dt, 0.0)
    a = dt * (-jnp.exp(alog_ref[...]))
    row = lax.broadcasted_iota(jnp.int32, (ch, ch), 0)
    col = lax.broadcasted_iota(jnp.int32, (ch, ch), 1)
    causal = row >= col
    hi, mid, lo = _split3(a)
    tril = causal.astype(BF16)
    cs = _dot(tril, hi) + _dot(tril, mid) + _dot(tril, lo)
    cs_t = cs.T
    expand = exp_ref[...]
    dt_x = _dot_f32_lhs(dt, expand)
    cs_x = _dot_f32_lhs(cs, expand)
    cs_last_x = cs_x[ch - 1:ch, :]
    xdt_full = xs * dt_x
    in_decay = jnp.exp(cs_x)
    out_decay = jnp.exp(cs_last_x - cs_x)
    chunk_decay = jnp.exp(cs_last_x)
    xw = (xdt_full * out_decay).astype(BF16)
    xdt_b = xdt_full.astype(BF16)
    hpg = SSD_HEADS // SSD_GROUPS
    for g in range(SSD_GROUPS):
        gs = slice(g * SSD_STATE, (g + 1) * SSD_STATE)
        ws = slice(g * SSD_GROUP_W, (g + 1) * SSD_GROUP_W)
        bg = bm[:, gs]
        cg = cm[:, gs]
        cb = lax.dot_general(cg, bg, (((1,), (1,)), ((), ())), preferred_element_type=F32)
        prev = state[g]
        ybuf[:, ws] = _dot(cg, prev.astype(BF16)) * in_decay[:, ws]
        for h in range(hpg):
            hh = g * hpg + h
            seg = cs[:, hh:hh + 1] - cs_t[hh:hh + 1, :]
            lmat = jnp.where(causal, jnp.exp(seg), 0.0)
            hs = slice(hh * SSD_HEAD_DIM, (hh + 1) * SSD_HEAD_DIM)
            ybuf[:, hs] += _dot((cb * lmat).astype(BF16), xdt_b[:, hs])
        upd = lax.dot_general(bg, xw[:, ws], (((0,), (0,)), ((), ())), preferred_element_type=F32)
        state[g] = prev * chunk_decay[:, ws] + upd

    y = ybuf[...] + xs * dskip_ref[...]
    y = y * _silu(z_ref[rows, :].astype(F32))
    parts = []
    for g in range(SSD_GROUPS):
        ws = slice(g * SSD_GROUP_W, (g + 1) * SSD_GROUP_W)
        parts.append(_rms_scale(y[:, ws]))
    o_ref[rows, :] = (jnp.concatenate(parts, axis=1) * ng_ref[...]).astype(o_ref.dtype)


def _ssd_mixer(z, xbc, dt_raw, conv_w, conv_b, dt_bias, a_log, d_skip, norm_g, bsz):
    t = z.shape[0]
    seq = t // bsz
    nc = seq // SSD_CHUNK
    pad = lambda v: jnp.pad(v.astype(F32), (0, LANES - SSD_HEADS))[None]
    d_x = jnp.repeat(d_skip.astype(F32), SSD_HEAD_DIM)[None]
    hid = jnp.arange(SSD_WIDTH) // SSD_HEAD_DIM
    expand = (jnp.arange(LANES)[:, None] == hid[None, :]).astype(BF16)
    r = jnp.arange((SSD_CONV - 1) * SSD_CHUNK)
    pick = r % SSD_CHUNK + _CONV_TAIL - (r // SSD_CHUNK + 1)
    shift = (jnp.arange(_CONV_TAIL + SSD_CHUNK)[None, :] == pick[:, None]).astype(BF16)
    assert nc % SSD_STEP_CHUNKS == 0
    ns = nc // SSD_STEP_CHUNKS
    rows = SSD_STEP_CHUNKS * SSD_CHUNK
    blk = lambda b, c: (b * ns + c, 0)
    return pl.pallas_call(
        _ssd_kernel,
        grid=(bsz, ns),
        in_specs=[pl.BlockSpec((rows, SSD_WIDTH), blk),
                  pl.BlockSpec((rows, SSD_CONV_CH), blk),
                  pl.BlockSpec((rows, DT_PAD), blk),
                  _const_spec((SSD_CONV, SSD_CONV_CH)), _const_spec((1, SSD_CONV_CH)),
                  _const_spec((1, LANES)), _const_spec((1, LANES)),
                  _const_spec((1, SSD_WIDTH)), _const_spec((1, SSD_WIDTH)),
                  _const_spec((LANES, SSD_WIDTH)), _const_spec(shift.shape)],
        out_specs=pl.BlockSpec((rows, SSD_WIDTH), blk),
        out_shape=jax.ShapeDtypeStruct((t, SSD_WIDTH), BF16),
        scratch_shapes=[pltpu.VMEM((3, _CONV_TAIL, SSD_CONV_CH), BF16),
                        pltpu.VMEM((SSD_GROUPS, SSD_STATE, SSD_GROUP_W), F32),
                        pltpu.VMEM((SSD_STEP_CHUNKS, SSD_CHUNK, SSD_WIDTH), F32)],
        compiler_params=_cparams(("arbitrary", "arbitrary")),
        name="ssd_mixer",
    )(z, xbc, dt_raw, conv_w.astype(F32), conv_b.astype(F32)[None], pad(dt_bias), pad(a_log),
      d_x, norm_g.astype(F32)[None], expand, shift)


def _route(logits):
    lane = lax.broadcasted_iota(jnp.int32, logits.shape, 1).astype(F32)
    big = float(ROUTE_PAD)
    is_g = lane < MOE_GROUPS
    gl = jnp.where(is_g, logits, -jnp.inf)
    gmax = jnp.max(gl, axis=1, keepdims=True)
    gsel = jnp.min(jnp.where(gl == gmax, lane, big), axis=1, keepdims=True)
    pg = 1.0 / jnp.sum(jnp.where(is_g, jnp.exp(logits - gmax), 0.0), axis=1, keepdims=True)
    lo = MOE_GROUPS + MOE_PER_GROUP * gsel
    ev = jnp.where((lane >= lo) & (lane < lo + MOE_PER_GROUP), logits, -jnp.inf)
    m1 = jnp.max(ev, axis=1, keepdims=True)
    i1 = jnp.min(jnp.where(ev == m1, lane, big), axis=1, keepdims=True)
    ev2 = jnp.where(lane == i1, -jnp.inf, ev)
    m2 = jnp.max(ev2, axis=1, keepdims=True)
    i2 = jnp.min(jnp.where(ev2 == m2, lane, big), axis=1, keepdims=True)
    e21 = jnp.exp(m2 - m1)
    w1 = pg / (1.0 + e21)
    w2 = pg * e21 / (1.0 + e21)
    out = jnp.where(lane == 0, i1 - MOE_GROUPS, 0.0)
    out = jnp.where(lane == 1, i2 - MOE_GROUPS, out)
    out = jnp.where(lane == 2, w1, out)
    return jnp.where(lane == 3, w2, out)


def _merge_kernel(*refs, n_x):
    x_refs = refs[:n_x]
    (ys5_ref, u_ref, yb_ref, yc_ref, n1g_ref, wg_ref, bg_ref, d_ref, w1_ref, w2_ref, ps5_ref, pat_ref,
     pssd_ref, wo_ref, n2g_ref, wrh_ref, wrm_ref, br_ref, x1_o, h2_o, route_o) = refs[n_x:]
    x = _x_sum(x_refs)
    hb = (_rms_scale(x) * n1g_ref[...]).astype(BF16)
    ya = ys5_ref[...].astype(F32) + d_ref[...] * u_ref[...].astype(F32)
    yab = jax.nn.gelu(ya).astype(BF16)
    ya = _dot(yab, w1_ref[...]) * jax.nn.sigmoid(_dot(yab, w2_ref[...]))
    branches = ((ya.astype(BF16), ps5_ref), (yb_ref[...], pat_ref), (yc_ref[...], pssd_ref))
    merged = None
    for b, (yv, p_ref) in enumerate(branches):
        gate = jax.nn.sigmoid(_dot(hb, wg_ref[:, b * D_MODEL:(b + 1) * D_MODEL]) + bg_ref[b:b + 1, :])
        term = gate * _dot(yv, p_ref[...])
        merged = term if merged is None else merged + term
    x1 = x + _dot(merged.astype(BF16), wo_ref[...])
    x1_o[...] = x1
    h2 = _rms_scale(x1) * n2g_ref[...]
    _store_row_tiles(h2_o, 0, h2)
    h_hi = h2.astype(BF16)
    h_mid = (h2 - h_hi.astype(F32)).astype(BF16)
    logits = _dot(h_hi, wrh_ref[...]) + _dot(h_hi, wrm_ref[...]) + _dot(h_mid, wrh_ref[...]) + br_ref[...]
    route_o[...] = _route(logits)


def _merge(xs, t, ys5, u, yb, yc, lw):
    tm = min(TM_PROJ, t)
    row = lambda w: pl.BlockSpec((tm, w), lambda i: (i, 0))
    consts = [lw['n1g'], lw['w_gate'], lw['b_gate'], lw['s5_d'], lw['glu_w1'], lw['glu_w2'], lw['p_s5'],
              lw['p_attn'], lw['p_ssd'], lw['w_out'], lw['n2g'], lw['w_router_hi'], lw['w_router_mid'],
              lw['b_router']]
    x_arrs, x_specs = _x_parts(xs, tm)
    return pl.pallas_call(
        functools.partial(_merge_kernel, n_x=len(x_arrs)),
        grid=(t // tm,),
        in_specs=x_specs + [row(S5_WIDTH), row(S5_WIDTH), row(ATTN_Q), row(SSD_WIDTH)]
                 + [_const_spec(c.shape) for c in consts],
        out_specs=[row(D_MODEL), pl.BlockSpec((tm * ROW_TILE, LANES), lambda i: (i, 0)), row(ROUTE_PAD)],
        out_shape=[jax.ShapeDtypeStruct((t, D_MODEL), F32), jax.ShapeDtypeStruct((t * ROW_TILE, LANES), F32),
                   jax.ShapeDtypeStruct((t, ROUTE_PAD), F32)],
        compiler_params=_cparams(("parallel",)),
        name="merge_router",
    )(*x_arrs, ys5, u, yb, yc, *consts)


def _moe_plan(route, t):
    n_exp = MOE_EXPERTS
    e = route[:, 0:2].astype(jnp.int32).reshape(-1)
    onehot = (e[:, None] == jnp.arange(n_exp, dtype=jnp.int32)[None, :]).astype(jnp.int32)
    csum = jnp.cumsum(onehot, axis=0)
    rank = jnp.sum(csum * onehot, axis=1) - 1
    counts = csum[-1]
    padded = ((counts + TM_G - 1) // TM_G) * TM_G
    pend = jnp.cumsum(padded)
    dest = (pend - padded)[e] + rank
    n_tiles = -(-(2 * t + n_exp * (TM_G - 1)) // TM_G)
    n_rows = n_tiles * TM_G
    pair = jnp.full((n_rows,), -1, jnp.int32).at[dest].set(jnp.arange(2 * t, dtype=jnp.int32),
                                                            unique_indices=True)
    valid = pair >= 0
    tok, k = pair // 2, pair % 2
    src = jnp.where(valid, tok, 0)
    spare = t + jnp.arange(n_rows, dtype=jnp.int32) % _MOE_SPARE
    dst = jnp.where(valid, k * (t + _MOE_SPARE) + tok, spare)
    tile_start = jnp.arange(n_tiles, dtype=jnp.int32) * TM_G
    tile_e = jnp.sum((pend[None, :] <= tile_start[:, None]).astype(jnp.int32), axis=1)
    n_used = pend[-1:] // TM_G
    tile_e = jnp.minimum(tile_e, n_exp - 1)
    tile_e = jnp.minimum(tile_e, tile_e[n_used[0] - 1])
    real = jnp.sum(valid.reshape(n_tiles, TM_G).astype(jnp.int32), axis=1)
    n_issue = ((real + _DMA_GROUP - 1) // _DMA_GROUP) * _DMA_GROUP
    return (src.reshape(n_tiles, 1, TM_G), dst.reshape(n_tiles, 1, TM_G), tile_e, n_used.astype(jnp.int32),
            n_issue.astype(jnp.int32))


_MOE_SPARE = 2 * TM_G
_DMA_GROUP = 64
assert TM_G % _DMA_GROUP == 0


def _gmm_kernel(te_ref, nu_ref, ni_ref, src_ref, srcn_ref, dst_ref, h2_hbm, wg_ref, wu_ref, wd_ref, out_hbm,
                xbuf, ybuf, gsem, ssem):
    del te_ref
    i = pl.program_id(0)
    n_grid = pl.num_programs(0)
    last = nu_ref[0] - 1
    slot = i % 2
    cnt_cur = ni_ref[i]
    cnt_next = ni_ref[jnp.minimum(i + 1, n_grid - 1)]

    tile_rows = TM_G * ROW_TILE
    groups = TM_G // _DMA_GROUP

    def row_tile(ref, s, r):
        return ref.at[pl.ds((s * TM_G + r) * ROW_TILE, ROW_TILE), :]

    def slot_rows(ref, s):
        return ref.at[pl.ds(s * tile_rows, tile_rows), :]

    def group_rows(ref, s, q):
        return ref.at[pl.ds((s * TM_G + q * _DMA_GROUP) * ROW_TILE, _DMA_GROUP * ROW_TILE), :]

    def for_groups(cnt, body):
        for q in range(groups):
            pl.when(cnt > q * _DMA_GROUP)(functools.partial(body, q))

    def gather_start(idx_ref, s, cnt):
        def body(q):
            for r in range(q * _DMA_GROUP, (q + 1) * _DMA_GROUP):
                src = h2_hbm.at[pl.ds(pl.multiple_of(idx_ref[0, 0, r], ROW_TILE), ROW_TILE), :]
                pltpu.make_async_copy(src, row_tile(xbuf, s, r), gsem.at[s]).start(priority=r % 2)
        for_groups(cnt, body)

    def gather_wait(s, cnt):
        def body(q):
            pltpu.make_async_copy(h2_hbm.at[pl.ds(0, _DMA_GROUP * ROW_TILE), :], group_rows(xbuf, s, q),
                                  gsem.at[s]).wait()
        for_groups(cnt, body)

    def scatter_start(s, cnt):
        def body(q):
            for r in range(q * _DMA_GROUP, (q + 1) * _DMA_GROUP):
                dst = out_hbm.at[pl.ds(pl.multiple_of(dst_ref[0, 0, r], ROW_TILE), ROW_TILE), :]
                pltpu.make_async_copy(row_tile(ybuf, s, r), dst, ssem.at[s]).start(priority=r % 2)
        for_groups(cnt, body)

    def scatter_wait(s, cnt):
        def body(q):
            pltpu.make_async_copy(group_rows(ybuf, s, q), out_hbm.at[pl.ds(0, _DMA_GROUP * ROW_TILE), :],
                                  ssem.at[s]).wait()
        for_groups(cnt, body)

    @pl.when(i == 0)
    def _():
        xbuf[...] = jnp.zeros_like(xbuf)
        gather_start(src_ref, 0, cnt_cur)
        ybuf[...] = jnp.zeros_like(ybuf)
        plane = out_hbm.shape[0] // 2
        spare0 = plane - _MOE_SPARE * ROW_TILE
        fills = [pltpu.make_async_copy(slot_rows(ybuf, s),
                                       out_hbm.at[pl.ds(k * plane + spare0 + s * tile_rows, tile_rows), :],
                                       ssem.at[s]) for k in range(2) for s in range(2)]
        for f in fills:
            f.start()
        for f in fills:
            f.wait()

    @pl.when(i <= last)
    def _():
        gather_wait(slot, cnt_cur)

        @pl.when(i >= 2)
        def _():
            scatter_wait(slot, ni_ref[jnp.maximum(i - 2, 0)])

        gather_start(srcn_ref, 1 - slot, cnt_next)

        xb = _load_row_tiles(xbuf, slot * TM_G, TM_G).astype(BF16)
        hid = _silu(_dot(xb, wg_ref[0, 0].astype(BF16))) * _dot(xb, wu_ref[0, 0].astype(BF16))
        _store_row_tiles(ybuf, slot * TM_G, _dot(hid.astype(BF16), wd_ref[0, 0].astype(BF16)))
        scatter_start(slot, cnt_cur)

        @pl.when(i == last)
        def _():
            scatter_wait(slot, cnt_cur)
            gather_wait(1 - slot, cnt_next)

        @pl.when((i == last) & (i >= 1))
        def _():
            scatter_wait(1 - slot, ni_ref[jnp.maximum(i - 1, 0)])


def _moe_sparse(h2, route, w_gate, w_up, w_down, l):
    t = h2.shape[0] // ROW_TILE
    src, dst, tile_e, n_used, n_issue = _moe_plan(route, t)
    src, dst = src * ROW_TILE, dst * ROW_TILE
    n_tiles = src.shape[0]
    smem = lambda imap: pl.BlockSpec((1, 1, TM_G), imap, memory_space=pltpu.SMEM)
    grid_spec = pltpu.PrefetchScalarGridSpec(
        num_scalar_prefetch=3,
        grid=(n_tiles,),
        in_specs=[smem(lambda i, te, nu, ni: (i, 0, 0)),
                  smem(lambda i, te, nu, ni: (jnp.minimum(i + 1, n_tiles - 1), 0, 0)),
                  smem(lambda i, te, nu, ni: (i, 0, 0)),
                  pl.BlockSpec(memory_space=pl.ANY),
                  pl.BlockSpec((1, 1, D_MODEL, MOE_FF), lambda i, te, nu, ni: (l, te[i], 0, 0)),
                  pl.BlockSpec((1, 1, D_MODEL, MOE_FF), lambda i, te, nu, ni: (l, te[i], 0, 0)),
                  pl.BlockSpec((1, 1, MOE_FF, D_MODEL), lambda i, te, nu, ni: (l, te[i], 0, 0))],
        out_specs=pl.BlockSpec(memory_space=pl.ANY),
        scratch_shapes=[pltpu.VMEM((2 * TM_G * ROW_TILE, LANES), F32), pltpu.VMEM((2 * TM_G * ROW_TILE, LANES), F32),
                        pltpu.SemaphoreType.DMA((2,)), pltpu.SemaphoreType.DMA((2,))])
    plane = (t + _MOE_SPARE) * ROW_TILE
    out = pl.pallas_call(
        _gmm_kernel,
        grid_spec=grid_spec,
        out_shape=jax.ShapeDtypeStruct((2 * plane, LANES), F32),
        compiler_params=_cparams(("arbitrary",)),
        name="moe_gmm",
    )(tile_e, n_used, n_issue, src, src, dst, h2, w_gate, w_up, w_down)
    return out.reshape(2, plane, LANES)


def _sum_kernel(*refs):
    refs[-1][...] = _x_sum(refs[:-1])


def _residual_sum(xs, t):
    tm = min(TM_PROJ, t)
    x_arrs, x_specs = _x_parts(xs, tm)
    return pl.pallas_call(
        _sum_kernel,
        grid=(t // tm,),
        in_specs=x_specs,
        out_specs=pl.BlockSpec((tm, D_MODEL), lambda i: (i, 0)),
        out_shape=jax.ShapeDtypeStruct((t, D_MODEL), F32),
        compiler_params=_cparams(("parallel",)),
        name="residual_sum",
    )(*x_arrs)


def _cast_kernel(*refs):
    n = len(refs) // 2
    for x_ref, o_ref in zip(refs[:n], refs[n:]):
        o_ref[...] = x_ref[...].astype(o_ref.dtype)


def _cast_layer_bf16(ws, l):
    halves = 2
    in_specs = [pl.BlockSpec((None, w.shape[1] // halves, w.shape[2]), lambda i: (l, i, 0)) for w in ws]
    out_specs = [pl.BlockSpec((w.shape[1] // halves, w.shape[2]), lambda i: (i, 0)) for w in ws]
    return pl.pallas_call(
        _cast_kernel,
        grid=(halves,),
        in_specs=in_specs,
        out_specs=out_specs,
        out_shape=[jax.ShapeDtypeStruct(w.shape[1:], BF16) for w in ws],
        compiler_params=_cparams(("parallel",)),
        name="cast_weights",
    )(*[w.astype(F32) for w in ws])


_GATE_SHIFT = N_MIX % LANES
assert _GATE_SHIFT % SUBLANES == 0


def _gate_w_kernel(a_ref, b_ref, o_ref):
    full = jnp.concatenate([a_ref[...], b_ref[...]], axis=0)
    o_ref[...] = full[_GATE_SHIFT:_GATE_SHIFT + LANES, :].T.astype(o_ref.dtype)


def _gate_weights(w_in_t, l):
    base = (N_MIX - _GATE_SHIFT) // LANES
    return pl.pallas_call(
        _gate_w_kernel,
        grid=(N_BRANCH * D_MODEL // LANES,),
        in_specs=[pl.BlockSpec((None, LANES, D_MODEL), lambda k: (l, base + k, 0)),
                  pl.BlockSpec((None, LANES, D_MODEL), lambda k: (l, base + k + 1, 0))],
        out_specs=pl.BlockSpec((D_MODEL, LANES), lambda k: (0, k)),
        out_shape=jax.ShapeDtypeStruct((D_MODEL, N_BRANCH * D_MODEL), BF16),
        compiler_params=_cparams(("parallel",)),
        name="gate_weights",
    )(w_in_t, w_in_t)


def _layer_weights(l, p):
    glu_w1, glu_w2, p_s5, p_attn, p_ssd, w_out = _cast_layer_bf16(
        [p['s5_glu_w1'], p['s5_glu_w2'], p['p_s5'], p['p_attn'], p['p_ssd'], p['w_out']], l)
    w_router = jnp.concatenate([p['w_router_group'][l], p['w_router_expert'][l]], axis=1).astype(F32)
    npad = ROUTE_PAD - w_router.shape[1]
    b_router = jnp.concatenate([p['b_router_group'][l], p['b_router_expert'][l]]).astype(F32)
    w_router = jnp.pad(w_router, ((0, 0), (0, npad)))
    w_router_hi = w_router.astype(BF16)
    return dict(
        n1g=p['norm1_g'][l].astype(F32)[None],
        w_gate=_gate_weights(p['w_in_t'], l), b_gate=p['b_gate'][l].astype(F32),
        s5_d=p['s5_d'][l].astype(F32)[None],
        glu_w1=glu_w1, glu_w2=glu_w2, p_s5=p_s5, p_attn=p_attn, p_ssd=p_ssd, w_out=w_out,
        n2g=p['norm2_g'][l].astype(F32)[None],
        w_router_hi=w_router_hi, w_router_mid=(w_router - w_router_hi.astype(F32)).astype(BF16),
        b_router=jnp.pad(b_router, (0, npad))[None])


def _layer(xs, t, l, p, s5_tabs, bsz):
    lw = _layer_weights(l, p)
    u, q, k, v, z, xbc, dt_raw, *x_sum = _inproj(xs, t, lw['n1g'], p['w_in_t'], l)
    if x_sum:
        xs = x_sum
    ys5 = _s5_mixer(u, s5_tabs, bsz)
    yb = _swa_attention(q, k, v, p['q_norm_g'][l], p['k_norm_g'][l], p['attn_sinks'][l], bsz)
    yc = _ssd_mixer(z, xbc, dt_raw, p['ssd_conv_w'][l], p['ssd_conv_b'][l], p['ssd_dt_bias'][l],
                    p['ssd_a_log'][l], p['ssd_d'][l], p['ssd_norm_g'][l], bsz)
    x1, h2, route = _merge(xs, t, ys5, u, yb, yc, lw)
    moe = _moe_sparse(h2, route, p['w_exp_gate'], p['w_exp_up'], p['w_exp_down'], l)
    return [x1, route, moe]


def kernel(x, norm1_g, w_in, b_gate, s5_lambda_re, s5_lambda_im, s5_b_re, s5_b_im, s5_c_re, s5_c_im, s5_d, s5_log_dt, s5_glu_w1, s5_glu_w2, q_norm_g, k_norm_g, attn_sinks, ssd_conv_w, ssd_conv_b, ssd_dt_bias, ssd_a_log, ssd_d, ssd_norm_g, p_s5, p_attn, p_ssd, w_out, norm2_g, w_router_group, b_router_group, w_router_expert, b_router_expert, w_exp_gate, w_exp_up, w_exp_down):
    p = dict(norm1_g=norm1_g, w_in=w_in, b_gate=b_gate, s5_lambda_re=s5_lambda_re, s5_lambda_im=s5_lambda_im,
             s5_b_re=s5_b_re, s5_b_im=s5_b_im, s5_c_re=s5_c_re, s5_c_im=s5_c_im, s5_d=s5_d,
             s5_log_dt=s5_log_dt, s5_glu_w1=s5_glu_w1, s5_glu_w2=s5_glu_w2, q_norm_g=q_norm_g,
             k_norm_g=k_norm_g, attn_sinks=attn_sinks, ssd_conv_w=ssd_conv_w, ssd_conv_b=ssd_conv_b,
             ssd_dt_bias=ssd_dt_bias, ssd_a_log=ssd_a_log, ssd_d=ssd_d, ssd_norm_g=ssd_norm_g, p_s5=p_s5,
             p_attn=p_attn, p_ssd=p_ssd, w_out=w_out, norm2_g=norm2_g, w_router_group=w_router_group,
             b_router_group=b_router_group, w_router_expert=w_router_expert, b_router_expert=b_router_expert,
             w_exp_gate=w_exp_gate, w_exp_up=w_exp_up, w_exp_down=w_exp_down)
    p['w_in_t'] = jnp.swapaxes(w_in.astype(F32), 1, 2)
    bsz, seq, dm = x.shape
    depth = w_in.shape[0]
    t = bsz * seq
    xs = [x.reshape(t, dm)]
    for l in range(depth):
        s5_tabs = _s5_tables(s5_lambda_re[l], s5_lambda_im[l], s5_b_re[l], s5_b_im[l], s5_c_re[l], s5_c_im[l],
                             s5_log_dt[l])
        xs = _layer(xs, t, l, p, s5_tabs, bsz)
    return _residual_sum(xs, t).reshape(bsz, seq, dm)
```

```python
import functools
import math

import jax
import jax.numpy as jnp
from jax import lax
from jax.experimental import pallas as pl
from jax.experimental.pallas import tpu as pltpu

F32 = jnp.float32
BF16 = jnp.bfloat16

D_MODEL = 1024
NORM_EPS = 1e-6
S5_WIDTH = 512
S5_GROUP = 16
S5_GROUPS = 32
S5_STATE = 64
HEAD_DIM = 64
ATTN_HEADS = 8
ATTN_KV_HEADS = 2
ATTN_REP = ATTN_HEADS // ATTN_KV_HEADS
ATTN_Q = ATTN_HEADS * HEAD_DIM
ATTN_KV = ATTN_KV_HEADS * HEAD_DIM
ATTN_BLOCK = 128
ROPE_THETA = 10000.0
SSD_WIDTH = 1024
SSD_HEAD_DIM = 64
SSD_HEADS = 16
SSD_GROUPS = 2
SSD_STATE = 64
SSD_CONV = 4
SSD_CHUNK = 128
SSD_BC = SSD_GROUPS * SSD_STATE
SSD_CONV_CH = SSD_WIDTH + 2 * SSD_BC
SSD_GROUP_W = SSD_WIDTH // SSD_GROUPS
N_BRANCH = 3
MOE_GROUPS = 4
MOE_PER_GROUP = 8
MOE_EXPERTS = 32
MOE_FF = 512
N_MIX = S5_WIDTH + ATTN_Q + 2 * ATTN_KV + SSD_WIDTH + SSD_CONV_CH + SSD_HEADS

LANES = 128
SUBLANES = 8
VMEM_LIMIT_BYTES = 56 * 1024 * 1024

S5_CHUNK = 16
S5_TG = 8
ATTN_TILE = 512
TM_PROJ = 512
TM_G = 256
SSD_STEP_CHUNKS = 2
DT_PAD = LANES
N_MIX_PAD = N_MIX - SSD_HEADS + DT_PAD
ROUTE_PAD = LANES


def _cparams(semantics):
    return pltpu.CompilerParams(dimension_semantics=semantics, vmem_limit_bytes=VMEM_LIMIT_BYTES)


def _const_spec(shape):
    zeros = (0,) * len(shape)
    return pl.BlockSpec(shape, lambda *_: zeros, pipeline_mode=pl.Buffered(1))


def _dot(a, b):
    return jnp.dot(a, b, preferred_element_type=F32)


def _split3(a):
    hi = a.astype(BF16)
    r1 = a - hi.astype(F32)
    mid = r1.astype(BF16)
    lo = (r1 - mid.astype(F32)).astype(BF16)
    return hi, mid, lo


def _dot_f32_lhs(a, b_bf16):
    hi, mid, lo = _split3(a)
    return _dot(hi, b_bf16) + _dot(mid, b_bf16) + _dot(lo, b_bf16)


def _rms_scale(x):
    return x * lax.rsqrt(jnp.mean(x * x, axis=-1, keepdims=True) + NORM_EPS)


def _silu(x):
    return x * jax.nn.sigmoid(x)


_OFF_U = 0
_OFF_Q = _OFF_U + S5_WIDTH
_OFF_K = _OFF_Q + ATTN_Q
_OFF_V = _OFF_K + ATTN_KV
_OFF_Z = _OFF_V + ATTN_KV
_OFF_XBC = _OFF_Z + SSD_WIDTH
_OFF_DT = _OFF_XBC + SSD_CONV_CH


ROW_TILE = D_MODEL // LANES


def _store_row_tiles(ref, start, val):
    n = val.shape[0]
    for j in range(ROW_TILE):
        ref[pl.ds(start * ROW_TILE + j, n, stride=ROW_TILE), :] = val[:, j * LANES:(j + 1) * LANES]


def _load_row_tiles(ref, start, n):
    return jnp.concatenate([ref[pl.ds(start * ROW_TILE + j, n, stride=ROW_TILE), :] for j in range(ROW_TILE)],
                           axis=1)


def _x_parts(xs, tm):
    row = lambda w: pl.BlockSpec((tm, w), lambda i: (i, 0))
    if len(xs) == 1:
        return list(xs), [row(D_MODEL)]
    x1, route, moe = xs
    plane = lambda k: pl.BlockSpec((None, tm * ROW_TILE, LANES), lambda i: (k, i, 0))
    return [x1, route, moe, moe], [row(D_MODEL), row(ROUTE_PAD), plane(0), plane(1)]


def _x_sum(x_refs):
    if len(x_refs) == 1:
        return x_refs[0][...]
    x1_ref, route_ref, m0_ref, m1_ref = x_refs
    r = route_ref[...]
    n = x1_ref.shape[0]
    return x1_ref[...] + r[:, 2:3] * _load_row_tiles(m0_ref, 0, n) + r[:, 3:4] * _load_row_tiles(m1_ref, 0, n)


def _inproj_kernel(*refs, n_x):
    x_refs, (g_ref, w_ref), outs, wb = refs[:n_x], refs[n_x:n_x + 2], refs[n_x + 2:-1], refs[-1]

    @pl.when(pl.program_id(0) == 0)
    def _():
        for j in range(N_MIX_PAD // LANES):
            wb[:, j * LANES:(j + 1) * LANES] = w_ref[j * LANES:(j + 1) * LANES, :].T.astype(BF16)

    x = _x_sum(x_refs)
    hb = (_rms_scale(x) * g_ref[...]).astype(BF16)
    offs = (_OFF_U, _OFF_Q, _OFF_K, _OFF_V, _OFF_Z, _OFF_XBC, _OFF_DT)
    for o_ref, off in zip(outs, offs):
        width = o_ref.shape[1]
        o_ref[...] = _dot(hb, wb[:, off:off + width]).astype(o_ref.dtype)
    if len(outs) > len(offs):
        outs[-1][...] = x


def _inproj(xs, t, g, w_in, l):
    tm = min(TM_PROJ, t)
    widths = (S5_WIDTH, ATTN_Q, ATTN_KV, ATTN_KV, SSD_WIDTH, SSD_CONV_CH, DT_PAD)
    dtypes = (F32, BF16, BF16, BF16, BF16, BF16, F32)
    if len(xs) > 1:
        widths, dtypes = widths + (D_MODEL,), dtypes + (F32,)
    x_arrs, x_specs = _x_parts(xs, tm)
    w_spec = pl.BlockSpec((None, N_MIX_PAD, D_MODEL), lambda i: (l, 0, 0), pipeline_mode=pl.Buffered(1))
    return pl.pallas_call(
        functools.partial(_inproj_kernel, n_x=len(x_arrs)),
        grid=(t // tm,),
        in_specs=x_specs + [_const_spec((1, D_MODEL)), w_spec],
        out_specs=[pl.BlockSpec((tm, w), lambda i: (i, 0)) for w in widths],
        out_shape=[jax.ShapeDtypeStruct((t, w), d) for w, d in zip(widths, dtypes)],
        scratch_shapes=[pltpu.VMEM((D_MODEL, N_MIX_PAD), BF16)],
        compiler_params=_cparams(("arbitrary",)),
        name="inproj",
    )(*x_arrs, g, w_in)


def _s5_tables(lam_re, lam_im, b_re, b_im, c_re, c_im, log_dt):
    hp = lax.Precision.HIGHEST
    g_n, p_n = lam_re.shape
    c_n = b_re.shape[-1]
    t_n = S5_CHUNK
    lr, li = lam_re.astype(F32), lam_im.astype(F32)
    dt = jnp.exp(log_dt.astype(F32))[:, None]
    mag = jnp.exp(lr * dt)
    ab_re = mag * jnp.cos(li * dt)
    ab_im = mag * jnp.sin(li * dt)
    nr = ab_re - 1.0
    den = lr * lr + li * li
    f_re = (nr * lr + ab_im * li) / den
    f_im = (ab_im * lr - nr * li) / den
    br, bi = b_re.astype(F32), b_im.astype(F32)
    bb_re = f_re[..., None] * br - f_im[..., None] * bi
    bb_im = f_re[..., None] * bi + f_im[..., None] * br
    j = jnp.arange(t_n + 1, dtype=F32)[:, None, None]
    pmag = jnp.exp(lr * dt * j)
    ang = li * dt * j
    p_re = pmag * jnp.cos(ang)
    p_im = pmag * jnp.sin(ang)
    cr, ci = c_re.astype(F32), c_im.astype(F32)
    ca_re = cr[None] * p_re[:, :, None, :] - ci[None] * p_im[:, :, None, :]
    ca_im = cr[None] * p_im[:, :, None, :] + ci[None] * p_re[:, :, None, :]
    q_re = p_re[t_n - 1 - jnp.arange(t_n)]
    q_im = p_im[t_n - 1 - jnp.arange(t_n)]
    bs_re = q_re[..., None] * bb_re[None] - q_im[..., None] * bb_im[None]
    bs_im = q_re[..., None] * bb_im[None] + q_im[..., None] * bb_re[None]

    nt = g_n // S5_TG

    def in_strip(a):
        a = a.reshape(t_n, nt, S5_TG, p_n, c_n).transpose(1, 0, 4, 2, 3)
        return a.reshape(nt, t_n, c_n, S5_TG * p_n)

    def out_strip(a):
        a = a.reshape(t_n + 1, nt, S5_TG, c_n, p_n).transpose(1, 4, 0, 2, 3)
        return a.reshape(nt, p_n, (t_n + 1) * S5_TG * c_n)

    return dict(
        bs_re=in_strip(bs_re), bs_im=in_strip(bs_im), co_re=out_strip(ca_re), co_im=out_strip(-ca_im),
        at_re=p_re[t_n].reshape(nt, 1, S5_TG * p_n), at_im=p_im[t_n].reshape(nt, 1, S5_TG * p_n))


_S5_XW = S5_CHUNK * LANES
_S5_SW = S5_TG * S5_STATE


def _group_block(strip, lanes_per_group):
    rpg = strip.shape[0]
    full = jnp.concatenate([strip] * S5_TG, axis=0)
    row_g = lax.broadcasted_iota(jnp.int32, full.shape, 0) // rpg
    lane_g = (lax.broadcasted_iota(jnp.int32, full.shape, 1) % (S5_TG * lanes_per_group)) // lanes_per_group
    return jnp.where(row_g == lane_g, full, 0.0)


def _dot_hi_mid(a, b):
    a_hi, a_mid, _ = _split3(a)
    b_hi, b_mid, _ = _split3(b)
    return _dot(a_hi, b_hi) + _dot(a_hi, b_mid) + _dot(a_mid, b_hi)


def _s5_build_tables(bre_ref, bim_ref, cre_ref, cim_ref, big, bsre, bsim, core, coim):
    lag = lambda ref, j: _group_block(ref[0, :, j * LANES:(j + 1) * LANES], S5_GROUP)
    b0_re = _group_block(bre_ref[0, S5_CHUNK - 1], S5_STATE)
    b0_im = _group_block(bim_ref[0, S5_CHUNK - 1], S5_STATE)
    blocks = [(_dot_hi_mid(b0_re, lag(cre_ref, j)) + _dot_hi_mid(b0_im, lag(cim_ref, j))).astype(BF16)
              for j in range(S5_CHUNK)]
    zero = jnp.zeros((LANES, LANES), BF16)
    for s in range(S5_CHUNK):
        for t in range(S5_CHUNK):
            big[s * LANES:(s + 1) * LANES, t * LANES:(t + 1) * LANES] = blocks[t - s] if t >= s else zero
    for src, dst in ((bre_ref, bsre), (bim_ref, bsim)):
        for s in range(S5_CHUNK):
            dst[s * LANES:(s + 1) * LANES, :] = _group_block(src[0, s], S5_STATE).astype(BF16)
    for src, dst in ((cre_ref, core), (cim_ref, coim)):
        for t in range(S5_CHUNK):
            dst[:, t * LANES:(t + 1) * LANES] = lag(src, t + 1).astype(BF16)


def _s5_kernel(u_ref, bre_ref, bim_ref, cre_ref, cim_ref, are_ref, aim_ref, y_ref,
               big, bsre, bsim, core, coim, sre, sim, hre, him):
    @pl.when(pl.program_id(1) == 0)
    def _():
        _s5_build_tables(bre_ref, bim_ref, cre_ref, cim_ref, big, bsre, bsim, core, coim)

    nck = u_ref.shape[0] // S5_CHUNK
    xcat = jnp.concatenate([u_ref[pl.ds(s, nck, stride=S5_CHUNK), :].astype(BF16) for s in range(S5_CHUNK)],
                           axis=1)
    sre[...] = _dot(xcat, bsre[...])
    sim[...] = _dot(xcat, bsim[...])
    a_re = are_ref[0]
    a_im = aim_ref[0]

    def step(i, carry):
        h_re, h_im = carry
        base = pl.multiple_of(i * SUBLANES, SUBLANES)
        s_re = sre[pl.ds(base, SUBLANES), :]
        s_im = sim[pl.ds(base, SUBLANES), :]
        ent_re, ent_im = [], []
        for j in range(SUBLANES):
            ent_re.append(h_re)
            ent_im.append(h_im)
            h_re, h_im = (a_re * h_re - a_im * h_im + s_re[j:j + 1], a_re * h_im + a_im * h_re + s_im[j:j + 1])
        hre[pl.ds(base, SUBLANES), :] = jnp.concatenate(ent_re, axis=0)
        him[pl.ds(base, SUBLANES), :] = jnp.concatenate(ent_im, axis=0)
        return h_re, h_im

    zero = jnp.zeros((1, _S5_SW), F32)
    lax.fori_loop(0, nck // SUBLANES, step, (zero, zero))
    y = (_dot(xcat, big[...]) + _dot(hre[...].astype(BF16), core[...])
         + _dot(him[...].astype(BF16), coim[...]))
    for t in range(S5_CHUNK):
        y_ref[pl.ds(t, nck, stride=S5_CHUNK), :] = y[:, t * LANES:(t + 1) * LANES]


def _s5_mixer(u, tabs, bsz):
    t = u.shape[0]
    seq = t // bsz
    nck = seq // S5_CHUNK
    nt = S5_GROUPS // S5_TG
    strip = lambda a: pl.BlockSpec((1,) + a.shape[1:], lambda x, b: (x,) + (0,) * (a.ndim - 1))
    strips = [tabs['bs_re'], tabs['bs_im'], tabs['co_re'], tabs['co_im'], tabs['at_re'], tabs['at_im']]
    return pl.pallas_call(
        _s5_kernel,
        grid=(nt, bsz),
        in_specs=[pl.BlockSpec((seq, LANES), lambda x, b: (b, x))] + [strip(a) for a in strips],
        out_specs=pl.BlockSpec((seq, LANES), lambda x, b: (b, x)),
        out_shape=jax.ShapeDtypeStruct((t, S5_WIDTH), F32),
        scratch_shapes=[pltpu.VMEM((_S5_XW, _S5_XW), BF16),
                        pltpu.VMEM((_S5_XW, _S5_SW), BF16), pltpu.VMEM((_S5_XW, _S5_SW), BF16),
                        pltpu.VMEM((_S5_SW, _S5_XW), BF16), pltpu.VMEM((_S5_SW, _S5_XW), BF16)]
                       + [pltpu.VMEM((nck, _S5_SW), F32)] * 4,
        compiler_params=_cparams(("arbitrary", "arbitrary")),
        name="s5_mixer",
    )(u, *strips)


def _swap_rope_halves(y):
    w = y.shape[1]
    lane = lax.broadcasted_iota(jnp.int32, y.shape, 1)
    lower = (lane % HEAD_DIM) < (HEAD_DIM // 2)
    return jnp.where(lower, pltpu.roll(y, w - HEAD_DIM // 2, 1), pltpu.roll(y, HEAD_DIM // 2, 1))


def _norm_rope(x, gain, head_mean, cos2, sin2):
    reps = x.shape[1] // LANES
    ms = _dot_f32_lhs(x * x, head_mean)
    y = x * lax.rsqrt(ms + NORM_EPS) * gain
    if reps > 1:
        cos2 = jnp.concatenate([cos2] * reps, axis=1)
        sin2 = jnp.concatenate([sin2] * reps, axis=1)
    return y * cos2 + _swap_rope_halves(y) * sin2


def _attn_kernel(q_ref, kc_ref, kp_ref, vc_ref, vp_ref, cosc_ref, sinc_ref, cosp_ref, sinp_ref,
                 qg_ref, kg_ref, sink_ref, hmq_ref, hmk_ref, o_ref):
    seq_start = pl.program_id(1) == 0
    blk = ATTN_BLOCK
    nblk = q_ref.shape[0] // blk
    q = _norm_rope(q_ref[...].astype(F32), qg_ref[...], hmq_ref[...], cosc_ref[...], sinc_ref[...])
    q = (q * (HEAD_DIM ** -0.5)).astype(BF16)
    kc = _norm_rope(kc_ref[...].astype(F32), kg_ref[...], hmk_ref[...], cosc_ref[...], sinc_ref[...]).astype(BF16)
    kp = _norm_rope(kp_ref[...].astype(F32), kg_ref[...], hmk_ref[...], cosp_ref[...], sinp_ref[...]).astype(BF16)
    k_all = jnp.concatenate([kp, kc], axis=0)
    v_all = jnp.concatenate([vp_ref[...], vc_ref[...]], axis=0)
    shape = (2 * blk, ATTN_REP * blk)
    kj = lax.broadcasted_iota(jnp.int32, shape, 0)
    qi = lax.broadcasted_iota(jnp.int32, shape, 1) % blk + blk
    band = (kj <= qi) & (qi - kj < blk)
    band_first = band & ((kj >= blk) | jnp.logical_not(seq_start))
    sinks = sink_ref[...]
    for j in range(ATTN_KV_HEADS):
        sl = slice(j * HEAD_DIM, (j + 1) * HEAD_DIM)
        heads = [j * ATTN_REP + r for r in range(ATTN_REP)]
        sink = jnp.concatenate([jnp.broadcast_to(sinks[:, h:h + 1], (1, blk)) for h in heads], axis=1)
        for n in range(nblk):
            rows = slice(n * blk, (n + 1) * blk)
            kb = k_all[n * blk:(n + 2) * blk, sl]
            vb = v_all[n * blk:(n + 2) * blk, sl]
            q4 = jnp.concatenate([q[rows, h * HEAD_DIM:(h + 1) * HEAD_DIM] for h in heads], axis=0)
            s = lax.dot_general(kb, q4, (((1,), (1,)), ((), ())), preferred_element_type=F32)
            s = jnp.where(band_first if n == 0 else band, s, -jnp.inf)
            m = jnp.maximum(jnp.max(s, axis=0, keepdims=True), sink)
            p = jnp.exp(s - m)
            denom = jnp.sum(p, axis=0, keepdims=True) + jnp.exp(sink - m)
            pn = (p * (1.0 / denom)).astype(BF16)
            out = lax.dot_general(pn, vb, (((0,), (0,)), ((), ())), preferred_element_type=F32)
            for r, h in enumerate(heads):
                o_ref[rows, h * HEAD_DIM:(h + 1) * HEAD_DIM] = out[r * blk:(r + 1) * blk].astype(o_ref.dtype)


def _rope_tables(seq):
    half = HEAD_DIM // 2
    inv = jnp.power(ROPE_THETA, -jnp.arange(half, dtype=F32) * 2.0 / HEAD_DIM)
    ang = jnp.arange(seq, dtype=F32)[:, None] * inv[None, :]
    cos, sin = jnp.cos(ang), jnp.sin(ang)
    cos2 = jnp.concatenate([cos, cos, cos, cos], axis=1)
    sin2 = jnp.concatenate([-sin, sin, -sin, sin], axis=1)
    return cos2, sin2


def _head_mean_matrix(width):
    i = jnp.arange(width)
    return jnp.where((i[:, None] // HEAD_DIM) == (i[None, :] // HEAD_DIM), 1.0 / HEAD_DIM, 0.0).astype(BF16)


def _swa_attention(q, k, v, q_g, k_g, sinks, bsz):
    t = q.shape[0]
    seq = t // bsz
    tile = min(ATTN_TILE, seq)
    nt = seq // tile
    bpt = tile // ATTN_BLOCK
    nb = seq // ATTN_BLOCK
    cos2, sin2 = _rope_tables(seq)
    qg = jnp.tile(q_g.astype(F32), ATTN_HEADS)[None]
    kg = jnp.tile(k_g.astype(F32), ATTN_KV_HEADS)[None]
    cur = lambda b, n: (b * nt + n, 0)
    prev = lambda b, n: (b * nb + jnp.maximum(n * bpt - 1, 0), 0)
    tcur = lambda b, n: (n, 0)
    tprev = lambda b, n: (jnp.maximum(n * bpt - 1, 0), 0)
    blk = ATTN_BLOCK
    return pl.pallas_call(
        _attn_kernel,
        grid=(bsz, nt),
        in_specs=[pl.BlockSpec((tile, ATTN_Q), cur),
                  pl.BlockSpec((tile, ATTN_KV), cur), pl.BlockSpec((blk, ATTN_KV), prev),
                  pl.BlockSpec((tile, ATTN_KV), cur), pl.BlockSpec((blk, ATTN_KV), prev),
                  pl.BlockSpec((tile, LANES), tcur), pl.BlockSpec((tile, LANES), tcur),
                  pl.BlockSpec((blk, LANES), tprev), pl.BlockSpec((blk, LANES), tprev),
                  _const_spec((1, ATTN_Q)), _const_spec((1, ATTN_KV)), _const_spec((1, ATTN_HEADS)),
                  _const_spec((ATTN_Q, ATTN_Q)), _const_spec((ATTN_KV, ATTN_KV))],
        out_specs=pl.BlockSpec((tile, ATTN_Q), cur),
        out_shape=jax.ShapeDtypeStruct((t, ATTN_Q), BF16),
        compiler_params=_cparams(("parallel", "parallel")),
        name="swa_attention",
    )(q, k, k, v, v, cos2, sin2, cos2, sin2, qg, kg, sinks.astype(F32)[None],
      _head_mean_matrix(ATTN_Q), _head_mean_matrix(ATTN_KV))


_CONV_TAIL = SSD_CHUNK


def _ssd_kernel(z_ref, xbc_ref, dt_ref, cw_ref, cb_ref, dtb_ref, alog_ref, dskip_ref, ng_ref, exp_ref, shift_ref,
                o_ref, tail, state, ybufs):
    @pl.when(pl.program_id(1) == 0)
    def _():
        tail[0] = jnp.zeros(tail.shape[1:], tail.dtype)
        state[...] = jnp.zeros_like(state)

    first = pl.program_id(1) * SSD_STEP_CHUNKS
    for i in range(SSD_STEP_CHUNKS):
        _ssd_chunk(i, (first + i) % 3, (first + i + 1) % 3, z_ref, xbc_ref, dt_ref, cw_ref, cb_ref, dtb_ref,
                   alog_ref, dskip_ref, ng_ref, exp_ref, shift_ref, o_ref, tail, state, ybufs.at[i])


def _ssd_chunk(i, prev_slot, next_slot, z_ref, xbc_ref, dt_ref, cw_ref, cb_ref, dtb_ref, alog_ref, dskip_ref,
               ng_ref, exp_ref, shift_ref, o_ref, tail, state, ybuf):
    ch = SSD_CHUNK
    rows = slice(i * ch, (i + 1) * ch)
    cur = xbc_ref[rows, :]
    shifted = _dot(shift_ref[...], jnp.concatenate([tail[prev_slot], cur], axis=0))
    tail[next_slot] = cur
    conv = cb_ref[...] + cw_ref[SSD_CONV - 1:SSD_CONV, :] * cur.astype(F32)
    for d in range(1, SSD_CONV):
        conv = conv + cw_ref[SSD_CONV - 1 - d:SSD_CONV - d, :] * shifted[(d - 1) * ch:d * ch]
    act = _silu(conv)
    xs = act[:, :SSD_WIDTH]
    bm = act[:, SSD_WIDTH:SSD_WIDTH + SSD_BC].astype(BF16)
    cm = act[:, SSD_WIDTH + SSD_BC:].astype(BF16)

    lane = lax.broadcasted_iota(jnp.int32, (ch, LANES), 1)
    xdt = dt_ref[rows, :] + dtb_ref[...]
    dt = jnp.maximum(xdt, 0.0) + jnp.log1p(jnp.exp(-jnp.abs(xdt)))
    dt = jnp.where(lane < SSD_HEADS, dt, 0.0)
    a = dt * (-jnp.exp(alog_ref[...]))
    row = lax.broadcasted_iota(jnp.int32, (ch, ch), 0)
    col = lax.broadcasted_iota(jnp.int32, (ch, ch), 1)
    causal = row >= col
    hi, mid, lo = _split3(a)
    tril = causal.astype(BF16)
    cs = _dot(tril, hi) + _dot(tril, mid) + _dot(tril, lo)
    cs_t = cs.T
    expand = exp_ref[...]
    dt_x = _dot_f32_lhs(dt, expand)
    cs_x = _dot_f32_lhs(cs, expand)
    cs_last_x = cs_x[ch - 1:ch, :]
    xdt_full = xs * dt_x
    in_decay = jnp.exp(cs_x)
    out_decay = jnp.exp(cs_last_x - cs_x)
    chunk_decay = jnp.exp(cs_last_x)
    xw = (xdt_full * out_decay).astype(BF16)
    xdt_b = xdt_full.astype(BF16)
    hpg = SSD_HEADS // SSD_GROUPS
    for g in range(SSD_GROUPS):
        gs = slice(g * SSD_STATE, (g + 1) * SSD_STATE)
        ws = slice(g * SSD_GROUP_W, (g + 1) * SSD_GROUP_W)
        bg = bm[:, gs]
        cg = cm[:, gs]
        cb = lax.dot_general(cg, bg, (((1,), (1,)), ((), ())), preferred_element_type=F32)
        prev = state[g]
        ybuf[:, ws] = _dot(cg, prev.astype(BF16)) * in_decay[:, ws]
        for h in range(hpg):
            hh = g * hpg + h
            seg = cs[:, hh:hh + 1] - cs_t[hh:hh + 1, :]
            lmat = jnp.where(causal, jnp.exp(seg), 0.0)
            hs = slice(hh * SSD_HEAD_DIM, (hh + 1) * SSD_HEAD_DIM)
            ybuf[:, hs] += _dot((cb * lmat).astype(BF16), xdt_b[:, hs])
        upd = lax.dot_general(bg, xw[:, ws], (((0,), (0,)), ((), ())), preferred_element_type=F32)
        state[g] = prev * chunk_decay[:, ws] + upd

    y = ybuf[...] + xs * dskip_ref[...]
    y = y * _silu(z_ref[rows, :].astype(F32))
    parts = []
    for g in range(SSD_GROUPS):
        ws = slice(g * SSD_GROUP_W, (g + 1) * SSD_GROUP_W)
        parts.append(_rms_scale(y[:, ws]))
    o_ref[rows, :] = (jnp.concatenate(parts, axis=1) * ng_ref[...]).astype(o_ref.dtype)


def _ssd_mixer(z, xbc, dt_raw, conv_w, conv_b, dt_bias, a_log, d_skip, norm_g, bsz):
    t = z.shape[0]
    seq = t // bsz
    nc = seq // SSD_CHUNK
    pad = lambda v: jnp.pad(v.astype(F32), (0, LANES - SSD_HEADS))[None]
    d_x = jnp.repeat(d_skip.astype(F32), SSD_HEAD_DIM)[None]
    hid = jnp.arange(SSD_WIDTH) // SSD_HEAD_DIM
    expand = (jnp.arange(LANES)[:, None] == hid[None, :]).astype(BF16)
    r = jnp.arange((SSD_CONV - 1) * SSD_CHUNK)
    pick = r % SSD_CHUNK + _CONV_TAIL - (r // SSD_CHUNK + 1)
    shift = (jnp.arange(_CONV_TAIL + SSD_CHUNK)[None, :] == pick[:, None]).astype(BF16)
    assert nc % SSD_STEP_CHUNKS == 0
    ns = nc // SSD_STEP_CHUNKS
    rows = SSD_STEP_CHUNKS * SSD_CHUNK
    blk = lambda b, c: (b * ns + c, 0)
    return pl.pallas_call(
        _ssd_kernel,
        grid=(bsz, ns),
        in_specs=[pl.BlockSpec((rows, SSD_WIDTH), blk),
                  pl.BlockSpec((rows, SSD_CONV_CH), blk),
                  pl.BlockSpec((rows, DT_PAD), blk),
                  _const_spec((SSD_CONV, SSD_CONV_CH)), _const_spec((1, SSD_CONV_CH)),
                  _const_spec((1, LANES)), _const_spec((1, LANES)),
                  _const_spec((1, SSD_WIDTH)), _const_spec((1, SSD_WIDTH)),
                  _const_spec((LANES, SSD_WIDTH)), _const_spec(shift.shape)],
        out_specs=pl.BlockSpec((rows, SSD_WIDTH), blk),
        out_shape=jax.ShapeDtypeStruct((t, SSD_WIDTH), BF16),
        scratch_shapes=[pltpu.VMEM((3, _CONV_TAIL, SSD_CONV_CH), BF16),
                        pltpu.VMEM((SSD_GROUPS, SSD_STATE, SSD_GROUP_W), F32),
                        pltpu.VMEM((SSD_STEP_CHUNKS, SSD_CHUNK, SSD_WIDTH), F32)],
        compiler_params=_cparams(("arbitrary", "arbitrary")),
        name="ssd_mixer",
    )(z, xbc, dt_raw, conv_w.astype(F32), conv_b.astype(F32)[None], pad(dt_bias), pad(a_log),
      d_x, norm_g.astype(F32)[None], expand, shift)


def _route(logits):
    lane = lax.broadcasted_iota(jnp.int32, logits.shape, 1).astype(F32)
    big = float(ROUTE_PAD)
    is_g = lane < MOE_GROUPS
    gl = jnp.where(is_g, logits, -jnp.inf)
    gmax = jnp.max(gl, axis=1, keepdims=True)
    gsel = jnp.min(jnp.where(gl == gmax, lane, big), axis=1, keepdims=True)
    pg = 1.0 / jnp.sum(jnp.where(is_g, jnp.exp(logits - gmax), 0.0), axis=1, keepdims=True)
    lo = MOE_GROUPS + MOE_PER_GROUP * gsel
    ev = jnp.where((lane >= lo) & (lane < lo + MOE_PER_GROUP), logits, -jnp.inf)
    m1 = jnp.max(ev, axis=1, keepdims=True)
    i1 = jnp.min(jnp.where(ev == m1, lane, big), axis=1, keepdims=True)
    ev2 = jnp.where(lane == i1, -jnp.inf, ev)
    m2 = jnp.max(ev2, axis=1, keepdims=True)
    i2 = jnp.min(jnp.where(ev2 == m2, lane, big), axis=1, keepdims=True)
    e21 = jnp.exp(m2 - m1)
    w1 = pg / (1.0 + e21)
    w2 = pg * e21 / (1.0 + e21)
    out = jnp.where(lane == 0, i1 - MOE_GROUPS, 0.0)
    out = jnp.where(lane == 1, i2 - MOE_GROUPS, out)
    out = jnp.where(lane == 2, w1, out)
    return jnp.where(lane == 3, w2, out)


def _merge_kernel(*refs, n_x):
    x_refs = refs[:n_x]
    (ys5_ref, u_ref, yb_ref, yc_ref, n1g_ref, wg_ref, bg_ref, d_ref, w1_ref, w2_ref, ps5_ref, pat_ref,
     pssd_ref, wo_ref, n2g_ref, wrh_ref, wrm_ref, br_ref, x1_o, h2_o, route_o) = refs[n_x:]
    x = _x_sum(x_refs)
    hb = (_rms_scale(x) * n1g_ref[...]).astype(BF16)
    ya = ys5_ref[...].astype(F32) + d_ref[...] * u_ref[...].astype(F32)
    yab = jax.nn.gelu(ya).astype(BF16)
    ya = _dot(yab, w1_ref[...]) * jax.nn.sigmoid(_dot(yab, w2_ref[...]))
    branches = ((ya.astype(BF16), ps5_ref), (yb_ref[...], pat_ref), (yc_ref[...], pssd_ref))
    merged = None
    for b, (yv, p_ref) in enumerate(branches):
        gate = jax.nn.sigmoid(_dot(hb, wg_ref[:, b * D_MODEL:(b + 1) * D_MODEL]) + bg_ref[b:b + 1, :])
        term = gate * _dot(yv, p_ref[...])
        merged = term if merged is None else merged + term
    x1 = x + _dot(merged.astype(BF16), wo_ref[...])
    x1_o[...] = x1
    h2 = _rms_scale(x1) * n2g_ref[...]
    _store_row_tiles(h2_o, 0, h2)
    h_hi = h2.astype(BF16)
    h_mid = (h2 - h_hi.astype(F32)).astype(BF16)
    logits = _dot(h_hi, wrh_ref[...]) + _dot(h_hi, wrm_ref[...]) + _dot(h_mid, wrh_ref[...]) + br_ref[...]
    route_o[...] = _route(logits)


def _merge(xs, t, ys5, u, yb, yc, lw):
    tm = min(TM_PROJ, t)
    row = lambda w: pl.BlockSpec((tm, w), lambda i: (i, 0))
    consts = [lw['n1g'], lw['w_gate'], lw['b_gate'], lw['s5_d'], lw['glu_w1'], lw['glu_w2'], lw['p_s5'],
              lw['p_attn'], lw['p_ssd'], lw['w_out'], lw['n2g'], lw['w_router_hi'], lw['w_router_mid'],
              lw['b_router']]
    x_arrs, x_specs = _x_parts(xs, tm)
    return pl.pallas_call(
        functools.partial(_merge_kernel, n_x=len(x_arrs)),
        grid=(t // tm,),
        in_specs=x_specs + [row(S5_WIDTH), row(S5_WIDTH), row(ATTN_Q), row(SSD_WIDTH)]
                 + [_const_spec(c.shape) for c in consts],
        out_specs=[row(D_MODEL), pl.BlockSpec((tm * ROW_TILE, LANES), lambda i: (i, 0)), row(ROUTE_PAD)],
        out_shape=[jax.ShapeDtypeStruct((t, D_MODEL), F32), jax.ShapeDtypeStruct((t * ROW_TILE, LANES), F32),
                   jax.ShapeDtypeStruct((t, ROUTE_PAD), F32)],
        compiler_params=_cparams(("parallel",)),
        name="merge_router",
    )(*x_arrs, ys5, u, yb, yc, *consts)


def _moe_plan(route, t):
    n_exp = MOE_EXPERTS
    e = route[:, 0:2].astype(jnp.int32).reshape(-1)
    onehot = (e[:, None] == jnp.arange(n_exp, dtype=jnp.int32)[None, :]).astype(jnp.int32)
    csum = jnp.cumsum(onehot, axis=0)
    rank = jnp.sum(csum * onehot, axis=1) - 1
    counts = csum[-1]
    padded = ((counts + TM_G - 1) // TM_G) * TM_G
    pend = jnp.cumsum(padded)
    dest = (pend - padded)[e] + rank
    n_tiles = -(-(2 * t + n_exp * (TM_G - 1)) // TM_G)
    n_rows = n_tiles * TM_G
    pair = jnp.full((n_rows,), -1, jnp.int32).at[dest].set(jnp.arange(2 * t, dtype=jnp.int32),
                                                            unique_indices=True)
    valid = pair >= 0
    tok, k = pair // 2, pair % 2
    src = jnp.where(valid, tok, jnp.arange(n_rows, dtype=jnp.int32) % t)
    spare = t + jnp.arange(n_rows, dtype=jnp.int32) % _MOE_SPARE
    dst = jnp.where(valid, k * (t + _MOE_SPARE) + tok, spare)
    tile_start = jnp.arange(n_tiles, dtype=jnp.int32) * TM_G
    tile_e = jnp.sum((pend[None, :] <= tile_start[:, None]).astype(jnp.int32), axis=1)
    n_used = pend[-1:] // TM_G
    tile_e = jnp.minimum(tile_e, n_exp - 1)
    tile_e = jnp.minimum(tile_e, tile_e[n_used[0] - 1])
    real = jnp.sum(valid.reshape(n_tiles, TM_G).astype(jnp.int32), axis=1)
    n_issue = ((real + _DMA_GROUP - 1) // _DMA_GROUP) * _DMA_GROUP
    return (src.reshape(n_tiles, 1, TM_G), dst.reshape(n_tiles, 1, TM_G), tile_e, n_used.astype(jnp.int32),
            n_issue.astype(jnp.int32))


_MOE_SPARE = 2 * TM_G
_DMA_GROUP = 32
assert TM_G % _DMA_GROUP == 0


def _gmm_kernel(te_ref, nu_ref, ni_ref, src_ref, srcn_ref, dst_ref, h2_hbm, wg_ref, wu_ref, wd_ref, out_hbm,
                xbuf, ybuf, gsem, ssem):
    del te_ref
    i = pl.program_id(0)
    n_grid = pl.num_programs(0)
    last = nu_ref[0] - 1
    slot = i % 2
    cnt_cur = ni_ref[i]
    cnt_next = ni_ref[jnp.minimum(i + 1, n_grid - 1)]

    tile_rows = TM_G * ROW_TILE
    groups = TM_G // _DMA_GROUP

    def row_tile(ref, s, r):
        return ref.at[pl.ds((s * TM_G + r) * ROW_TILE, ROW_TILE), :]

    def slot_rows(ref, s):
        return ref.at[pl.ds(s * tile_rows, tile_rows), :]

    def group_rows(ref, s, q):
        return ref.at[pl.ds((s * TM_G + q * _DMA_GROUP) * ROW_TILE, _DMA_GROUP * ROW_TILE), :]

    def for_groups(cnt, body):
        for q in range(groups):
            pl.when(cnt > q * _DMA_GROUP)(functools.partial(body, q))

    def gather_start(idx_ref, s, cnt):
        def body(q):
            for r in range(q * _DMA_GROUP, (q + 1) * _DMA_GROUP):
                src = h2_hbm.at[pl.ds(pl.multiple_of(idx_ref[0, 0, r], ROW_TILE), ROW_TILE), :]
                pltpu.make_async_copy(src, row_tile(xbuf, s, r), gsem.at[s]).start(priority=r % 2)
        for_groups(cnt, body)

    def gather_wait(s, cnt):
        def body(q):
            pltpu.make_async_copy(h2_hbm.at[pl.ds(0, _DMA_GROUP * ROW_TILE), :], group_rows(xbuf, s, q),
                                  gsem.at[s]).wait()
        for_groups(cnt, body)

    def scatter_start(s, cnt):
        def body(q):
            for r in range(q * _DMA_GROUP, (q + 1) * _DMA_GROUP):
                dst = out_hbm.at[pl.ds(pl.multiple_of(dst_ref[0, 0, r], ROW_TILE), ROW_TILE), :]
                pltpu.make_async_copy(row_tile(ybuf, s, r), dst, ssem.at[s]).start(priority=r % 2)
        for_groups(cnt, body)

    def scatter_wait(s, cnt):
        def body(q):
            pltpu.make_async_copy(group_rows(ybuf, s, q), out_hbm.at[pl.ds(0, _DMA_GROUP * ROW_TILE), :],
                                  ssem.at[s]).wait()
        for_groups(cnt, body)

    @pl.when(i == 0)
    def _():
        xbuf[...] = jnp.zeros_like(xbuf)
        gather_start(src_ref, 0, cnt_cur)
        ybuf[...] = jnp.zeros_like(ybuf)
        plane = out_hbm.shape[0] // 2
        spare0 = plane - _MOE_SPARE * ROW_TILE
        fills = [pltpu.make_async_copy(slot_rows(ybuf, s),
                                       out_hbm.at[pl.ds(k * plane + spare0 + s * tile_rows, tile_rows), :],
                                       ssem.at[s]) for k in range(2) for s in range(2)]
        for f in fills:
            f.start()
        for f in fills:
            f.wait()

    @pl.when(i <= last)
    def _():
        gather_wait(slot, cnt_cur)

        @pl.when(i >= 2)
        def _():
            scatter_wait(slot, ni_ref[jnp.maximum(i - 2, 0)])

        gather_start(srcn_ref, 1 - slot, cnt_next)

        xb = _load_row_tiles(xbuf, slot * TM_G, TM_G).astype(BF16)
        hid = _silu(_dot(xb, wg_ref[0, 0].astype(BF16))) * _dot(xb, wu_ref[0, 0].astype(BF16))
        _store_row_tiles(ybuf, slot * TM_G, _dot(hid.astype(BF16), wd_ref[0, 0].astype(BF16)))
        scatter_start(slot, cnt_cur)

        @pl.when(i == last)
        def _():
            scatter_wait(slot, cnt_cur)
            gather_wait(1 - slot, cnt_next)

        @pl.when((i == last) & (i >= 1))
        def _():
            scatter_wait(1 - slot, ni_ref[jnp.maximum(i - 1, 0)])


def _moe_sparse(h2, route, w_gate, w_up, w_down, l):
    t = h2.shape[0] // ROW_TILE
    src, dst, tile_e, n_used, n_issue = _moe_plan(route, t)
    src, dst = src * ROW_TILE, dst * ROW_TILE
    n_tiles = src.shape[0]
    smem = lambda imap: pl.BlockSpec((1, 1, TM_G), imap, memory_space=pltpu.SMEM)
    grid_spec = pltpu.PrefetchScalarGridSpec(
        num_scalar_prefetch=3,
        grid=(n_tiles,),
        in_specs=[smem(lambda i, te, nu, ni: (i, 0, 0)),
                  smem(lambda i, te, nu, ni: (jnp.minimum(i + 1, n_tiles - 1), 0, 0)),
                  smem(lambda i, te, nu, ni: (i, 0, 0)),
                  pl.BlockSpec(memory_space=pl.ANY),
                  pl.BlockSpec((1, 1, D_MODEL, MOE_FF), lambda i, te, nu, ni: (l, te[i], 0, 0)),
                  pl.BlockSpec((1, 1, D_MODEL, MOE_FF), lambda i, te, nu, ni: (l, te[i], 0, 0)),
                  pl.BlockSpec((1, 1, MOE_FF, D_MODEL), lambda i, te, nu, ni: (l, te[i], 0, 0))],
        out_specs=pl.BlockSpec(memory_space=pl.ANY),
        scratch_shapes=[pltpu.VMEM((2 * TM_G * ROW_TILE, LANES), F32), pltpu.VMEM((2 * TM_G * ROW_TILE, LANES), F32),
                        pltpu.SemaphoreType.DMA((2,)), pltpu.SemaphoreType.DMA((2,))])
    plane = (t + _MOE_SPARE) * ROW_TILE
    out = pl.pallas_call(
        _gmm_kernel,
        grid_spec=grid_spec,
        out_shape=jax.ShapeDtypeStruct((2 * plane, LANES), F32),
        compiler_params=_cparams(("arbitrary",)),
        name="moe_gmm",
    )(tile_e, n_used, n_issue, src, src, dst, h2, w_gate, w_up, w_down)
    return out.reshape(2, plane, LANES)


def _sum_kernel(*refs):
    refs[-1][...] = _x_sum(refs[:-1])


def _residual_sum(xs, t):
    tm = min(TM_PROJ, t)
    x_arrs, x_specs = _x_parts(xs, tm)
    return pl.pallas_call(
        _sum_kernel,
        grid=(t // tm,),
        in_specs=x_specs,
        out_specs=pl.BlockSpec((tm, D_MODEL), lambda i: (i, 0)),
        out_shape=jax.ShapeDtypeStruct((t, D_MODEL), F32),
        compiler_params=_cparams(("parallel",)),
        name="residual_sum",
    )(*x_arrs)


def _cast_kernel(*refs):
    n = len(refs) // 2
    for x_ref, o_ref in zip(refs[:n], refs[n:]):
        o_ref[...] = x_ref[...].astype(o_ref.dtype)


def _cast_layer_bf16(ws, l):
    halves = 2
    in_specs = [pl.BlockSpec((None, w.shape[1] // halves, w.shape[2]), lambda i: (l, i, 0)) for w in ws]
    out_specs = [pl.BlockSpec((w.shape[1] // halves, w.shape[2]), lambda i: (i, 0)) for w in ws]
    return pl.pallas_call(
        _cast_kernel,
        grid=(halves,),
        in_specs=in_specs,
        out_specs=out_specs,
        out_shape=[jax.ShapeDtypeStruct(w.shape[1:], BF16) for w in ws],
        compiler_params=_cparams(("parallel",)),
        name="cast_weights",
    )(*[w.astype(F32) for w in ws])


_GATE_SHIFT = N_MIX % LANES
assert _GATE_SHIFT % SUBLANES == 0


def _gate_w_kernel(a_ref, b_ref, o_ref):
    full = jnp.concatenate([a_ref[...], b_ref[...]], axis=0)
    o_ref[...] = full[_GATE_SHIFT:_GATE_SHIFT + LANES, :].T.astype(o_ref.dtype)


def _gate_weights(w_in_t, l):
    base = (N_MIX - _GATE_SHIFT) // LANES
    return pl.pallas_call(
        _gate_w_kernel,
        grid=(N_BRANCH * D_MODEL // LANES,),
        in_specs=[pl.BlockSpec((None, LANES, D_MODEL), lambda k: (l, base + k, 0)),
                  pl.BlockSpec((None, LANES, D_MODEL), lambda k: (l, base + k + 1, 0))],
        out_specs=pl.BlockSpec((D_MODEL, LANES), lambda k: (0, k)),
        out_shape=jax.ShapeDtypeStruct((D_MODEL, N_BRANCH * D_MODEL), BF16),
        compiler_params=_cparams(("parallel",)),
        name="gate_weights",
    )(w_in_t, w_in_t)


def _layer_weights(l, p):
    glu_w1, glu_w2, p_s5, p_attn, p_ssd, w_out = _cast_layer_bf16(
        [p['s5_glu_w1'], p['s5_glu_w2'], p['p_s5'], p['p_attn'], p['p_ssd'], p['w_out']], l)
    w_router = jnp.concatenate([p['w_router_group'][l], p['w_router_expert'][l]], axis=1).astype(F32)
    npad = ROUTE_PAD - w_router.shape[1]
    b_router = jnp.concatenate([p['b_router_group'][l], p['b_router_expert'][l]]).astype(F32)
    w_router = jnp.pad(w_router, ((0, 0), (0, npad)))
    w_router_hi = w_router.astype(BF16)
    return dict(
        n1g=p['norm1_g'][l].astype(F32)[None],
        w_gate=_gate_weights(p['w_in_t'], l), b_gate=p['b_gate'][l].astype(F32),
        s5_d=p['s5_d'][l].astype(F32)[None],
        glu_w1=glu_w1, glu_w2=glu_w2, p_s5=p_s5, p_attn=p_attn, p_ssd=p_ssd, w_out=w_out,
        n2g=p['norm2_g'][l].astype(F32)[None],
        w_router_hi=w_router_hi, w_router_mid=(w_router - w_router_hi.astype(F32)).astype(BF16),
        b_router=jnp.pad(b_router, (0, npad))[None])


def _layer(xs, t, l, p, s5_tabs, bsz):
    lw = _layer_weights(l, p)
    u, q, k, v, z, xbc, dt_raw, *x_sum = _inproj(xs, t, lw['n1g'], p['w_in_t'], l)
    if x_sum:
        xs = x_sum
    ys5 = _s5_mixer(u, s5_tabs, bsz)
    yb = _swa_attention(q, k, v, p['q_norm_g'][l], p['k_norm_g'][l], p['attn_sinks'][l], bsz)
    yc = _ssd_mixer(z, xbc, dt_raw, p['ssd_conv_w'][l], p['ssd_conv_b'][l], p['ssd_dt_bias'][l],
                    p['ssd_a_log'][l], p['ssd_d'][l], p['ssd_norm_g'][l], bsz)
    x1, h2, route = _merge(xs, t, ys5, u, yb, yc, lw)
    moe = _moe_sparse(h2, route, p['w_exp_gate'], p['w_exp_up'], p['w_exp_down'], l)
    return [x1, route, moe]


def kernel(x, norm1_g, w_in, b_gate, s5_lambda_re, s5_lambda_im, s5_b_re, s5_b_im, s5_c_re, s5_c_im, s5_d, s5_log_dt, s5_glu_w1, s5_glu_w2, q_norm_g, k_norm_g, attn_sinks, ssd_conv_w, ssd_conv_b, ssd_dt_bias, ssd_a_log, ssd_d, ssd_norm_g, p_s5, p_attn, p_ssd, w_out, norm2_g, w_router_group, b_router_group, w_router_expert, b_router_expert, w_exp_gate, w_exp_up, w_exp_down):
    p = dict(norm1_g=norm1_g, w_in=w_in, b_gate=b_gate, s5_lambda_re=s5_lambda_re, s5_lambda_im=s5_lambda_im,
             s5_b_re=s5_b_re, s5_b_im=s5_b_im, s5_c_re=s5_c_re, s5_c_im=s5_c_im, s5_d=s5_d,
             s5_log_dt=s5_log_dt, s5_glu_w1=s5_glu_w1, s5_glu_w2=s5_glu_w2, q_norm_g=q_norm_g,
             k_norm_g=k_norm_g, attn_sinks=attn_sinks, ssd_conv_w=ssd_conv_w, ssd_conv_b=ssd_conv_b,
             ssd_dt_bias=ssd_dt_bias, ssd_a_log=ssd_a_log, ssd_d=ssd_d, ssd_norm_g=ssd_norm_g, p_s5=p_s5,
             p_attn=p_attn, p_ssd=p_ssd, w_out=w_out, norm2_g=norm2_g, w_router_group=w_router_group,
             b_router_group=b_router_group, w_router_expert=w_router_expert, b_router_expert=b_router_expert,
             w_exp_gate=w_exp_gate, w_exp_up=w_exp_up, w_exp_down=w_exp_down)
    p['w_in_t'] = jnp.swapaxes(w_in.astype(F32), 1, 2)
    bsz, seq, dm = x.shape
    depth = w_in.shape[0]
    t = bsz * seq
    xs = [x.reshape(t, dm)]
    for l in range(depth):
        s5_tabs = _s5_tables(s5_lambda_re[l], s5_lambda_im[l], s5_b_re[l], s5_b_im[l], s5_c_re[l], s5_c_im[l],
                             s5_log_dt[l])
        xs = _layer(xs, t, l, p, s5_tabs, bsz)
    return _residual_sum(xs, t).reshape(bsz, seq, dm)
```

```python
import functools

import jax
import jax.numpy as jnp
from jax import lax
from jax.experimental import pallas as pl
from jax.experimental.pallas import tpu as pltpu

F32 = jnp.float32
BF16 = jnp.bfloat16

D_MODEL = 1024
NORM_EPS = 1e-6
S5_WIDTH = 512
S5_GROUP = 16
S5_GROUPS = 32
S5_STATE = 64
HEAD_DIM = 64
ATTN_HEADS = 8
ATTN_KV_HEADS = 2
ATTN_REP = ATTN_HEADS // ATTN_KV_HEADS
ATTN_Q = ATTN_HEADS * HEAD_DIM
ATTN_KV = ATTN_KV_HEADS * HEAD_DIM
ATTN_BLOCK = 128
ROPE_THETA = 10000.0
SSD_WIDTH = 1024
SSD_HEAD_DIM = 64
SSD_HEADS = 16
SSD_GROUPS = 2
SSD_STATE = 64
SSD_CONV = 4
SSD_CHUNK = 128
SSD_BC = SSD_GROUPS * SSD_STATE
SSD_CONV_CH = SSD_WIDTH + 2 * SSD_BC
SSD_GROUP_W = SSD_WIDTH // SSD_GROUPS
N_BRANCH = 3
MOE_GROUPS = 4
MOE_PER_GROUP = 8
MOE_EXPERTS = 32
MOE_FF = 512
N_MIX = S5_WIDTH + ATTN_Q + 2 * ATTN_KV + SSD_WIDTH + SSD_CONV_CH + SSD_HEADS

LANES = 128
SUBLANES = 8
VMEM_LIMIT_BYTES = 56 * 1024 * 1024

S5_CHUNK = 16
S5_TG = 8
ATTN_TILE = 512
TM_PROJ = 512
TM_G = 256
SSD_STEP_CHUNKS = 2
DT_PAD = LANES
N_MIX_PAD = N_MIX - SSD_HEADS + DT_PAD
ROUTE_PAD = LANES


def _cparams(semantics):
    return pltpu.CompilerParams(dimension_semantics=semantics, vmem_limit_bytes=VMEM_LIMIT_BYTES)


def _const_spec(shape):
    zeros = (0,) * len(shape)
    return pl.BlockSpec(shape, lambda *_: zeros, pipeline_mode=pl.Buffered(1))


def _dot(a, b):
    return jnp.dot(a, b, preferred_element_type=F32)


def _split3(a):
    hi = a.astype(BF16)
    r1 = a - hi.astype(F32)
    mid = r1.astype(BF16)
    lo = (r1 - mid.astype(F32)).astype(BF16)
    return hi, mid, lo


def _dot_f32_lhs(a, b_bf16):
    hi, mid, lo = _split3(a)
    return _dot(hi, b_bf16) + _dot(mid, b_bf16) + _dot(lo, b_bf16)


def _rms_scale(x):
    return x * lax.rsqrt(jnp.mean(x * x, axis=-1, keepdims=True) + NORM_EPS)


def _silu(x):
    return x * jax.nn.sigmoid(x)


_OFF_U = 0
_OFF_Q = _OFF_U + S5_WIDTH
_OFF_K = _OFF_Q + ATTN_Q
_OFF_V = _OFF_K + ATTN_KV
_OFF_Z = _OFF_V + ATTN_KV
_OFF_XBC = _OFF_Z + SSD_WIDTH
_OFF_DT = _OFF_XBC + SSD_CONV_CH


ROW_TILE = D_MODEL // LANES


def _store_row_tiles(ref, start, val):
    n = val.shape[0]
    for j in range(ROW_TILE):
        ref[pl.ds(start * ROW_TILE + j, n, stride=ROW_TILE), :] = val[:, j * LANES:(j + 1) * LANES]


def _load_row_tiles(ref, start, n):
    return jnp.concatenate([ref[pl.ds(start * ROW_TILE + j, n, stride=ROW_TILE), :] for j in range(ROW_TILE)],
                           axis=1)


def _x_parts(xs, tm):
    row = lambda w: pl.BlockSpec((tm, w), lambda i: (i, 0))
    if len(xs) == 1:
        return list(xs), [row(D_MODEL)]
    x1, route, moe = xs
    plane = lambda k: pl.BlockSpec((None, tm * ROW_TILE, LANES), lambda i: (k, i, 0))
    return [x1, route, moe, moe], [row(D_MODEL), row(ROUTE_PAD), plane(0), plane(1)]


def _x_sum(x_refs):
    if len(x_refs) == 1:
        return x_refs[0][...]
    x1_ref, route_ref, m0_ref, m1_ref = x_refs
    r = route_ref[...]
    n = x1_ref.shape[0]
    return x1_ref[...] + r[:, 2:3] * _load_row_tiles(m0_ref, 0, n) + r[:, 3:4] * _load_row_tiles(m1_ref, 0, n)


def _inproj_kernel(*refs, n_x):
    x_refs, (g_ref, w_ref), outs, wb = refs[:n_x], refs[n_x:n_x + 2], refs[n_x + 2:-1], refs[-1]

    @pl.when(pl.program_id(0) == 0)
    def _():
        for j in range(N_MIX_PAD // LANES):
            wb[:, j * LANES:(j + 1) * LANES] = w_ref[j * LANES:(j + 1) * LANES, :].T.astype(BF16)

    x = _x_sum(x_refs)
    hb = (_rms_scale(x) * g_ref[...]).astype(BF16)
    offs = (_OFF_U, _OFF_Q, _OFF_K, _OFF_V, _OFF_Z, _OFF_XBC, _OFF_DT)
    for o_ref, off in zip(outs, offs):
        width = o_ref.shape[1]
        o_ref[...] = _dot(hb, wb[:, off:off + width]).astype(o_ref.dtype)
    if len(outs) > len(offs):
        outs[-1][...] = x


def _inproj(xs, t, g, w_in, l):
    tm = min(TM_PROJ, t)
    widths = (S5_WIDTH, ATTN_Q, ATTN_KV, ATTN_KV, SSD_WIDTH, SSD_CONV_CH, DT_PAD)
    dtypes = (F32, BF16, BF16, BF16, BF16, BF16, F32)
    if len(xs) > 1:
        widths, dtypes = widths + (D_MODEL,), dtypes + (F32,)
    x_arrs, x_specs = _x_parts(xs, tm)
    w_spec = pl.BlockSpec((None, N_MIX_PAD, D_MODEL), lambda i: (l, 0, 0), pipeline_mode=pl.Buffered(1))
    return pl.pallas_call(
        functools.partial(_inproj_kernel, n_x=len(x_arrs)),
        grid=(t // tm,),
        in_specs=x_specs + [_const_spec((1, D_MODEL)), w_spec],
        out_specs=[pl.BlockSpec((tm, w), lambda i: (i, 0)) for w in widths],
        out_shape=[jax.ShapeDtypeStruct((t, w), d) for w, d in zip(widths, dtypes)],
        scratch_shapes=[pltpu.VMEM((D_MODEL, N_MIX_PAD), BF16)],
        compiler_params=_cparams(("arbitrary",)),
        name="inproj",
    )(*x_arrs, g, w_in)


def _s5_tables(lam_re, lam_im, b_re, b_im, c_re, c_im, log_dt):
    g_n, p_n = lam_re.shape
    c_n = b_re.shape[-1]
    t_n = S5_CHUNK
    lr, li = lam_re.astype(F32), lam_im.astype(F32)
    dt = jnp.exp(log_dt.astype(F32))[:, None]
    mag = jnp.exp(lr * dt)
    ab_re = mag * jnp.cos(li * dt)
    ab_im = mag * jnp.sin(li * dt)
    nr = ab_re - 1.0
    den = lr * lr + li * li
    f_re = (nr * lr + ab_im * li) / den
    f_im = (ab_im * lr - nr * li) / den
    br, bi = b_re.astype(F32), b_im.astype(F32)
    bb_re = f_re[..., None] * br - f_im[..., None] * bi
    bb_im = f_re[..., None] * bi + f_im[..., None] * br
    j = jnp.arange(t_n + 1, dtype=F32)[:, None, None]
    pmag = jnp.exp(lr * dt * j)
    ang = li * dt * j
    p_re = pmag * jnp.cos(ang)
    p_im = pmag * jnp.sin(ang)
    cr, ci = c_re.astype(F32), c_im.astype(F32)
    ca_re = cr[None] * p_re[:, :, None, :] - ci[None] * p_im[:, :, None, :]
    ca_im = cr[None] * p_im[:, :, None, :] + ci[None] * p_re[:, :, None, :]
    q_re = p_re[t_n - 1 - jnp.arange(t_n)]
    q_im = p_im[t_n - 1 - jnp.arange(t_n)]
    bs_re = q_re[..., None] * bb_re[None] - q_im[..., None] * bb_im[None]
    bs_im = q_re[..., None] * bb_im[None] + q_im[..., None] * bb_re[None]

    nt = g_n // S5_TG

    def in_strip(a):
        a = a.reshape(t_n, nt, S5_TG, p_n, c_n).transpose(1, 0, 4, 2, 3)
        return a.reshape(nt, t_n, c_n, S5_TG * p_n)

    def out_strip(a):
        a = a.reshape(t_n + 1, nt, S5_TG, c_n, p_n).transpose(1, 4, 0, 2, 3)
        return a.reshape(nt, p_n, (t_n + 1) * S5_TG * c_n)

    return dict(
        bs_re=in_strip(bs_re), bs_im=in_strip(bs_im), co_re=out_strip(ca_re), co_im=out_strip(-ca_im),
        at_re=p_re[t_n].reshape(nt, 1, S5_TG * p_n), at_im=p_im[t_n].reshape(nt, 1, S5_TG * p_n))


_S5_XW = S5_CHUNK * LANES
_S5_SW = S5_TG * S5_STATE


def _group_block(strip, lanes_per_group):
    rpg = strip.shape[0]
    full = jnp.concatenate([strip] * S5_TG, axis=0)
    row_g = lax.broadcasted_iota(jnp.int32, full.shape, 0) // rpg
    lane_g = (lax.broadcasted_iota(jnp.int32, full.shape, 1) % (S5_TG * lanes_per_group)) // lanes_per_group
    return jnp.where(row_g == lane_g, full, 0.0)


def _dot_hi_mid(a, b):
    a_hi, a_mid, _ = _split3(a)
    b_hi, b_mid, _ = _split3(b)
    return _dot(a_hi, b_hi) + _dot(a_hi, b_mid) + _dot(a_mid, b_hi)


def _s5_build_tables(bre_ref, bim_ref, cre_ref, cim_ref, big, bsre, bsim, core, coim):
    lag = lambda ref, j: _group_block(ref[0, :, j * LANES:(j + 1) * LANES], S5_GROUP)
    b0_re = _group_block(bre_ref[0, S5_CHUNK - 1], S5_STATE)
    b0_im = _group_block(bim_ref[0, S5_CHUNK - 1], S5_STATE)
    blocks = [(_dot_hi_mid(b0_re, lag(cre_ref, j)) + _dot_hi_mid(b0_im, lag(cim_ref, j))).astype(BF16)
              for j in range(S5_CHUNK)]
    zero = jnp.zeros((LANES, LANES), BF16)
    for s in range(S5_CHUNK):
        for t in range(S5_CHUNK):
            big[s * LANES:(s + 1) * LANES, t * LANES:(t + 1) * LANES] = blocks[t - s] if t >= s else zero
    for src, dst in ((bre_ref, bsre), (bim_ref, bsim)):
        for s in range(S5_CHUNK):
            dst[s * LANES:(s + 1) * LANES, :] = _group_block(src[0, s], S5_STATE).astype(BF16)
    for src, dst in ((cre_ref, core), (cim_ref, coim)):
        for t in range(S5_CHUNK):
            dst[:, t * LANES:(t + 1) * LANES] = lag(src, t + 1).astype(BF16)


def _s5_kernel(u_ref, bre_ref, bim_ref, cre_ref, cim_ref, are_ref, aim_ref, y_ref,
               big, bsre, bsim, core, coim, sre, sim, hre, him):
    @pl.when(pl.program_id(1) == 0)
    def _():
        _s5_build_tables(bre_ref, bim_ref, cre_ref, cim_ref, big, bsre, bsim, core, coim)

    nck = u_ref.shape[0] // S5_CHUNK
    xcat = jnp.concatenate([u_ref[pl.ds(s, nck, stride=S5_CHUNK), :].astype(BF16) for s in range(S5_CHUNK)],
                           axis=1)
    sre[...] = _dot(xcat, bsre[...])
    sim[...] = _dot(xcat, bsim[...])
    a_re = are_ref[0]
    a_im = aim_ref[0]

    def step(i, carry):
        h_re, h_im = carry
        base = pl.multiple_of(i * SUBLANES, SUBLANES)
        s_re = sre[pl.ds(base, SUBLANES), :]
        s_im = sim[pl.ds(base, SUBLANES), :]
        ent_re, ent_im = [], []
        for j in range(SUBLANES):
            ent_re.append(h_re)
            ent_im.append(h_im)
            h_re, h_im = (a_re * h_re - a_im * h_im + s_re[j:j + 1], a_re * h_im + a_im * h_re + s_im[j:j + 1])
        hre[pl.ds(base, SUBLANES), :] = jnp.concatenate(ent_re, axis=0)
        him[pl.ds(base, SUBLANES), :] = jnp.concatenate(ent_im, axis=0)
        return h_re, h_im

    zero = jnp.zeros((1, _S5_SW), F32)
    lax.fori_loop(0, nck // SUBLANES, step, (zero, zero))
    y = (_dot(xcat, big[...]) + _dot(hre[...].astype(BF16), core[...])
         + _dot(him[...].astype(BF16), coim[...]))
    for t in range(S5_CHUNK):
        y_ref[pl.ds(t, nck, stride=S5_CHUNK), :] = y[:, t * LANES:(t + 1) * LANES]


def _s5_mixer(u, tabs, bsz):
    t = u.shape[0]
    seq = t // bsz
    nck = seq // S5_CHUNK
    nt = S5_GROUPS // S5_TG
    strip = lambda a: pl.BlockSpec((1,) + a.shape[1:], lambda x, b: (x,) + (0,) * (a.ndim - 1))
    strips = [tabs['bs_re'], tabs['bs_im'], tabs['co_re'], tabs['co_im'], tabs['at_re'], tabs['at_im']]
    return pl.pallas_call(
        _s5_kernel,
        grid=(nt, bsz),
        in_specs=[pl.BlockSpec((seq, LANES), lambda x, b: (b, x))] + [strip(a) for a in strips],
        out_specs=pl.BlockSpec((seq, LANES), lambda x, b: (b, x)),
        out_shape=jax.ShapeDtypeStruct((t, S5_WIDTH), F32),
        scratch_shapes=[pltpu.VMEM((_S5_XW, _S5_XW), BF16),
                        pltpu.VMEM((_S5_XW, _S5_SW), BF16), pltpu.VMEM((_S5_XW, _S5_SW), BF16),
                        pltpu.VMEM((_S5_SW, _S5_XW), BF16), pltpu.VMEM((_S5_SW, _S5_XW), BF16)]
                       + [pltpu.VMEM((nck, _S5_SW), F32)] * 4,
        compiler_params=_cparams(("arbitrary", "arbitrary")),
        name="s5_mixer",
    )(u, *strips)


def _swap_rope_halves(y):
    w = y.shape[1]
    lane = lax.broadcasted_iota(jnp.int32, y.shape, 1)
    lower = (lane % HEAD_DIM) < (HEAD_DIM // 2)
    return jnp.where(lower, pltpu.roll(y, w - HEAD_DIM // 2, 1), pltpu.roll(y, HEAD_DIM // 2, 1))


def _norm_rope(x, gain, head_mean, cos2, sin2):
    reps = x.shape[1] // LANES
    ms = _dot_f32_lhs(x * x, head_mean)
    y = x * lax.rsqrt(ms + NORM_EPS) * gain
    if reps > 1:
        cos2 = jnp.concatenate([cos2] * reps, axis=1)
        sin2 = jnp.concatenate([sin2] * reps, axis=1)
    return y * cos2 + _swap_rope_halves(y) * sin2


def _attn_kernel(q_ref, kc_ref, kp_ref, vc_ref, vp_ref, cosc_ref, sinc_ref, cosp_ref, sinp_ref,
                 qg_ref, kg_ref, sink_ref, hmq_ref, hmk_ref, o_ref):
    seq_start = pl.program_id(1) == 0
    blk = ATTN_BLOCK
    nblk = q_ref.shape[0] // blk
    q = _norm_rope(q_ref[...].astype(F32), qg_ref[...], hmq_ref[...], cosc_ref[...], sinc_ref[...])
    q = (q * (HEAD_DIM ** -0.5)).astype(BF16)
    kc = _norm_rope(kc_ref[...].astype(F32), kg_ref[...], hmk_ref[...], cosc_ref[...], sinc_ref[...]).astype(BF16)
    kp = _norm_rope(kp_ref[...].astype(F32), kg_ref[...], hmk_ref[...], cosp_ref[...], sinp_ref[...]).astype(BF16)
    k_all = jnp.concatenate([kp, kc], axis=0)
    v_all = jnp.concatenate([vp_ref[...], vc_ref[...]], axis=0)
    shape = (2 * blk, ATTN_REP * blk)
    kj = lax.broadcasted_iota(jnp.int32, shape, 0)
    qi = lax.broadcasted_iota(jnp.int32, shape, 1) % blk + blk
    band = (kj <= qi) & (qi - kj < blk)
    band_first = band & ((kj >= blk) | jnp.logical_not(seq_start))
    sinks = sink_ref[...]
    for j in range(ATTN_KV_HEADS):
        sl = slice(j * HEAD_DIM, (j + 1) * HEAD_DIM)
        heads = [j * ATTN_REP + r for r in range(ATTN_REP)]
        sink = jnp.concatenate([jnp.broadcast_to(sinks[:, h:h + 1], (1, blk)) for h in heads], axis=1)
        for n in range(nblk):
            rows = slice(n * blk, (n + 1) * blk)
            kb = k_all[n * blk:(n + 2) * blk, sl]
            vb = v_all[n * blk:(n + 2) * blk, sl]
            q4 = jnp.concatenate([q[rows, h * HEAD_DIM:(h + 1) * HEAD_DIM] for h in heads], axis=0)
            s = lax.dot_general(kb, q4, (((1,), (1,)), ((), ())), preferred_element_type=F32)
            s = jnp.where(band_first if n == 0 else band, s, -jnp.inf)
            m = jnp.maximum(jnp.max(s, axis=0, keepdims=True), sink)
            p = jnp.exp(s - m)
            denom = jnp.sum(p, axis=0, keepdims=True) + jnp.exp(sink - m)
            pn = (p * (1.0 / denom)).astype(BF16)
            out = lax.dot_general(pn, vb, (((0,), (0,)), ((), ())), preferred_element_type=F32)
            for r, h in enumerate(heads):
                o_ref[rows, h * HEAD_DIM:(h + 1) * HEAD_DIM] = out[r * blk:(r + 1) * blk].astype(o_ref.dtype)


def _rope_tables(seq):
    half = HEAD_DIM // 2
    inv = jnp.power(ROPE_THETA, -jnp.arange(half, dtype=F32) * 2.0 / HEAD_DIM)
    ang = jnp.arange(seq, dtype=F32)[:, None] * inv[None, :]
    cos, sin = jnp.cos(ang), jnp.sin(ang)
    cos2 = jnp.concatenate([cos, cos, cos, cos], axis=1)
    sin2 = jnp.concatenate([-sin, sin, -sin, sin], axis=1)
    return cos2, sin2


def _head_mean_matrix(width):
    i = jnp.arange(width)
    return jnp.where((i[:, None] // HEAD_DIM) == (i[None, :] // HEAD_DIM), 1.0 / HEAD_DIM, 0.0).astype(BF16)


def _swa_attention(q, k, v, q_g, k_g, sinks, bsz):
    t = q.shape[0]
    seq = t // bsz
    tile = min(ATTN_TILE, seq)
    nt = seq // tile
    bpt = tile // ATTN_BLOCK
    nb = seq // ATTN_BLOCK
    cos2, sin2 = _rope_tables(seq)
    qg = jnp.tile(q_g.astype(F32), ATTN_HEADS)[None]
    kg = jnp.tile(k_g.astype(F32), ATTN_KV_HEADS)[None]
    cur = lambda b, n: (b * nt + n, 0)
    prev = lambda b, n: (b * nb + jnp.maximum(n * bpt - 1, 0), 0)
    tcur = lambda b, n: (n, 0)
    tprev = lambda b, n: (jnp.maximum(n * bpt - 1, 0), 0)
    blk = ATTN_BLOCK
    return pl.pallas_call(
        _attn_kernel,
        grid=(bsz, nt),
        in_specs=[pl.BlockSpec((tile, ATTN_Q), cur),
                  pl.BlockSpec((tile, ATTN_KV), cur), pl.BlockSpec((blk, ATTN_KV), prev),
                  pl.BlockSpec((tile, ATTN_KV), cur), pl.BlockSpec((blk, ATTN_KV), prev),
                  pl.BlockSpec((tile, LANES), tcur), pl.BlockSpec((tile, LANES), tcur),
                  pl.BlockSpec((blk, LANES), tprev), pl.BlockSpec((blk, LANES), tprev),
                  _const_spec((1, ATTN_Q)), _const_spec((1, ATTN_KV)), _const_spec((1, ATTN_HEADS)),
                  _const_spec((ATTN_Q, ATTN_Q)), _const_spec((ATTN_KV, ATTN_KV))],
        out_specs=pl.BlockSpec((tile, ATTN_Q), cur),
        out_shape=jax.ShapeDtypeStruct((t, ATTN_Q), BF16),
        compiler_params=_cparams(("parallel", "parallel")),
        name="swa_attention",
    )(q, k, k, v, v, cos2, sin2, cos2, sin2, qg, kg, sinks.astype(F32)[None],
      _head_mean_matrix(ATTN_Q), _head_mean_matrix(ATTN_KV))


_CONV_TAIL = SSD_CHUNK


def _ssd_kernel(z_ref, xbc_ref, dt_ref, cw_ref, cb_ref, dtb_ref, alog_ref, dskip_ref, ng_ref, exp_ref, shift_ref,
                o_ref, tail, state, ybufs):
    @pl.when(pl.program_id(1) == 0)
    def _():
        tail[0] = jnp.zeros(tail.shape[1:], tail.dtype)
        state[...] = jnp.zeros_like(state)

    first = pl.program_id(1) * SSD_STEP_CHUNKS
    for i in range(SSD_STEP_CHUNKS):
        _ssd_chunk(i, (first + i) % 3, (first + i + 1) % 3, z_ref, xbc_ref, dt_ref, cw_ref, cb_ref, dtb_ref,
                   alog_ref, dskip_ref, ng_ref, exp_ref, shift_ref, o_ref, tail, state, ybufs.at[i])


def _ssd_chunk(i, prev_slot, next_slot, z_ref, xbc_ref, dt_ref, cw_ref, cb_ref, dtb_ref, alog_ref, dskip_ref,
               ng_ref, exp_ref, shift_ref, o_ref, tail, state, ybuf):
    ch = SSD_CHUNK
    rows = slice(i * ch, (i + 1) * ch)
    cur = xbc_ref[rows, :]
    shifted = _dot(shift_ref[...], jnp.concatenate([tail[prev_slot], cur], axis=0))
    tail[next_slot] = cur
    conv = cb_ref[...] + cw_ref[SSD_CONV - 1:SSD_CONV, :] * cur.astype(F32)
    for d in range(1, SSD_CONV):
        conv = conv + cw_ref[SSD_CONV - 1 - d:SSD_CONV - d, :] * shifted[(d - 1) * ch:d * ch]
    act = _silu(conv)
    xs = act[:, :SSD_WIDTH]
    bm = act[:, SSD_WIDTH:SSD_WIDTH + SSD_BC].astype(BF16)
    cm = act[:, SSD_WIDTH + SSD_BC:].astype(BF16)

    lane = lax.broadcasted_iota(jnp.int32, (ch, LANES), 1)
    xdt = dt_ref[rows, :] + dtb_ref[...]
    dt = jnp.maximum(xdt, 0.0) + jnp.log1p(jnp.exp(-jnp.abs(xdt)))
    dt = jnp.where(lane < SSD_HEADS, dt, 0.0)
    a = dt * (-jnp.exp(alog_ref[...]))
    row = lax.broadcasted_iota(jnp.int32, (ch, ch), 0)
    col = lax.broadcasted_iota(jnp.int32, (ch, ch), 1)
    causal = row >= col
    hi, mid, lo = _split3(a)
    tril = causal.astype(BF16)
    cs = _dot(tril, hi) + _dot(tril, mid) + _dot(tril, lo)
    cs_t = cs.T
    expand = exp_ref[...]
    dt_x = _dot_f32_lhs(dt, expand)
    cs_x = _dot_f32_lhs(cs, expand)
    cs_last_x = cs_x[ch - 1:ch, :]
    xdt_full = xs * dt_x
    in_decay = jnp.exp(cs_x)
    out_decay = jnp.exp(cs_last_x - cs_x)
    chunk_decay = jnp.exp(cs_last_x)
    xw = (xdt_full * out_decay).astype(BF16)
    xdt_b = xdt_full.astype(BF16)
    hpg = SSD_HEADS // SSD_GROUPS
    for g in range(SSD_GROUPS):
        gs = slice(g * SSD_STATE, (g + 1) * SSD_STATE)
        ws = slice(g * SSD_GROUP_W, (g + 1) * SSD_GROUP_W)
        bg = bm[:, gs]
        cg = cm[:, gs]
        cb = lax.dot_general(cg, bg, (((1,), (1,)), ((), ())), preferred_element_type=F32)
        prev = state[g]
        ybuf[:, ws] = _dot(cg, prev.astype(BF16)) * in_decay[:, ws]
        for h in range(hpg):
            hh = g * hpg + h
            seg = cs[:, hh:hh + 1] - cs_t[hh:hh + 1, :]
            lmat = jnp.where(causal, jnp.exp(seg), 0.0)
            hs = slice(hh * SSD_HEAD_DIM, (hh + 1) * SSD_HEAD_DIM)
            ybuf[:, hs] += _dot((cb * lmat).astype(BF16), xdt_b[:, hs])
        upd = lax.dot_general(bg, xw[:, ws], (((0,), (0,)), ((), ())), preferred_element_type=F32)
        state[g] = prev * chunk_decay[:, ws] + upd

    y = ybuf[...] + xs * dskip_ref[...]
    y = y * _silu(z_ref[rows, :].astype(F32))
    parts = []
    for g in range(SSD_GROUPS):
        ws = slice(g * SSD_GROUP_W, (g + 1) * SSD_GROUP_W)
        parts.append(_rms_scale(y[:, ws]))
    o_ref[rows, :] = (jnp.concatenate(parts, axis=1) * ng_ref[...]).astype(o_ref.dtype)


def _ssd_mixer(z, xbc, dt_raw, conv_w, conv_b, dt_bias, a_log, d_skip, norm_g, bsz):
    t = z.shape[0]
    seq = t // bsz
    nc = seq // SSD_CHUNK
    pad = lambda v: jnp.pad(v.astype(F32), (0, LANES - SSD_HEADS))[None]
    d_x = jnp.repeat(d_skip.astype(F32), SSD_HEAD_DIM)[None]
    hid = jnp.arange(SSD_WIDTH) // SSD_HEAD_DIM
    expand = (jnp.arange(LANES)[:, None] == hid[None, :]).astype(BF16)
    r = jnp.arange((SSD_CONV - 1) * SSD_CHUNK)
    pick = r % SSD_CHUNK + _CONV_TAIL - (r // SSD_CHUNK + 1)
    shift = (jnp.arange(_CONV_TAIL + SSD_CHUNK)[None, :] == pick[:, None]).astype(BF16)
    assert nc % SSD_STEP_CHUNKS == 0
    ns = nc // SSD_STEP_CHUNKS
    rows = SSD_STEP_CHUNKS * SSD_CHUNK
    blk = lambda b, c: (b * ns + c, 0)
    return pl.pallas_call(
        _ssd_kernel,
        grid=(bsz, ns),
        in_specs=[pl.BlockSpec((rows, SSD_WIDTH), blk),
                  pl.BlockSpec((rows, SSD_CONV_CH), blk),
                  pl.BlockSpec((rows, DT_PAD), blk),
                  _const_spec((SSD_CONV, SSD_CONV_CH)), _const_spec((1, SSD_CONV_CH)),
                  _const_spec((1, LANES)), _const_spec((1, LANES)),
                  _const_spec((1, SSD_WIDTH)), _const_spec((1, SSD_WIDTH)),
                  _const_spec((LANES, SSD_WIDTH)), _const_spec(shift.shape)],
        out_specs=pl.BlockSpec((rows, SSD_WIDTH), blk),
        out_shape=jax.ShapeDtypeStruct((t, SSD_WIDTH), BF16),
        scratch_shapes=[pltpu.VMEM((3, _CONV_TAIL, SSD_CONV_CH), BF16),
                        pltpu.VMEM((SSD_GROUPS, SSD_STATE, SSD_GROUP_W), F32),
                        pltpu.VMEM((SSD_STEP_CHUNKS, SSD_CHUNK, SSD_WIDTH), F32)],
        compiler_params=_cparams(("arbitrary", "arbitrary")),
        name="ssd_mixer",
    )(z, xbc, dt_raw, conv_w.astype(F32), conv_b.astype(F32)[None], pad(dt_bias), pad(a_log),
      d_x, norm_g.astype(F32)[None], expand, shift)


def _route(logits):
    lane = lax.broadcasted_iota(jnp.int32, logits.shape, 1).astype(F32)
    big = float(ROUTE_PAD)
    is_g = lane < MOE_GROUPS
    gl = jnp.where(is_g, logits, -jnp.inf)
    gmax = jnp.max(gl, axis=1, keepdims=True)
    gsel = jnp.min(jnp.where(gl == gmax, lane, big), axis=1, keepdims=True)
    pg = 1.0 / jnp.sum(jnp.where(is_g, jnp.exp(logits - gmax), 0.0), axis=1, keepdims=True)
    lo = MOE_GROUPS + MOE_PER_GROUP * gsel
    ev = jnp.where((lane >= lo) & (lane < lo + MOE_PER_GROUP), logits, -jnp.inf)
    m1 = jnp.max(ev, axis=1, keepdims=True)
    i1 = jnp.min(jnp.where(ev == m1, lane, big), axis=1, keepdims=True)
    ev2 = jnp.where(lane == i1, -jnp.inf, ev)
    m2 = jnp.max(ev2, axis=1, keepdims=True)
    i2 = jnp.min(jnp.where(ev2 == m2, lane, big), axis=1, keepdims=True)
    e21 = jnp.exp(m2 - m1)
    w1 = pg / (1.0 + e21)
    w2 = pg * e21 / (1.0 + e21)
    out = jnp.where(lane == 0, i1 - MOE_GROUPS, 0.0)
    out = jnp.where(lane == 1, i2 - MOE_GROUPS, out)
    out = jnp.where(lane == 2, w1, out)
    return jnp.where(lane == 3, w2, out)


def _merge_kernel(*refs, n_x):
    x_refs = refs[:n_x]
    (ys5_ref, u_ref, yb_ref, yc_ref, n1g_ref, wg_ref, bg_ref, d_ref, w1_ref, w2_ref, ps5_ref, pat_ref,
     pssd_ref, wo_ref, n2g_ref, wrh_ref, wrm_ref, br_ref, x1_o, h2_o, route_o) = refs[n_x:]
    x = _x_sum(x_refs)
    hb = (_rms_scale(x) * n1g_ref[...]).astype(BF16)
    ya = ys5_ref[...].astype(F32) + d_ref[...] * u_ref[...].astype(F32)
    yab = jax.nn.gelu(ya).astype(BF16)
    ya = _dot(yab, w1_ref[...]) * jax.nn.sigmoid(_dot(yab, w2_ref[...]))
    branches = ((ya.astype(BF16), ps5_ref), (yb_ref[...], pat_ref), (yc_ref[...], pssd_ref))
    merged = None
    for b, (yv, p_ref) in enumerate(branches):
        gate = jax.nn.sigmoid(_dot(hb, wg_ref[:, b * D_MODEL:(b + 1) * D_MODEL]) + bg_ref[b:b + 1, :])
        term = gate * _dot(yv, p_ref[...])
        merged = term if merged is None else merged + term
    x1 = x + _dot(merged.astype(BF16), wo_ref[...])
    x1_o[...] = x1
    h2 = _rms_scale(x1) * n2g_ref[...]
    _store_row_tiles(h2_o, 0, h2)
    h_hi = h2.astype(BF16)
    h_mid = (h2 - h_hi.astype(F32)).astype(BF16)
    logits = _dot(h_hi, wrh_ref[...]) + _dot(h_hi, wrm_ref[...]) + _dot(h_mid, wrh_ref[...]) + br_ref[...]
    route_o[...] = _route(logits)


def _merge(xs, t, ys5, u, yb, yc, lw):
    tm = min(TM_PROJ, t)
    row = lambda w: pl.BlockSpec((tm, w), lambda i: (i, 0))
    consts = [lw['n1g'], lw['w_gate'], lw['b_gate'], lw['s5_d'], lw['glu_w1'], lw['glu_w2'], lw['p_s5'],
              lw['p_attn'], lw['p_ssd'], lw['w_out'], lw['n2g'], lw['w_router_hi'], lw['w_router_mid'],
              lw['b_router']]
    x_arrs, x_specs = _x_parts(xs, tm)
    return pl.pallas_call(
        functools.partial(_merge_kernel, n_x=len(x_arrs)),
        grid=(t // tm,),
        in_specs=x_specs + [row(S5_WIDTH), row(S5_WIDTH), row(ATTN_Q), row(SSD_WIDTH)]
                 + [_const_spec(c.shape) for c in consts],
        out_specs=[row(D_MODEL), pl.BlockSpec((tm * ROW_TILE, LANES), lambda i: (i, 0)), row(ROUTE_PAD)],
        out_shape=[jax.ShapeDtypeStruct((t, D_MODEL), F32), jax.ShapeDtypeStruct((t * ROW_TILE, LANES), F32),
                   jax.ShapeDtypeStruct((t, ROUTE_PAD), F32)],
        compiler_params=_cparams(("parallel",)),
        name="merge_router",
    )(*x_arrs, ys5, u, yb, yc, *consts)


def _moe_plan(route, t):
    n_exp = MOE_EXPERTS
    e = route[:, 0:2].astype(jnp.int32).reshape(-1)
    onehot = (e[:, None] == jnp.arange(n_exp, dtype=jnp.int32)[None, :]).astype(jnp.int32)
    csum = jnp.cumsum(onehot, axis=0)
    rank = jnp.sum(csum * onehot, axis=1) - 1
    counts = csum[-1]
    padded = ((counts + TM_G - 1) // TM_G) * TM_G
    pend = jnp.cumsum(padded)
    dest = (pend - padded)[e] + rank
    n_tiles = -(-(2 * t + n_exp * (TM_G - 1)) // TM_G)
    n_rows = n_tiles * TM_G
    pair = jnp.full((n_rows,), -1, jnp.int32).at[dest].set(jnp.arange(2 * t, dtype=jnp.int32),
                                                            unique_indices=True)
    valid = pair >= 0
    tok, k = pair // 2, pair % 2
    src = jnp.where(valid, tok, jnp.arange(n_rows, dtype=jnp.int32) % t)
    spare = t + jnp.arange(n_rows, dtype=jnp.int32) % _MOE_SPARE
    dst = jnp.where(valid, k * (t + _MOE_SPARE) + tok, spare)
    tile_start = jnp.arange(n_tiles, dtype=jnp.int32) * TM_G
    tile_e = jnp.sum((pend[None, :] <= tile_start[:, None]).astype(jnp.int32), axis=1)
    n_used = pend[-1:] // TM_G
    tile_e = jnp.minimum(tile_e, n_exp - 1)
    tile_e = jnp.minimum(tile_e, tile_e[n_used[0] - 1])
    real = jnp.sum(valid.reshape(n_tiles, TM_G).astype(jnp.int32), axis=1)
    n_issue = ((real + _DMA_GROUP - 1) // _DMA_GROUP) * _DMA_GROUP
    return (src.reshape(n_tiles, 1, TM_G), dst.reshape(n_tiles, 1, TM_G), tile_e, n_used.astype(jnp.int32),
            n_issue.astype(jnp.int32))


_MOE_SPARE = 2 * TM_G
_DMA_GROUP = 32
assert TM_G % _DMA_GROUP == 0


def _gmm_kernel(te_ref, nu_ref, ni_ref, src_ref, srcn_ref, dst_ref, h2_hbm, wg_ref, wu_ref, wd_ref, out_hbm,
                xbuf, ybuf, gsem, ssem):
    del te_ref
    i = pl.program_id(0)
    n_grid = pl.num_programs(0)
    last = nu_ref[0] - 1
    slot = i % 2
    cnt_cur = ni_ref[i]
    cnt_next = ni_ref[jnp.minimum(i + 1, n_grid - 1)]

    tile_rows = TM_G * ROW_TILE
    groups = TM_G // _DMA_GROUP

    def row_tile(ref, s, r):
        return ref.at[pl.ds((s * TM_G + r) * ROW_TILE, ROW_TILE), :]

    def slot_rows(ref, s):
        return ref.at[pl.ds(s * tile_rows, tile_rows), :]

    def group_rows(ref, s, q):
        return ref.at[pl.ds((s * TM_G + q * _DMA_GROUP) * ROW_TILE, _DMA_GROUP * ROW_TILE), :]

    def for_groups(cnt, body):
        for q in range(groups):
            pl.when(cnt > q * _DMA_GROUP)(functools.partial(body, q))

    def gather_start(idx_ref, s, cnt):
        def body(q):
            for r in range(q * _DMA_GROUP, (q + 1) * _DMA_GROUP):
                src = h2_hbm.at[pl.ds(pl.multiple_of(idx_ref[0, 0, r], ROW_TILE), ROW_TILE), :]
                pltpu.make_async_copy(src, row_tile(xbuf, s, r), gsem.at[s]).start(priority=r % 2)
        for_groups(cnt, body)

    def gather_wait(s, cnt):
        def body(q):
            pltpu.make_async_copy(h2_hbm.at[pl.ds(0, _DMA_GROUP * ROW_TILE), :], group_rows(xbuf, s, q),
                                  gsem.at[s]).wait()
        for_groups(cnt, body)

    def scatter_start(s, cnt):
        def body(q):
            for r in range(q * _DMA_GROUP, (q + 1) * _DMA_GROUP):
                dst = out_hbm.at[pl.ds(pl.multiple_of(dst_ref[0, 0, r], ROW_TILE), ROW_TILE), :]
                pltpu.make_async_copy(row_tile(ybuf, s, r), dst, ssem.at[s]).start(priority=r % 2)
        for_groups(cnt, body)

    def scatter_wait(s, cnt):
        def body(q):
            pltpu.make_async_copy(group_rows(ybuf, s, q), out_hbm.at[pl.ds(0, _DMA_GROUP * ROW_TILE), :],
                                  ssem.at[s]).wait()
        for_groups(cnt, body)

    @pl.when(i == 0)
    def _():
        xbuf[...] = jnp.zeros_like(xbuf)
        gather_start(src_ref, 0, cnt_cur)
        ybuf[...] = jnp.zeros_like(ybuf)
        plane = out_hbm.shape[0] // 2
        spare0 = plane - _MOE_SPARE * ROW_TILE
        fills = [pltpu.make_async_copy(slot_rows(ybuf, s),
                                       out_hbm.at[pl.ds(k * plane + spare0 + s * tile_rows, tile_rows), :],
                                       ssem.at[s]) for k in range(2) for s in range(2)]
        for f in fills:
            f.start()
        for f in fills:
            f.wait()

    @pl.when(i <= last)
    def _():
        gather_wait(slot, cnt_cur)

        @pl.when(i >= 2)
        def _():
            scatter_wait(slot, ni_ref[jnp.maximum(i - 2, 0)])

        gather_start(srcn_ref, 1 - slot, cnt_next)

        xb = _load_row_tiles(xbuf, slot * TM_G, TM_G).astype(BF16)
        hid = _silu(_dot(xb, wg_ref[0, 0].astype(BF16))) * _dot(xb, wu_ref[0, 0].astype(BF16))
        _store_row_tiles(ybuf, slot * TM_G, _dot(hid.astype(BF16), wd_ref[0, 0].astype(BF16)))
        scatter_start(slot, cnt_cur)

        @pl.when(i == last)
        def _():
            scatter_wait(slot, cnt_cur)
            gather_wait(1 - slot, cnt_next)

        @pl.when((i == last) & (i >= 1))
        def _():
            scatter_wait(1 - slot, ni_ref[jnp.maximum(i - 1, 0)])


def _moe_sparse(h2, route, w_gate, w_up, w_down, l):
    t = h2.shape[0] // ROW_TILE
    src, dst, tile_e, n_used, n_issue = _moe_plan(route, t)
    src, dst = src * ROW_TILE, dst * ROW_TILE
    n_tiles = src.shape[0]
    smem = lambda imap: pl.BlockSpec((1, 1, TM_G), imap, memory_space=pltpu.SMEM)
    grid_spec = pltpu.PrefetchScalarGridSpec(
        num_scalar_prefetch=3,
        grid=(n_tiles,),
        in_specs=[smem(lambda i, te, nu, ni: (i, 0, 0)),
                  smem(lambda i, te, nu, ni: (jnp.minimum(i + 1, n_tiles - 1), 0, 0)),
                  smem(lambda i, te, nu, ni: (i, 0, 0)),
                  pl.BlockSpec(memory_space=pl.ANY),
                  pl.BlockSpec((1, 1, D_MODEL, MOE_FF), lambda i, te, nu, ni: (l, te[i], 0, 0)),
                  pl.BlockSpec((1, 1, D_MODEL, MOE_FF), lambda i, te, nu, ni: (l, te[i], 0, 0)),
                  pl.BlockSpec((1, 1, MOE_FF, D_MODEL), lambda i, te, nu, ni: (l, te[i], 0, 0))],
        out_specs=pl.BlockSpec(memory_space=pl.ANY),
        scratch_shapes=[pltpu.VMEM((2 * TM_G * ROW_TILE, LANES), F32), pltpu.VMEM((2 * TM_G * ROW_TILE, LANES), F32),
                        pltpu.SemaphoreType.DMA((2,)), pltpu.SemaphoreType.DMA((2,))])
    plane = (t + _MOE_SPARE) * ROW_TILE
    out = pl.pallas_call(
        _gmm_kernel,
        grid_spec=grid_spec,
        out_shape=jax.ShapeDtypeStruct((2 * plane, LANES), F32),
        compiler_params=_cparams(("arbitrary",)),
        name="moe_gmm",
    )(tile_e, n_used, n_issue, src, src, dst, h2, w_gate, w_up, w_down)
    return out.reshape(2, plane, LANES)


def _sum_kernel(*refs):
    refs[-1][...] = _x_sum(refs[:-1])


def _residual_sum(xs, t):
    tm = min(TM_PROJ, t)
    x_arrs, x_specs = _x_parts(xs, tm)
    return pl.pallas_call(
        _sum_kernel,
        grid=(t // tm,),
        in_specs=x_specs,
        out_specs=pl.BlockSpec((tm, D_MODEL), lambda i: (i, 0)),
        out_shape=jax.ShapeDtypeStruct((t, D_MODEL), F32),
        compiler_params=_cparams(("parallel",)),
        name="residual_sum",
    )(*x_arrs)


def _cast_kernel(*refs):
    n = len(refs) // 2
    for x_ref, o_ref in zip(refs[:n], refs[n:]):
        o_ref[...] = x_ref[...].astype(o_ref.dtype)


def _cast_layer_bf16(ws, l):
    halves = 2
    in_specs = [pl.BlockSpec((None, w.shape[1] // halves, w.shape[2]), lambda i: (l, i, 0)) for w in ws]
    out_specs = [pl.BlockSpec((w.shape[1] // halves, w.shape[2]), lambda i: (i, 0)) for w in ws]
    return pl.pallas_call(
        _cast_kernel,
        grid=(halves,),
        in_specs=in_specs,
        out_specs=out_specs,
        out_shape=[jax.ShapeDtypeStruct(w.shape[1:], BF16) for w in ws],
        compiler_params=_cparams(("parallel",)),
        name="cast_weights",
    )(*[w.astype(F32) for w in ws])


_GATE_SHIFT = N_MIX % LANES
assert _GATE_SHIFT % SUBLANES == 0


def _gate_w_kernel(a_ref, b_ref, o_ref):
    full = jnp.concatenate([a_ref[...], b_ref[...]], axis=0)
    o_ref[...] = full[_GATE_SHIFT:_GATE_SHIFT + LANES, :].T.astype(o_ref.dtype)


def _gate_weights(w_in_t, l):
    base = (N_MIX - _GATE_SHIFT) // LANES
    return pl.pallas_call(
        _gate_w_kernel,
        grid=(N_BRANCH * D_MODEL // LANES,),
        in_specs=[pl.BlockSpec((None, LANES, D_MODEL), lambda k: (l, base + k, 0)),
                  pl.BlockSpec((None, LANES, D_MODEL), lambda k: (l, base + k + 1, 0))],
        out_specs=pl.BlockSpec((D_MODEL, LANES), lambda k: (0, k)),
        out_shape=jax.ShapeDtypeStruct((D_MODEL, N_BRANCH * D_MODEL), BF16),
        compiler_params=_cparams(("parallel",)),
        name="gate_weights",
    )(w_in_t, w_in_t)


def _layer_weights(l, p):
    glu_w1, glu_w2, p_s5, p_attn, p_ssd, w_out = _cast_layer_bf16(
        [p['s5_glu_w1'], p['s5_glu_w2'], p['p_s5'], p['p_attn'], p['p_ssd'], p['w_out']], l)
    w_router = jnp.concatenate([p['w_router_group'][l], p['w_router_expert'][l]], axis=1).astype(F32)
    npad = ROUTE_PAD - w_router.shape[1]
    b_router = jnp.concatenate([p['b_router_group'][l], p['b_router_expert'][l]]).astype(F32)
    w_router = jnp.pad(w_router, ((0, 0), (0, npad)))
    w_router_hi = w_router.astype(BF16)
    return dict(
        n1g=p['norm1_g'][l].astype(F32)[None],
        w_gate=_gate_weights(p['w_in_t'], l), b_gate=p['b_gate'][l].astype(F32),
        s5_d=p['s5_d'][l].astype(F32)[None],
        glu_w1=glu_w1, glu_w2=glu_w2, p_s5=p_s5, p_attn=p_attn, p_ssd=p_ssd, w_out=w_out,
        n2g=p['norm2_g'][l].astype(F32)[None],
        w_router_hi=w_router_hi, w_router_mid=(w_router - w_router_hi.astype(F32)).astype(BF16),
        b_router=jnp.pad(b_router, (0, npad))[None])


def _layer(xs, t, l, p, s5_tabs, bsz):
    lw = _layer_weights(l, p)
    u, q, k, v, z, xbc, dt_raw, *x_sum = _inproj(xs, t, lw['n1g'], p['w_in_t'], l)
    if x_sum:
        xs = x_sum
    ys5 = _s5_mixer(u, s5_tabs, bsz)
    yb = _swa_attention(q, k, v, p['q_norm_g'][l], p['k_norm_g'][l], p['attn_sinks'][l], bsz)
    yc = _ssd_mixer(z, xbc, dt_raw, p['ssd_conv_w'][l], p['ssd_conv_b'][l], p['ssd_dt_bias'][l],
                    p['ssd_a_log'][l], p['ssd_d'][l], p['ssd_norm_g'][l], bsz)
    x1, h2, route = _merge(xs, t, ys5, u, yb, yc, lw)
    moe = _moe_sparse(h2, route, p['w_exp_gate'], p['w_exp_up'], p['w_exp_down'], l)
    return [x1, route, moe]


def kernel(x, norm1_g, w_in, b_gate, s5_lambda_re, s5_lambda_im, s5_b_re, s5_b_im, s5_c_re, s5_c_im, s5_d, s5_log_dt, s5_glu_w1, s5_glu_w2, q_norm_g, k_norm_g, attn_sinks, ssd_conv_w, ssd_conv_b, ssd_dt_bias, ssd_a_log, ssd_d, ssd_norm_g, p_s5, p_attn, p_ssd, w_out, norm2_g, w_router_group, b_router_group, w_router_expert, b_router_expert, w_exp_gate, w_exp_up, w_exp_down):
    p = dict(norm1_g=norm1_g, w_in=w_in, b_gate=b_gate, s5_lambda_re=s5_lambda_re, s5_lambda_im=s5_lambda_im,
             s5_b_re=s5_b_re, s5_b_im=s5_b_im, s5_c_re=s5_c_re, s5_c_im=s5_c_im, s5_d=s5_d,
             s5_log_dt=s5_log_dt, s5_glu_w1=s5_glu_w1, s5_glu_w2=s5_glu_w2, q_norm_g=q_norm_g,
             k_norm_g=k_norm_g, attn_sinks=attn_sinks, ssd_conv_w=ssd_conv_w, ssd_conv_b=ssd_conv_b,
             ssd_dt_bias=ssd_dt_bias, ssd_a_log=ssd_a_log, ssd_d=ssd_d, ssd_norm_g=ssd_norm_g, p_s5=p_s5,
             p_attn=p_attn, p_ssd=p_ssd, w_out=w_out, norm2_g=norm2_g, w_router_group=w_router_group,
             b_router_group=b_router_group, w_router_expert=w_router_expert, b_router_expert=b_router_expert,
             w_exp_gate=w_exp_gate, w_exp_up=w_exp_up, w_exp_down=w_exp_down)
    p['w_in_t'] = jnp.swapaxes(w_in.astype(F32), 1, 2)
    bsz, seq, dm = x.shape
    depth = w_in.shape[0]
    t = bsz * seq
    xs = [x.reshape(t, dm)]
    for l in range(depth):
        s5_tabs = _s5_tables(s5_lambda_re[l], s5_lambda_im[l], s5_b_re[l], s5_b_im[l], s5_c_re[l], s5_c_im[l],
                             s5_log_dt[l])
        xs = _layer(xs, t, l, p, s5_tabs, bsz)
    return _residual_sum(xs, t).reshape(bsz, seq, dm)
```
